```python
import math
import jax, jax.numpy as jnp
from jax import lax
import numpy as np

D_MODEL = 2048
BATCH = 2
SEQ = 8192
DEPTH = 2
DEC_BATCH = 2
DEC_SEQ = 4096
PAST_LEN = 128

GRID_W = 64
HEAD_DIM = 128
N_HEADS_A = 4
B_PAIRS = ((128, 1), (512, 4), (2048, 16))
N_HEADS_B_GROUP = 4
N_HEADS_B = N_HEADS_B_GROUP * len(B_PAIRS)
N_HEADS_QKV = N_HEADS_A + N_HEADS_B
WIN_H = 8
WIN_W = 16
ROPE_THETA = 10000.0
D_FF = 5632
N_EXPERTS = 8
TOP_K = 2
D_FF_EXPERT = 7168
MOE_BLOCK = 128
D_INNER = 2 * D_MODEL
SSM_HEAD_DIM = 64
SSM_HEADS = D_INNER // SSM_HEAD_DIM
SSM_GROUPS = 8
SSM_HPG = SSM_HEADS // SSM_GROUPS
SSM_STATE = 128
SSM_CONV = 5
SSM_CHUNK = 128
CONV_DIM = D_INNER + 2 * SSM_GROUPS * SSM_STATE
IN_PROJ_C = D_INNER + CONV_DIM + 2 * SSM_HEADS
N_EVEN = (DEPTH + 1) // 2
N_ODD = DEPTH // 2
EPS = 1e-6
F32 = jnp.float32

kernel_name = "hybrid_natten_dilated_ssd_moe_encoder"


def rmsnorm(x, g):
    xf = x.astype(F32)
    y = xf * lax.rsqrt(jnp.mean(xf * xf, axis=-1, keepdims=True) + EPS)
    return (y * g.astype(F32)).astype(x.dtype)


def rope(t, pos):
    half = HEAD_DIM // 2
    inv = ROPE_THETA ** (-jnp.arange(half, dtype=F32) / half)
    ang = pos[:, None] * inv[None, :]
    cos = jnp.cos(ang)[None, :, None, :]
    sin = jnp.sin(ang)[None, :, None, :]
    tf = t.astype(F32)
    t1, t2 = tf[..., :half], tf[..., half:]
    return jnp.concatenate([t1 * cos - t2 * sin, t2 * cos + t1 * sin], axis=-1).astype(t.dtype)


def neighbourhood_attention(q, k, v, rpb):
    b, T, h, hd = q.shape
    rows = T // GRID_W
    kh = min(WIN_H, rows)
    qg = q.reshape(b, rows, GRID_W, h, hd)
    kg = k.reshape(b, rows, GRID_W, h, hd)
    vg = v.reshape(b, rows, GRID_W, h, hd)
    r = jnp.arange(rows)
    row_start = jnp.clip(r - WIN_H // 2, 0, rows - kh)
    row_idx = row_start[:, None] + jnp.arange(kh)[None, :]
    k_rb = kg[:, row_idx]
    v_rb = vg[:, row_idx]
    c = jnp.arange(GRID_W)
    col_start = jnp.clip(c - WIN_W // 2, 0, GRID_W - WIN_W)
    col_ok = (c[None, :] >= col_start[:, None]) & (c[None, :] < col_start[:, None] + WIN_W)
    d_row = row_idx - r[:, None] + (WIN_H - 1)
    d_col = jnp.clip(c[None, :] - c[:, None] + (WIN_W - 1), 0, 2 * WIN_W - 2)
    bias = rpb[:, d_row[:, None, :, None], d_col[None, :, None, :]].astype(F32)
    s = jnp.einsum('brqhd,brikhd->bhrqik', qg, k_rb, preferred_element_type=F32) * (hd ** -0.5)
    s = s + bias[None]
    s = jnp.where(col_ok[None, None, None, :, None, :], s, -jnp.inf)
    p = jax.nn.softmax(s.reshape(b, h, rows, GRID_W, kh * GRID_W), axis=-1)
    p = p.reshape(b, h, rows, GRID_W, kh, GRID_W).astype(v.dtype)
    o = jnp.einsum('bhrqik,brikhd->brqhd', p, v_rb)
    return o.reshape(b, T, h, hd)


def dilated_group(q, k, v, window, dilation):
    b, T, h, hd = q.shape
    rad = window // (2 * dilation)
    blk = rad
    n = T // dilation
    nb = -(-n // blk)
    npad = nb * blk

    def sub(t):
        return t.reshape(b, n, dilation, h, hd).transpose(0, 2, 1, 3, 4)

    def kwin(t):
        tp = jnp.pad(sub(t), ((0, 0), (0, 0), (blk, npad - n + blk), (0, 0), (0, 0)))
        tp = tp.reshape(b, dilation, nb + 2, blk, h, hd)
        return jnp.concatenate([tp[:, :, :-2], tp[:, :, 1:-1], tp[:, :, 2:]], axis=3)

    qs = jnp.pad(sub(q), ((0, 0), (0, 0), (0, npad - n), (0, 0), (0, 0))).reshape(b, dilation, nb, blk, h, hd)
    kw = kwin(k)
    vw = kwin(v)
    s = jnp.einsum('bcjqhe,bcjkhe->bchjqk', qs, kw, preferred_element_type=F32) * (hd ** -0.5)
    qm = jnp.arange(nb)[:, None, None] * blk + jnp.arange(blk)[None, :, None]
    km = (jnp.arange(nb)[:, None, None] - 1) * blk + jnp.arange(3 * blk)[None, None, :]
    ok = ((jnp.abs(km - qm) <= rad) & (km >= 0) & (km < n)) | (qm >= n)
    s = jnp.where(ok, s, -jnp.inf)
    m = jnp.max(s, axis=-1, keepdims=True)
    e = jnp.exp(s - m)
    den = jnp.sum(e, axis=-1)
    o = jnp.einsum('bchjqk,bcjkhe->bcjqhe', e.astype(v.dtype), vw, preferred_element_type=F32)
    o = o / den.transpose(0, 1, 3, 4, 2)[..., None]
    lse = (m[..., 0] + jnp.log(den)).transpose(0, 1, 3, 4, 2)
    o = o.reshape(b, dilation, npad, h, hd)[:, :, :n].transpose(0, 2, 1, 3, 4).reshape(b, T, h, hd)
    lse = lse.reshape(b, dilation, npad, h)[:, :, :n].transpose(0, 2, 1, 3).reshape(b, T, h)
    return o, lse


def hybrid_attention(h, w_qkv, rpb, w_o):
    b, T, _ = h.shape
    qkv = (h @ w_qkv).reshape(b, T, 3, N_HEADS_QKV, HEAD_DIM)
    q, k, v = qkv[:, :, 0], qkv[:, :, 1], qkv[:, :, 2]
    o_a = neighbourhood_attention(q[:, :, :N_HEADS_A], k[:, :, :N_HEADS_A], v[:, :, :N_HEADS_A], rpb)
    pos = jnp.arange(T, dtype=F32)
    q_b = rope(q[:, :, N_HEADS_A:], pos)
    k_b = rope(k[:, :, N_HEADS_A:], pos)
    v_b = v[:, :, N_HEADS_A:]
    outs, lses = [], []
    for g, (window, dilation) in enumerate(B_PAIRS):
        sl = slice(g * N_HEADS_B_GROUP, (g + 1) * N_HEADS_B_GROUP)
        o_g, l_g = dilated_group(q_b[:, :, sl], k_b[:, :, sl], v_b[:, :, sl], window, dilation)
        outs.append(o_g)
        lses.append(l_g)
    wts = jax.nn.softmax(jnp.stack(lses, axis=0), axis=0)
    o_b = jnp.sum(wts[..., None] * jnp.stack(outs, axis=0), axis=0)
    o = jnp.concatenate([o_a.reshape(b, T, -1), o_b.astype(h.dtype).reshape(b, T, -1)], axis=-1)
    return o @ w_o


def swiglu(h, w_gate, w_up, w_down):
    return (jax.nn.silu(h @ w_gate) * (h @ w_up)) @ w_down


def depthwise_conv(x, w, bias):
    C = x.shape[-1]
    out = lax.conv_general_dilated(x, w[:, None, :], window_strides=(1,),
                                   padding=[(SSM_CONV // 2, SSM_CONV // 2)],
                                   dimension_numbers=('NWC', 'WIO', 'NWC'),
                                   feature_group_count=C)
    return out + bias


def ssd_chunked(xs, dt, A, Bm, Cm):
    b, T, G, HPG, P = xs.shape
    N = Bm.shape[-1]
    L = SSM_CHUNK
    nc = T // L
    dt = dt.reshape(b, T, G, HPG)
    a = dt * A.reshape(G, HPG)

    def chunks(t):
        return jnp.moveaxis(t.reshape((b, nc, L) + t.shape[2:]), 1, 0)

    tril = jnp.tril(jnp.ones((L, L), dtype=bool))

    def step(state, inp):
        x_c, dt_c, a_c, b_c, c_c = inp
        acum = jnp.cumsum(a_c, axis=1)
        seg = acum[:, :, None] - acum[:, None, :]
        decay = jnp.exp(jnp.where(tril[None, :, :, None, None], seg, -jnp.inf))
        cb = jnp.einsum('blgn,bsgn->blsg', c_c, b_c)
        w = cb[..., None] * decay * dt_c[:, None]
        y = jnp.einsum('blsgh,bsghp->blghp', w, x_c)
        y = y + jnp.einsum('blgn,bghpn->blghp', c_c, state) * jnp.exp(acum)[..., None]
        to_end = jnp.exp(acum[:, -1:] - acum) * dt_c
        state = state * jnp.exp(acum[:, -1])[..., None, None] + jnp.einsum('bsgn,bsgh,bsghp->bghpn', b_c, to_end, x_c)
        return state, y

    state0 = jnp.zeros((b, G, HPG, P, N), F32)
    _, ys = lax.scan(step, state0, (chunks(xs.astype(F32)), chunks(dt), chunks(a),
                                    chunks(Bm.astype(F32)), chunks(Cm.astype(F32))))
    return jnp.moveaxis(ys, 0, 1).reshape(b, T, G, HPG, P)


def mamba2_mixer(h, w_in, conv_w, conv_b, dt_bias, a_log, d_skip, g_gate, w_out):
    b, T, _ = h.shape
    zxbcdt = h @ w_in
    z = zxbcdt[..., :D_INNER]
    xbc = zxbcdt[..., D_INNER:D_INNER + CONV_DIM]
    dt_raw = zxbcdt[..., D_INNER + CONV_DIM:].reshape(b, T, 2, SSM_HEADS)
    xbc = jax.nn.silu(depthwise_conv(xbc, conv_w, conv_b))
    gn = SSM_GROUPS * SSM_STATE
    xs = xbc[..., :D_INNER].reshape(b, T, SSM_GROUPS, SSM_HPG, SSM_HEAD_DIM)
    Bm = xbc[..., D_INNER:D_INNER + gn].reshape(b, T, SSM_GROUPS, SSM_STATE)
    Cm = xbc[..., D_INNER + gn:].reshape(b, T, SSM_GROUPS, SSM_STATE)
    dt = jax.nn.softplus(dt_raw.astype(F32) + dt_bias.astype(F32))
    A = -jnp.exp(a_log.astype(F32))
    y_f = ssd_chunked(xs, dt[:, :, 0], A[0], Bm, Cm)
    y_b = ssd_chunked(xs[:, ::-1], dt[:, ::-1, 1], A[1], Bm[:, ::-1], Cm[:, ::-1])[:, ::-1]
    y = y_f + y_b + d_skip.astype(F32).reshape(SSM_GROUPS, SSM_HPG)[..., None] * xs.astype(F32)
    y = y.reshape(b, T, D_INNER) * jax.nn.silu(z.astype(F32))
    y = y * lax.rsqrt(jnp.mean(y * y, axis=-1, keepdims=True) + EPS) * g_gate.astype(F32)
    return y.astype(h.dtype) @ w_out


def moe_swiglu(h, w_router, w_gate, w_up, w_down):
    b, T, Dm = h.shape
    xf = h.reshape(-1, Dm)
    N = xf.shape[0]
    logits = jnp.einsum('nd,de->ne', xf, w_router, preferred_element_type=F32)
    top_val, top_idx = lax.top_k(logits, TOP_K)
    gates = jax.nn.softmax(top_val, axis=-1)
    e_flat = top_idx.reshape(-1)
    tok_flat = jnp.repeat(jnp.arange(N), TOP_K)
    g_flat = gates.reshape(-1)
    order = jnp.argsort(e_flat)
    es, ts, gs = e_flat[order], tok_flat[order], g_flat[order]
    counts = jnp.bincount(e_flat, length=N_EXPERTS)
    padded = ((counts + MOE_BLOCK - 1) // MOE_BLOCK) * MOE_BLOCK
    start = jnp.cumsum(counts) - counts
    ends = jnp.cumsum(padded)
    pstart = ends - padded
    dst = pstart[es] + (jnp.arange(N * TOP_K) - start[es])
    n_blocks = -(-(N * TOP_K) // MOE_BLOCK) + N_EXPERTS
    P = n_blocks * MOE_BLOCK
    tok_pad = jnp.full((P,), N, dtype=jnp.int32).at[dst].set(ts.astype(jnp.int32))
    gate_pad = jnp.zeros((P,), F32).at[dst].set(gs)
    blk_expert = jnp.minimum(jnp.searchsorted(ends, jnp.arange(n_blocks) * MOE_BLOCK, side='right'), N_EXPERTS - 1)
    x_pad = jnp.concatenate([xf, jnp.zeros((1, Dm), xf.dtype)], axis=0)[tok_pad].reshape(n_blocks, MOE_BLOCK, Dm)

    def expert_block(args):
        xb, e = args
        hid = jax.nn.silu(xb @ w_gate[e]) * (xb @ w_up[e])
        return hid @ w_down[e]

    y = lax.map(expert_block, (x_pad, blk_expert)).reshape(P, Dm).astype(F32) * gate_pad[:, None]
    out = jax.ops.segment_sum(y, tok_pad, num_segments=N + 1)[:N]
    return out.astype(h.dtype).reshape(b, T, Dm)


def _trunk(x, g_mix, g_ffn, w_qkv, rpb, w_o, w_ff_gate, w_ff_up, w_ff_down,
           w_in_c, conv_w, conv_b, dt_bias, a_log, d_skip, g_gate, w_out_c,
           w_router, w_e_gate, w_e_up, w_e_down, g_final):
    for layer in range(DEPTH):
        i = layer // 2
        if layer % 2 == 0:
            x = x + hybrid_attention(rmsnorm(x, g_mix[layer]), w_qkv[i], rpb[i], w_o[i])
            x = x + swiglu(rmsnorm(x, g_ffn[layer]), w_ff_gate[i], w_ff_up[i], w_ff_down[i])
        else:
            x = x + mamba2_mixer(rmsnorm(x, g_mix[layer]), w_in_c[i], conv_w[i], conv_b[i], dt_bias[i],
                                 a_log[i], d_skip[i], g_gate[i], w_out_c[i])
            x = x + moe_swiglu(rmsnorm(x, g_ffn[layer]), w_router[i], w_e_gate[i], w_e_up[i], w_e_down[i])
    return rmsnorm(x, g_final)


def setup_inputs(seed: int = 0) -> dict:
    key = jax.random.key(seed)
    ks = jax.random.split(key, 24)

    def nrm(k, shape, fan_in):
        return jax.random.normal(k, shape, F32) * (fan_in ** -0.5)

    def gain(k, shape):
        return 1.0 + 0.02 * jax.random.normal(k, shape, F32)

    dt0 = jnp.exp(jax.random.uniform(ks[13], (N_ODD, 2, SSM_HEADS), F32, math.log(1e-3), math.log(1e-1)))
    return {
        "x_prompt": jax.random.normal(ks[0], (BATCH, SEQ, D_MODEL), F32),
        "x_sample": jax.random.normal(ks[1], (DEC_BATCH, DEC_SEQ, D_MODEL), F32),
        "g_mix": gain(ks[2], (DEPTH, D_MODEL)),
        "g_ffn": gain(ks[3], (DEPTH, D_MODEL)),
        "w_qkv": nrm(ks[4], (N_EVEN, D_MODEL, 3 * N_HEADS_QKV * HEAD_DIM), D_MODEL),
        "rpb": 0.1 * jax.random.normal(ks[5], (N_EVEN, N_HEADS_A, 2 * WIN_H - 1, 2 * WIN_W - 1), F32),
        "w_o": nrm(ks[6], (N_EVEN, (N_HEADS_A + N_HEADS_B_GROUP) * HEAD_DIM, D_MODEL), (N_HEADS_A + N_HEADS_B_GROUP) * HEAD_DIM),
        "w_ff_gate": nrm(ks[7], (N_EVEN, D_MODEL, D_FF), D_MODEL),
        "w_ff_up": nrm(ks[8], (N_EVEN, D_MODEL, D_FF), D_MODEL),
        "w_ff_down": nrm(ks[9], (N_EVEN, D_FF, D_MODEL), D_FF),
        "w_in_c": nrm(ks[10], (N_ODD, D_MODEL, IN_PROJ_C), D_MODEL),
        "conv_w": nrm(ks[11], (N_ODD, SSM_CONV, CONV_DIM), SSM_CONV),
        "conv_b": 0.01 * jax.random.normal(ks[12], (N_ODD, CONV_DIM), F32),
        "dt_bias": dt0 + jnp.log(-jnp.expm1(-dt0)),
        "a_log": jnp.log(jax.random.uniform(ks[14], (N_ODD, 2, SSM_HEADS), F32, 1.0, 16.0)),
        "d_skip": 1.0 + 0.1 * jax.random.normal(ks[15], (N_ODD, SSM_HEADS), F32),
        "g_gate": gain(ks[16], (N_ODD, D_INNER)),
        "w_out_c": nrm(ks[17], (N_ODD, D_INNER, D_MODEL), D_INNER),
        "w_router": nrm(ks[18], (N_ODD, D_MODEL, N_EXPERTS), D_MODEL),
        "w_e_gate": nrm(ks[19], (N_ODD, N_EXPERTS, D_MODEL, D_FF_EXPERT), D_MODEL),
        "w_e_up": nrm(ks[20], (N_ODD, N_EXPERTS, D_MODEL, D_FF_EXPERT), D_MODEL),
        "w_e_down": nrm(ks[21], (N_ODD, N_EXPERTS, D_FF_EXPERT, D_MODEL), D_FF_EXPERT),
        "g_final": gain(ks[22], (D_MODEL,)),
    }


def reference(x_prompt, x_sample, g_mix, g_ffn, w_qkv, rpb, w_o, w_ff_gate, w_ff_up, w_ff_down,
              w_in_c, conv_w, conv_b, dt_bias, a_log, d_skip, g_gate, w_out_c,
              w_router, w_e_gate, w_e_up, w_e_down, g_final):
    y_prompt = _trunk(x_prompt, g_mix, g_ffn, w_qkv, rpb, w_o, w_ff_gate, w_ff_up, w_ff_down,
                      w_in_c, conv_w, conv_b, dt_bias, a_log, d_skip, g_gate, w_out_c,
                      w_router, w_e_gate, w_e_up, w_e_down, g_final)
    y_sample = _trunk(x_sample, g_mix, g_ffn, w_qkv, rpb, w_o, w_ff_gate, w_ff_up, w_ff_down,
                      w_in_c, conv_w, conv_b, dt_bias, a_log, d_skip, g_gate, w_out_c,
                      w_router, w_e_gate, w_e_up, w_e_down, g_final)
    return (y_prompt, y_sample)
```

```python
import functools
import math

import jax
import jax.numpy as jnp
import numpy as np
from jax import lax
from jax.experimental import pallas as pl
from jax.experimental.pallas import tpu as pltpu

F32 = jnp.float32
BF16 = jnp.bfloat16
EPS = 1e-6
NEG_INF = float("-inf")

GRID_W = 64
HEAD_DIM = 128
N_HEADS_A = 4
N_HEADS_B_GROUP = 4
DILATIONS = (1, 4, 16)
BAND_RADIUS = 64
N_HEADS_QKV = N_HEADS_A + N_HEADS_B_GROUP * len(DILATIONS)
WIN_H = 8
WIN_W = 16
ROPE_THETA = 10000.0
SSM_HEAD_DIM = 64
SSM_GROUPS = 8
SSM_STATE = 128
SSM_CONV = 5
SSM_CHUNK = 128
N_EXPERTS = 8
TOP_K = 2

V7X_VMEM_BYTES = 64 * 1024 * 1024
LANES = 128
BF16_SUBLANES = 16

NAT_ROWS = 8
NAT_KROWS = NAT_ROWS + WIN_H - 1
DIL_TQ = 256
DIL_KC = 256
MOE_TM = 512
GATHER_ROWS = 256


def _cparams(semantics, vmem_estimate):
    limit = int(min(max(2 * vmem_estimate, 32 * 1024 * 1024), V7X_VMEM_BYTES - 8 * 1024 * 1024))
    return pltpu.CompilerParams(dimension_semantics=semantics, vmem_limit_bytes=limit)


def _rmsnorm(x, g):
    return x * lax.rsqrt(jnp.mean(x * x, axis=-1, keepdims=True) + EPS) * g


def _sigmoid(x):
    return 1.0 / (1.0 + jnp.exp(-x))


def _softplus(x):
    return jnp.maximum(x, 0.0) + jnp.log(1.0 + jnp.exp(-jnp.abs(x)))


def _dot(a, b):
    return jnp.dot(a, b, preferred_element_type=F32)


def _dot_nt(a, b):
    return lax.dot_general(a, b, (((1,), (1,)), ((), ())), preferred_element_type=F32)


def _norm_matmul_kernel(x_ref, g_ref, w_ref, o_ref, xn_ref):
    @pl.when(pl.program_id(1) == 0)
    def _():
        xn_ref[...] = _rmsnorm(x_ref[...], g_ref[...]).astype(BF16)

    o_ref[...] = _dot(xn_ref[...], w_ref[...]).astype(o_ref.dtype)


def norm_matmul(x, g, w, *, tn, out_dtype, tm=512):
    n, k = x.shape
    m = w.shape[1]
    tm = min(tm, n)
    est = 2 * tm * k * 4 + tm * k * 2 + 2 * k * tn * 2 + 2 * tm * tn * 4
    return pl.pallas_call(
        _norm_matmul_kernel,
        grid=(n // tm, m // tn),
        in_specs=[
            pl.BlockSpec((tm, k), lambda i, j: (i, 0)),
            pl.BlockSpec((1, k), lambda i, j: (0, 0)),
            pl.BlockSpec((k, tn), lambda i, j: (0, j)),
        ],
        out_specs=pl.BlockSpec((None, tm, tn), lambda i, j: (j, i, 0)),
        out_shape=jax.ShapeDtypeStruct((m // tn, n, tn), out_dtype),
        scratch_shapes=[pltpu.VMEM((tm, k), BF16)],
        compiler_params=_cparams(("parallel", "arbitrary"), est),
        name="norm_matmul",
    )(x, g.reshape(1, k), w)


def _qkv_kernel(x_ref, g_ref, w_ref, cos_ref, sin_ref, o_ref, xn_ref, *, heads_per_tile, scale):
    j = pl.program_id(1)
    tiles_per_part = N_HEADS_QKV // heads_per_tile

    @pl.when(j == 0)
    def _():
        xn_ref[...] = _rmsnorm(x_ref[...], g_ref[...]).astype(BF16)

    r = _dot(xn_ref[...], w_ref[...])
    r = r * jnp.where(j < tiles_per_part, scale, 1.0)
    use_rope = jnp.logical_and(j < 2 * tiles_per_part, j % tiles_per_part != 0)

    @pl.when(use_rope)
    def _():
        c = cos_ref[...]
        s = sin_ref[...]
        for h in range(heads_per_tile):
            p = r[:, h * HEAD_DIM:(h + 1) * HEAD_DIM]
            o_ref[h] = (p * c + pltpu.roll(p, HEAD_DIM // 2, 1) * s).astype(o_ref.dtype)

    @pl.when(jnp.logical_not(use_rope))
    def _():
        for h in range(heads_per_tile):
            o_ref[h] = r[:, h * HEAD_DIM:(h + 1) * HEAD_DIM].astype(o_ref.dtype)


def qkv_proj(x, g, w, cos, sin, *, seq_len, tm=512):
    n, k = x.shape
    m = w.shape[1]
    hpt = N_HEADS_A
    tn = hpt * HEAD_DIM
    tm = min(tm, seq_len)
    tiles_per_seq = seq_len // tm
    est = 2 * tm * k * 4 + tm * k * 2 + 2 * k * tn * 2 + 2 * tm * tn * 2 + 4 * tm * HEAD_DIM * 4 + tm * tn * 4
    return pl.pallas_call(
        functools.partial(_qkv_kernel, heads_per_tile=hpt, scale=HEAD_DIM ** -0.5),
        grid=(n // tm, m // tn),
        in_specs=[
            pl.BlockSpec((tm, k), lambda i, j: (i, 0)),
            pl.BlockSpec((1, k), lambda i, j: (0, 0)),
            pl.BlockSpec((k, tn), lambda i, j: (0, j)),
            pl.BlockSpec((tm, HEAD_DIM), lambda i, j: (i % tiles_per_seq, 0)),
            pl.BlockSpec((tm, HEAD_DIM), lambda i, j: (i % tiles_per_seq, 0)),
        ],
        out_specs=pl.BlockSpec((hpt, tm, HEAD_DIM), lambda i, j: (j, i, 0)),
        out_shape=jax.ShapeDtypeStruct((m // HEAD_DIM, n, HEAD_DIM), BF16),
        scratch_shapes=[pltpu.VMEM((tm, k), BF16)],
        compiler_params=_cparams(("parallel", "arbitrary"), est),
        name="qkv_proj",
    )(x, g.reshape(1, k), w, cos, sin)


def rope_tables(seq_len):
    half = HEAD_DIM // 2
    inv = ROPE_THETA ** (-jnp.arange(half, dtype=F32) / half)
    ang = jnp.arange(seq_len, dtype=F32)[:, None] * inv[None, :]
    cos = jnp.cos(ang)
    sin = jnp.sin(ang)
    return jnp.concatenate([cos, cos], axis=1), jnp.concatenate([-sin, sin], axis=1)


def _natten_kernel(q_ref, k_ref, v_ref, bias_ref, o_ref, *, n_blocks, grid_rows):
    blk = pl.program_id(2)
    first_row = jnp.where(blk == 0, 0,
                          jnp.where(blk == n_blocks - 1, grid_rows - NAT_KROWS, blk * NAT_ROWS - WIN_H // 2))
    start = pl.multiple_of(first_row * GRID_W, GRID_W)
    kw = k_ref[pl.ds(start, NAT_KROWS * GRID_W), :]
    vw = v_ref[pl.ds(start, NAT_KROWS * GRID_W), :]
    s = _dot_nt(q_ref[...], kw) + bias_ref[...]
    m = jnp.max(s, axis=-1, keepdims=True)
    p = jnp.exp(s - m)
    l = jnp.sum(p, axis=-1, keepdims=True)
    o_ref[...] = (_dot(p.astype(BF16), vw) / l).astype(o_ref.dtype)


def natten_bias_tiles(rpb, grid_rows):
    n_blocks = grid_rows // NAT_ROWS
    r0s = np.array([0, NAT_ROWS, (n_blocks - 1) * NAT_ROWS])
    k0s = np.array([0, NAT_ROWS - WIN_H // 2, grid_rows - NAT_KROWS])
    r = r0s[:, None] + np.arange(NAT_ROWS)[None, :]
    kr = k0s[:, None] + np.arange(NAT_KROWS)[None, :]
    rs = np.clip(r - WIN_H // 2, 0, grid_rows - WIN_H)
    row_ok = (kr[:, None, :] >= rs[:, :, None]) & (kr[:, None, :] < rs[:, :, None] + WIN_H)
    d_row = np.clip(kr[:, None, :] - r[:, :, None] + (WIN_H - 1), 0, 2 * WIN_H - 2)
    c = np.arange(GRID_W)
    cs = np.clip(c - WIN_W // 2, 0, GRID_W - WIN_W)
    col_ok = (c[None, :] >= cs[:, None]) & (c[None, :] < cs[:, None] + WIN_W)
    d_col = np.clip(c[None, :] - c[:, None] + (WIN_W - 1), 0, 2 * WIN_W - 2)
    bias = rpb[:, d_row[:, :, None, :, None], d_col[None, None, :, None, :]]
    ok = row_ok[:, :, None, :, None] & col_ok[None, None, :, None, :]
    bias = jnp.where(ok[None], bias.astype(F32), NEG_INF)
    bias = jnp.transpose(bias, (1, 0, 2, 3, 4, 5))
    return bias.reshape(3, rpb.shape[0], NAT_ROWS * GRID_W, NAT_KROWS * GRID_W)


def natten(qkv, bias_tiles, *, batch, seq_len):
    n = batch * seq_len
    grid_rows = seq_len // GRID_W
    n_blocks = grid_rows // NAT_ROWS
    tq = NAT_ROWS * GRID_W
    tk = NAT_KROWS * GRID_W

    def tile_kind(blk):
        return jnp.where(blk == 0, 0, jnp.where(blk == n_blocks - 1, 2, 1))

    est = 4 * seq_len * HEAD_DIM * 2 + 2 * tq * tk * 4 + 3 * tq * tk * 4
    return pl.pallas_call(
        functools.partial(_natten_kernel, n_blocks=n_blocks, grid_rows=grid_rows),
        grid=(batch, N_HEADS_A, n_blocks),
        in_specs=[
            pl.BlockSpec((None, tq, HEAD_DIM), lambda b, h, i: (h, b * n_blocks + i, 0)),
            pl.BlockSpec((None, seq_len, HEAD_DIM), lambda b, h, i: (N_HEADS_QKV + h, b, 0)),
            pl.BlockSpec((None, seq_len, HEAD_DIM), lambda b, h, i: (2 * N_HEADS_QKV + h, b, 0)),
            pl.BlockSpec((None, None, tq, tk), lambda b, h, i: (tile_kind(i), h, 0, 0)),
        ],
        out_specs=pl.BlockSpec((tq, HEAD_DIM), lambda b, h, i: (b * n_blocks + i, h)),
        out_shape=jax.ShapeDtypeStruct((n, N_HEADS_A * HEAD_DIM), BF16),
        compiler_params=_cparams(("parallel", "parallel", "arbitrary"), est),
        name="natten",
    )(qkv, qkv, qkv, bias_tiles)


def _dilated_kernel(q0, q1, q2, k0, k1, k2, v0, v1, v2, o_ref, *, seq_len):
    tq, kc = DIL_TQ, DIL_KC
    t0 = pl.program_id(2) * tq
    rel0 = lax.broadcasted_iota(jnp.int32, (tq, kc), 1) - lax.broadcasted_iota(jnp.int32, (tq, kc), 0)
    m = jnp.full((tq, 1), -1e30, F32)
    l = jnp.zeros((tq, 1), F32)
    acc = jnp.zeros((tq, HEAD_DIM), F32)
    for q_ref, k_ref, v_ref, dil in ((q0, k0, v0, DILATIONS[0]), (q1, k1, v1, DILATIONS[1]), (q2, k2, v2, DILATIONS[2])):
        reach = BAND_RADIUS * dil
        halo = -(-reach // kc) * kc
        n_chunks = (tq + 2 * halo) // kc
        window_start = t0 - halo
        lo = jnp.maximum(0, (halo - t0) // kc)
        hi = jnp.minimum(n_chunks, (seq_len - window_start) // kc)
        q = q_ref[...]

        def body(ci, carry, k_ref=k_ref, v_ref=v_ref, dil=dil, reach=reach, halo=halo, window_start=window_start, q=q):
            m, l, acc = carry
            ks = pl.multiple_of(window_start + ci * kc, kc)
            kk = k_ref[pl.ds(ks, kc), :]
            vv = v_ref[pl.ds(ks, kc), :]
            s = _dot_nt(q, kk)
            rel = rel0 + (ci * kc - halo)
            ok = jnp.logical_and(jnp.abs(rel) <= reach, (rel & (dil - 1)) == 0)
            s = jnp.where(ok, s, NEG_INF)
            m_new = jnp.maximum(m, jnp.max(s, axis=-1, keepdims=True))
            alpha = jnp.exp(m - m_new)
            p = jnp.exp(s - m_new)
            l = alpha * l + jnp.sum(p, axis=-1, keepdims=True)
            acc = alpha * acc + _dot(p.astype(BF16), vv)
            return m_new, l, acc

        m, l, acc = lax.fori_loop(lo, hi, body, (m, l, acc))
    o_ref[...] = (acc / l).astype(o_ref.dtype)


def dilated_attention(qkv, *, batch, seq_len):
    n = batch * seq_len
    nq = seq_len // DIL_TQ

    def q_spec(g):
        return pl.BlockSpec((None, DIL_TQ, HEAD_DIM),
                            lambda b, j, i: (N_HEADS_A + N_HEADS_B_GROUP * g + j, b * nq + i, 0))

    def kv_spec(part, g):
        return pl.BlockSpec((None, seq_len, HEAD_DIM),
                            lambda b, j, i: (part * N_HEADS_QKV + N_HEADS_A + N_HEADS_B_GROUP * g + j, b, 0))

    est = 12 * seq_len * HEAD_DIM * 2 + 8 * DIL_TQ * DIL_KC * 4
    groups = range(len(DILATIONS))
    return pl.pallas_call(
        functools.partial(_dilated_kernel, seq_len=seq_len),
        grid=(batch, N_HEADS_B_GROUP, nq),
        in_specs=[q_spec(g) for g in groups] + [kv_spec(1, g) for g in groups] + [kv_spec(2, g) for g in groups],
        out_specs=pl.BlockSpec((DIL_TQ, HEAD_DIM), lambda b, j, i: (b * nq + i, j)),
        out_shape=jax.ShapeDtypeStruct((n, N_HEADS_B_GROUP * HEAD_DIM), BF16),
        compiler_params=_cparams(("parallel", "parallel", "arbitrary"), est),
        name="dilated_attention",
    )(*([qkv] * 9))


def _attn_out_kernel(x_ref, oa_ref, ob_ref, w_ref, o_ref):
    ka = oa_ref.shape[1]
    o_ref[...] = x_ref[...] + _dot(oa_ref[...], w_ref[:ka, :]) + _dot(ob_ref[...], w_ref[ka:, :])


def attn_out(x, o_a, o_b, w, *, tm=512):
    n, d = x.shape
    ka, kb = o_a.shape[1], o_b.shape[1]
    tm = min(tm, n)
    est = 4 * tm * d * 4 + 2 * (ka + kb) * d * 2 + 2 * tm * (ka + kb) * 2
    return pl.pallas_call(
        _attn_out_kernel,
        grid=(n // tm,),
        in_specs=[
            pl.BlockSpec((tm, d), lambda i: (i, 0)),
            pl.BlockSpec((tm, ka), lambda i: (i, 0)),
            pl.BlockSpec((tm, kb), lambda i: (i, 0)),
            pl.BlockSpec((ka + kb, d), lambda i: (0, 0)),
        ],
        out_specs=pl.BlockSpec((tm, d), lambda i: (i, 0)),
        out_shape=jax.ShapeDtypeStruct((n, d), F32),
        compiler_params=_cparams(("parallel",), est),
        name="attn_out",
    )(x, o_a, o_b, w)


def _ffn_kernel(x_ref, g_ref, wg_ref, wu_ref, wd_ref, o_ref, xn_ref):
    f = pl.program_id(1)

    @pl.when(f == 0)
    def _():
        xn_ref[...] = _rmsnorm(x_ref[...], g_ref[...]).astype(BF16)
        o_ref[...] = x_ref[...]

    xn = xn_ref[...]
    gate = _dot(xn, wg_ref[...])
    up = _dot(xn, wu_ref[...])
    hid = (gate * _sigmoid(gate) * up).astype(BF16)
    o_ref[...] += _dot(hid, wd_ref[...])


def ffn(x, g, wg, wu, wd, *, tm=512, tf=512):
    n, d = x.shape
    ff = wg.shape[1]
    tm = min(tm, n)
    est = 4 * tm * d * 4 + tm * d * 2 + 6 * d * tf * 2 + 3 * tm * tf * 4
    return pl.pallas_call(
        _ffn_kernel,
        grid=(n // tm, ff // tf),
        in_specs=[
            pl.BlockSpec((tm, d), lambda i, f: (i, 0)),
            pl.BlockSpec((1, d), lambda i, f: (0, 0)),
            pl.BlockSpec((d, tf), lambda i, f: (0, f)),
            pl.BlockSpec((d, tf), lambda i, f: (0, f)),
            pl.BlockSpec((tf, d), lambda i, f: (f, 0)),
        ],
        out_specs=pl.BlockSpec((tm, d), lambda i, f: (i, 0)),
        out_shape=jax.ShapeDtypeStruct((n, d), F32),
        scratch_shapes=[pltpu.VMEM((tm, d), BF16)],
        compiler_params=_cparams(("parallel", "arbitrary"), est),
        name="ffn",
    )(x, g.reshape(1, d), wg, wu, wd)


def _conv_kernel(xm_ref, xp_ref, xn_ref, w_ref, b_ref, o_ref, ext_ref, *, tr, n_row_blocks):
    i = pl.program_id(2)
    hb = BF16_SUBLANES
    pad = SSM_CONV // 2
    ext_ref[0:hb, :] = jnp.where(i > 0, xp_ref[...].astype(F32), 0.0)
    ext_ref[hb:hb + tr, :] = xm_ref[...].astype(F32)
    ext_ref[hb + tr:2 * hb + tr, :] = jnp.where(i < n_row_blocks - 1, xn_ref[...].astype(F32), 0.0)
    acc = jnp.broadcast_to(b_ref[...], o_ref.shape)
    for k in range(SSM_CONV):
        acc = acc + ext_ref[pl.ds(hb - pad + k, tr), :] * w_ref[k:k + 1, :]
    o_ref[...] = (acc * _sigmoid(acc)).astype(o_ref.dtype)


def conv_silu(zx, conv_w, conv_b, *, width, first_tile, n_tiles, col_offset, batch, seq_len, tr=512):
    n = batch * seq_len
    tile_w = zx.shape[2]
    per = tile_w // width
    tr = min(tr, seq_len)
    nr = seq_len // tr
    hb = BF16_SUBLANES
    seq_hb = seq_len // hb
    n_hb = n // hb

    def main_map(c, b, i):
        return (first_tile + c // per, b * nr + i, c % per)

    def prev_map(c, b, i):
        return (first_tile + c // per, jnp.maximum(b * seq_hb + i * (tr // hb) - 1, 0), c % per)

    def next_map(c, b, i):
        return (first_tile + c // per, jnp.minimum(b * seq_hb + (i + 1) * (tr // hb), n_hb - 1), c % per)

    est = 4 * tr * width * 2 + (tr + 2 * hb) * width * 4 + 4 * tr * width * 4
    return pl.pallas_call(
        functools.partial(_conv_kernel, tr=tr, n_row_blocks=nr),
        grid=(n_tiles, batch, nr),
        in_specs=[
            pl.BlockSpec((None, tr, width), main_map),
            pl.BlockSpec((None, hb, width), prev_map),
            pl.BlockSpec((None, hb, width), next_map),
            pl.BlockSpec((SSM_CONV, width), lambda c, b, i: (0, col_offset // width + c)),
            pl.BlockSpec((1, width), lambda c, b, i: (0, col_offset // width + c)),
        ],
        out_specs=pl.BlockSpec((None, tr, width), lambda c, b, i: (c, b * nr + i, 0)),
        out_shape=jax.ShapeDtypeStruct((n_tiles, n, width), BF16),
        scratch_shapes=[pltpu.VMEM((tr + 2 * hb, width), F32)],
        compiler_params=_cparams(("parallel", "parallel", "arbitrary"), est),
        name="conv_silu",
    )(zx, zx, zx, conv_w, conv_b.reshape(1, -1))


def _lane_cumsum(a):
    lane = lax.broadcasted_iota(jnp.int32, a.shape, 1)
    shift = 1
    while shift < a.shape[1]:
        a = a + jnp.where(lane >= shift, pltpu.roll(a, shift, 1), 0.0)
        shift *= 2
    return a


def _ssd_kernel(x_ref, bc_ref, dt_ref, dtb_ref, alog_ref, dskip_ref, y_ref, s_ref, *, hpg):
    L = SSM_CHUNK
    P = SSM_HEAD_DIM
    phase = pl.program_id(1)
    pairs = hpg // 2

    @pl.when(pl.program_id(2) == 0)
    def _():
        s_ref[...] = jnp.zeros_like(s_ref)

    li = lax.broadcasted_iota(jnp.int32, (L, L), 0)
    si = lax.broadcasted_iota(jnp.int32, (L, L), 1)
    causal = li >= si
    anti = li <= si
    low_lanes = lax.broadcasted_iota(jnp.int32, (L, 2 * P), 1) < P
    high_lanes = jnp.logical_not(low_lanes)
    low_lanes_row = lax.broadcasted_iota(jnp.int32, (1, 2 * P), 1) < P

    def pair_lanes(col0, col1):
        return jnp.where(low_lanes, jnp.broadcast_to(col0, (L, 2 * P)), jnp.broadcast_to(col1, (L, 2 * P)))

    def group_body(g, fwd):
        xg = x_ref[g]
        bg = bc_ref[g]
        cg = bc_ref[SSM_GROUPS + g]
        dt_r = _softplus(dt_ref[g] + dtb_ref[g])
        a_r = dt_r * (-jnp.exp(alog_ref[g]))
        cum_r = _lane_cumsum(a_r)
        suf_r = cum_r[:, L - 1:L] - cum_r + a_r
        row_is_fwd = lax.broadcasted_iota(jnp.int32, cum_r.shape, 0) < hpg
        seg_r = jnp.where(row_is_fwd, cum_r, suf_r)
        stacked = jnp.concatenate([seg_r, dt_r, jnp.zeros((L - 4 * hpg, L), F32)], axis=0)
        cols = stacked.T
        state = s_ref[g]
        carried = _dot(cg, state.astype(BF16))
        bt = bg.astype(F32).T.astype(BF16)
        if fwd:
            cb = _dot_nt(cg, bg)
            off = 0
        else:
            off = hpg
        for p in range(pairs):
            lanes = slice(p * 2 * P, (p + 1) * 2 * P)
            xp = xg[:, lanes]
            xpf = xp.astype(F32)
            h0, h1 = off + 2 * p, off + 2 * p + 1
            seg0, seg1 = cols[:, h0:h0 + 1], cols[:, h1:h1 + 1]
            y = carried[:, lanes] * pair_lanes(jnp.exp(seg0), jnp.exp(seg1))
            if fwd:
                y = y + dskip_ref[g][:, lanes] * xpf
                for hh, keep_low in ((2 * p, True), (2 * p + 1, False)):
                    f_col = cols[:, hh:hh + 1]
                    f_row = seg_r[hh:hh + 1, :]
                    wf = jnp.exp(jnp.where(causal, f_col - f_row, NEG_INF)) * dt_r[hh:hh + 1, :]
                    b_col = cols[:, hpg + hh:hpg + hh + 1]
                    b_row = seg_r[hpg + hh:hpg + hh + 1, :]
                    wb = jnp.exp(jnp.where(anti, b_col - b_row, NEG_INF)) * dt_r[hpg + hh:hpg + hh + 1, :]
                    w = (cb * (wf + wb)).astype(BF16)
                    xm = jnp.where(low_lanes if keep_low else high_lanes, xpf, 0.0).astype(BF16)
                    y = y + _dot(w, xm)
                tot0, tot1 = cols[L - 1:L, h0:h0 + 1], cols[L - 1:L, h1:h1 + 1]
            else:
                tot0, tot1 = cols[0:1, h0:h0 + 1], cols[0:1, h1:h1 + 1]
            y_ref[g, :, lanes] = y.astype(y_ref.dtype)
            dt0 = cols[:, 2 * hpg + h0:2 * hpg + h0 + 1]
            dt1 = cols[:, 2 * hpg + h1:2 * hpg + h1 + 1]
            coef = pair_lanes(jnp.exp(tot0 - seg0) * dt0, jnp.exp(tot1 - seg1) * dt1)
            xw = (xpf * coef).astype(BF16)
            decay = jnp.where(low_lanes_row, jnp.exp(tot0), jnp.exp(tot1))
            s_ref[g, :, lanes] = state[:, lanes] * decay + _dot(bt, xw)

    @pl.when(phase == 0)
    def _():
        lax.fori_loop(0, SSM_GROUPS, lambda g, c: (group_body(g, True), c)[1], 0)

    @pl.when(phase == 1)
    def _():
        lax.fori_loop(0, SSM_GROUPS, lambda g, c: (group_body(g, False), c)[1], 0)


def ssd(xs, bc, dt_rows, dt_bias_rows, a_log_rows, d_skip_rows, *, batch, seq_len):
    groups, n, gw = xs.shape
    hpg = gw // SSM_HEAD_DIM
    L = SSM_CHUNK
    nc = seq_len // L

    def chunk(b, ph, k):
        return b * nc + jnp.where(ph == 0, k, nc - 1 - k)

    est = 2 * (groups * L * gw * 2 * 2 + 2 * groups * L * SSM_STATE * 2 + groups * 2 * hpg * L * 4) \
        + groups * SSM_STATE * gw * 4 + 64 * L * L * 4
    return pl.pallas_call(
        functools.partial(_ssd_kernel, hpg=hpg),
        grid=(batch, 2, nc),
        in_specs=[
            pl.BlockSpec((groups, L, gw), lambda b, ph, k: (0, chunk(b, ph, k), 0)),
            pl.BlockSpec((2 * groups, L, SSM_STATE), lambda b, ph, k: (0, chunk(b, ph, k), 0)),
            pl.BlockSpec((groups, None, 2 * hpg, L), lambda b, ph, k: (0, chunk(b, ph, k), 0, 0)),
            pl.BlockSpec((groups, 2 * hpg, 1), lambda b, ph, k: (0, 0, 0)),
            pl.BlockSpec((groups, 2 * hpg, 1), lambda b, ph, k: (0, 0, 0)),
            pl.BlockSpec((groups, 1, gw), lambda b, ph, k: (0, 0, 0)),
        ],
        out_specs=pl.BlockSpec((None, groups, L, gw), lambda b, ph, k: (ph, 0, chunk(b, ph, k), 0)),
        out_shape=jax.ShapeDtypeStruct((2, groups, n, gw), BF16),
        scratch_shapes=[pltpu.VMEM((groups, SSM_STATE, gw), F32)],
        compiler_params=_cparams(("parallel", "arbitrary", "arbitrary"), est),
        name="ssd",
    )(xs, bc, dt_rows, dt_bias_rows, a_log_rows, d_skip_rows)


def _mamba_out_kernel(y_ref, z_ref, gg_ref, w_ref, x_ref, o_ref, acc_ref, ssq_ref, *, d_inner):
    g = pl.program_id(1)

    @pl.when(g == 0)
    def _():
        acc_ref[...] = jnp.zeros_like(acc_ref)
        ssq_ref[...] = jnp.zeros_like(ssq_ref)

    z = z_ref[...].astype(F32)
    yz = (y_ref[0].astype(F32) + y_ref[1].astype(F32)) * (z * _sigmoid(z))
    ssq_ref[...] += jnp.sum(yz * yz, axis=-1, keepdims=True)
    acc_ref[...] += _dot((yz * gg_ref[...]).astype(BF16), w_ref[...])

    @pl.when(g == pl.num_programs(1) - 1)
    def _():
        o_ref[...] = x_ref[...] + acc_ref[...] * lax.rsqrt(ssq_ref[...] / d_inner + EPS)


def mamba_out(y, zx, g_gate, w_out, x, *, tm=512):
    _, groups, n, gw = y.shape
    d = x.shape[1]
    tm = min(tm, n)
    est = 4 * tm * gw * 2 + 2 * tm * gw * 2 + 2 * gw * d * 2 + 5 * tm * d * 4
    return pl.pallas_call(
        functools.partial(_mamba_out_kernel, d_inner=groups * gw),
        grid=(n // tm, groups),
        in_specs=[
            pl.BlockSpec((2, None, tm, gw), lambda i, g: (0, g, i, 0)),
            pl.BlockSpec((None, tm, gw), lambda i, g: (g, i, 0)),
            pl.BlockSpec((1, gw), lambda i, g: (0, g)),
            pl.BlockSpec((gw, d), lambda i, g: (g, 0)),
            pl.BlockSpec((tm, d), lambda i, g: (i, 0)),
        ],
        out_specs=pl.BlockSpec((tm, d), lambda i, g: (i, 0)),
        out_shape=jax.ShapeDtypeStruct((n, d), F32),
        scratch_shapes=[pltpu.VMEM((tm, d), F32), pltpu.VMEM((tm, 1), F32)],
        compiler_params=_cparams(("parallel", "arbitrary"), est),
        name="mamba_out",
    )(y, zx, g_gate.reshape(1, -1), w_out, x)


def _router_kernel(x_ref, g_ref, w_ref, o_ref):
    xn = _rmsnorm(x_ref[...], g_ref[...])
    logits = jnp.dot(xn, w_ref[...], preferred_element_type=F32, precision=lax.Precision.HIGHEST)
    lane = lax.broadcasted_iota(jnp.int32, logits.shape, 1)
    logits = jnp.where(lane < N_EXPERTS, logits, NEG_INF)
    v1 = jnp.max(logits, axis=-1, keepdims=True)
    i1 = jnp.min(jnp.where(logits == v1, lane, LANES), axis=-1, keepdims=True)
    rest = jnp.where(lane == i1, NEG_INF, logits)
    v2 = jnp.max(rest, axis=-1, keepdims=True)
    i2 = jnp.min(jnp.where(rest == v2, lane, LANES), axis=-1, keepdims=True)
    e2 = jnp.exp(v2 - v1)
    g1 = 1.0 / (1.0 + e2)
    g2 = e2 / (1.0 + e2)
    out = jnp.where(lane == 0, i1.astype(F32),
                    jnp.where(lane == 1, i2.astype(F32),
                              jnp.where(lane == 2, g1, jnp.where(lane == 3, g2, 0.0))))
    o_ref[...] = out


def router(x, g, w_router, *, tm=512):
    n, d = x.shape
    tm = min(tm, n)
    w = jnp.zeros((d, LANES), F32).at[:, :w_router.shape[1]].set(w_router)
    est = 2 * tm * d * 4 + 2 * d * LANES * 4 + 2 * tm * LANES * 4 + 2 * tm * d * 4
    return pl.pallas_call(
        _router_kernel,
        grid=(n // tm,),
        in_specs=[
            pl.BlockSpec((tm, d), lambda i: (i, 0)),
            pl.BlockSpec((1, d), lambda i: (0, 0)),
            pl.BlockSpec((d, LANES), lambda i: (0, 0)),
        ],
        out_specs=pl.BlockSpec((tm, LANES), lambda i: (i, 0)),
        out_shape=jax.ShapeDtypeStruct((n, LANES), F32),
        compiler_params=_cparams(("parallel",), est),
        name="router",
    )(x, g.reshape(1, d), w)


def _row_copy(src_hbm, dst_vmem, sem, src_row, dst_row):
    return pltpu.make_async_copy(src_hbm.at[pl.ds(src_row, 1), :], dst_vmem.at[pl.ds(dst_row, 1), :], sem)


def _gather_norm_kernel(tok_ref, x_hbm, g_ref, o_ref, buf_ref, sem):
    rows = buf_ref.shape[0]
    base = pl.program_id(0) * rows

    def start(r, c):
        _row_copy(x_hbm, buf_ref, sem, tok_ref[base + r], r).start()
        return c

    def wait(r, c):
        _row_copy(x_hbm, buf_ref, sem, 0, r).wait()
        return c

    lax.fori_loop(0, rows, start, 0)
    lax.fori_loop(0, rows, wait, 0)
    o_ref[...] = _rmsnorm(buf_ref[...], g_ref[...]).astype(o_ref.dtype)


def gather_norm(tok_of_slot, x, g, *, rows=GATHER_ROWS):
    slots = tok_of_slot.shape[0]
    d = x.shape[1]
    est = rows * d * 4 * 3 + 2 * rows * d * 2
    return pl.pallas_call(
        _gather_norm_kernel,
        grid_spec=pltpu.PrefetchScalarGridSpec(
            num_scalar_prefetch=1,
            grid=(slots // rows,),
            in_specs=[
                pl.BlockSpec(memory_space=pl.ANY),
                pl.BlockSpec((1, d), lambda i, tok: (0, 0)),
            ],
            out_specs=pl.BlockSpec((rows, d), lambda i, tok: (i, 0)),
            scratch_shapes=[pltpu.VMEM((rows, d), F32), pltpu.SemaphoreType.DMA(())],
        ),
        out_shape=jax.ShapeDtypeStruct((slots, d), BF16),
        compiler_params=_cparams(("arbitrary",), est),
        name="gather_norm",
    )(tok_of_slot, x, g.reshape(1, d))


def _expert_kernel(blk_e_ref, n_used_ref, x_ref, wg_ref, wu_ref, wd_ref, o_ref):
    i = pl.program_id(0)
    f = pl.program_id(1)

    @pl.when(f == 0)
    def _():
        o_ref[...] = jnp.zeros_like(o_ref)

    @pl.when(i < n_used_ref[0])
    def _():
        x = x_ref[...]
        gate = _dot(x, wg_ref[...])
        up = _dot(x, wu_ref[...])
        hid = (gate * _sigmoid(gate) * up).astype(BF16)
        o_ref[...] += _dot(hid, wd_ref[...])


def expert_ffn(blk_expert, n_used, xs, wg, wu, wd, *, tm=MOE_TM, tf=512):
    slots, d = xs.shape
    ff = wg.shape[2]
    nf = ff // tf
    n_blocks = slots // tm

    def live(i, n_used):
        return jnp.minimum(i, n_used[0] - 1)

    def f_eff(i, f, n_used):
        return jnp.where(i < n_used[0], f, nf - 1)

    est = 2 * tm * d * 2 + 6 * d * tf * 2 + 2 * tm * d * 4 + 3 * tm * tf * 4
    return pl.pallas_call(
        _expert_kernel,
        grid_spec=pltpu.PrefetchScalarGridSpec(
            num_scalar_prefetch=2,
            grid=(n_blocks, nf),
            in_specs=[
                pl.BlockSpec((tm, d), lambda i, f, be, nu: (live(i, nu), 0)),
                pl.BlockSpec((None, d, tf), lambda i, f, be, nu: (be[live(i, nu)], 0, f_eff(i, f, nu))),
                pl.BlockSpec((None, d, tf), lambda i, f, be, nu: (be[live(i, nu)], 0, f_eff(i, f, nu))),
                pl.BlockSpec((None, tf, d), lambda i, f, be, nu: (be[live(i, nu)], f_eff(i, f, nu), 0)),
            ],
            out_specs=pl.BlockSpec((tm, d), lambda i, f, be, nu: (i, 0)),
        ),
        out_shape=jax.ShapeDtypeStruct((slots, d), F32),
        compiler_params=_cparams(("arbitrary", "arbitrary"), est),
        name="expert_ffn",
    )(blk_expert, n_used, xs, wg, wu, wd)


def _combine_kernel(slot_ref, y_hbm, x_ref, r_ref, g_ref, o_ref, buf_ref, sem):
    rows = x_ref.shape[0]
    base = pl.program_id(0) * rows

    def start(r, c):
        for k in range(TOP_K):
            _row_copy(y_hbm, buf_ref.at[k], sem, slot_ref[TOP_K * (base + r) + k], r).start()
        return c

    def wait(r, c):
        for k in range(TOP_K):
            _row_copy(y_hbm, buf_ref.at[k], sem, 0, r).wait()
        return c

    lax.fori_loop(0, rows, start, 0)
    lax.fori_loop(0, rows, wait, 0)
    gates = r_ref[...]
    out = x_ref[...] + gates[:, 2:3] * buf_ref[0] + gates[:, 3:4] * buf_ref[1]
    o_ref[...] = _rmsnorm(out, g_ref[...])


def combine_norm(slot_of_assignment, ys, x, routed, g_final, *, rows=GATHER_ROWS):
    n, d = x.shape
    rows = min(rows, n)
    est = 2 * rows * d * 4 + 4 * rows * d * 4 + 4 * rows * d * 4
    return pl.pallas_call(
        _combine_kernel,
        grid_spec=pltpu.PrefetchScalarGridSpec(
            num_scalar_prefetch=1,
            grid=(n // rows,),
            in_specs=[
                pl.BlockSpec(memory_space=pl.ANY),
                pl.BlockSpec((rows, d), lambda i, s: (i, 0)),
                pl.BlockSpec((rows, LANES), lambda i, s: (i, 0)),
                pl.BlockSpec((1, d), lambda i, s: (0, 0)),
            ],
            out_specs=pl.BlockSpec((rows, d), lambda i, s: (i, 0)),
            scratch_shapes=[pltpu.VMEM((TOP_K, rows, d), F32), pltpu.SemaphoreType.DMA(())],
        ),
        out_shape=jax.ShapeDtypeStruct((n, d), F32),
        compiler_params=_cparams(("arbitrary",), est),
        name="combine_norm",
    )(slot_of_assignment, ys, x, routed, g_final.reshape(1, d))


def moe_plan(routed, *, tm):
    n = routed.shape[0]
    experts = routed[:, :TOP_K].astype(jnp.int32).reshape(-1)
    onehot = (experts[:, None] == jnp.arange(N_EXPERTS)[None, :]).astype(jnp.int32)
    rank = jnp.sum((jnp.cumsum(onehot, axis=0) - onehot) * onehot, axis=1)
    counts = jnp.sum(onehot, axis=0)
    padded = ((counts + tm - 1) // tm) * tm
    ends = jnp.cumsum(padded)
    starts = ends - padded
    slot = (starts[experts] + rank).astype(jnp.int32)
    n_blocks = (n * TOP_K) // tm + N_EXPERTS
    tok = jnp.repeat(jnp.arange(n, dtype=jnp.int32), TOP_K)
    tok_of_slot = jnp.zeros((n_blocks * tm,), jnp.int32).at[slot].set(tok)
    blk_expert = jnp.minimum(
        jnp.searchsorted(ends, jnp.arange(n_blocks, dtype=jnp.int32) * tm, side="right"), N_EXPERTS - 1
    ).astype(jnp.int32)
    n_used = (ends[-1] // tm).astype(jnp.int32).reshape(1)
    return slot, tok_of_slot, blk_expert, n_used


def _prepare_weights(w_qkv, w_o, w_ff_gate, w_ff_up, w_ff_down, w_in_c, w_out_c, w_e_gate, w_e_up, w_e_down):
    d_inner = w_out_c.shape[1]
    main_cols = 2 * d_inner + 2 * SSM_GROUPS * SSM_STATE
    return dict(
        w_qkv=w_qkv[0].astype(BF16), w_o=w_o[0].astype(BF16),
        w_ff_gate=w_ff_gate[0].astype(BF16), w_ff_up=w_ff_up[0].astype(BF16), w_ff_down=w_ff_down[0].astype(BF16),
        w_in_main=w_in_c[0][:, :main_cols].astype(BF16), w_in_dt=w_in_c[0][:, main_cols:].astype(BF16),
        w_out=w_out_c[0].astype(BF16),
        w_e_gate=w_e_gate[0].astype(BF16), w_e_up=w_e_up[0].astype(BF16), w_e_down=w_e_down[0].astype(BF16),
    )


def _trunk(x3, wb, g_mix, g_ffn, rpb, conv_w, conv_b, dt_bias, a_log, d_skip, g_gate, w_router, g_final):
    batch, seq_len, d = x3.shape
    n = batch * seq_len
    x = x3.reshape(n, d)

    cos, sin = rope_tables(seq_len)
    qkv = qkv_proj(x, g_mix[0], wb["w_qkv"], cos, sin, seq_len=seq_len)
    bias_tiles = natten_bias_tiles(rpb[0], seq_len // GRID_W)
    o_a = natten(qkv, bias_tiles, batch=batch, seq_len=seq_len)
    o_b = dilated_attention(qkv, batch=batch, seq_len=seq_len)
    x = attn_out(x, o_a, o_b, wb["w_o"])
    x = ffn(x, g_ffn[0], wb["w_ff_gate"], wb["w_ff_up"], wb["w_ff_down"])

    d_inner = wb["w_out"].shape[0]
    gw = d_inner // SSM_GROUPS
    hpg = gw // SSM_HEAD_DIM
    heads = SSM_GROUPS * hpg
    zx = norm_matmul(x, g_mix[1], wb["w_in_main"], tn=gw, out_dtype=BF16)
    dt_raw = norm_matmul(x, g_mix[1], wb["w_in_dt"], tn=2 * heads, out_dtype=F32)[0]
    z_tiles = d_inner // gw
    xs = conv_silu(zx, conv_w[0], conv_b[0], width=gw, first_tile=z_tiles, n_tiles=SSM_GROUPS,
                   col_offset=0, batch=batch, seq_len=seq_len)
    bc = conv_silu(zx, conv_w[0], conv_b[0], width=SSM_STATE, first_tile=2 * z_tiles, n_tiles=2 * SSM_GROUPS,
                   col_offset=d_inner, batch=batch, seq_len=seq_len)
    L = SSM_CHUNK
    dt_rows = dt_raw.reshape(n // L, L, 2, SSM_GROUPS, hpg).transpose(3, 0, 2, 4, 1).reshape(SSM_GROUPS, n // L, 2 * hpg, L)

    def per_group_rows(p):
        return p.reshape(2, SSM_GROUPS, hpg).transpose(1, 0, 2).reshape(SSM_GROUPS, 2 * hpg, 1)

    d_skip_rows = jnp.repeat(d_skip[0].reshape(SSM_GROUPS, 1, hpg), SSM_HEAD_DIM, axis=2)
    y = ssd(xs, bc, dt_rows, per_group_rows(dt_bias[0]), per_group_rows(a_log[0]), d_skip_rows,
            batch=batch, seq_len=seq_len)
    x = mamba_out(y, zx, g_gate[0], wb["w_out"], x)

    routed = router(x, g_ffn[1], w_router[0])
    slot, tok_of_slot, blk_expert, n_used = moe_plan(routed, tm=MOE_TM)
    xs_sorted = gather_norm(tok_of_slot, x, g_ffn[1])
    ys = expert_ffn(blk_expert, n_used, xs_sorted, wb["w_e_gate"], wb["w_e_up"], wb["w_e_down"])
    out = combine_norm(slot, ys, x, routed, g_final)
    return out.reshape(batch, seq_len, d)


def kernel(x_prompt, x_sample, g_mix, g_ffn, w_qkv, rpb, w_o, w_ff_gate, w_ff_up, w_ff_down, w_in_c, conv_w, conv_b,
           dt_bias, a_log, d_skip, g_gate, w_out_c, w_router, w_e_gate, w_e_up, w_e_down, g_final):
    wb = _prepare_weights(w_qkv, w_o, w_ff_gate, w_ff_up, w_ff_down, w_in_c, w_out_c, w_e_gate, w_e_up, w_e_down)
    args = (wb, g_mix, g_ffn, rpb, conv_w, conv_b, dt_bias, a_log, d_skip, g_gate, w_router, g_final)
    return _trunk(x_prompt, *args), _trunk(x_sample, *args)
```

```python
import functools
import math

import jax
import jax.numpy as jnp
import numpy as np
from jax import lax
from jax.experimental import pallas as pl
from jax.experimental.pallas import tpu as pltpu

F32 = jnp.float32
BF16 = jnp.bfloat16
EPS = 1e-6
NEG_INF = float("-inf")

GRID_W = 64
HEAD_DIM = 128
N_HEADS_A = 4
N_HEADS_B_GROUP = 4
DILATIONS = (1, 4, 16)
BAND_RADIUS = 64
N_HEADS_QKV = N_HEADS_A + N_HEADS_B_GROUP * len(DILATIONS)
WIN_H = 8
WIN_W = 16
ROPE_THETA = 10000.0
SSM_HEAD_DIM = 64
SSM_GROUPS = 8
SSM_STATE = 128
SSM_CONV = 5
SSM_CHUNK = 128
N_EXPERTS = 8
TOP_K = 2

V7X_VMEM_BYTES = 64 * 1024 * 1024
LANES = 128
BF16_SUBLANES = 16

NAT_ROWS = 8
NAT_KROWS = NAT_ROWS + WIN_H - 1
DIL_TQ = 256
DIL_KC = 256
MOE_TM = 512
GATHER_ROWS = 256


def _cparams(semantics, vmem_estimate):
    limit = int(min(max(2 * vmem_estimate, 32 * 1024 * 1024), V7X_VMEM_BYTES - 8 * 1024 * 1024))
    return pltpu.CompilerParams(dimension_semantics=semantics, vmem_limit_bytes=limit)


def _rmsnorm(x, g):
    return x * lax.rsqrt(jnp.mean(x * x, axis=-1, keepdims=True) + EPS) * g


def _sigmoid(x):
    return 1.0 / (1.0 + jnp.exp(-x))


def _softplus(x):
    return jnp.maximum(x, 0.0) + jnp.log(1.0 + jnp.exp(-jnp.abs(x)))


def _dot(a, b):
    return jnp.dot(a, b, preferred_element_type=F32)


def _dot_nt(a, b):
    return lax.dot_general(a, b, (((1,), (1,)), ((), ())), preferred_element_type=F32)


def _norm_matmul_kernel(x_ref, g_ref, w_ref, o_ref, xn_ref):
    @pl.when(pl.program_id(1) == 0)
    def _():
        xn_ref[...] = _rmsnorm(x_ref[...], g_ref[...]).astype(BF16)

    o_ref[...] = _dot(xn_ref[...], w_ref[...]).astype(o_ref.dtype)


def norm_matmul(x, g, w, *, tn, out_dtype, tm=1024):
    n, k = x.shape
    m = w.shape[1]
    tm = min(tm, n)
    est = 2 * tm * k * 4 + tm * k * 2 + 2 * k * tn * 2 + 2 * tm * tn * 4
    return pl.pallas_call(
        _norm_matmul_kernel,
        grid=(n // tm, m // tn),
        in_specs=[
            pl.BlockSpec((tm, k), lambda i, j: (i, 0)),
            pl.BlockSpec((1, k), lambda i, j: (0, 0)),
            pl.BlockSpec((k, tn), lambda i, j: (0, j)),
        ],
        out_specs=pl.BlockSpec((None, tm, tn), lambda i, j: (j, i, 0)),
        out_shape=jax.ShapeDtypeStruct((m // tn, n, tn), out_dtype),
        scratch_shapes=[pltpu.VMEM((tm, k), BF16)],
        compiler_params=_cparams(("parallel", "arbitrary"), est),
        name="norm_matmul",
    )(x, g.reshape(1, k), w)


def _qkv_kernel(x_ref, g_ref, w_ref, cos_ref, sin_ref, o_ref, xn_ref, *, heads_per_tile, scale):
    j = pl.program_id(1)
    tiles_per_part = N_HEADS_QKV // heads_per_tile

    @pl.when(j == 0)
    def _():
        xn_ref[...] = _rmsnorm(x_ref[...], g_ref[...]).astype(BF16)

    r = _dot(xn_ref[...], w_ref[...])
    r = r * jnp.where(j < tiles_per_part, scale, 1.0)
    use_rope = jnp.logical_and(j < 2 * tiles_per_part, j % tiles_per_part != 0)

    @pl.when(use_rope)
    def _():
        c = cos_ref[...]
        s = sin_ref[...]
        for h in range(heads_per_tile):
            p = r[:, h * HEAD_DIM:(h + 1) * HEAD_DIM]
            o_ref[h] = (p * c + pltpu.roll(p, HEAD_DIM // 2, 1) * s).astype(o_ref.dtype)

    @pl.when(jnp.logical_not(use_rope))
    def _():
        for h in range(heads_per_tile):
            o_ref[h] = r[:, h * HEAD_DIM:(h + 1) * HEAD_DIM].astype(o_ref.dtype)


def qkv_proj(x, g, w, cos, sin, *, seq_len, tm=1024):
    n, k = x.shape
    m = w.shape[1]
    hpt = N_HEADS_A
    tn = hpt * HEAD_DIM
    tm = min(tm, seq_len)
    tiles_per_seq = seq_len // tm
    est = 2 * tm * k * 4 + tm * k * 2 + 2 * k * tn * 2 + 2 * tm * tn * 2 + 4 * tm * HEAD_DIM * 4 + tm * tn * 4
    return pl.pallas_call(
        functools.partial(_qkv_kernel, heads_per_tile=hpt, scale=HEAD_DIM ** -0.5),
        grid=(n // tm, m // tn),
        in_specs=[
            pl.BlockSpec((tm, k), lambda i, j: (i, 0)),
            pl.BlockSpec((1, k), lambda i, j: (0, 0)),
            pl.BlockSpec((k, tn), lambda i, j: (0, j)),
            pl.BlockSpec((tm, HEAD_DIM), lambda i, j: (i % tiles_per_seq, 0)),
            pl.BlockSpec((tm, HEAD_DIM), lambda i, j: (i % tiles_per_seq, 0)),
        ],
        out_specs=pl.BlockSpec((hpt, tm, HEAD_DIM), lambda i, j: (j, i, 0)),
        out_shape=jax.ShapeDtypeStruct((m // HEAD_DIM, n, HEAD_DIM), BF16),
        scratch_shapes=[pltpu.VMEM((tm, k), BF16)],
        compiler_params=_cparams(("parallel", "arbitrary"), est),
        name="qkv_proj",
    )(x, g.reshape(1, k), w, cos, sin)


def rope_tables(seq_len):
    half = HEAD_DIM // 2
    inv = ROPE_THETA ** (-jnp.arange(half, dtype=F32) / half)
    ang = jnp.arange(seq_len, dtype=F32)[:, None] * inv[None, :]
    cos = jnp.cos(ang)
    sin = jnp.sin(ang)
    return jnp.concatenate([cos, cos], axis=1), jnp.concatenate([-sin, sin], axis=1)


def _natten_kernel(q_ref, k_ref, v_ref, bias_ref, o_ref, *, n_blocks, grid_rows):
    blk = pl.program_id(2)
    first_row = jnp.where(blk == 0, 0,
                          jnp.where(blk == n_blocks - 1, grid_rows - NAT_KROWS, blk * NAT_ROWS - WIN_H // 2))
    start = pl.multiple_of(first_row * GRID_W, GRID_W)
    kw = k_ref[pl.ds(start, NAT_KROWS * GRID_W), :]
    vw = v_ref[pl.ds(start, NAT_KROWS * GRID_W), :]
    s = _dot_nt(q_ref[...], kw) + bias_ref[...]
    m = jnp.max(s, axis=-1, keepdims=True)
    p = jnp.exp(s - m)
    l = jnp.sum(p, axis=-1, keepdims=True)
    o_ref[...] = (_dot(p.astype(BF16), vw) / l).astype(o_ref.dtype)


def natten_bias_tiles(rpb, grid_rows):
    n_blocks = grid_rows // NAT_ROWS
    r0s = np.array([0, NAT_ROWS, (n_blocks - 1) * NAT_ROWS])
    k0s = np.array([0, NAT_ROWS - WIN_H // 2, grid_rows - NAT_KROWS])
    r = r0s[:, None] + np.arange(NAT_ROWS)[None, :]
    kr = k0s[:, None] + np.arange(NAT_KROWS)[None, :]
    rs = np.clip(r - WIN_H // 2, 0, grid_rows - WIN_H)
    row_ok = (kr[:, None, :] >= rs[:, :, None]) & (kr[:, None, :] < rs[:, :, None] + WIN_H)
    d_row = np.clip(kr[:, None, :] - r[:, :, None] + (WIN_H - 1), 0, 2 * WIN_H - 2)
    c = np.arange(GRID_W)
    cs = np.clip(c - WIN_W // 2, 0, GRID_W - WIN_W)
    col_ok = (c[None, :] >= cs[:, None]) & (c[None, :] < cs[:, None] + WIN_W)
    d_col = np.clip(c[None, :] - c[:, None] + (WIN_W - 1), 0, 2 * WIN_W - 2)
    sel_row = (d_row[..., None] == np.arange(2 * WIN_H - 1)).astype(np.float32)
    sel_col = (d_col[..., None] == np.arange(2 * WIN_W - 1)).astype(np.float32)
    rows = jnp.einsum("tikr,hrc->thikc", sel_row, rpb.astype(F32), precision=lax.Precision.HIGHEST)
    bias = jnp.einsum("thikc,qwc->thiqkw", rows, sel_col, precision=lax.Precision.HIGHEST)
    ok = row_ok[:, :, None, :, None] & col_ok[None, None, :, None, :]
    bias = jnp.where(ok[:, None], bias, NEG_INF)
    return bias.reshape(3, rpb.shape[0], NAT_ROWS * GRID_W, NAT_KROWS * GRID_W)


def natten(qkv, bias_tiles, *, batch, seq_len):
    n = batch * seq_len
    grid_rows = seq_len // GRID_W
    n_blocks = grid_rows // NAT_ROWS
    tq = NAT_ROWS * GRID_W
    tk = NAT_KROWS * GRID_W

    def tile_kind(blk):
        return jnp.where(blk == 0, 0, jnp.where(blk == n_blocks - 1, 2, 1))

    est = 4 * seq_len * HEAD_DIM * 2 + 2 * tq * tk * 4 + 3 * tq * tk * 4
    return pl.pallas_call(
        functools.partial(_natten_kernel, n_blocks=n_blocks, grid_rows=grid_rows),
        grid=(batch, N_HEADS_A, n_blocks),
        in_specs=[
            pl.BlockSpec((None, tq, HEAD_DIM), lambda b, h, i: (h, b * n_blocks + i, 0)),
            pl.BlockSpec((None, seq_len, HEAD_DIM), lambda b, h, i: (N_HEADS_QKV + h, b, 0)),
            pl.BlockSpec((None, seq_len, HEAD_DIM), lambda b, h, i: (2 * N_HEADS_QKV + h, b, 0)),
            pl.BlockSpec((None, None, tq, tk), lambda b, h, i: (tile_kind(i), h, 0, 0)),
        ],
        out_specs=pl.BlockSpec((tq, HEAD_DIM), lambda b, h, i: (b * n_blocks + i, h)),
        out_shape=jax.ShapeDtypeStruct((n, N_HEADS_A * HEAD_DIM), BF16),
        compiler_params=_cparams(("parallel", "parallel", "arbitrary"), est),
        name="natten",
    )(qkv, qkv, qkv, bias_tiles)


def _dilated_kernel(q0, q1, q2, k0, k1, k2, v0, v1, v2, o_ref, *, seq_len):
    tq, kc = DIL_TQ, DIL_KC
    t0 = pl.program_id(2) * tq
    rel0 = lax.broadcasted_iota(jnp.int32, (tq, kc), 1) - lax.broadcasted_iota(jnp.int32, (tq, kc), 0)
    m = jnp.full((tq, 1), -1e30, F32)
    l = jnp.zeros((tq, 1), F32)
    acc = jnp.zeros((tq, HEAD_DIM), F32)
    for q_ref, k_ref, v_ref, dil in ((q0, k0, v0, DILATIONS[0]), (q1, k1, v1, DILATIONS[1]), (q2, k2, v2, DILATIONS[2])):
        reach = BAND_RADIUS * dil
        halo = -(-reach // kc) * kc
        n_chunks = (tq + 2 * halo) // kc
        window_start = t0 - halo
        lo = jnp.maximum(0, (halo - t0) // kc)
        hi = jnp.minimum(n_chunks, (seq_len - window_start) // kc)
        q = q_ref[...]

        def body(ci, carry, k_ref=k_ref, v_ref=v_ref, dil=dil, reach=reach, halo=halo, window_start=window_start, q=q):
            m, l, acc = carry
            ks = pl.multiple_of(window_start + ci * kc, kc)
            kk = k_ref[pl.ds(ks, kc), :]
            vv = v_ref[pl.ds(ks, kc), :]
            s = _dot_nt(q, kk)
            rel = rel0 + (ci * kc - halo)
            ok = jnp.logical_and(jnp.abs(rel) <= reach, (rel & (dil - 1)) == 0)
            s = jnp.where(ok, s, NEG_INF)
            m_new = jnp.maximum(m, jnp.max(s, axis=-1, keepdims=True))
            alpha = jnp.exp(m - m_new)
            p = jnp.exp(s - m_new)
            l = alpha * l + jnp.sum(p, axis=-1, keepdims=True)
            acc = alpha * acc + _dot(p.astype(BF16), vv)
            return m_new, l, acc

        m, l, acc = lax.fori_loop(lo, hi, body, (m, l, acc))
    o_ref[...] = (acc / l).astype(o_ref.dtype)


def dilated_attention(qkv, *, batch, seq_len):
    n = batch * seq_len
    nq = seq_len // DIL_TQ

    def q_spec(g):
        return pl.BlockSpec((None, DIL_TQ, HEAD_DIM),
                            lambda b, j, i: (N_HEADS_A + N_HEADS_B_GROUP * g + j, b * nq + i, 0))

    def kv_spec(part, g):
        return pl.BlockSpec((None, seq_len, HEAD_DIM),
                            lambda b, j, i: (part * N_HEADS_QKV + N_HEADS_A + N_HEADS_B_GROUP * g + j, b, 0))

    est = 12 * seq_len * HEAD_DIM * 2 + 8 * DIL_TQ * DIL_KC * 4
    groups = range(len(DILATIONS))
    return pl.pallas_call(
        functools.partial(_dilated_kernel, seq_len=seq_len),
        grid=(batch, N_HEADS_B_GROUP, nq),
        in_specs=[q_spec(g) for g in groups] + [kv_spec(1, g) for g in groups] + [kv_spec(2, g) for g in groups],
        out_specs=pl.BlockSpec((DIL_TQ, HEAD_DIM), lambda b, j, i: (b * nq + i, j)),
        out_shape=jax.ShapeDtypeStruct((n, N_HEADS_B_GROUP * HEAD_DIM), BF16),
        compiler_params=_cparams(("parallel", "parallel", "arbitrary"), est),
        name="dilated_attention",
    )(*([qkv] * 9))


def _attn_out_kernel(x_ref, oa_ref, ob_ref, w_ref, o_ref):
    ka = oa_ref.shape[1]
    o_ref[...] = x_ref[...] + _dot(oa_ref[...], w_ref[:ka, :]) + _dot(ob_ref[...], w_ref[ka:, :])


def attn_out(x, o_a, o_b, w, *, tm=512):
    n, d = x.shape
    ka, kb = o_a.shape[1], o_b.shape[1]
    tm = min(tm, n)
    est = 4 * tm * d * 4 + 2 * (ka + kb) * d * 2 + 2 * tm * (ka + kb) * 2
    return pl.pallas_call(
        _attn_out_kernel,
        grid=(n // tm,),
        in_specs=[
            pl.BlockSpec((tm, d), lambda i: (i, 0)),
            pl.BlockSpec((tm, ka), lambda i: (i, 0)),
            pl.BlockSpec((tm, kb), lambda i: (i, 0)),
            pl.BlockSpec((ka + kb, d), lambda i: (0, 0)),
        ],
        out_specs=pl.BlockSpec((tm, d), lambda i: (i, 0)),
        out_shape=jax.ShapeDtypeStruct((n, d), F32),
        compiler_params=_cparams(("parallel",), est),
        name="attn_out",
    )(x, o_a, o_b, w)


def _ffn_kernel(x_ref, g_ref, wg_ref, wu_ref, wd_ref, o_ref, xn_ref):
    f = pl.program_id(1)

    @pl.when(f == 0)
    def _():
        xn_ref[...] = _rmsnorm(x_ref[...], g_ref[...]).astype(BF16)
        o_ref[...] = x_ref[...]

    xn = xn_ref[...]
    gate = _dot(xn, wg_ref[...])
    up = _dot(xn, wu_ref[...])
    hid = (gate * _sigmoid(gate) * up).astype(BF16)
    o_ref[...] += _dot(hid, wd_ref[...])


def ffn(x, g, wg, wu, wd, *, tm=512, tf=512):
    n, d = x.shape
    ff = wg.shape[1]
    tm = min(tm, n)
    est = 4 * tm * d * 4 + tm * d * 2 + 6 * d * tf * 2 + 3 * tm * tf * 4
    return pl.pallas_call(
        _ffn_kernel,
        grid=(n // tm, ff // tf),
        in_specs=[
            pl.BlockSpec((tm, d), lambda i, f: (i, 0)),
            pl.BlockSpec((1, d), lambda i, f: (0, 0)),
            pl.BlockSpec((d, tf), lambda i, f: (0, f)),
            pl.BlockSpec((d, tf), lambda i, f: (0, f)),
            pl.BlockSpec((tf, d), lambda i, f: (f, 0)),
        ],
        out_specs=pl.BlockSpec((tm, d), lambda i, f: (i, 0)),
        out_shape=jax.ShapeDtypeStruct((n, d), F32),
        scratch_shapes=[pltpu.VMEM((tm, d), BF16)],
        compiler_params=_cparams(("parallel", "arbitrary"), est),
        name="ffn",
    )(x, g.reshape(1, d), wg, wu, wd)


def _conv_kernel(xm_ref, xp_ref, xn_ref, w_ref, b_ref, o_ref, ext_ref, *, tr, n_row_blocks):
    i = pl.program_id(2)
    hb = BF16_SUBLANES
    pad = SSM_CONV // 2
    ext_ref[0:hb, :] = jnp.where(i > 0, xp_ref[...].astype(F32), 0.0)
    ext_ref[hb:hb + tr, :] = xm_ref[...].astype(F32)
    ext_ref[hb + tr:2 * hb + tr, :] = jnp.where(i < n_row_blocks - 1, xn_ref[...].astype(F32), 0.0)
    acc = jnp.broadcast_to(b_ref[...], o_ref.shape)
    for k in range(SSM_CONV):
        acc = acc + ext_ref[pl.ds(hb - pad + k, tr), :] * w_ref[k:k + 1, :]
    o_ref[...] = (acc * _sigmoid(acc)).astype(o_ref.dtype)


def conv_silu(zx, conv_w, conv_b, *, width, first_tile, n_tiles, col_offset, batch, seq_len, tr=512):
    n = batch * seq_len
    tile_w = zx.shape[2]
    per = tile_w // width
    tr = min(tr, seq_len)
    nr = seq_len // tr
    hb = BF16_SUBLANES
    seq_hb = seq_len // hb
    n_hb = n // hb

    def main_map(c, b, i):
        return (first_tile + c // per, b * nr + i, c % per)

    def prev_map(c, b, i):
        return (first_tile + c // per, jnp.maximum(b * seq_hb + i * (tr // hb) - 1, 0), c % per)

    def next_map(c, b, i):
        return (first_tile + c // per, jnp.minimum(b * seq_hb + (i + 1) * (tr // hb), n_hb - 1), c % per)

    est = 4 * tr * width * 2 + (tr + 2 * hb) * width * 4 + 4 * tr * width * 4
    return pl.pallas_call(
        functools.partial(_conv_kernel, tr=tr, n_row_blocks=nr),
        grid=(n_tiles, batch, nr),
        in_specs=[
            pl.BlockSpec((None, tr, width), main_map),
            pl.BlockSpec((None, hb, width), prev_map),
            pl.BlockSpec((None, hb, width), next_map),
            pl.BlockSpec((SSM_CONV, width), lambda c, b, i: (0, col_offset // width + c)),
            pl.BlockSpec((1, width), lambda c, b, i: (0, col_offset // width + c)),
        ],
        out_specs=pl.BlockSpec((None, tr, width), lambda c, b, i: (c, b * nr + i, 0)),
        out_shape=jax.ShapeDtypeStruct((n_tiles, n, width), BF16),
        scratch_shapes=[pltpu.VMEM((tr + 2 * hb, width), F32)],
        compiler_params=_cparams(("parallel", "parallel", "arbitrary"), est),
        name="conv_silu",
    )(zx, zx, zx, conv_w, conv_b.reshape(1, -1))


def _lane_cumsum(a):
    lane = lax.broadcasted_iota(jnp.int32, a.shape, 1)
    shift = 1
    while shift < a.shape[1]:
        a = a + jnp.where(lane >= shift, pltpu.roll(a, shift, 1), 0.0)
        shift *= 2
    return a


def _ssd_kernel(x_ref, bc_ref, dt_ref, dtb_ref, alog_ref, dskip_ref, y_ref, s_ref, *, hpg):
    L = SSM_CHUNK
    P = SSM_HEAD_DIM
    phase = pl.program_id(1)
    pairs = hpg // 2

    @pl.when(pl.program_id(2) == 0)
    def _():
        s_ref[...] = jnp.zeros_like(s_ref)

    li = lax.broadcasted_iota(jnp.int32, (L, L), 0)
    si = lax.broadcasted_iota(jnp.int32, (L, L), 1)
    causal = li >= si
    anti = li <= si
    low_lanes = lax.broadcasted_iota(jnp.int32, (L, 2 * P), 1) < P
    high_lanes = jnp.logical_not(low_lanes)
    low_lanes_row = lax.broadcasted_iota(jnp.int32, (1, 2 * P), 1) < P

    def pair_lanes(col0, col1):
        return jnp.where(low_lanes, jnp.broadcast_to(col0, (L, 2 * P)), jnp.broadcast_to(col1, (L, 2 * P)))

    def group_body(g, fwd):
        xg = x_ref[g]
        bg = bc_ref[g]
        cg = bc_ref[SSM_GROUPS + g]
        dt_r = _softplus(dt_ref[g] + dtb_ref[g])
        a_r = dt_r * (-jnp.exp(alog_ref[g]))
        cum_r = _lane_cumsum(a_r)
        suf_r = cum_r[:, L - 1:L] - cum_r + a_r
        row_is_fwd = lax.broadcasted_iota(jnp.int32, cum_r.shape, 0) < hpg
        seg_r = jnp.where(row_is_fwd, cum_r, suf_r)
        stacked = jnp.concatenate([seg_r, dt_r, jnp.zeros((L - 4 * hpg, L), F32)], axis=0)
        cols = stacked.T
        state = s_ref[g]
        carried = _dot(cg, state.astype(BF16))
        bt = bg.astype(F32).T.astype(BF16)
        if fwd:
            cb = _dot_nt(cg, bg)
            off = 0
        else:
            off = hpg
        for p in range(pairs):
            lanes = slice(p * 2 * P, (p + 1) * 2 * P)
            xp = xg[:, lanes]
            xpf = xp.astype(F32)
            h0, h1 = off + 2 * p, off + 2 * p + 1
            seg0, seg1 = cols[:, h0:h0 + 1], cols[:, h1:h1 + 1]
            y = carried[:, lanes] * pair_lanes(jnp.exp(seg0), jnp.exp(seg1))
            if fwd:
                y = y + dskip_ref[g][:, lanes] * xpf
                for hh, keep_low in ((2 * p, True), (2 * p + 1, False)):
                    f_col = cols[:, hh:hh + 1]
                    f_row = seg_r[hh:hh + 1, :]
                    wf = jnp.exp(jnp.where(causal, f_col - f_row, NEG_INF)) * dt_r[hh:hh + 1, :]
                    b_col = cols[:, hpg + hh:hpg + hh + 1]
                    b_row = seg_r[hpg + hh:hpg + hh + 1, :]
                    wb = jnp.exp(jnp.where(anti, b_col - b_row, NEG_INF)) * dt_r[hpg + hh:hpg + hh + 1, :]
                    w = (cb * (wf + wb)).astype(BF16)
                    xm = jnp.where(low_lanes if keep_low else high_lanes, xpf, 0.0).astype(BF16)
                    y = y + _dot(w, xm)
                tot0, tot1 = cols[L - 1:L, h0:h0 + 1], cols[L - 1:L, h1:h1 + 1]
            else:
                tot0, tot1 = cols[0:1, h0:h0 + 1], cols[0:1, h1:h1 + 1]
            y_ref[g, :, lanes] = y.astype(y_ref.dtype)
            dt0 = cols[:, 2 * hpg + h0:2 * hpg + h0 + 1]
            dt1 = cols[:, 2 * hpg + h1:2 * hpg + h1 + 1]
            coef = pair_lanes(jnp.exp(tot0 - seg0) * dt0, jnp.exp(tot1 - seg1) * dt1)
            xw = (xpf * coef).astype(BF16)
            decay = jnp.where(low_lanes_row, jnp.exp(tot0), jnp.exp(tot1))
            s_ref[g, :, lanes] = state[:, lanes] * decay + _dot(bt, xw)

    @pl.when(phase == 0)
    def _():
        lax.fori_loop(0, SSM_GROUPS, lambda g, c: (group_body(g, True), c)[1], 0, unroll=2)

    @pl.when(phase == 1)
    def _():
        lax.fori_loop(0, SSM_GROUPS, lambda g, c: (group_body(g, False), c)[1], 0, unroll=2)


def ssd(xs, bc, dt_rows, dt_bias_rows, a_log_rows, d_skip_rows, *, batch, seq_len):
    groups, n, gw = xs.shape
    hpg = gw // SSM_HEAD_DIM
    L = SSM_CHUNK
    nc = seq_len // L

    def chunk(b, ph, k):
        return b * nc + jnp.where(ph == 0, k, nc - 1 - k)

    est = 2 * (groups * L * gw * 2 * 2 + 2 * groups * L * SSM_STATE * 2 + groups * 2 * hpg * L * 4) \
        + groups * SSM_STATE * gw * 4 + 64 * L * L * 4
    return pl.pallas_call(
        functools.partial(_ssd_kernel, hpg=hpg),
        grid=(batch, 2, nc),
        in_specs=[
            pl.BlockSpec((groups, L, gw), lambda b, ph, k: (0, chunk(b, ph, k), 0)),
            pl.BlockSpec((2 * groups, L, SSM_STATE), lambda b, ph, k: (0, chunk(b, ph, k), 0)),
            pl.BlockSpec((groups, None, 2 * hpg, L), lambda b, ph, k: (0, chunk(b, ph, k), 0, 0)),
            pl.BlockSpec((groups, 2 * hpg, 1), lambda b, ph, k: (0, 0, 0)),
            pl.BlockSpec((groups, 2 * hpg, 1), lambda b, ph, k: (0, 0, 0)),
            pl.BlockSpec((groups, 1, gw), lambda b, ph, k: (0, 0, 0)),
        ],
        out_specs=pl.BlockSpec((None, groups, L, gw), lambda b, ph, k: (ph, 0, chunk(b, ph, k), 0)),
        out_shape=jax.ShapeDtypeStruct((2, groups, n, gw), BF16),
        scratch_shapes=[pltpu.VMEM((groups, SSM_STATE, gw), F32)],
        compiler_params=_cparams(("parallel", "arbitrary", "arbitrary"), est),
        name="ssd",
    )(xs, bc, dt_rows, dt_bias_rows, a_log_rows, d_skip_rows)


def _mamba_out_kernel(y_ref, z_ref, gg_ref, w_ref, x_ref, o_ref, acc_ref, ssq_ref, *, d_inner):
    step = pl.program_id(1)
    groups_per_step, _, gw = z_ref.shape

    @pl.when(step == 0)
    def _():
        acc_ref[...] = jnp.zeros_like(acc_ref)
        ssq_ref[...] = jnp.zeros_like(ssq_ref)

    for k in range(groups_per_step):
        z = z_ref[k].astype(F32)
        yz = (y_ref[0, k].astype(F32) + y_ref[1, k].astype(F32)) * (z * _sigmoid(z))
        ssq_ref[...] += jnp.sum(yz * yz, axis=-1, keepdims=True)
        cols = slice(k * gw, (k + 1) * gw)
        acc_ref[...] += _dot((yz * gg_ref[:, cols]).astype(BF16), w_ref[cols, :])

    @pl.when(step == pl.num_programs(1) - 1)
    def _():
        o_ref[...] = x_ref[...] + acc_ref[...] * lax.rsqrt(ssq_ref[...] / d_inner + EPS)


def mamba_out(y, zx, g_gate, w_out, x, *, tm=512, groups_per_step=2):
    _, groups, n, gw = y.shape
    d = x.shape[1]
    tm = min(tm, n)
    gps = groups_per_step
    est = gps * (4 * tm * gw * 2 + 2 * tm * gw * 2 + 2 * gw * d * 2) + 5 * tm * d * 4
    return pl.pallas_call(
        functools.partial(_mamba_out_kernel, d_inner=groups * gw),
        grid=(n // tm, groups // gps),
        in_specs=[
            pl.BlockSpec((2, gps, tm, gw), lambda i, s: (0, s, i, 0)),
            pl.BlockSpec((gps, tm, gw), lambda i, s: (s, i, 0)),
            pl.BlockSpec((1, gps * gw), lambda i, s: (0, s)),
            pl.BlockSpec((gps * gw, d), lambda i, s: (s, 0)),
            pl.BlockSpec((tm, d), lambda i, s: (i, 0)),
        ],
        out_specs=pl.BlockSpec((tm, d), lambda i, s: (i, 0)),
        out_shape=jax.ShapeDtypeStruct((n, d), F32),
        scratch_shapes=[pltpu.VMEM((tm, d), F32), pltpu.VMEM((tm, 1), F32)],
        compiler_params=_cparams(("parallel", "arbitrary"), est),
        name="mamba_out",
    )(y, zx, g_gate.reshape(1, -1), w_out, x)


def _router_kernel(x_ref, g_ref, w_ref, o_ref):
    xn = _rmsnorm(x_ref[...], g_ref[...])
    logits = jnp.dot(xn, w_ref[...], preferred_element_type=F32, precision=lax.Precision.HIGHEST)
    lane = lax.broadcasted_iota(jnp.int32, logits.shape, 1)
    logits = jnp.where(lane < N_EXPERTS, logits, NEG_INF)
    v1 = jnp.max(logits, axis=-1, keepdims=True)
    i1 = jnp.min(jnp.where(logits == v1, lane, LANES), axis=-1, keepdims=True)
    rest = jnp.where(lane == i1, NEG_INF, logits)
    v2 = jnp.max(rest, axis=-1, keepdims=True)
    i2 = jnp.min(jnp.where(rest == v2, lane, LANES), axis=-1, keepdims=True)
    e2 = jnp.exp(v2 - v1)
    g1 = 1.0 / (1.0 + e2)
    g2 = e2 / (1.0 + e2)
    out = jnp.where(lane == 0, i1.astype(F32),
                    jnp.where(lane == 1, i2.astype(F32),
                              jnp.where(lane == 2, g1, jnp.where(lane == 3, g2, 0.0))))
    o_ref[...] = out


def router(x, g, w_router, *, tm=512):
    n, d = x.shape
    tm = min(tm, n)
    w = jnp.zeros((d, LANES), F32).at[:, :w_router.shape[1]].set(w_router)
    est = 2 * tm * d * 4 + 2 * d * LANES * 4 + 2 * tm * LANES * 4 + 2 * tm * d * 4
    return pl.pallas_call(
        _router_kernel,
        grid=(n // tm,),
        in_specs=[
            pl.BlockSpec((tm, d), lambda i: (i, 0)),
            pl.BlockSpec((1, d), lambda i: (0, 0)),
            pl.BlockSpec((d, LANES), lambda i: (0, 0)),
        ],
        out_specs=pl.BlockSpec((tm, LANES), lambda i: (i, 0)),
        out_shape=jax.ShapeDtypeStruct((n, LANES), F32),
        compiler_params=_cparams(("parallel",), est),
        name="router",
    )(x, g.reshape(1, d), w)


def _row_copy(src_hbm, dst_vmem, sem, src_row, dst_row):
    return pltpu.make_async_copy(src_hbm.at[pl.ds(src_row, 1), :], dst_vmem.at[pl.ds(dst_row, 1), :], sem)


def _gather_norm_kernel(tok_ref, x_hbm, g_ref, o_ref, buf_ref, sem):
    rows = buf_ref.shape[0]
    base = pl.program_id(0) * rows

    def start(r, c):
        _row_copy(x_hbm, buf_ref, sem, tok_ref[base + r], r).start()
        return c

    def wait(r, c):
        _row_copy(x_hbm, buf_ref, sem, 0, r).wait()
        return c

    lax.fori_loop(0, rows, start, 0, unroll=8)
    lax.fori_loop(0, rows, wait, 0, unroll=8)
    o_ref[...] = _rmsnorm(buf_ref[...], g_ref[...]).astype(o_ref.dtype)


def gather_norm(tok_of_slot, x, g, *, rows=GATHER_ROWS):
    slots = tok_of_slot.shape[0]
    d = x.shape[1]
    est = rows * d * 4 * 3 + 2 * rows * d * 2
    return pl.pallas_call(
        _gather_norm_kernel,
        grid_spec=pltpu.PrefetchScalarGridSpec(
            num_scalar_prefetch=1,
            grid=(slots // rows,),
            in_specs=[
                pl.BlockSpec(memory_space=pl.ANY),
                pl.BlockSpec((1, d), lambda i, tok: (0, 0)),
            ],
            out_specs=pl.BlockSpec((rows, d), lambda i, tok: (i, 0)),
            scratch_shapes=[pltpu.VMEM((rows, d), F32), pltpu.SemaphoreType.DMA(())],
        ),
        out_shape=jax.ShapeDtypeStruct((slots, d), BF16),
        compiler_params=_cparams(("arbitrary",), est),
        name="gather_norm",
    )(tok_of_slot, x, g.reshape(1, d))


def _expert_kernel(blk_e_ref, n_used_ref, x_ref, wg_ref, wu_ref, wd_ref, o_ref):
    i = pl.program_id(0)
    f = pl.program_id(1)

    @pl.when(f == 0)
    def _():
        o_ref[...] = jnp.zeros_like(o_ref)

    @pl.when(i < n_used_ref[0])
    def _():
        x = x_ref[...]
        gate = _dot(x, wg_ref[...])
        up = _dot(x, wu_ref[...])
        hid = (gate * _sigmoid(gate) * up).astype(BF16)
        o_ref[...] += _dot(hid, wd_ref[...])


def expert_ffn(blk_expert, n_used, xs, wg, wu, wd, *, tm=MOE_TM, tf=512):
    slots, d = xs.shape
    ff = wg.shape[2]
    nf = ff // tf
    n_blocks = slots // tm

    def live(i, n_used):
        return jnp.minimum(i, n_used[0] - 1)

    def f_eff(i, f, n_used):
        return jnp.where(i < n_used[0], f, nf - 1)

    est = 2 * tm * d * 2 + 6 * d * tf * 2 + 2 * tm * d * 4 + 3 * tm * tf * 4
    return pl.pallas_call(
        _expert_kernel,
        grid_spec=pltpu.PrefetchScalarGridSpec(
            num_scalar_prefetch=2,
            grid=(n_blocks, nf),
            in_specs=[
                pl.BlockSpec((tm, d), lambda i, f, be, nu: (live(i, nu), 0)),
                pl.BlockSpec((None, d, tf), lambda i, f, be, nu: (be[live(i, nu)], 0, f_eff(i, f, nu))),
                pl.BlockSpec((None, d, tf), lambda i, f, be, nu: (be[live(i, nu)], 0, f_eff(i, f, nu))),
                pl.BlockSpec((None, tf, d), lambda i, f, be, nu: (be[live(i, nu)], f_eff(i, f, nu), 0)),
            ],
            out_specs=pl.BlockSpec((tm, d), lambda i, f, be, nu: (i, 0)),
        ),
        out_shape=jax.ShapeDtypeStruct((slots, d), F32),
        compiler_params=_cparams(("arbitrary", "arbitrary"), est),
        name="expert_ffn",
    )(blk_expert, n_used, xs, wg, wu, wd)


def _combine_kernel(slot_ref, y_hbm, x_ref, r_ref, g_ref, o_ref, buf_ref, sem):
    rows = x_ref.shape[0]
    base = pl.program_id(0) * rows

    def start(r, c):
        for k in range(TOP_K):
            _row_copy(y_hbm, buf_ref.at[k], sem, slot_ref[TOP_K * (base + r) + k], r).start()
        return c

    def wait(r, c):
        for k in range(TOP_K):
            _row_copy(y_hbm, buf_ref.at[k], sem, 0, r).wait()
        return c

    lax.fori_loop(0, rows, start, 0, unroll=8)
    lax.fori_loop(0, rows, wait, 0, unroll=8)
    gates = r_ref[...]
    out = x_ref[...] + gates[:, 2:3] * buf_ref[0] + gates[:, 3:4] * buf_ref[1]
    o_ref[...] = _rmsnorm(out, g_ref[...])


def combine_norm(slot_of_assignment, ys, x, routed, g_final, *, rows=GATHER_ROWS):
    n, d = x.shape
    rows = min(rows, n)
    est = 2 * rows * d * 4 + 4 * rows * d * 4 + 4 * rows * d * 4
    return pl.pallas_call(
        _combine_kernel,
        grid_spec=pltpu.PrefetchScalarGridSpec(
            num_scalar_prefetch=1,
            grid=(n // rows,),
            in_specs=[
                pl.BlockSpec(memory_space=pl.ANY),
                pl.BlockSpec((rows, d), lambda i, s: (i, 0)),
                pl.BlockSpec((rows, LANES), lambda i, s: (i, 0)),
                pl.BlockSpec((1, d), lambda i, s: (0, 0)),
            ],
            out_specs=pl.BlockSpec((rows, d), lambda i, s: (i, 0)),
            scratch_shapes=[pltpu.VMEM((TOP_K, rows, d), F32), pltpu.SemaphoreType.DMA(())],
        ),
        out_shape=jax.ShapeDtypeStruct((n, d), F32),
        compiler_params=_cparams(("arbitrary",), est),
        name="combine_norm",
    )(slot_of_assignment, ys, x, routed, g_final.reshape(1, d))


def moe_plan(routed, *, tm):
    n = routed.shape[0]
    experts = routed[:, :TOP_K].astype(jnp.int32).reshape(-1)
    onehot = (experts[:, None] == jnp.arange(N_EXPERTS)[None, :]).astype(jnp.int32)
    rank = jnp.sum((jnp.cumsum(onehot, axis=0) - onehot) * onehot, axis=1)
    counts = jnp.sum(onehot, axis=0)
    padded = ((counts + tm - 1) // tm) * tm
    ends = jnp.cumsum(padded)
    starts = ends - padded
    slot = (starts[experts] + rank).astype(jnp.int32)
    n_blocks = (n * TOP_K) // tm + N_EXPERTS
    tok = jnp.repeat(jnp.arange(n, dtype=jnp.int32), TOP_K)
    tok_of_slot = jnp.zeros((n_blocks * tm,), jnp.int32).at[slot].set(tok)
    blk_expert = jnp.minimum(
        jnp.searchsorted(ends, jnp.arange(n_blocks, dtype=jnp.int32) * tm, side="right"), N_EXPERTS - 1
    ).astype(jnp.int32)
    n_used = (ends[-1] // tm).astype(jnp.int32).reshape(1)
    return slot, tok_of_slot, blk_expert, n_used


def _prepare_weights(w_qkv, w_o, w_ff_gate, w_ff_up, w_ff_down, w_in_c, w_out_c, w_e_gate, w_e_up, w_e_down):
    d_inner = w_out_c.shape[1]
    main_cols = 2 * d_inner + 2 * SSM_GROUPS * SSM_STATE
    return dict(
        w_qkv=w_qkv[0].astype(BF16), w_o=w_o[0].astype(BF16),
        w_ff_gate=w_ff_gate[0].astype(BF16), w_ff_up=w_ff_up[0].astype(BF16), w_ff_down=w_ff_down[0].astype(BF16),
        w_in_main=w_in_c[0][:, :main_cols].astype(BF16), w_in_dt=w_in_c[0][:, main_cols:].astype(BF16),
        w_out=w_out_c[0].astype(BF16),
        w_e_gate=w_e_gate[0].astype(BF16), w_e_up=w_e_up[0].astype(BF16), w_e_down=w_e_down[0].astype(BF16),
    )


def _trunk(x3, wb, g_mix, g_ffn, rpb, conv_w, conv_b, dt_bias, a_log, d_skip, g_gate, w_router, g_final):
    batch, seq_len, d = x3.shape
    n = batch * seq_len
    x = x3.reshape(n, d)

    cos, sin = rope_tables(seq_len)
    qkv = qkv_proj(x, g_mix[0], wb["w_qkv"], cos, sin, seq_len=seq_len)
    bias_tiles = natten_bias_tiles(rpb[0], seq_len // GRID_W)
    o_a = natten(qkv, bias_tiles, batch=batch, seq_len=seq_len)
    o_b = dilated_attention(qkv, batch=batch, seq_len=seq_len)
    x = attn_out(x, o_a, o_b, wb["w_o"])
    x = ffn(x, g_ffn[0], wb["w_ff_gate"], wb["w_ff_up"], wb["w_ff_down"])

    d_inner = wb["w_out"].shape[0]
    gw = d_inner // SSM_GROUPS
    hpg = gw // SSM_HEAD_DIM
    heads = SSM_GROUPS * hpg
    zx = norm_matmul(x, g_mix[1], wb["w_in_main"], tn=gw, out_dtype=BF16)
    dt_raw = norm_matmul(x, g_mix[1], wb["w_in_dt"], tn=2 * heads, out_dtype=F32)[0]
    z_tiles = d_inner // gw
    xs = conv_silu(zx, conv_w[0], conv_b[0], width=gw, first_tile=z_tiles, n_tiles=SSM_GROUPS,
                   col_offset=0, batch=batch, seq_len=seq_len, tr=2048)
    bc = conv_silu(zx, conv_w[0], conv_b[0], width=SSM_STATE, first_tile=2 * z_tiles, n_tiles=2 * SSM_GROUPS,
                   col_offset=d_inner, batch=batch, seq_len=seq_len, tr=4096)
    L = SSM_CHUNK
    dt_rows = dt_raw.reshape(n // L, L, 2, SSM_GROUPS, hpg).transpose(3, 0, 2, 4, 1).reshape(SSM_GROUPS, n // L, 2 * hpg, L)

    def per_group_rows(p):
        return p.reshape(2, SSM_GROUPS, hpg).transpose(1, 0, 2).reshape(SSM_GROUPS, 2 * hpg, 1)

    d_skip_rows = jnp.repeat(d_skip[0].reshape(SSM_GROUPS, 1, hpg), SSM_HEAD_DIM, axis=2)
    y = ssd(xs, bc, dt_rows, per_group_rows(dt_bias[0]), per_group_rows(a_log[0]), d_skip_rows,
            batch=batch, seq_len=seq_len)
    x = mamba_out(y, zx, g_gate[0], wb["w_out"], x)

    routed = router(x, g_ffn[1], w_router[0])
    slot, tok_of_slot, blk_expert, n_used = moe_plan(routed, tm=MOE_TM)
    xs_sorted = gather_norm(tok_of_slot, x, g_ffn[1])
    ys = expert_ffn(blk_expert, n_used, xs_sorted, wb["w_e_gate"], wb["w_e_up"], wb["w_e_down"])
    out = combine_norm(slot, ys, x, routed, g_final)
    return out.reshape(batch, seq_len, d)


def kernel(x_prompt, x_sample, g_mix, g_ffn, w_qkv, rpb, w_o, w_ff_gate, w_ff_up, w_ff_down, w_in_c, conv_w, conv_b,
           dt_bias, a_log, d_skip, g_gate, w_out_c, w_router, w_e_gate, w_e_up, w_e_down, g_final):
    wb = _prepare_weights(w_qkv, w_o, w_ff_gate, w_ff_up, w_ff_down, w_in_c, w_out_c, w_e_gate, w_e_up, w_e_down)
    args = (wb, g_mix, g_ffn, rpb, conv_w, conv_b, dt_bias, a_log, d_skip, g_gate, w_router, g_final)
    return _trunk(x_prompt, *args), _trunk(x_sample, *args)
```

```python
import functools
import math

import jax
import jax.numpy as jnp
import numpy as np
from jax import lax
from jax.experimental import pallas as pl
from jax.experimental.pallas import tpu as pltpu

F32 = jnp.float32
BF16 = jnp.bfloat16
EPS = 1e-6
NEG_INF = float("-inf")

GRID_W = 64
HEAD_DIM = 128
N_HEADS_A = 4
N_HEADS_B_GROUP = 4
DILATIONS = (1, 4, 16)
BAND_RADIUS = 64
N_HEADS_QKV = N_HEADS_A + N_HEADS_B_GROUP * len(DILATIONS)
WIN_H = 8
WIN_W = 16
ROPE_THETA = 10000.0
SSM_HEAD_DIM = 64
SSM_GROUPS = 8
SSM_STATE = 128
SSM_CONV = 5
SSM_CHUNK = 128
N_EXPERTS = 8
TOP_K = 2

V7X_VMEM_BYTES = 64 * 1024 * 1024
LANES = 128
BF16_SUBLANES = 16

NAT_ROWS = 8
NAT_KROWS = NAT_ROWS + WIN_H - 1
DIL_TQ = 512
DIL_KC = (256, 256, 512)
DIL_PARTS = 4
MOE_TM = 512
GATHER_ROWS = 256


def _cparams(semantics, vmem_estimate):
    limit = int(min(max(2 * vmem_estimate, 32 * 1024 * 1024), V7X_VMEM_BYTES - 8 * 1024 * 1024))
    return pltpu.CompilerParams(dimension_semantics=semantics, vmem_limit_bytes=limit)


def _rmsnorm(x, g):
    return x * lax.rsqrt(jnp.mean(x * x, axis=-1, keepdims=True) + EPS) * g


def _sigmoid(x):
    return 1.0 / (1.0 + jnp.exp(-x))


def _softplus(x):
    return jnp.maximum(x, 0.0) + jnp.log(1.0 + jnp.exp(-jnp.abs(x)))


def _dot(a, b):
    return jnp.dot(a, b, preferred_element_type=F32)


def _dot_nt(a, b):
    return lax.dot_general(a, b, (((1,), (1,)), ((), ())), preferred_element_type=F32)


def _norm_matmul_kernel(x_ref, g_ref, w_ref, o_ref, xn_ref):
    @pl.when(pl.program_id(1) == 0)
    def _():
        xn_ref[...] = _rmsnorm(x_ref[...], g_ref[...]).astype(BF16)

    o_ref[...] = _dot(xn_ref[...], w_ref[...]).astype(o_ref.dtype)


def norm_matmul(x, g, w, *, tn, out_dtype, tm=1024):
    n, k = x.shape
    m = w.shape[1]
    tm = min(tm, n)
    est = 2 * tm * k * 4 + tm * k * 2 + 2 * k * tn * 2 + 2 * tm * tn * 4
    return pl.pallas_call(
        _norm_matmul_kernel,
        grid=(n // tm, m // tn),
        in_specs=[
            pl.BlockSpec((tm, k), lambda i, j: (i, 0)),
            pl.BlockSpec((1, k), lambda i, j: (0, 0)),
            pl.BlockSpec((k, tn), lambda i, j: (0, j)),
        ],
        out_specs=pl.BlockSpec((None, tm, tn), lambda i, j: (j, i, 0)),
        out_shape=jax.ShapeDtypeStruct((m // tn, n, tn), out_dtype),
        scratch_shapes=[pltpu.VMEM((tm, k), BF16)],
        compiler_params=_cparams(("parallel", "arbitrary"), est),
        name="norm_matmul",
    )(x, g.reshape(1, k), w)


def _qkv_kernel(x_ref, g_ref, w_ref, cos_ref, sin_ref, o_ref, xn_ref, *, heads_per_tile, scale):
    j = pl.program_id(1)
    tiles_per_part = N_HEADS_QKV // heads_per_tile

    @pl.when(j == 0)
    def _():
        xn_ref[...] = _rmsnorm(x_ref[...], g_ref[...]).astype(BF16)

    r = _dot(xn_ref[...], w_ref[...])
    r = r * jnp.where(j < tiles_per_part, scale, 1.0)
    use_rope = jnp.logical_and(j < 2 * tiles_per_part, j % tiles_per_part != 0)

    @pl.when(use_rope)
    def _():
        c = cos_ref[...]
        s = sin_ref[...]
        for h in range(heads_per_tile):
            p = r[:, h * HEAD_DIM:(h + 1) * HEAD_DIM]
            o_ref[h] = (p * c + pltpu.roll(p, HEAD_DIM // 2, 1) * s).astype(o_ref.dtype)

    @pl.when(jnp.logical_not(use_rope))
    def _():
        for h in range(heads_per_tile):
            o_ref[h] = r[:, h * HEAD_DIM:(h + 1) * HEAD_DIM].astype(o_ref.dtype)


def qkv_proj(x, g, w, cos, sin, *, seq_len, tm=1024):
    n, k = x.shape
    m = w.shape[1]
    hpt = N_HEADS_A
    tn = hpt * HEAD_DIM
    tm = min(tm, seq_len)
    tiles_per_seq = seq_len // tm
    est = 2 * tm * k * 4 + tm * k * 2 + 2 * k * tn * 2 + 2 * tm * tn * 2 + 4 * tm * HEAD_DIM * 4 + tm * tn * 4
    return pl.pallas_call(
        functools.partial(_qkv_kernel, heads_per_tile=hpt, scale=HEAD_DIM ** -0.5),
        grid=(n // tm, m // tn),
        in_specs=[
            pl.BlockSpec((tm, k), lambda i, j: (i, 0)),
            pl.BlockSpec((1, k), lambda i, j: (0, 0)),
            pl.BlockSpec((k, tn), lambda i, j: (0, j)),
            pl.BlockSpec((tm, HEAD_DIM), lambda i, j: (i % tiles_per_seq, 0)),
            pl.BlockSpec((tm, HEAD_DIM), lambda i, j: (i % tiles_per_seq, 0)),
        ],
        out_specs=pl.BlockSpec((hpt, tm, HEAD_DIM), lambda i, j: (j, i, 0)),
        out_shape=jax.ShapeDtypeStruct((m // HEAD_DIM, n, HEAD_DIM), BF16),
        scratch_shapes=[pltpu.VMEM((tm, k), BF16)],
        compiler_params=_cparams(("parallel", "arbitrary"), est),
        name="qkv_proj",
    )(x, g.reshape(1, k), w, cos, sin)


def rope_tables(seq_len):
    half = HEAD_DIM // 2
    inv = ROPE_THETA ** (-jnp.arange(half, dtype=F32) / half)
    ang = jnp.arange(seq_len, dtype=F32)[:, None] * inv[None, :]
    cos = jnp.cos(ang)
    sin = jnp.sin(ang)
    return jnp.concatenate([cos, cos], axis=1), jnp.concatenate([-sin, sin], axis=1)


def _natten_kernel(q_ref, k_ref, v_ref, bias_ref, o_ref, *, n_blocks, grid_rows):
    blk = pl.program_id(2)
    first_row = jnp.where(blk == 0, 0,
                          jnp.where(blk == n_blocks - 1, grid_rows - NAT_KROWS, blk * NAT_ROWS - WIN_H // 2))
    start = pl.multiple_of(first_row * GRID_W, GRID_W)
    kw = k_ref[pl.ds(start, NAT_KROWS * GRID_W), :]
    vw = v_ref[pl.ds(start, NAT_KROWS * GRID_W), :]
    s = _dot_nt(q_ref[...], kw) + bias_ref[...]
    m = jnp.max(s, axis=-1, keepdims=True)
    p = jnp.exp(s - m)
    l = jnp.sum(p, axis=-1, keepdims=True)
    o_ref[...] = (_dot(p.astype(BF16), vw) / l).astype(o_ref.dtype)


def natten_bias_tiles(rpb, grid_rows):
    n_blocks = grid_rows // NAT_ROWS
    r0s = np.array([0, NAT_ROWS, (n_blocks - 1) * NAT_ROWS])
    k0s = np.array([0, NAT_ROWS - WIN_H // 2, grid_rows - NAT_KROWS])
    r = r0s[:, None] + np.arange(NAT_ROWS)[None, :]
    kr = k0s[:, None] + np.arange(NAT_KROWS)[None, :]
    rs = np.clip(r - WIN_H // 2, 0, grid_rows - WIN_H)
    row_ok = (kr[:, None, :] >= rs[:, :, None]) & (kr[:, None, :] < rs[:, :, None] + WIN_H)
    d_row = np.clip(kr[:, None, :] - r[:, :, None] + (WIN_H - 1), 0, 2 * WIN_H - 2)
    c = np.arange(GRID_W)
    cs = np.clip(c - WIN_W // 2, 0, GRID_W - WIN_W)
    col_ok = (c[None, :] >= cs[:, None]) & (c[None, :] < cs[:, None] + WIN_W)
    d_col = np.clip(c[None, :] - c[:, None] + (WIN_W - 1), 0, 2 * WIN_W - 2)
    sel_row = (d_row[..., None] == np.arange(2 * WIN_H - 1)).astype(np.float32)
    sel_col = (d_col[..., None] == np.arange(2 * WIN_W - 1)).astype(np.float32)
    rows = jnp.einsum("tikr,hrc->thikc", sel_row, rpb.astype(F32), precision=lax.Precision.HIGHEST)
    bias = jnp.einsum("thikc,qwc->thiqkw", rows, sel_col, precision=lax.Precision.HIGHEST)
    ok = row_ok[:, :, None, :, None] & col_ok[None, None, :, None, :]
    bias = jnp.where(ok[:, None], bias, NEG_INF)
    return bias.reshape(3, rpb.shape[0], NAT_ROWS * GRID_W, NAT_KROWS * GRID_W)


def natten(qkv, bias_tiles, *, batch, seq_len):
    n = batch * seq_len
    grid_rows = seq_len // GRID_W
    n_blocks = grid_rows // NAT_ROWS
    tq = NAT_ROWS * GRID_W
    tk = NAT_KROWS * GRID_W

    def tile_kind(blk):
        return jnp.where(blk == 0, 0, jnp.where(blk == n_blocks - 1, 2, 1))

    est = 4 * seq_len * HEAD_DIM * 2 + 2 * tq * tk * 4 + 3 * tq * tk * 4
    return pl.pallas_call(
        functools.partial(_natten_kernel, n_blocks=n_blocks, grid_rows=grid_rows),
        grid=(batch, N_HEADS_A, n_blocks),
        in_specs=[
            pl.BlockSpec((None, tq, HEAD_DIM), lambda b, h, i: (h, b * n_blocks + i, 0)),
            pl.BlockSpec((None, seq_len, HEAD_DIM), lambda b, h, i: (N_HEADS_QKV + h, b, 0)),
            pl.BlockSpec((None, seq_len, HEAD_DIM), lambda b, h, i: (2 * N_HEADS_QKV + h, b, 0)),
            pl.BlockSpec((None, None, tq, tk), lambda b, h, i: (tile_kind(i), h, 0, 0)),
        ],
        out_specs=pl.BlockSpec((tq, HEAD_DIM), lambda b, h, i: (b * n_blocks + i, h)),
        out_shape=jax.ShapeDtypeStruct((n, N_HEADS_A * HEAD_DIM), BF16),
        compiler_params=_cparams(("parallel", "parallel", "arbitrary"), est),
        name="natten",
    )(qkv, qkv, qkv, bias_tiles)


class _DilatedGeometry:
    def __init__(self, dil, kc):
        self.dil, self.kc = dil, kc
        self.reach = BAND_RADIUS * dil
        self.tp = DIL_TQ // DIL_PARTS
        assert kc % self.tp == 0
        self.halo = -(-self.reach // kc) * kc
        self.n_chunks = (DIL_TQ + 2 * self.halo) // kc
        self.n_tiles = (kc // self.tp) * (self.n_chunks - 1) + DIL_PARTS

    def tile_index(self, chunk, part):
        return (self.kc // self.tp) * chunk + (DIL_PARTS - 1 - part)

    def bias_tiles(self):
        row = np.arange(self.tp)[:, None]
        col = np.arange(self.kc)[None, :]
        tiles = []
        for u in range(self.n_tiles):
            rel = col - row + (u - (DIL_PARTS - 1)) * self.tp - self.halo
            ok = (np.abs(rel) <= self.reach) & (rel % self.dil == 0)
            tiles.append(np.where(ok, 0.0, -np.inf))
        return np.stack(tiles).astype(np.float32)


DIL_GEOMETRY = tuple(_DilatedGeometry(d, kc) for d, kc in zip(DILATIONS, DIL_KC))


def _dilated_kernel(b0, b1, b2, q0, q1, q2, k0, k1, k2, v0, v1, v2, o_ref, *, seq_len):
    tq = DIL_TQ
    parts = DIL_PARTS
    tp = tq // parts
    t0 = pl.program_id(2) * tq
    carry = tuple((jnp.full((tp, 1), -1e30, F32), jnp.zeros((tp, 1), F32), jnp.zeros((tp, HEAD_DIM), F32))
                  for _ in range(parts))
    for geo, bias_ref, q_ref, k_ref, v_ref in zip(DIL_GEOMETRY, (b0, b1, b2), (q0, q1, q2), (k0, k1, k2), (v0, v1, v2)):
        kc = geo.kc
        window_start = t0 - geo.halo
        lo = jnp.maximum(0, (geo.halo - t0) // kc)
        hi = jnp.minimum(geo.n_chunks, (seq_len - window_start) // kc)
        qs = tuple(q_ref[p * tp:(p + 1) * tp, :] for p in range(parts))

        def body(ci, carry, geo=geo, bias_ref=bias_ref, k_ref=k_ref, v_ref=v_ref, window_start=window_start, qs=qs):
            ks = pl.multiple_of(window_start + ci * geo.kc, geo.kc)
            kk = k_ref[pl.ds(ks, geo.kc), :]
            vv = v_ref[pl.ds(ks, geo.kc), :]
            scores = [_dot_nt(qs[p], kk) + bias_ref[geo.tile_index(ci, p)] for p in range(parts)]
            stats = []
            for p in range(parts):
                m, l, _ = carry[p]
                m_new = jnp.maximum(m, jnp.max(scores[p], axis=-1, keepdims=True))
                alpha = jnp.exp(m - m_new)
                e = jnp.exp(scores[p] - m_new)
                stats.append((m_new, alpha, alpha * l + jnp.sum(e, axis=-1, keepdims=True), e.astype(BF16)))
            return tuple((m_new, l, alpha * carry[p][2] + _dot(e, vv))
                         for p, (m_new, alpha, l, e) in enumerate(stats))

        carry = lax.fori_loop(lo, hi, body, carry)
    for p in range(parts):
        _, l, acc = carry[p]
        o_ref[p * tp:(p + 1) * tp, :] = (acc / l).astype(o_ref.dtype)


def dilated_attention(qkv, *, batch, seq_len):
    n = batch * seq_len
    nq = seq_len // DIL_TQ
    biases = [geo.bias_tiles() for geo in DIL_GEOMETRY]

    def q_spec(g):
        return pl.BlockSpec((None, DIL_TQ, HEAD_DIM),
                            lambda b, j, i: (N_HEADS_A + N_HEADS_B_GROUP * g + j, b * nq + i, 0))

    def kv_spec(part, g):
        return pl.BlockSpec((None, seq_len, HEAD_DIM),
                            lambda b, j, i: (part * N_HEADS_QKV + N_HEADS_A + N_HEADS_B_GROUP * g + j, b, 0))

    est = 12 * seq_len * HEAD_DIM * 2 + 8 * DIL_TQ * max(DIL_KC) * 4 + 2 * sum(b.size for b in biases) * 4
    groups = range(len(DILATIONS))
    return pl.pallas_call(
        functools.partial(_dilated_kernel, seq_len=seq_len),
        grid=(batch, N_HEADS_B_GROUP, nq),
        in_specs=[pl.BlockSpec(b.shape, lambda b_, j, i: (0, 0, 0)) for b in biases]
        + [q_spec(g) for g in groups] + [kv_spec(1, g) for g in groups] + [kv_spec(2, g) for g in groups],
        out_specs=pl.BlockSpec((DIL_TQ, HEAD_DIM), lambda b, j, i: (b * nq + i, j)),
        out_shape=jax.ShapeDtypeStruct((n, N_HEADS_B_GROUP * HEAD_DIM), BF16),
        compiler_params=_cparams(("parallel", "parallel", "arbitrary"), est),
        name="dilated_attention",
    )(*biases, *([qkv] * 9))


def _attn_out_kernel(x_ref, oa_ref, ob_ref, w_ref, o_ref):
    ka = oa_ref.shape[1]
    o_ref[...] = x_ref[...] + _dot(oa_ref[...], w_ref[:ka, :]) + _dot(ob_ref[...], w_ref[ka:, :])


def attn_out(x, o_a, o_b, w, *, tm=512):
    n, d = x.shape
    ka, kb = o_a.shape[1], o_b.shape[1]
    tm = min(tm, n)
    est = 4 * tm * d * 4 + 2 * (ka + kb) * d * 2 + 2 * tm * (ka + kb) * 2
    return pl.pallas_call(
        _attn_out_kernel,
        grid=(n // tm,),
        in_specs=[
            pl.BlockSpec((tm, d), lambda i: (i, 0)),
            pl.BlockSpec((tm, ka), lambda i: (i, 0)),
            pl.BlockSpec((tm, kb), lambda i: (i, 0)),
            pl.BlockSpec((ka + kb, d), lambda i: (0, 0)),
        ],
        out_specs=pl.BlockSpec((tm, d), lambda i: (i, 0)),
        out_shape=jax.ShapeDtypeStruct((n, d), F32),
        compiler_params=_cparams(("parallel",), est),
        name="attn_out",
    )(x, o_a, o_b, w)


def _ffn_kernel(x_ref, g_ref, wg_ref, wu_ref, wd_ref, o_ref, xn_ref):
    f = pl.program_id(1)

    @pl.when(f == 0)
    def _():
        xn_ref[...] = _rmsnorm(x_ref[...], g_ref[...]).astype(BF16)
        o_ref[...] = x_ref[...]

    xn = xn_ref[...]
    gate = _dot(xn, wg_ref[...])
    up = _dot(xn, wu_ref[...])
    hid = (gate * _sigmoid(gate) * up).astype(BF16)
    o_ref[...] += _dot(hid, wd_ref[...])


def ffn(x, g, wg, wu, wd, *, tm=512, tf=512):
    n, d = x.shape
    ff = wg.shape[1]
    tm = min(tm, n)
    est = 4 * tm * d * 4 + tm * d * 2 + 6 * d * tf * 2 + 3 * tm * tf * 4
    return pl.pallas_call(
        _ffn_kernel,
        grid=(n // tm, ff // tf),
        in_specs=[
            pl.BlockSpec((tm, d), lambda i, f: (i, 0)),
            pl.BlockSpec((1, d), lambda i, f: (0, 0)),
            pl.BlockSpec((d, tf), lambda i, f: (0, f)),
            pl.BlockSpec((d, tf), lambda i, f: (0, f)),
            pl.BlockSpec((tf, d), lambda i, f: (f, 0)),
        ],
        out_specs=pl.BlockSpec((tm, d), lambda i, f: (i, 0)),
        out_shape=jax.ShapeDtypeStruct((n, d), F32),
        scratch_shapes=[pltpu.VMEM((tm, d), BF16)],
        compiler_params=_cparams(("parallel", "arbitrary"), est),
        name="ffn",
    )(x, g.reshape(1, d), wg, wu, wd)


def _conv_kernel(xm_ref, xp_ref, xn_ref, w_ref, b_ref, o_ref, *rest, tr, n_row_blocks):
    ext_ref = rest[-1]
    i = pl.program_id(2)
    hb = BF16_SUBLANES
    pad = SSM_CONV // 2
    ext_ref[0:hb, :] = jnp.where(i > 0, xp_ref[...].astype(F32), 0.0)
    ext_ref[hb:hb + tr, :] = xm_ref[...].astype(F32)
    ext_ref[hb + tr:2 * hb + tr, :] = jnp.where(i < n_row_blocks - 1, xn_ref[...].astype(F32), 0.0)
    acc = jnp.broadcast_to(b_ref[...], o_ref.shape)
    for k in range(SSM_CONV):
        acc = acc + ext_ref[pl.ds(hb - pad + k, tr), :] * w_ref[k:k + 1, :]
    out = acc * _sigmoid(acc)
    o_ref[...] = out.astype(o_ref.dtype)
    if len(rest) == 2:
        ot_ref = rest[0]
        L = ot_ref.shape[-1]
        for c in range(tr // L):
            ot_ref[c] = out[c * L:(c + 1) * L, :].T.astype(ot_ref.dtype)


def conv_silu(zx, conv_w, conv_b, *, width, first_tile, n_tiles, col_offset, batch, seq_len, tr, transposed=False):
    n = batch * seq_len
    tile_w = zx.shape[2]
    per = tile_w // width
    tr = min(tr, seq_len)
    nr = seq_len // tr
    hb = BF16_SUBLANES
    seq_hb = seq_len // hb
    n_hb = n // hb

    def main_map(c, b, i):
        return (first_tile + c // per, b * nr + i, c % per)

    def prev_map(c, b, i):
        return (first_tile + c // per, jnp.maximum(b * seq_hb + i * (tr // hb) - 1, 0), c % per)

    def next_map(c, b, i):
        return (first_tile + c // per, jnp.minimum(b * seq_hb + (i + 1) * (tr // hb), n_hb - 1), c % per)

    L = SSM_CHUNK
    out_specs = [pl.BlockSpec((None, tr, width), lambda c, b, i: (c, b * nr + i, 0))]
    out_shape = [jax.ShapeDtypeStruct((n_tiles, n, width), BF16)]
    if transposed:
        out_specs.append(pl.BlockSpec((None, tr // L, width, L), lambda c, b, i: (c, b * nr + i, 0, 0)))
        out_shape.append(jax.ShapeDtypeStruct((n_tiles, n // L, width, L), BF16))
    est = 4 * tr * width * 2 + (tr + 2 * hb) * width * 4 + 4 * tr * width * 4 + 4 * tr * width * 2
    outs = pl.pallas_call(
        functools.partial(_conv_kernel, tr=tr, n_row_blocks=nr),
        grid=(n_tiles, batch, nr),
        in_specs=[
            pl.BlockSpec((None, tr, width), main_map),
            pl.BlockSpec((None, hb, width), prev_map),
            pl.BlockSpec((None, hb, width), next_map),
            pl.BlockSpec((SSM_CONV, width), lambda c, b, i: (0, col_offset // width + c)),
            pl.BlockSpec((1, width), lambda c, b, i: (0, col_offset // width + c)),
        ],
        out_specs=out_specs,
        out_shape=out_shape,
        scratch_shapes=[pltpu.VMEM((tr + 2 * hb, width), F32)],
        compiler_params=_cparams(("parallel", "parallel", "arbitrary"), est),
        name="conv_silu",
    )(zx, zx, zx, conv_w, conv_b.reshape(1, -1))
    return outs if transposed else outs[0]


def _lane_cumsum(a):
    lane = lax.broadcasted_iota(jnp.int32, a.shape, 1)
    shift = 1
    while shift < a.shape[1]:
        a = a + jnp.where(lane >= shift, pltpu.roll(a, shift, 1), 0.0)
        shift *= 2
    return a


def _ssd_decay_rows(dt_ref, dtbias_ref, alog_ref, seg_scr, dt_scr, *, hpg):
    L = SSM_CHUNK
    dt_all = _softplus(dt_ref[...] + dtbias_ref[...])
    a_all = dt_all * (-jnp.exp(alog_ref[...]))
    cum = _lane_cumsum(a_all)
    suf = cum[:, L - 1:L] - cum + a_all
    row = lax.broadcasted_iota(jnp.int32, cum.shape, 0)
    seg_scr[...] = jnp.where((row & (2 * hpg - 1)) < hpg, cum, suf)
    dt_scr[...] = dt_all


def _ssd_kernel(xa_ref, ba_ref, bta_ref, ca_ref, dta_ref, xb_ref, btb_ref, cb_ref, dtb_ref,
                dtbias_ref, alog_ref, dskip_ref, yf_ref, yb_ref,
                sf_ref, sb_ref, sega_scr, dta_scr, segb_scr, dtb_scr, *, hpg):
    L = SSM_CHUNK
    P = SSM_HEAD_DIM
    pairs = hpg // 2

    @pl.when(pl.program_id(1) == 0)
    def _():
        sf_ref[...] = jnp.zeros_like(sf_ref)
        sb_ref[...] = jnp.zeros_like(sb_ref)

    _ssd_decay_rows(dta_ref, dtbias_ref, alog_ref, sega_scr, dta_scr, hpg=hpg)
    _ssd_decay_rows(dtb_ref, dtbias_ref, alog_ref, segb_scr, dtb_scr, hpg=hpg)

    li = lax.broadcasted_iota(jnp.int32, (L, L), 0)
    si = lax.broadcasted_iota(jnp.int32, (L, L), 1)
    causal = li >= si
    anti = li <= si
    low_lanes = lax.broadcasted_iota(jnp.int32, (L, 2 * P), 1) < P
    high_lanes = jnp.logical_not(low_lanes)
    low_lanes_row = lax.broadcasted_iota(jnp.int32, (1, 2 * P), 1) < P

    def lanes_of(col):
        return jnp.broadcast_to(col, (L, 2 * P))

    def columns(seg_r, dt_r):
        stacked = jnp.concatenate([seg_r, dt_r, jnp.zeros((L - 4 * hpg, L), F32)], axis=0)
        return stacked.T

    def advance_state(s_ref, g, lanes, state, bt, xpf, cols, h0, h1, tot_row):
        tot0, tot1 = cols[tot_row:tot_row + 1, h0:h0 + 1], cols[tot_row:tot_row + 1, h1:h1 + 1]
        c0 = jnp.exp(tot0 - cols[:, h0:h0 + 1]) * cols[:, 2 * hpg + h0:2 * hpg + h0 + 1]
        c1 = jnp.exp(tot1 - cols[:, h1:h1 + 1]) * cols[:, 2 * hpg + h1:2 * hpg + h1 + 1]
        xw = (xpf * jnp.where(low_lanes, lanes_of(c0), lanes_of(c1))).astype(BF16)
        decay = jnp.where(low_lanes_row, jnp.exp(tot0), jnp.exp(tot1))
        s_ref[g, :, lanes] = state[:, lanes] * decay + _dot(bt, xw)

    def group_body(g, c):
        r0 = pl.multiple_of(g * 2 * hpg, 2 * hpg)

        seg_r = sega_scr[pl.ds(r0, 2 * hpg), :]
        dt_r = dta_scr[pl.ds(r0, 2 * hpg), :]
        cols = columns(seg_r, dt_r)
        xg = xa_ref[g]
        cg = ca_ref[g]
        state = sf_ref[g]
        carried = _dot(cg, state.astype(BF16))
        cbm = _dot_nt(cg, ba_ref[g])
        bt = bta_ref[g]
        for p in range(pairs):
            lanes = slice(p * 2 * P, (p + 1) * 2 * P)
            xpf = xg[:, lanes].astype(F32)
            h0, h1 = 2 * p, 2 * p + 1
            f0, f1 = lanes_of(cols[:, h0:h0 + 1]), lanes_of(cols[:, h1:h1 + 1])
            y = carried[:, lanes] * jnp.where(low_lanes, jnp.exp(f0), jnp.exp(f1)) + dskip_ref[g][:, lanes] * xpf
            for hh, f_cols, keep in ((h0, f0, low_lanes), (h1, f1, high_lanes)):
                wf = jnp.exp(jnp.where(causal, f_cols - seg_r[hh:hh + 1, :], NEG_INF)) * dt_r[hh:hh + 1, :]
                b_cols = lanes_of(cols[:, hpg + hh:hpg + hh + 1])
                wb = jnp.exp(jnp.where(anti, b_cols - seg_r[hpg + hh:hpg + hh + 1, :], NEG_INF)) \
                    * dt_r[hpg + hh:hpg + hh + 1, :]
                w = (cbm * (wf + wb)).astype(BF16)
                y = y + _dot(w, jnp.where(keep, xpf, 0.0).astype(BF16))
            yf_ref[g, :, lanes] = y.astype(yf_ref.dtype)
            advance_state(sf_ref, g, lanes, state, bt, xpf, cols, h0, h1, L - 1)

        cols = columns(segb_scr[pl.ds(r0, 2 * hpg), :], dtb_scr[pl.ds(r0, 2 * hpg), :])
        xg = xb_ref[g]
        state = sb_ref[g]
        carried = _dot(cb_ref[g], state.astype(BF16))
        bt = btb_ref[g]
        for p in range(pairs):
            lanes = slice(p * 2 * P, (p + 1) * 2 * P)
            xpf = xg[:, lanes].astype(F32)
            h0, h1 = hpg + 2 * p, hpg + 2 * p + 1
            e0, e1 = jnp.exp(cols[:, h0:h0 + 1]), jnp.exp(cols[:, h1:h1 + 1])
            y = carried[:, lanes] * jnp.where(low_lanes, lanes_of(e0), lanes_of(e1))
            yb_ref[g, :, lanes] = y.astype(yb_ref.dtype)
            advance_state(sb_ref, g, lanes, state, bt, xpf, cols, h0, h1, 0)
        return c

    lax.fori_loop(0, SSM_GROUPS, group_body, 0)


def ssd(xs, b_nat, b_t, c_nat, dt_rows, dt_bias_rows, a_log_rows, d_skip_rows, *, batch, seq_len):
    groups, n, gw = xs.shape
    hpg = gw // SSM_HEAD_DIM
    assert hpg & (hpg - 1) == 0
    L = SSM_CHUNK
    nc = seq_len // L
    rows = groups * 2 * hpg

    def fwd(b, k):
        return b * nc + k

    def bwd(b, k):
        return b * nc + nc - 1 - k

    def specs(chunk, with_b_nat):
        out = [pl.BlockSpec((groups, L, gw), lambda b, k: (0, chunk(b, k), 0))]
        if with_b_nat:
            out.append(pl.BlockSpec((groups, L, SSM_STATE), lambda b, k: (0, chunk(b, k), 0)))
        out += [
            pl.BlockSpec((groups, None, SSM_STATE, L), lambda b, k: (0, chunk(b, k), 0, 0)),
            pl.BlockSpec((groups, L, SSM_STATE), lambda b, k: (0, chunk(b, k), 0)),
            pl.BlockSpec((None, rows, L), lambda b, k: (chunk(b, k), 0, 0)),
        ]
        return out

    const = [
        pl.BlockSpec((rows, 1), lambda b, k: (0, 0)),
        pl.BlockSpec((rows, 1), lambda b, k: (0, 0)),
        pl.BlockSpec((groups, 1, gw), lambda b, k: (0, 0, 0)),
    ]
    y_shape = jax.ShapeDtypeStruct((groups, n, gw), BF16)
    est = 4 * (groups * L * gw * 2 + 3 * groups * L * SSM_STATE * 2 + rows * L * 4) + 4 * groups * L * gw * 2 \
        + 2 * groups * SSM_STATE * gw * 4 + 4 * rows * L * 4 + 64 * L * L * 4
    return pl.pallas_call(
        functools.partial(_ssd_kernel, hpg=hpg),
        grid=(batch, nc),
        in_specs=specs(fwd, True) + specs(bwd, False) + const,
        out_specs=[pl.BlockSpec((groups, L, gw), lambda b, k: (0, fwd(b, k), 0)),
                   pl.BlockSpec((groups, L, gw), lambda b, k: (0, bwd(b, k), 0))],
        out_shape=[y_shape, y_shape],
        scratch_shapes=[pltpu.VMEM((groups, SSM_STATE, gw), F32), pltpu.VMEM((groups, SSM_STATE, gw), F32)]
        + [pltpu.VMEM((rows, L), F32)] * 4,
        compiler_params=_cparams(("parallel", "arbitrary"), est),
        name="ssd",
    )(xs, b_nat, b_t, c_nat, dt_rows, xs, b_t, c_nat, dt_rows, dt_bias_rows, a_log_rows, d_skip_rows)


def _mamba_out_kernel(yf_ref, yb_ref, z_ref, gg_ref, w_ref, x_ref, o_ref, acc_ref, ssq_ref, *, d_inner):
    step = pl.program_id(1)
    groups_per_step, _, gw = z_ref.shape

    @pl.when(step == 0)
    def _():
        acc_ref[...] = jnp.zeros_like(acc_ref)
        ssq_ref[...] = jnp.zeros_like(ssq_ref)

    for k in range(groups_per_step):
        z = z_ref[k].astype(F32)
        yz = (yf_ref[k].astype(F32) + yb_ref[k].astype(F32)) * (z * _sigmoid(z))
        ssq_ref[...] += jnp.sum(yz * yz, axis=-1, keepdims=True)
        cols = slice(k * gw, (k + 1) * gw)
        acc_ref[...] += _dot((yz * gg_ref[:, cols]).astype(BF16), w_ref[cols, :])

    @pl.when(step == pl.num_programs(1) - 1)
    def _():
        o_ref[...] = x_ref[...] + acc_ref[...] * lax.rsqrt(ssq_ref[...] / d_inner + EPS)


def mamba_out(y_f, y_b, zx, g_gate, w_out, x, *, tm=512, groups_per_step=2):
    groups, n, gw = y_f.shape
    d = x.shape[1]
    tm = min(tm, n)
    gps = groups_per_step
    est = gps * (4 * tm * gw * 2 + 2 * tm * gw * 2 + 2 * gw * d * 2) + 5 * tm * d * 4
    return pl.pallas_call(
        functools.partial(_mamba_out_kernel, d_inner=groups * gw),
        grid=(n // tm, groups // gps),
        in_specs=[
            pl.BlockSpec((gps, tm, gw), lambda i, s: (s, i, 0)),
            pl.BlockSpec((gps, tm, gw), lambda i, s: (s, i, 0)),
            pl.BlockSpec((gps, tm, gw), lambda i, s: (s, i, 0)),
            pl.BlockSpec((1, gps * gw), lambda i, s: (0, s)),
            pl.BlockSpec((gps * gw, d), lambda i, s: (s, 0)),
            pl.BlockSpec((tm, d), lambda i, s: (i, 0)),
        ],
        out_specs=pl.BlockSpec((tm, d), lambda i, s: (i, 0)),
        out_shape=jax.ShapeDtypeStruct((n, d), F32),
        scratch_shapes=[pltpu.VMEM((tm, d), F32), pltpu.VMEM((tm, 1), F32)],
        compiler_params=_cparams(("parallel", "arbitrary"), est),
        name="mamba_out",
    )(y_f, y_b, zx, g_gate.reshape(1, -1), w_out, x)


def _router_kernel(x_ref, g_ref, w_ref, o_ref):
    xn = _rmsnorm(x_ref[...], g_ref[...])
    logits = jnp.dot(xn, w_ref[...], preferred_element_type=F32, precision=lax.Precision.HIGHEST)
    lane = lax.broadcasted_iota(jnp.int32, logits.shape, 1)
    logits = jnp.where(lane < N_EXPERTS, logits, NEG_INF)
    v1 = jnp.max(logits, axis=-1, keepdims=True)
    i1 = jnp.min(jnp.where(logits == v1, lane, LANES), axis=-1, keepdims=True)
    rest = jnp.where(lane == i1, NEG_INF, logits)
    v2 = jnp.max(rest, axis=-1, keepdims=True)
    i2 = jnp.min(jnp.where(rest == v2, lane, LANES), axis=-1, keepdims=True)
    e2 = jnp.exp(v2 - v1)
    g1 = 1.0 / (1.0 + e2)
    g2 = e2 / (1.0 + e2)
    out = jnp.where(lane == 0, i1.astype(F32),
                    jnp.where(lane == 1, i2.astype(F32),
                              jnp.where(lane == 2, g1, jnp.where(lane == 3, g2, 0.0))))
    o_ref[...] = out


def router(x, g, w_router, *, tm=512):
    n, d = x.shape
    tm = min(tm, n)
    w = jnp.zeros((d, LANES), F32).at[:, :w_router.shape[1]].set(w_router)
    est = 2 * tm * d * 4 + 2 * d * LANES * 4 + 2 * tm * LANES * 4 + 2 * tm * d * 4
    return pl.pallas_call(
        _router_kernel,
        grid=(n // tm,),
        in_specs=[
            pl.BlockSpec((tm, d), lambda i: (i, 0)),
            pl.BlockSpec((1, d), lambda i: (0, 0)),
            pl.BlockSpec((d, LANES), lambda i: (0, 0)),
        ],
        out_specs=pl.BlockSpec((tm, LANES), lambda i: (i, 0)),
        out_shape=jax.ShapeDtypeStruct((n, LANES), F32),
        compiler_params=_cparams(("parallel",), est),
        name="router",
    )(x, g.reshape(1, d), w)


def _row_copy(src_hbm, dst_vmem, sem, src_row, dst_row):
    return pltpu.make_async_copy(src_hbm.at[pl.ds(src_row, 1), :], dst_vmem.at[pl.ds(dst_row, 1), :], sem)


def _gather_norm_kernel(tok_ref, x_hbm, g_ref, o_ref, buf_ref, sem):
    rows = buf_ref.shape[0]
    base = pl.program_id(0) * rows

    def start(r, c):
        _row_copy(x_hbm, buf_ref, sem, tok_ref[base + r], r).start()
        return c

    def wait(r, c):
        _row_copy(x_hbm, buf_ref, sem, 0, r).wait()
        return c

    lax.fori_loop(0, rows, start, 0, unroll=8)
    lax.fori_loop(0, rows, wait, 0, unroll=8)
    o_ref[...] = _rmsnorm(buf_ref[...], g_ref[...]).astype(o_ref.dtype)


def gather_norm(tok_of_slot, x, g, *, rows=GATHER_ROWS):
    slots = tok_of_slot.shape[0]
    d = x.shape[1]
    est = rows * d * 4 * 3 + 2 * rows * d * 2
    return pl.pallas_call(
        _gather_norm_kernel,
        grid_spec=pltpu.PrefetchScalarGridSpec(
            num_scalar_prefetch=1,
            grid=(slots // rows,),
            in_specs=[
                pl.BlockSpec(memory_space=pl.ANY),
                pl.BlockSpec((1, d), lambda i, tok: (0, 0)),
            ],
            out_specs=pl.BlockSpec((rows, d), lambda i, tok: (i, 0)),
            scratch_shapes=[pltpu.VMEM((rows, d), F32), pltpu.SemaphoreType.DMA(())],
        ),
        out_shape=jax.ShapeDtypeStruct((slots, d), BF16),
        compiler_params=_cparams(("arbitrary",), est),
        name="gather_norm",
    )(tok_of_slot, x, g.reshape(1, d))


def _expert_kernel(blk_e_ref, n_used_ref, x_ref, wg_ref, wu_ref, wd_ref, o_ref):
    i = pl.program_id(0)
    f = pl.program_id(1)

    @pl.when(f == 0)
    def _():
        o_ref[...] = jnp.zeros_like(o_ref)

    @pl.when(i < n_used_ref[0])
    def _():
        x = x_ref[...]
        gate = _dot(x, wg_ref[...])
        up = _dot(x, wu_ref[...])
        hid = (gate * _sigmoid(gate) * up).astype(BF16)
        o_ref[...] += _dot(hid, wd_ref[...])


def expert_ffn(blk_expert, n_used, xs, wg, wu, wd, *, tm=MOE_TM, tf=1024):
    slots, d = xs.shape
    ff = wg.shape[2]
    nf = ff // tf
    n_blocks = slots // tm

    def live(i, n_used):
        return jnp.minimum(i, n_used[0] - 1)

    def f_eff(i, f, n_used):
        return jnp.where(i < n_used[0], f, nf - 1)

    est = 2 * tm * d * 2 + 6 * d * tf * 2 + 2 * tm * d * 4 + 3 * tm * tf * 4
    return pl.pallas_call(
        _expert_kernel,
        grid_spec=pltpu.PrefetchScalarGridSpec(
            num_scalar_prefetch=2,
            grid=(n_blocks, nf),
            in_specs=[
                pl.BlockSpec((tm, d), lambda i, f, be, nu: (live(i, nu), 0)),
                pl.BlockSpec((None, d, tf), lambda i, f, be, nu: (be[live(i, nu)], 0, f_eff(i, f, nu))),
                pl.BlockSpec((None, d, tf), lambda i, f, be, nu: (be[live(i, nu)], 0, f_eff(i, f, nu))),
                pl.BlockSpec((None, tf, d), lambda i, f, be, nu: (be[live(i, nu)], f_eff(i, f, nu), 0)),
            ],
            out_specs=pl.BlockSpec((tm, d), lambda i, f, be, nu: (i, 0)),
        ),
        out_shape=jax.ShapeDtypeStruct((slots, d), F32),
        compiler_params=_cparams(("arbitrary", "arbitrary"), est),
        name="expert_ffn",
    )(blk_expert, n_used, xs, wg, wu, wd)


def _combine_kernel(slot_ref, y_hbm, x_ref, r_ref, g_ref, o_ref, buf_ref, sem):
    rows = x_ref.shape[0]
    base = pl.program_id(0) * rows

    def start(r, c):
        for k in range(TOP_K):
            _row_copy(y_hbm, buf_ref.at[k], sem, slot_ref[TOP_K * (base + r) + k], r).start()
        return c

    def wait(r, c):
        for k in range(TOP_K):
            _row_copy(y_hbm, buf_ref.at[k], sem, 0, r).wait()
        return c

    lax.fori_loop(0, rows, start, 0, unroll=8)
    lax.fori_loop(0, rows, wait, 0, unroll=8)
    gates = r_ref[...]
    out = x_ref[...] + gates[:, 2:3] * buf_ref[0] + gates[:, 3:4] * buf_ref[1]
    o_ref[...] = _rmsnorm(out, g_ref[...])


def combine_norm(slot_of_assignment, ys, x, routed, g_final, *, rows=GATHER_ROWS):
    n, d = x.shape
    rows = min(rows, n)
    est = 2 * rows * d * 4 + 4 * rows * d * 4 + 4 * rows * d * 4
    return pl.pallas_call(
        _combine_kernel,
        grid_spec=pltpu.PrefetchScalarGridSpec(
            num_scalar_prefetch=1,
            grid=(n // rows,),
            in_specs=[
                pl.BlockSpec(memory_space=pl.ANY),
                pl.BlockSpec((rows, d), lambda i, s: (i, 0)),
                pl.BlockSpec((rows, LANES), lambda i, s: (i, 0)),
                pl.BlockSpec((1, d), lambda i, s: (0, 0)),
            ],
            out_specs=pl.BlockSpec((rows, d), lambda i, s: (i, 0)),
            scratch_shapes=[pltpu.VMEM((TOP_K, rows, d), F32), pltpu.SemaphoreType.DMA(())],
        ),
        out_shape=jax.ShapeDtypeStruct((n, d), F32),
        compiler_params=_cparams(("arbitrary",), est),
        name="combine_norm",
    )(slot_of_assignment, ys, x, routed, g_final.reshape(1, d))


def moe_plan(routed, *, tm):
    n = routed.shape[0]
    experts = routed[:, :TOP_K].astype(jnp.int32).reshape(-1)
    onehot = (experts[:, None] == jnp.arange(N_EXPERTS)[None, :]).astype(jnp.int32)
    rank = jnp.sum((jnp.cumsum(onehot, axis=0) - onehot) * onehot, axis=1)
    counts = jnp.sum(onehot, axis=0)
    padded = ((counts + tm - 1) // tm) * tm
    ends = jnp.cumsum(padded)
    starts = ends - padded
    slot = (starts[experts] + rank).astype(jnp.int32)
    n_blocks = (n * TOP_K) // tm + N_EXPERTS
    tok = jnp.repeat(jnp.arange(n, dtype=jnp.int32), TOP_K)
    tok_of_slot = jnp.zeros((n_blocks * tm,), jnp.int32).at[slot].set(tok)
    blk_expert = jnp.minimum(
        jnp.searchsorted(ends, jnp.arange(n_blocks, dtype=jnp.int32) * tm, side="right"), N_EXPERTS - 1
    ).astype(jnp.int32)
    n_used = (ends[-1] // tm).astype(jnp.int32).reshape(1)
    return slot, tok_of_slot, blk_expert, n_used


def _prepare_weights(w_qkv, w_o, w_ff_gate, w_ff_up, w_ff_down, w_in_c, w_out_c, w_e_gate, w_e_up, w_e_down):
    d_inner = w_out_c.shape[1]
    main_cols = 2 * d_inner + 2 * SSM_GROUPS * SSM_STATE
    return dict(
        w_qkv=w_qkv[0].astype(BF16), w_o=w_o[0].astype(BF16),
        w_ff_gate=w_ff_gate[0].astype(BF16), w_ff_up=w_ff_up[0].astype(BF16), w_ff_down=w_ff_down[0].astype(BF16),
        w_in_main=w_in_c[0][:, :main_cols].astype(BF16), w_in_dt=w_in_c[0][:, main_cols:].astype(BF16),
        w_out=w_out_c[0].astype(BF16),
        w_e_gate=w_e_gate[0].astype(BF16), w_e_up=w_e_up[0].astype(BF16), w_e_down=w_e_down[0].astype(BF16),
    )


def _trunk(x3, wb, g_mix, g_ffn, rpb, conv_w, conv_b, dt_bias, a_log, d_skip, g_gate, w_router, g_final):
    batch, seq_len, d = x3.shape
    n = batch * seq_len
    x = x3.reshape(n, d)

    cos, sin = rope_tables(seq_len)
    qkv = qkv_proj(x, g_mix[0], wb["w_qkv"], cos, sin, seq_len=seq_len)
    bias_tiles = natten_bias_tiles(rpb[0], seq_len // GRID_W)
    o_a = natten(qkv, bias_tiles, batch=batch, seq_len=seq_len)
    o_b = dilated_attention(qkv, batch=batch, seq_len=seq_len)
    x = attn_out(x, o_a, o_b, wb["w_o"])
    x = ffn(x, g_ffn[0], wb["w_ff_gate"], wb["w_ff_up"], wb["w_ff_down"])

    d_inner = wb["w_out"].shape[0]
    gw = d_inner // SSM_GROUPS
    hpg = gw // SSM_HEAD_DIM
    heads = SSM_GROUPS * hpg
    zx = norm_matmul(x, g_mix[1], wb["w_in_main"], tn=gw, out_dtype=BF16)
    dt_raw = norm_matmul(x, g_mix[1], wb["w_in_dt"], tn=2 * heads, out_dtype=F32)[0]
    z_tiles = d_inner // gw
    xs = conv_silu(zx, conv_w[0], conv_b[0], width=gw, first_tile=z_tiles, n_tiles=SSM_GROUPS,
                   col_offset=0, batch=batch, seq_len=seq_len, tr=2048)
    bc_tiles = SSM_GROUPS * SSM_STATE // gw
    b_nat, b_t = conv_silu(zx, conv_w[0], conv_b[0], width=SSM_STATE, first_tile=2 * z_tiles, n_tiles=SSM_GROUPS,
                           col_offset=d_inner, batch=batch, seq_len=seq_len, tr=4096, transposed=True)
    c_nat = conv_silu(zx, conv_w[0], conv_b[0], width=SSM_STATE, first_tile=2 * z_tiles + bc_tiles,
                      n_tiles=SSM_GROUPS, col_offset=d_inner + SSM_GROUPS * SSM_STATE,
                      batch=batch, seq_len=seq_len, tr=4096)
    L = SSM_CHUNK
    rows = SSM_GROUPS * 2 * hpg
    dt_rows = dt_raw.reshape(n // L, L, 2, SSM_GROUPS, hpg).transpose(0, 3, 2, 4, 1).reshape(n // L, rows, L)

    def per_row(p):
        return p.reshape(2, SSM_GROUPS, hpg).transpose(1, 0, 2).reshape(rows, 1)

    d_skip_rows = jnp.repeat(d_skip[0].reshape(SSM_GROUPS, 1, hpg), SSM_HEAD_DIM, axis=2)
    y_f, y_b = ssd(xs, b_nat, b_t, c_nat, dt_rows, per_row(dt_bias[0]), per_row(a_log[0]), d_skip_rows,
                   batch=batch, seq_len=seq_len)
    x = mamba_out(y_f, y_b, zx, g_gate[0], wb["w_out"], x)

    routed = router(x, g_ffn[1], w_router[0])
    slot, tok_of_slot, blk_expert, n_used = moe_plan(routed, tm=MOE_TM)
    xs_sorted = gather_norm(tok_of_slot, x, g_ffn[1])
    ys = expert_ffn(blk_expert, n_used, xs_sorted, wb["w_e_gate"], wb["w_e_up"], wb["w_e_down"])
    out = combine_norm(slot, ys, x, routed, g_final)
    return out.reshape(batch, seq_len, d)


def kernel(x_prompt, x_sample, g_mix, g_ffn, w_qkv, rpb, w_o, w_ff_gate, w_ff_up, w_ff_down, w_in_c, conv_w, conv_b,
           dt_bias, a_log, d_skip, g_gate, w_out_c, w_router, w_e_gate, w_e_up, w_e_down, g_final):
    wb = _prepare_weights(w_qkv, w_o, w_ff_gate, w_ff_up, w_ff_down, w_in_c, w_out_c, w_e_gate, w_e_up, w_e_down)
    args = (wb, g_mix, g_ffn, rpb, conv_w, conv_b, dt_bias, a_log, d_skip, g_gate, w_router, g_final)
    return _trunk(x_prompt, *args), _trunk(x_sample, *args)
```

```python
import functools
import math

import jax
import jax.numpy as jnp
import numpy as np
from jax import lax
from jax.experimental import pallas as pl
from jax.experimental.pallas import tpu as pltpu

F32 = jnp.float32
BF16 = jnp.bfloat16
EPS = 1e-6
NEG_INF = float("-inf")

GRID_W = 64
HEAD_DIM = 128
N_HEADS_A = 4
N_HEADS_B_GROUP = 4
DILATIONS = (1, 4, 16)
BAND_RADIUS = 64
N_HEADS_QKV = N_HEADS_A + N_HEADS_B_GROUP * len(DILATIONS)
WIN_H = 8
WIN_W = 16
ROPE_THETA = 10000.0
SSM_HEAD_DIM = 64
SSM_GROUPS = 8
SSM_STATE = 128
SSM_CONV = 5
SSM_CHUNK = 128
N_EXPERTS = 8
TOP_K = 2

V7X_VMEM_BYTES = 64 * 1024 * 1024
LANES = 128
BF16_SUBLANES = 16

NAT_ROWS = 8
NAT_KROWS = NAT_ROWS + WIN_H - 1
DIL_TQ = 512
DIL_KC = (256, 256, 512)
DIL_PARTS = 4
MOE_TM = 1024
GATHER_ROWS = 256


def _cparams(semantics, vmem_estimate):
    limit = int(min(max(2 * vmem_estimate, 32 * 1024 * 1024), V7X_VMEM_BYTES - 8 * 1024 * 1024))
    return pltpu.CompilerParams(dimension_semantics=semantics, vmem_limit_bytes=limit)


def _rmsnorm(x, g):
    return x * lax.rsqrt(jnp.mean(x * x, axis=-1, keepdims=True) + EPS) * g


def _sigmoid(x):
    return 1.0 / (1.0 + jnp.exp(-x))


def _softplus(x):
    return jnp.maximum(x, 0.0) + jnp.log(1.0 + jnp.exp(-jnp.abs(x)))


def _dot(a, b):
    return jnp.dot(a, b, preferred_element_type=F32)


def _dot_nt(a, b):
    return lax.dot_general(a, b, (((1,), (1,)), ((), ())), preferred_element_type=F32)


def _norm_matmul_kernel(x_ref, g_ref, w_ref, o_ref, xn_ref):
    @pl.when(pl.program_id(1) == 0)
    def _():
        xn_ref[...] = _rmsnorm(x_ref[...], g_ref[...]).astype(BF16)

    o_ref[...] = _dot(xn_ref[...], w_ref[...]).astype(o_ref.dtype)


def norm_matmul(x, g, w, *, tn, out_dtype, tm=1024):
    n, k = x.shape
    m = w.shape[1]
    tm = min(tm, n)
    est = 2 * tm * k * 4 + tm * k * 2 + 2 * k * tn * 2 + 2 * tm * tn * 4
    return pl.pallas_call(
        _norm_matmul_kernel,
        grid=(n // tm, m // tn),
        in_specs=[
            pl.BlockSpec((tm, k), lambda i, j: (i, 0)),
            pl.BlockSpec((1, k), lambda i, j: (0, 0)),
            pl.BlockSpec((k, tn), lambda i, j: (0, j)),
        ],
        out_specs=pl.BlockSpec((None, tm, tn), lambda i, j: (j, i, 0)),
        out_shape=jax.ShapeDtypeStruct((m // tn, n, tn), out_dtype),
        scratch_shapes=[pltpu.VMEM((tm, k), BF16)],
        compiler_params=_cparams(("parallel", "arbitrary"), est),
        name="norm_matmul",
    )(x, g.reshape(1, k), w)


def _qkv_kernel(x_ref, g_ref, w_ref, cos_ref, sin_ref, o_ref, xn_ref, *, heads_per_tile, scale):
    j = pl.program_id(1)
    tiles_per_part = N_HEADS_QKV // heads_per_tile

    @pl.when(j == 0)
    def _():
        xn_ref[...] = _rmsnorm(x_ref[...], g_ref[...]).astype(BF16)

    use_rope = jnp.logical_and(j < 2 * tiles_per_part, j % tiles_per_part != 0)
    mult = jnp.where(j < tiles_per_part, scale, 1.0)
    c = jnp.where(use_rope, cos_ref[...], 1.0) * mult
    s = jnp.where(use_rope, sin_ref[...], 0.0) * mult
    xn = xn_ref[...]
    for pair in range(heads_per_tile // 2):
        r = _dot(xn, w_ref[:, pair * 2 * HEAD_DIM:(pair + 1) * 2 * HEAD_DIM])
        for k in range(2):
            p = r[:, k * HEAD_DIM:(k + 1) * HEAD_DIM]
            o_ref[2 * pair + k] = (p * c + pltpu.roll(p, HEAD_DIM // 2, 1) * s).astype(o_ref.dtype)


def qkv_proj(x, g, w, cos, sin, *, seq_len, tm=1024):
    n, k = x.shape
    m = w.shape[1]
    hpt = N_HEADS_A
    tn = hpt * HEAD_DIM
    tm = min(tm, seq_len)
    tiles_per_seq = seq_len // tm
    est = 2 * tm * k * 4 + tm * k * 2 + 2 * k * tn * 2 + 2 * tm * tn * 2 + 4 * tm * HEAD_DIM * 4 + tm * tn * 4
    return pl.pallas_call(
        functools.partial(_qkv_kernel, heads_per_tile=hpt, scale=HEAD_DIM ** -0.5),
        grid=(n // tm, m // tn),
        in_specs=[
            pl.BlockSpec((tm, k), lambda i, j: (i, 0)),
            pl.BlockSpec((1, k), lambda i, j: (0, 0)),
            pl.BlockSpec((k, tn), lambda i, j: (0, j)),
            pl.BlockSpec((tm, HEAD_DIM), lambda i, j: (i % tiles_per_seq, 0)),
            pl.BlockSpec((tm, HEAD_DIM), lambda i, j: (i % tiles_per_seq, 0)),
        ],
        out_specs=pl.BlockSpec((hpt, tm, HEAD_DIM), lambda i, j: (j, i, 0)),
        out_shape=jax.ShapeDtypeStruct((m // HEAD_DIM, n, HEAD_DIM), BF16),
        scratch_shapes=[pltpu.VMEM((tm, k), BF16)],
        compiler_params=_cparams(("parallel", "arbitrary"), est),
        name="qkv_proj",
    )(x, g.reshape(1, k), w, cos, sin)


def rope_tables(seq_len):
    half = HEAD_DIM // 2
    inv = ROPE_THETA ** (-jnp.arange(half, dtype=F32) / half)
    ang = jnp.arange(seq_len, dtype=F32)[:, None] * inv[None, :]
    cos = jnp.cos(ang)
    sin = jnp.sin(ang)
    return jnp.concatenate([cos, cos], axis=1), jnp.concatenate([-sin, sin], axis=1)


def _natten_kernel(q_ref, k_ref, v_ref, bias_ref, o_ref, *, n_blocks, grid_rows):
    blk = pl.program_id(2)
    first_row = jnp.where(blk == 0, 0,
                          jnp.where(blk == n_blocks - 1, grid_rows - NAT_KROWS, blk * NAT_ROWS - WIN_H // 2))
    start = pl.multiple_of(first_row * GRID_W, GRID_W)
    kw = k_ref[pl.ds(start, NAT_KROWS * GRID_W), :]
    vw = v_ref[pl.ds(start, NAT_KROWS * GRID_W), :]
    s = _dot_nt(q_ref[...], kw) + bias_ref[...]
    m = jnp.max(s, axis=-1, keepdims=True)
    p = jnp.exp(s - m)
    l = jnp.sum(p, axis=-1, keepdims=True)
    o_ref[...] = (_dot(p.astype(BF16), vw) / l).astype(o_ref.dtype)


def natten_bias_tiles(rpb, grid_rows):
    n_blocks = grid_rows // NAT_ROWS
    r0s = np.array([0, NAT_ROWS, (n_blocks - 1) * NAT_ROWS])
    k0s = np.array([0, NAT_ROWS - WIN_H // 2, grid_rows - NAT_KROWS])
    r = r0s[:, None] + np.arange(NAT_ROWS)[None, :]
    kr = k0s[:, None] + np.arange(NAT_KROWS)[None, :]
    rs = np.clip(r - WIN_H // 2, 0, grid_rows - WIN_H)
    row_ok = (kr[:, None, :] >= rs[:, :, None]) & (kr[:, None, :] < rs[:, :, None] + WIN_H)
    d_row = np.clip(kr[:, None, :] - r[:, :, None] + (WIN_H - 1), 0, 2 * WIN_H - 2)
    c = np.arange(GRID_W)
    cs = np.clip(c - WIN_W // 2, 0, GRID_W - WIN_W)
    col_ok = (c[None, :] >= cs[:, None]) & (c[None, :] < cs[:, None] + WIN_W)
    d_col = np.clip(c[None, :] - c[:, None] + (WIN_W - 1), 0, 2 * WIN_W - 2)
    sel_row = (d_row[..., None] == np.arange(2 * WIN_H - 1)).astype(np.float32)
    sel_col = (d_col[..., None] == np.arange(2 * WIN_W - 1)).astype(np.float32)
    rows = jnp.einsum("tikr,hrc->thikc", sel_row, rpb.astype(F32), precision=lax.Precision.HIGHEST)
    bias = jnp.einsum("thikc,qwc->thiqkw", rows, sel_col, precision=lax.Precision.HIGHEST)
    ok = row_ok[:, :, None, :, None] & col_ok[None, None, :, None, :]
    bias = jnp.where(ok[:, None], bias, NEG_INF)
    return bias.reshape(3, rpb.shape[0], NAT_ROWS * GRID_W, NAT_KROWS * GRID_W)


def natten(qkv, bias_tiles, *, batch, seq_len):
    n = batch * seq_len
    grid_rows = seq_len // GRID_W
    n_blocks = grid_rows // NAT_ROWS
    tq = NAT_ROWS * GRID_W
    tk = NAT_KROWS * GRID_W

    def tile_kind(blk):
        return jnp.where(blk == 0, 0, jnp.where(blk == n_blocks - 1, 2, 1))

    est = 4 * seq_len * HEAD_DIM * 2 + 2 * tq * tk * 4 + 3 * tq * tk * 4
    return pl.pallas_call(
        functools.partial(_natten_kernel, n_blocks=n_blocks, grid_rows=grid_rows),
        grid=(batch, N_HEADS_A, n_blocks),
        in_specs=[
            pl.BlockSpec((None, tq, HEAD_DIM), lambda b, h, i: (h, b * n_blocks + i, 0)),
            pl.BlockSpec((None, seq_len, HEAD_DIM), lambda b, h, i: (N_HEADS_QKV + h, b, 0)),
            pl.BlockSpec((None, seq_len, HEAD_DIM), lambda b, h, i: (2 * N_HEADS_QKV + h, b, 0)),
            pl.BlockSpec((None, None, tq, tk), lambda b, h, i: (tile_kind(i), h, 0, 0)),
        ],
        out_specs=pl.BlockSpec((tq, HEAD_DIM), lambda b, h, i: (b * n_blocks + i, h)),
        out_shape=jax.ShapeDtypeStruct((n, N_HEADS_A * HEAD_DIM), BF16),
        compiler_params=_cparams(("parallel", "parallel", "arbitrary"), est),
        name="natten",
    )(qkv, qkv, qkv, bias_tiles)


class _DilatedGeometry:
    def __init__(self, dil, kc):
        self.dil, self.kc = dil, kc
        self.reach = BAND_RADIUS * dil
        self.tp = DIL_TQ // DIL_PARTS
        assert kc % self.tp == 0
        self.halo = -(-self.reach // kc) * kc
        self.n_chunks = (DIL_TQ + 2 * self.halo) // kc
        self.n_tiles = (kc // self.tp) * (self.n_chunks - 1) + DIL_PARTS

    def tile_index(self, chunk, part):
        return (self.kc // self.tp) * chunk + (DIL_PARTS - 1 - part)

    def bias_tiles(self):
        row = np.arange(self.tp)[:, None]
        col = np.arange(self.kc)[None, :]
        tiles = []
        for u in range(self.n_tiles):
            rel = col - row + (u - (DIL_PARTS - 1)) * self.tp - self.halo
            ok = (np.abs(rel) <= self.reach) & (rel % self.dil == 0)
            tiles.append(np.where(ok, 0.0, -np.inf))
        return np.stack(tiles).astype(np.float32)


DIL_GEOMETRY = tuple(_DilatedGeometry(d, kc) for d, kc in zip(DILATIONS, DIL_KC))


def _dilated_kernel(b0, b1, b2, q0, q1, q2, k0, k1, k2, v0, v1, v2, o_ref, *, seq_len):
    tq = DIL_TQ
    parts = DIL_PARTS
    tp = tq // parts
    t0 = pl.program_id(2) * tq
    carry = tuple((jnp.full((tp, 1), -1e30, F32), jnp.zeros((tp, 1), F32), jnp.zeros((tp, HEAD_DIM), F32))
                  for _ in range(parts))
    for geo, bias_ref, q_ref, k_ref, v_ref in zip(DIL_GEOMETRY, (b0, b1, b2), (q0, q1, q2), (k0, k1, k2), (v0, v1, v2)):
        kc = geo.kc
        window_start = t0 - geo.halo
        lo = jnp.maximum(0, (geo.halo - t0) // kc)
        hi = jnp.minimum(geo.n_chunks, (seq_len - window_start) // kc)
        qs = tuple(q_ref[p * tp:(p + 1) * tp, :] for p in range(parts))

        def body(ci, carry, geo=geo, bias_ref=bias_ref, k_ref=k_ref, v_ref=v_ref, window_start=window_start, qs=qs):
            ks = pl.multiple_of(window_start + ci * geo.kc, geo.kc)
            kk = k_ref[pl.ds(ks, geo.kc), :]
            vv = v_ref[pl.ds(ks, geo.kc), :]
            scores = [_dot_nt(qs[p], kk) + bias_ref[geo.tile_index(ci, p)] for p in range(parts)]
            stats = []
            for p in range(parts):
                m, l, _ = carry[p]
                m_new = jnp.maximum(m, jnp.max(scores[p], axis=-1, keepdims=True))
                alpha = jnp.exp(m - m_new)
                e = jnp.exp(scores[p] - m_new)
                stats.append((m_new, alpha, alpha * l + jnp.sum(e, axis=-1, keepdims=True), e.astype(BF16)))
            return tuple((m_new, l, alpha * carry[p][2] + _dot(e, vv))
                         for p, (m_new, alpha, l, e) in enumerate(stats))

        carry = lax.fori_loop(lo, hi, body, carry)
    for p in range(parts):
        _, l, acc = carry[p]
        o_ref[p * tp:(p + 1) * tp, :] = (acc / l).astype(o_ref.dtype)


def dilated_attention(qkv, *, batch, seq_len):
    n = batch * seq_len
    nq = seq_len // DIL_TQ
    biases = [geo.bias_tiles() for geo in DIL_GEOMETRY]

    def q_spec(g):
        return pl.BlockSpec((None, DIL_TQ, HEAD_DIM),
                            lambda b, j, i: (N_HEADS_A + N_HEADS_B_GROUP * g + j, b * nq + i, 0))

    def kv_spec(part, g):
        return pl.BlockSpec((None, seq_len, HEAD_DIM),
                            lambda b, j, i: (part * N_HEADS_QKV + N_HEADS_A + N_HEADS_B_GROUP * g + j, b, 0))

    est = 12 * seq_len * HEAD_DIM * 2 + 8 * DIL_TQ * max(DIL_KC) * 4 + 2 * sum(b.size for b in biases) * 4
    groups = range(len(DILATIONS))
    return pl.pallas_call(
        functools.partial(_dilated_kernel, seq_len=seq_len),
        grid=(batch, N_HEADS_B_GROUP, nq),
        in_specs=[pl.BlockSpec(b.shape, lambda b_, j, i: (0, 0, 0)) for b in biases]
        + [q_spec(g) for g in groups] + [kv_spec(1, g) for g in groups] + [kv_spec(2, g) for g in groups],
        out_specs=pl.BlockSpec((DIL_TQ, HEAD_DIM), lambda b, j, i: (b * nq + i, j)),
        out_shape=jax.ShapeDtypeStruct((n, N_HEADS_B_GROUP * HEAD_DIM), BF16),
        compiler_params=_cparams(("parallel", "parallel", "arbitrary"), est),
        name="dilated_attention",
    )(*biases, *([qkv] * 9))


def _attn_out_kernel(x_ref, oa_ref, ob_ref, w_ref, o_ref):
    ka = oa_ref.shape[1]
    o_ref[...] = x_ref[...] + _dot(oa_ref[...], w_ref[:ka, :]) + _dot(ob_ref[...], w_ref[ka:, :])


def attn_out(x, o_a, o_b, w, *, tm=512):
    n, d = x.shape
    ka, kb = o_a.shape[1], o_b.shape[1]
    tm = min(tm, n)
    est = 4 * tm * d * 4 + 2 * (ka + kb) * d * 2 + 2 * tm * (ka + kb) * 2
    return pl.pallas_call(
        _attn_out_kernel,
        grid=(n // tm,),
        in_specs=[
            pl.BlockSpec((tm, d), lambda i: (i, 0)),
            pl.BlockSpec((tm, ka), lambda i: (i, 0)),
            pl.BlockSpec((tm, kb), lambda i: (i, 0)),
            pl.BlockSpec((ka + kb, d), lambda i: (0, 0)),
        ],
        out_specs=pl.BlockSpec((tm, d), lambda i: (i, 0)),
        out_shape=jax.ShapeDtypeStruct((n, d), F32),
        compiler_params=_cparams(("parallel",), est),
        name="attn_out",
    )(x, o_a, o_b, w)


def _ffn_kernel(x_ref, g_ref, wg_ref, wu_ref, wd_ref, o_ref, xn_ref):
    f = pl.program_id(1)

    @pl.when(f == 0)
    def _():
        xn_ref[...] = _rmsnorm(x_ref[...], g_ref[...]).astype(BF16)
        o_ref[...] = x_ref[...]

    xn = xn_ref[...]
    gate = _dot(xn, wg_ref[...])
    up = _dot(xn, wu_ref[...])
    hid = (gate * _sigmoid(gate) * up).astype(BF16)
    o_ref[...] += _dot(hid, wd_ref[...])


def ffn(x, g, wg, wu, wd, *, tm=1024, tf=512):
    n, d = x.shape
    ff = wg.shape[1]
    tm = min(tm, n)
    est = 3 * tm * d * 4 + tm * d * 2 + 6 * d * tf * 2 + 3 * tm * tf * 4
    return pl.pallas_call(
        _ffn_kernel,
        grid=(n // tm, ff // tf),
        in_specs=[
            pl.BlockSpec((tm, d), lambda i, f: (i, 0), pipeline_mode=pl.Buffered(1)),
            pl.BlockSpec((1, d), lambda i, f: (0, 0)),
            pl.BlockSpec((d, tf), lambda i, f: (0, f)),
            pl.BlockSpec((d, tf), lambda i, f: (0, f)),
            pl.BlockSpec((tf, d), lambda i, f: (f, 0)),
        ],
        out_specs=pl.BlockSpec((tm, d), lambda i, f: (i, 0)),
        out_shape=jax.ShapeDtypeStruct((n, d), F32),
        scratch_shapes=[pltpu.VMEM((tm, d), BF16)],
        compiler_params=_cparams(("parallel", "arbitrary"), est),
        name="ffn",
    )(x, g.reshape(1, d), wg, wu, wd)


def _conv_kernel(xm_ref, xp_ref, xn_ref, w_ref, b_ref, o_ref, *rest, tr, n_row_blocks):
    ext_ref = rest[-1]
    i = pl.program_id(2)
    hb = BF16_SUBLANES
    pad = SSM_CONV // 2
    ext_ref[0:hb, :] = jnp.where(i > 0, xp_ref[...].astype(F32), 0.0)
    ext_ref[hb:hb + tr, :] = xm_ref[...].astype(F32)
    ext_ref[hb + tr:2 * hb + tr, :] = jnp.where(i < n_row_blocks - 1, xn_ref[...].astype(F32), 0.0)
    acc = jnp.broadcast_to(b_ref[...], o_ref.shape)
    for k in range(SSM_CONV):
        acc = acc + ext_ref[pl.ds(hb - pad + k, tr), :] * w_ref[k:k + 1, :]
    out = acc * _sigmoid(acc)
    o_ref[...] = out.astype(o_ref.dtype)
    if len(rest) == 2:
        ot_ref = rest[0]
        L = ot_ref.shape[-1]
        for c in range(tr // L):
            ot_ref[c] = out[c * L:(c + 1) * L, :].T.astype(ot_ref.dtype)


def conv_silu(zx, conv_w, conv_b, *, width, first_tile, n_tiles, col_offset, batch, seq_len, tr, transposed=False):
    n = batch * seq_len
    tile_w = zx.shape[2]
    per = tile_w // width
    tr = min(tr, seq_len)
    nr = seq_len // tr
    hb = BF16_SUBLANES
    seq_hb = seq_len // hb
    n_hb = n // hb

    def main_map(c, b, i):
        return (first_tile + c // per, b * nr + i, c % per)

    def prev_map(c, b, i):
        return (first_tile + c // per, jnp.maximum(b * seq_hb + i * (tr // hb) - 1, 0), c % per)

    def next_map(c, b, i):
        return (first_tile + c // per, jnp.minimum(b * seq_hb + (i + 1) * (tr // hb), n_hb - 1), c % per)

    L = SSM_CHUNK
    out_specs = [pl.BlockSpec((None, tr, width), lambda c, b, i: (c, b * nr + i, 0))]
    out_shape = [jax.ShapeDtypeStruct((n_tiles, n, width), BF16)]
    if transposed:
        out_specs.append(pl.BlockSpec((None, tr // L, width, L), lambda c, b, i: (c, b * nr + i, 0, 0)))
        out_shape.append(jax.ShapeDtypeStruct((n_tiles, n // L, width, L), BF16))
    est = 4 * tr * width * 2 + (tr + 2 * hb) * width * 4 + 4 * tr * width * 4 + 4 * tr * width * 2
    outs = pl.pallas_call(
        functools.partial(_conv_kernel, tr=tr, n_row_blocks=nr),
        grid=(n_tiles, batch, nr),
        in_specs=[
            pl.BlockSpec((None, tr, width), main_map),
            pl.BlockSpec((None, hb, width), prev_map),
            pl.BlockSpec((None, hb, width), next_map),
            pl.BlockSpec((SSM_CONV, width), lambda c, b, i: (0, col_offset // width + c)),
            pl.BlockSpec((1, width), lambda c, b, i: (0, col_offset // width + c)),
        ],
        out_specs=out_specs,
        out_shape=out_shape,
        scratch_shapes=[pltpu.VMEM((tr + 2 * hb, width), F32)],
        compiler_params=_cparams(("parallel", "parallel", "arbitrary"), est),
        name="conv_silu",
    )(zx, zx, zx, conv_w, conv_b.reshape(1, -1))
    return outs if transposed else outs[0]


def _lane_cumsum(a):
    lane = lax.broadcasted_iota(jnp.int32, a.shape, 1)
    shift = 1
    while shift < a.shape[1]:
        a = a + jnp.where(lane >= shift, pltpu.roll(a, shift, 1), 0.0)
        shift *= 2
    return a


def _ssd_decay_rows(dt_ref, dtbias_ref, alog_ref, seg_scr, dt_scr, *, hpg):
    L = SSM_CHUNK
    dt_all = _softplus(dt_ref[...] + dtbias_ref[...])
    a_all = dt_all * (-jnp.exp(alog_ref[...]))
    cum = _lane_cumsum(a_all)
    suf = cum[:, L - 1:L] - cum + a_all
    row = lax.broadcasted_iota(jnp.int32, cum.shape, 0)
    seg_scr[...] = jnp.where((row & (2 * hpg - 1)) < hpg, cum, suf)
    dt_scr[...] = dt_all


def _ssd_kernel(xa_ref, ba_ref, bta_ref, ca_ref, dta_ref, xb_ref, btb_ref, cb_ref, dtb_ref,
                dtbias_ref, alog_ref, dskip_ref, yf_ref, yb_ref,
                sf_ref, sb_ref, sega_scr, dta_scr, segb_scr, dtb_scr, *, hpg):
    L = SSM_CHUNK
    P = SSM_HEAD_DIM
    pairs = hpg // 2

    @pl.when(pl.program_id(1) == 0)
    def _():
        sf_ref[...] = jnp.zeros_like(sf_ref)
        sb_ref[...] = jnp.zeros_like(sb_ref)

    _ssd_decay_rows(dta_ref, dtbias_ref, alog_ref, sega_scr, dta_scr, hpg=hpg)
    _ssd_decay_rows(dtb_ref, dtbias_ref, alog_ref, segb_scr, dtb_scr, hpg=hpg)

    li = lax.broadcasted_iota(jnp.int32, (L, L), 0)
    si = lax.broadcasted_iota(jnp.int32, (L, L), 1)
    causal = li >= si
    anti = li <= si
    low_lanes = lax.broadcasted_iota(jnp.int32, (L, 2 * P), 1) < P
    high_lanes = jnp.logical_not(low_lanes)
    low_lanes_row = lax.broadcasted_iota(jnp.int32, (1, 2 * P), 1) < P

    def lanes_of(col):
        return jnp.broadcast_to(col, (L, 2 * P))

    def columns(seg_r, dt_r):
        stacked = jnp.concatenate([seg_r, dt_r, jnp.zeros((L - 4 * hpg, L), F32)], axis=0)
        return stacked.T

    def advance_state(s_ref, g, lanes, state, bt, xpf, cols, h0, h1, tot_row):
        tot0, tot1 = cols[tot_row:tot_row + 1, h0:h0 + 1], cols[tot_row:tot_row + 1, h1:h1 + 1]
        c0 = jnp.exp(tot0 - cols[:, h0:h0 + 1]) * cols[:, 2 * hpg + h0:2 * hpg + h0 + 1]
        c1 = jnp.exp(tot1 - cols[:, h1:h1 + 1]) * cols[:, 2 * hpg + h1:2 * hpg + h1 + 1]
        xw = (xpf * jnp.where(low_lanes, lanes_of(c0), lanes_of(c1))).astype(BF16)
        decay = jnp.where(low_lanes_row, jnp.exp(tot0), jnp.exp(tot1))
        s_ref[g, :, lanes] = state[:, lanes] * decay + _dot(bt, xw)

    def group_body(g, c):
        r0 = pl.multiple_of(g * 2 * hpg, 2 * hpg)

        seg_r = sega_scr[pl.ds(r0, 2 * hpg), :]
        dt_r = dta_scr[pl.ds(r0, 2 * hpg), :]
        cols = columns(seg_r, dt_r)
        xg = xa_ref[g]
        cg = ca_ref[g]
        state = sf_ref[g]
        carried = _dot(cg, state.astype(BF16))
        cbm = _dot_nt(cg, ba_ref[g])
        bt = bta_ref[g]
        for p in range(pairs):
            lanes = slice(p * 2 * P, (p + 1) * 2 * P)
            xpf = xg[:, lanes].astype(F32)
            h0, h1 = 2 * p, 2 * p + 1
            f0, f1 = lanes_of(cols[:, h0:h0 + 1]), lanes_of(cols[:, h1:h1 + 1])
            y = carried[:, lanes] * jnp.where(low_lanes, jnp.exp(f0), jnp.exp(f1)) + dskip_ref[g][:, lanes] * xpf
            for hh, f_cols, keep in ((h0, f0, low_lanes), (h1, f1, high_lanes)):
                wf = jnp.exp(jnp.where(causal, f_cols - seg_r[hh:hh + 1, :], NEG_INF)) * dt_r[hh:hh + 1, :]
                b_cols = lanes_of(cols[:, hpg + hh:hpg + hh + 1])
                wb = jnp.exp(jnp.where(anti, b_cols - seg_r[hpg + hh:hpg + hh + 1, :], NEG_INF)) \
                    * dt_r[hpg + hh:hpg + hh + 1, :]
                w = (cbm * (wf + wb)).astype(BF16)
                y = y + _dot(w, jnp.where(keep, xpf, 0.0).astype(BF16))
            yf_ref[g, :, lanes] = y.astype(yf_ref.dtype)
            advance_state(sf_ref, g, lanes, state, bt, xpf, cols, h0, h1, L - 1)

        cols = columns(segb_scr[pl.ds(r0, 2 * hpg), :], dtb_scr[pl.ds(r0, 2 * hpg), :])
        xg = xb_ref[g]
        state = sb_ref[g]
        carried = _dot(cb_ref[g], state.astype(BF16))
        bt = btb_ref[g]
        for p in range(pairs):
            lanes = slice(p * 2 * P, (p + 1) * 2 * P)
            xpf = xg[:, lanes].astype(F32)
            h0, h1 = hpg + 2 * p, hpg + 2 * p + 1
            e0, e1 = jnp.exp(cols[:, h0:h0 + 1]), jnp.exp(cols[:, h1:h1 + 1])
            y = carried[:, lanes] * jnp.where(low_lanes, lanes_of(e0), lanes_of(e1))
            yb_ref[g, :, lanes] = y.astype(yb_ref.dtype)
            advance_state(sb_ref, g, lanes, state, bt, xpf, cols, h0, h1, 0)
        return c

    lax.fori_loop(0, SSM_GROUPS, group_body, 0)


def ssd(xs, b_nat, b_t, c_nat, dt_rows, dt_bias_rows, a_log_rows, d_skip_rows, *, batch, seq_len):
    groups, n, gw = xs.shape
    hpg = gw // SSM_HEAD_DIM
    assert hpg & (hpg - 1) == 0
    L = SSM_CHUNK
    nc = seq_len // L
    rows = groups * 2 * hpg

    def fwd(b, k):
        return b * nc + k

    def bwd(b, k):
        return b * nc + nc - 1 - k

    def specs(chunk, with_b_nat):
        out = [pl.BlockSpec((groups, L, gw), lambda b, k: (0, chunk(b, k), 0))]
        if with_b_nat:
            out.append(pl.BlockSpec((groups, L, SSM_STATE), lambda b, k: (0, chunk(b, k), 0)))
        out += [
            pl.BlockSpec((groups, None, SSM_STATE, L), lambda b, k: (0, chunk(b, k), 0, 0)),
            pl.BlockSpec((groups, L, SSM_STATE), lambda b, k: (0, chunk(b, k), 0)),
            pl.BlockSpec((None, rows, L), lambda b, k: (chunk(b, k), 0, 0)),
        ]
        return out

    const = [
        pl.BlockSpec((rows, 1), lambda b, k: (0, 0)),
        pl.BlockSpec((rows, 1), lambda b, k: (0, 0)),
        pl.BlockSpec((groups, 1, gw), lambda b, k: (0, 0, 0)),
    ]
    y_shape = jax.ShapeDtypeStruct((groups, n, gw), BF16)
    est = 4 * (groups * L * gw * 2 + 3 * groups * L * SSM_STATE * 2 + rows * L * 4) + 4 * groups * L * gw * 2 \
        + 2 * groups * SSM_STATE * gw * 4 + 4 * rows * L * 4 + 64 * L * L * 4
    return pl.pallas_call(
        functools.partial(_ssd_kernel, hpg=hpg),
        grid=(batch, nc),
        in_specs=specs(fwd, True) + specs(bwd, False) + const,
        out_specs=[pl.BlockSpec((groups, L, gw), lambda b, k: (0, fwd(b, k), 0)),
                   pl.BlockSpec((groups, L, gw), lambda b, k: (0, bwd(b, k), 0))],
        out_shape=[y_shape, y_shape],
        scratch_shapes=[pltpu.VMEM((groups, SSM_STATE, gw), F32), pltpu.VMEM((groups, SSM_STATE, gw), F32)]
        + [pltpu.VMEM((rows, L), F32)] * 4,
        compiler_params=_cparams(("parallel", "arbitrary"), est),
        name="ssd",
    )(xs, b_nat, b_t, c_nat, dt_rows, xs, b_t, c_nat, dt_rows, dt_bias_rows, a_log_rows, d_skip_rows)


def _mamba_out_kernel(yf_ref, yb_ref, z_ref, gg_ref, w_ref, x_ref, o_ref, ssq_ref, *, d_inner):
    step = pl.program_id(1)
    groups_per_step, _, gw = z_ref.shape

    @pl.when(step == 0)
    def _():
        o_ref[...] = jnp.zeros_like(o_ref)
        ssq_ref[...] = jnp.zeros_like(ssq_ref)

    gated = []
    for k in range(groups_per_step):
        z = z_ref[k].astype(F32)
        yz = (yf_ref[k].astype(F32) + yb_ref[k].astype(F32)) * (z * _sigmoid(z))
        ssq_ref[...] += jnp.sum(yz * yz, axis=-1, keepdims=True)
        gated.append((yz * gg_ref[:, k * gw:(k + 1) * gw]).astype(BF16))
    o_ref[...] += _dot(jnp.concatenate(gated, axis=1), w_ref[...])

    @pl.when(step == pl.num_programs(1) - 1)
    def _():
        o_ref[...] = x_ref[...] + o_ref[...] * lax.rsqrt(ssq_ref[...] / d_inner + EPS)


def mamba_out(y_f, y_b, zx, g_gate, w_out, x, *, tm=1024, groups_per_step=2):
    groups, n, gw = y_f.shape
    d = x.shape[1]
    tm = min(tm, n)
    gps = groups_per_step
    est = gps * (6 * tm * gw * 2 + 2 * gw * d * 2) + 3 * tm * d * 4 + 2 * tm * gps * gw * 4
    return pl.pallas_call(
        functools.partial(_mamba_out_kernel, d_inner=groups * gw),
        grid=(n // tm, groups // gps),
        in_specs=[
            pl.BlockSpec((gps, tm, gw), lambda i, s: (s, i, 0)),
            pl.BlockSpec((gps, tm, gw), lambda i, s: (s, i, 0)),
            pl.BlockSpec((gps, tm, gw), lambda i, s: (s, i, 0)),
            pl.BlockSpec((1, gps * gw), lambda i, s: (0, s)),
            pl.BlockSpec((gps * gw, d), lambda i, s: (s, 0)),
            pl.BlockSpec((tm, d), lambda i, s: (i, 0), pipeline_mode=pl.Buffered(1)),
        ],
        out_specs=pl.BlockSpec((tm, d), lambda i, s: (i, 0)),
        out_shape=jax.ShapeDtypeStruct((n, d), F32),
        scratch_shapes=[pltpu.VMEM((tm, 1), F32)],
        compiler_params=_cparams(("parallel", "arbitrary"), est),
        name="mamba_out",
    )(y_f, y_b, zx, g_gate.reshape(1, -1), w_out, x)


def _router_kernel(xa_ref, xb_ref, g_ref, w_ref, o_ref, xn_ref, *, tiles_a):
    i = pl.program_id(0)

    @pl.when(i < tiles_a)
    def _():
        xn_ref[...] = _rmsnorm(xa_ref[...], g_ref[...])

    @pl.when(i >= tiles_a)
    def _():
        xn_ref[...] = _rmsnorm(xb_ref[...], g_ref[...])

    xn = xn_ref[...]
    logits = jnp.dot(xn, w_ref[...], preferred_element_type=F32, precision=lax.Precision.HIGHEST)
    lane = lax.broadcasted_iota(jnp.int32, logits.shape, 1)
    logits = jnp.where(lane < N_EXPERTS, logits, NEG_INF)
    v1 = jnp.max(logits, axis=-1, keepdims=True)
    i1 = jnp.min(jnp.where(logits == v1, lane, LANES), axis=-1, keepdims=True)
    rest = jnp.where(lane == i1, NEG_INF, logits)
    v2 = jnp.max(rest, axis=-1, keepdims=True)
    i2 = jnp.min(jnp.where(rest == v2, lane, LANES), axis=-1, keepdims=True)
    e2 = jnp.exp(v2 - v1)
    g1 = 1.0 / (1.0 + e2)
    g2 = e2 / (1.0 + e2)
    out = jnp.where(lane == 0, i1.astype(F32),
                    jnp.where(lane == 1, i2.astype(F32),
                              jnp.where(lane == 2, g1, jnp.where(lane == 3, g2, 0.0))))
    o_ref[...] = out


def router(x_a, x_b, g, w_router, *, tm=512):
    (na, d), nb = x_a.shape, x_b.shape[0]
    tm = min(tm, na, nb)
    tiles_a, tiles_b = na // tm, nb // tm
    w = jnp.zeros((d, LANES), F32).at[:, :w_router.shape[1]].set(w_router)
    est = 4 * tm * d * 4 + 2 * d * LANES * 4 + 2 * tm * LANES * 4 + 3 * tm * d * 4
    return pl.pallas_call(
        functools.partial(_router_kernel, tiles_a=tiles_a),
        grid=(tiles_a + tiles_b,),
        in_specs=[
            pl.BlockSpec((tm, d), lambda i: (jnp.minimum(i, tiles_a - 1), 0)),
            pl.BlockSpec((tm, d), lambda i: (jnp.maximum(i - tiles_a, 0), 0)),
            pl.BlockSpec((1, d), lambda i: (0, 0)),
            pl.BlockSpec((d, LANES), lambda i: (0, 0)),
        ],
        out_specs=[pl.BlockSpec((tm, LANES), lambda i: (i, 0)), pl.BlockSpec((tm, d), lambda i: (i, 0))],
        out_shape=[jax.ShapeDtypeStruct((na + nb, LANES), F32), jax.ShapeDtypeStruct((na + nb, d), F32)],
        compiler_params=_cparams(("parallel",), est),
        name="router",
    )(x_a, x_b, g.reshape(1, d), w)


def _row_copy(src_hbm, dst_vmem, sem, src_row, dst_row):
    return pltpu.make_async_copy(src_hbm.at[pl.ds(src_row, 1), :], dst_vmem.at[pl.ds(dst_row, 1), :], sem)


def _gather_kernel(tok_ref, x_hbm, o_ref, buf_ref, sem):
    rows = buf_ref.shape[0]
    base = pl.program_id(0) * rows

    def start(r, c):
        _row_copy(x_hbm, buf_ref, sem, tok_ref[base + r], r).start()
        return c

    def wait(r, c):
        _row_copy(x_hbm, buf_ref, sem, 0, r).wait()
        return c

    lax.fori_loop(0, rows, start, 0, unroll=8)
    lax.fori_loop(0, rows, wait, 0, unroll=8)
    o_ref[...] = buf_ref[...].astype(o_ref.dtype)


def gather_rows(tok_of_slot, x, *, rows=GATHER_ROWS):
    slots = tok_of_slot.shape[0]
    d = x.shape[1]
    est = rows * d * 4 * 2 + 2 * rows * d * 2
    return pl.pallas_call(
        _gather_kernel,
        grid_spec=pltpu.PrefetchScalarGridSpec(
            num_scalar_prefetch=1,
            grid=(slots // rows,),
            in_specs=[pl.BlockSpec(memory_space=pl.ANY)],
            out_specs=pl.BlockSpec((rows, d), lambda i, tok: (i, 0)),
            scratch_shapes=[pltpu.VMEM((rows, d), F32), pltpu.SemaphoreType.DMA(())],
        ),
        out_shape=jax.ShapeDtypeStruct((slots, d), BF16),
        compiler_params=_cparams(("arbitrary",), est),
        name="gather_rows",
    )(tok_of_slot, x)


def _expert_kernel(blk_e_ref, n_used_ref, x_ref, wg_ref, wu_ref, wd_ref, o_ref):
    i = pl.program_id(0)
    f = pl.program_id(1)

    @pl.when(f == 0)
    def _():
        o_ref[...] = jnp.zeros_like(o_ref)

    @pl.when(i < n_used_ref[0])
    def _():
        x = x_ref[...]
        gate = _dot(x, wg_ref[...].astype(BF16))
        up = _dot(x, wu_ref[...].astype(BF16))
        hid = (gate * _sigmoid(gate) * up).astype(BF16)
        o_ref[...] += _dot(hid, wd_ref[...].astype(BF16))


def expert_ffn(blk_expert, n_used, xs, wg, wu, wd, *, tm=MOE_TM, tf=512):
    slots, d = xs.shape
    ff = wg.shape[2]
    nf = ff // tf
    n_blocks = slots // tm

    def live(i, n_used):
        return jnp.minimum(i, n_used[0] - 1)

    def f_eff(i, f, n_used):
        return jnp.where(i < n_used[0], f, nf - 1)

    wbytes = wg.dtype.itemsize
    est = tm * d * 2 + 6 * d * tf * wbytes + 3 * d * tf * 2 + 2 * tm * d * 4 + 3 * tm * tf * 4
    return pl.pallas_call(
        _expert_kernel,
        grid_spec=pltpu.PrefetchScalarGridSpec(
            num_scalar_prefetch=2,
            grid=(n_blocks, nf),
            in_specs=[
                pl.BlockSpec((tm, d), lambda i, f, be, nu: (live(i, nu), 0), pipeline_mode=pl.Buffered(1)),
                pl.BlockSpec((None, d, tf), lambda i, f, be, nu: (be[live(i, nu)], 0, f_eff(i, f, nu))),
                pl.BlockSpec((None, d, tf), lambda i, f, be, nu: (be[live(i, nu)], 0, f_eff(i, f, nu))),
                pl.BlockSpec((None, tf, d), lambda i, f, be, nu: (be[live(i, nu)], f_eff(i, f, nu), 0)),
            ],
            out_specs=pl.BlockSpec((tm, d), lambda i, f, be, nu: (i, 0)),
        ),
        out_shape=jax.ShapeDtypeStruct((slots, d), F32),
        compiler_params=_cparams(("arbitrary", "arbitrary"), est),
        name="expert_ffn",
    )(blk_expert, n_used, xs, wg, wu, wd)


def _combine_kernel(slot_ref, y_hbm, x_ref, r_ref, g_ref, o_ref, buf_ref, sem):
    rows = x_ref.shape[0]
    base = pl.program_id(0) * rows

    def start(r, c):
        for k in range(TOP_K):
            _row_copy(y_hbm, buf_ref.at[k], sem, slot_ref[TOP_K * (base + r) + k], r).start()
        return c

    def wait(r, c):
        for k in range(TOP_K):
            _row_copy(y_hbm, buf_ref.at[k], sem, 0, r).wait()
        return c

    lax.fori_loop(0, rows, start, 0, unroll=8)
    lax.fori_loop(0, rows, wait, 0, unroll=8)
    gates = r_ref[...]
    out = x_ref[...] + gates[:, 2:3] * buf_ref[0] + gates[:, 3:4] * buf_ref[1]
    o_ref[...] = _rmsnorm(out, g_ref[...])


def combine_norm(slot_of_assignment, ys, x, routed, g_final, *, rows=GATHER_ROWS):
    n, d = x.shape
    rows = min(rows, n)
    est = 2 * rows * d * 4 + 4 * rows * d * 4 + 4 * rows * d * 4
    return pl.pallas_call(
        _combine_kernel,
        grid_spec=pltpu.PrefetchScalarGridSpec(
            num_scalar_prefetch=1,
            grid=(n // rows,),
            in_specs=[
                pl.BlockSpec(memory_space=pl.ANY),
                pl.BlockSpec((rows, d), lambda i, s: (i, 0)),
                pl.BlockSpec((rows, LANES), lambda i, s: (i, 0)),
                pl.BlockSpec((1, d), lambda i, s: (0, 0)),
            ],
            out_specs=pl.BlockSpec((rows, d), lambda i, s: (i, 0)),
            scratch_shapes=[pltpu.VMEM((TOP_K, rows, d), F32), pltpu.SemaphoreType.DMA(())],
        ),
        out_shape=jax.ShapeDtypeStruct((n, d), F32),
        compiler_params=_cparams(("arbitrary",), est),
        name="combine_norm",
    )(slot_of_assignment, ys, x, routed, g_final.reshape(1, d))


def moe_plan(routed, *, tm):
    n = routed.shape[0]
    experts = routed[:, :TOP_K].astype(jnp.int32).reshape(-1)
    onehot = (experts[:, None] == jnp.arange(N_EXPERTS)[None, :]).astype(jnp.int32)
    rank = jnp.sum((jnp.cumsum(onehot, axis=0) - onehot) * onehot, axis=1)
    counts = jnp.sum(onehot, axis=0)
    padded = ((counts + tm - 1) // tm) * tm
    ends = jnp.cumsum(padded)
    starts = ends - padded
    slot = (starts[experts] + rank).astype(jnp.int32)
    n_blocks = (n * TOP_K) // tm + N_EXPERTS
    tok = jnp.repeat(jnp.arange(n, dtype=jnp.int32), TOP_K)
    tok_of_slot = jnp.zeros((n_blocks * tm,), jnp.int32).at[slot].set(tok)
    blk_expert = jnp.minimum(
        jnp.searchsorted(ends, jnp.arange(n_blocks, dtype=jnp.int32) * tm, side="right"), N_EXPERTS - 1
    ).astype(jnp.int32)
    n_used = (ends[-1] // tm).astype(jnp.int32).reshape(1)
    return slot, tok_of_slot, blk_expert, n_used


def _prepare_weights(w_qkv, w_o, w_ff_gate, w_ff_up, w_ff_down, w_in_c, w_out_c):
    d_inner = w_out_c.shape[1]
    main_cols = 2 * d_inner + 2 * SSM_GROUPS * SSM_STATE
    return dict(
        w_qkv=w_qkv[0].astype(BF16), w_o=w_o[0].astype(BF16),
        w_ff_gate=w_ff_gate[0].astype(BF16), w_ff_up=w_ff_up[0].astype(BF16), w_ff_down=w_ff_down[0].astype(BF16),
        w_in_main=w_in_c[0][:, :main_cols].astype(BF16), w_in_dt=w_in_c[0][:, main_cols:].astype(BF16),
        w_out=w_out_c[0].astype(BF16),
    )


def _mixer_layers(x3, wb, g_mix, g_ffn, rpb, conv_w, conv_b, dt_bias, a_log, d_skip, g_gate):
    batch, seq_len, d = x3.shape
    n = batch * seq_len
    x = x3.reshape(n, d)

    cos, sin = rope_tables(seq_len)
    qkv = qkv_proj(x, g_mix[0], wb["w_qkv"], cos, sin, seq_len=seq_len)
    bias_tiles = natten_bias_tiles(rpb[0], seq_len // GRID_W)
    o_a = natten(qkv, bias_tiles, batch=batch, seq_len=seq_len)
    o_b = dilated_attention(qkv, batch=batch, seq_len=seq_len)
    x = attn_out(x, o_a, o_b, wb["w_o"])
    x = ffn(x, g_ffn[0], wb["w_ff_gate"], wb["w_ff_up"], wb["w_ff_down"])

    d_inner = wb["w_out"].shape[0]
    gw = d_inner // SSM_GROUPS
    hpg = gw // SSM_HEAD_DIM
    heads = SSM_GROUPS * hpg
    zx = norm_matmul(x, g_mix[1], wb["w_in_main"], tn=gw, out_dtype=BF16)
    dt_raw = norm_matmul(x, g_mix[1], wb["w_in_dt"], tn=2 * heads, out_dtype=F32)[0]
    z_tiles = d_inner // gw
    xs = conv_silu(zx, conv_w[0], conv_b[0], width=gw, first_tile=z_tiles, n_tiles=SSM_GROUPS,
                   col_offset=0, batch=batch, seq_len=seq_len, tr=2048)
    bc_tiles = SSM_GROUPS * SSM_STATE // gw
    b_nat, b_t = conv_silu(zx, conv_w[0], conv_b[0], width=SSM_STATE, first_tile=2 * z_tiles, n_tiles=SSM_GROUPS,
                           col_offset=d_inner, batch=batch, seq_len=seq_len, tr=4096, transposed=True)
    c_nat = conv_silu(zx, conv_w[0], conv_b[0], width=SSM_STATE, first_tile=2 * z_tiles + bc_tiles,
                      n_tiles=SSM_GROUPS, col_offset=d_inner + SSM_GROUPS * SSM_STATE,
                      batch=batch, seq_len=seq_len, tr=4096)
    L = SSM_CHUNK
    rows = SSM_GROUPS * 2 * hpg
    dt_rows = dt_raw.reshape(n // L, L, 2, SSM_GROUPS, hpg).transpose(0, 3, 2, 4, 1).reshape(n // L, rows, L)

    def per_row(p):
        return p.reshape(2, SSM_GROUPS, hpg).transpose(1, 0, 2).reshape(rows, 1)

    d_skip_rows = jnp.repeat(d_skip[0].reshape(SSM_GROUPS, 1, hpg), SSM_HEAD_DIM, axis=2)
    y_f, y_b = ssd(xs, b_nat, b_t, c_nat, dt_rows, per_row(dt_bias[0]), per_row(a_log[0]), d_skip_rows,
                   batch=batch, seq_len=seq_len)
    return mamba_out(y_f, y_b, zx, g_gate[0], wb["w_out"], x)


def kernel(x_prompt, x_sample, g_mix, g_ffn, w_qkv, rpb, w_o, w_ff_gate, w_ff_up, w_ff_down, w_in_c, conv_w, conv_b,
           dt_bias, a_log, d_skip, g_gate, w_out_c, w_router, w_e_gate, w_e_up, w_e_down, g_final):
    wb = _prepare_weights(w_qkv, w_o, w_ff_gate, w_ff_up, w_ff_down, w_in_c, w_out_c)
    args = (wb, g_mix, g_ffn, rpb, conv_w, conv_b, dt_bias, a_log, d_skip, g_gate)
    x_p = _mixer_layers(x_prompt, *args)
    x_s = _mixer_layers(x_sample, *args)

    n_p = x_p.shape[0]
    routed, xn = router(x_p, x_s, g_ffn[1], w_router[0])
    slot, tok_of_slot, blk_expert, n_used = moe_plan(routed, tm=MOE_TM)
    ys = expert_ffn(blk_expert, n_used, gather_rows(tok_of_slot, xn), w_e_gate[0], w_e_up[0], w_e_down[0])
    out_p = combine_norm(slot[:TOP_K * n_p], ys, x_p, routed[:n_p], g_final)
    out_s = combine_norm(slot[TOP_K * n_p:], ys, x_s, routed[n_p:], g_final)
    return out_p.reshape(x_prompt.shape), out_s.reshape(x_sample.shape)
```

```python
import functools
import math

import jax
import jax.numpy as jnp
import numpy as np
from jax import lax
from jax.experimental import pallas as pl
from jax.experimental.pallas import tpu as pltpu

F32 = jnp.float32
BF16 = jnp.bfloat16
EPS = 1e-6
NEG_INF = float("-inf")

GRID_W = 64
HEAD_DIM = 128
N_HEADS_A = 4
N_HEADS_B_GROUP = 4
DILATIONS = (1, 4, 16)
BAND_RADIUS = 64
N_HEADS_QKV = N_HEADS_A + N_HEADS_B_GROUP * len(DILATIONS)
WIN_H = 8
WIN_W = 16
ROPE_THETA = 10000.0
SSM_HEAD_DIM = 64
SSM_GROUPS = 8
SSM_STATE = 128
SSM_CONV = 5
SSM_CHUNK = 128
N_EXPERTS = 8
TOP_K = 2

V7X_VMEM_BYTES = 64 * 1024 * 1024
LANES = 128
BF16_SUBLANES = 16

NAT_ROWS = 8
NAT_KROWS = NAT_ROWS + WIN_H - 1
DIL_TQ = 512
DIL_KC = (256, 256, 512)
DIL_PARTS = 4
MOE_TM = 1024
GATHER_ROWS = 256


def _cparams(semantics, vmem_estimate):
    limit = int(min(max(2 * vmem_estimate, 32 * 1024 * 1024), V7X_VMEM_BYTES - 8 * 1024 * 1024))
    return pltpu.CompilerParams(dimension_semantics=semantics, vmem_limit_bytes=limit)


def _rmsnorm(x, g):
    return x * lax.rsqrt(jnp.mean(x * x, axis=-1, keepdims=True) + EPS) * g


def _sigmoid(x):
    return 1.0 / (1.0 + jnp.exp(-x))


def _softplus(x):
    return jnp.maximum(x, 0.0) + jnp.log(1.0 + jnp.exp(-jnp.abs(x)))


def _dot(a, b):
    return jnp.dot(a, b, preferred_element_type=F32)


def _dot_nt(a, b):
    return lax.dot_general(a, b, (((1,), (1,)), ((), ())), preferred_element_type=F32)


def _norm_matmul_kernel(x_ref, g_ref, w_ref, o_ref, xn_ref):
    @pl.when(pl.program_id(1) == 0)
    def _():
        xn_ref[...] = _rmsnorm(x_ref[...], g_ref[...]).astype(BF16)

    o_ref[...] = _dot(xn_ref[...], w_ref[...]).astype(o_ref.dtype)


def norm_matmul(x, g, w, *, tn, out_dtype, tm=1024):
    n, k = x.shape
    m = w.shape[1]
    tm = min(tm, n)
    est = 2 * tm * k * 4 + tm * k * 2 + 2 * k * tn * 2 + 2 * tm * tn * 4
    return pl.pallas_call(
        _norm_matmul_kernel,
        grid=(n // tm, m // tn),
        in_specs=[
            pl.BlockSpec((tm, k), lambda i, j: (i, 0)),
            pl.BlockSpec((1, k), lambda i, j: (0, 0)),
            pl.BlockSpec((k, tn), lambda i, j: (0, j)),
        ],
        out_specs=pl.BlockSpec((None, tm, tn), lambda i, j: (j, i, 0)),
        out_shape=jax.ShapeDtypeStruct((m // tn, n, tn), out_dtype),
        scratch_shapes=[pltpu.VMEM((tm, k), BF16)],
        compiler_params=_cparams(("parallel", "arbitrary"), est),
        name="norm_matmul",
    )(x, g.reshape(1, k), w)


def _qkv_kernel(x_ref, g_ref, w_ref, cos_ref, sin_ref, o_ref, xn_ref, *, heads_per_tile, scale):
    j = pl.program_id(1)
    tiles_per_part = N_HEADS_QKV // heads_per_tile

    @pl.when(j == 0)
    def _():
        xn_ref[...] = _rmsnorm(x_ref[...], g_ref[...]).astype(BF16)

    use_rope = jnp.logical_and(j < 2 * tiles_per_part, j % tiles_per_part != 0)
    mult = jnp.where(j < tiles_per_part, scale, 1.0)
    c = jnp.where(use_rope, cos_ref[...], 1.0) * mult
    s = jnp.where(use_rope, sin_ref[...], 0.0) * mult
    xn = xn_ref[...]
    for pair in range(heads_per_tile // 2):
        r = _dot(xn, w_ref[:, pair * 2 * HEAD_DIM:(pair + 1) * 2 * HEAD_DIM])
        for k in range(2):
            p = r[:, k * HEAD_DIM:(k + 1) * HEAD_DIM]
            o_ref[2 * pair + k] = (p * c + pltpu.roll(p, HEAD_DIM // 2, 1) * s).astype(o_ref.dtype)


def qkv_proj(x, g, w, cos, sin, *, seq_len, tm=1024):
    n, k = x.shape
    m = w.shape[1]
    hpt = N_HEADS_A
    tn = hpt * HEAD_DIM
    tm = min(tm, seq_len)
    tiles_per_seq = seq_len // tm
    est = 2 * tm * k * 4 + tm * k * 2 + 2 * k * tn * 2 + 2 * tm * tn * 2 + 4 * tm * HEAD_DIM * 4 + tm * tn * 4
    return pl.pallas_call(
        functools.partial(_qkv_kernel, heads_per_tile=hpt, scale=HEAD_DIM ** -0.5),
        grid=(n // tm, m // tn),
        in_specs=[
            pl.BlockSpec((tm, k), lambda i, j: (i, 0)),
            pl.BlockSpec((1, k), lambda i, j: (0, 0)),
            pl.BlockSpec((k, tn), lambda i, j: (0, j)),
            pl.BlockSpec((tm, HEAD_DIM), lambda i, j: (i % tiles_per_seq, 0)),
            pl.BlockSpec((tm, HEAD_DIM), lambda i, j: (i % tiles_per_seq, 0)),
        ],
        out_specs=pl.BlockSpec((hpt, tm, HEAD_DIM), lambda i, j: (j, i, 0)),
        out_shape=jax.ShapeDtypeStruct((m // HEAD_DIM, n, HEAD_DIM), BF16),
        scratch_shapes=[pltpu.VMEM((tm, k), BF16)],
        compiler_params=_cparams(("parallel", "arbitrary"), est),
        name="qkv_proj",
    )(x, g.reshape(1, k), w, cos, sin)


def rope_tables(seq_len):
    half = HEAD_DIM // 2
    inv = ROPE_THETA ** (-jnp.arange(half, dtype=F32) / half)
    ang = jnp.arange(seq_len, dtype=F32)[:, None] * inv[None, :]
    cos = jnp.cos(ang)
    sin = jnp.sin(ang)
    return jnp.concatenate([cos, cos], axis=1), jnp.concatenate([-sin, sin], axis=1)


def _natten_kernel(q_ref, k_ref, v_ref, bias_ref, o_ref, *, n_blocks, grid_rows):
    blk = pl.program_id(2)
    first_row = jnp.where(blk == 0, 0,
                          jnp.where(blk == n_blocks - 1, grid_rows - NAT_KROWS, blk * NAT_ROWS - WIN_H // 2))
    start = pl.multiple_of(first_row * GRID_W, GRID_W)
    kw = k_ref[pl.ds(start, NAT_KROWS * GRID_W), :]
    vw = v_ref[pl.ds(start, NAT_KROWS * GRID_W), :]
    s = _dot_nt(q_ref[...], kw) + bias_ref[...]
    m = jnp.max(s, axis=-1, keepdims=True)
    p = jnp.exp(s - m)
    l = jnp.sum(p, axis=-1, keepdims=True)
    o_ref[...] = (_dot(p.astype(BF16), vw) / l).astype(o_ref.dtype)


def natten_bias_tiles(rpb, grid_rows):
    n_blocks = grid_rows // NAT_ROWS
    r0s = np.array([0, NAT_ROWS, (n_blocks - 1) * NAT_ROWS])
    k0s = np.array([0, NAT_ROWS - WIN_H // 2, grid_rows - NAT_KROWS])
    r = r0s[:, None] + np.arange(NAT_ROWS)[None, :]
    kr = k0s[:, None] + np.arange(NAT_KROWS)[None, :]
    rs = np.clip(r - WIN_H // 2, 0, grid_rows - WIN_H)
    row_ok = (kr[:, None, :] >= rs[:, :, None]) & (kr[:, None, :] < rs[:, :, None] + WIN_H)
    d_row = np.clip(kr[:, None, :] - r[:, :, None] + (WIN_H - 1), 0, 2 * WIN_H - 2)
    c = np.arange(GRID_W)
    cs = np.clip(c - WIN_W // 2, 0, GRID_W - WIN_W)
    col_ok = (c[None, :] >= cs[:, None]) & (c[None, :] < cs[:, None] + WIN_W)
    d_col = np.clip(c[None, :] - c[:, None] + (WIN_W - 1), 0, 2 * WIN_W - 2)
    sel_row = (d_row[..., None] == np.arange(2 * WIN_H - 1)).astype(np.float32)
    sel_col = (d_col[..., None] == np.arange(2 * WIN_W - 1)).astype(np.float32)
    rows = jnp.einsum("tikr,hrc->thikc", sel_row, rpb.astype(F32), precision=lax.Precision.HIGHEST)
    bias = jnp.einsum("thikc,qwc->thiqkw", rows, sel_col, precision=lax.Precision.HIGHEST)
    ok = row_ok[:, :, None, :, None] & col_ok[None, None, :, None, :]
    bias = jnp.where(ok[:, None], bias, NEG_INF)
    return bias.reshape(3, rpb.shape[0], NAT_ROWS * GRID_W, NAT_KROWS * GRID_W)


def natten(qkv, bias_tiles, *, batch, seq_len):
    n = batch * seq_len
    grid_rows = seq_len // GRID_W
    n_blocks = grid_rows // NAT_ROWS
    tq = NAT_ROWS * GRID_W
    tk = NAT_KROWS * GRID_W

    def tile_kind(blk):
        return jnp.where(blk == 0, 0, jnp.where(blk == n_blocks - 1, 2, 1))

    est = 4 * seq_len * HEAD_DIM * 2 + 2 * tq * tk * 4 + 3 * tq * tk * 4
    return pl.pallas_call(
        functools.partial(_natten_kernel, n_blocks=n_blocks, grid_rows=grid_rows),
        grid=(batch, N_HEADS_A, n_blocks),
        in_specs=[
            pl.BlockSpec((None, tq, HEAD_DIM), lambda b, h, i: (h, b * n_blocks + i, 0)),
            pl.BlockSpec((None, seq_len, HEAD_DIM), lambda b, h, i: (N_HEADS_QKV + h, b, 0)),
            pl.BlockSpec((None, seq_len, HEAD_DIM), lambda b, h, i: (2 * N_HEADS_QKV + h, b, 0)),
            pl.BlockSpec((None, None, tq, tk), lambda b, h, i: (tile_kind(i), h, 0, 0)),
        ],
        out_specs=pl.BlockSpec((tq, HEAD_DIM), lambda b, h, i: (b * n_blocks + i, h)),
        out_shape=jax.ShapeDtypeStruct((n, N_HEADS_A * HEAD_DIM), BF16),
        compiler_params=_cparams(("parallel", "parallel", "arbitrary"), est),
        name="natten",
    )(qkv, qkv, qkv, bias_tiles)


class _DilatedGeometry:
    def __init__(self, dil, kc):
        self.dil, self.kc = dil, kc
        self.reach = BAND_RADIUS * dil
        self.tp = DIL_TQ // DIL_PARTS
        assert kc % self.tp == 0
        self.halo = -(-self.reach // kc) * kc
        self.n_chunks = (DIL_TQ + 2 * self.halo) // kc
        self.n_tiles = (kc // self.tp) * (self.n_chunks - 1) + DIL_PARTS

    def tile_index(self, chunk, part):
        return (self.kc // self.tp) * chunk + (DIL_PARTS - 1 - part)

    def bias_tiles(self):
        row = np.arange(self.tp)[:, None]
        col = np.arange(self.kc)[None, :]
        tiles = []
        for u in range(self.n_tiles):
            rel = col - row + (u - (DIL_PARTS - 1)) * self.tp - self.halo
            ok = (np.abs(rel) <= self.reach) & (rel % self.dil == 0)
            tiles.append(np.where(ok, 0.0, -np.inf))
        return np.stack(tiles).astype(np.float32)


DIL_GEOMETRY = tuple(_DilatedGeometry(d, kc) for d, kc in zip(DILATIONS, DIL_KC))


def _dilated_kernel(b0, b1, b2, q0, q1, q2, k0, k1, k2, v0, v1, v2, o_ref, *, seq_len):
    tq = DIL_TQ
    parts = DIL_PARTS
    tp = tq // parts
    t0 = pl.program_id(2) * tq
    carry = tuple((jnp.full((tp, 1), -1e30, F32), jnp.zeros((tp, 1), F32), jnp.zeros((tp, HEAD_DIM), F32))
                  for _ in range(parts))
    for geo, bias_ref, q_ref, k_ref, v_ref in zip(DIL_GEOMETRY, (b0, b1, b2), (q0, q1, q2), (k0, k1, k2), (v0, v1, v2)):
        kc = geo.kc
        window_start = t0 - geo.halo
        lo = jnp.maximum(0, (geo.halo - t0) // kc)
        hi = jnp.minimum(geo.n_chunks, (seq_len - window_start) // kc)
        qs = tuple(q_ref[p * tp:(p + 1) * tp, :] for p in range(parts))

        def body(ci, carry, geo=geo, bias_ref=bias_ref, k_ref=k_ref, v_ref=v_ref, window_start=window_start, qs=qs):
            ks = pl.multiple_of(window_start + ci * geo.kc, geo.kc)
            kk = k_ref[pl.ds(ks, geo.kc), :]
            vv = v_ref[pl.ds(ks, geo.kc), :]
            scores = [_dot_nt(qs[p], kk) + bias_ref[geo.tile_index(ci, p)] for p in range(parts)]
            stats = []
            for p in range(parts):
                m, l, _ = carry[p]
                m_new = jnp.maximum(m, jnp.max(scores[p], axis=-1, keepdims=True))
                alpha = jnp.exp(m - m_new)
                e = jnp.exp(scores[p] - m_new)
                stats.append((m_new, alpha, alpha * l + jnp.sum(e, axis=-1, keepdims=True), e.astype(BF16)))
            return tuple((m_new, l, alpha * carry[p][2] + _dot(e, vv))
                         for p, (m_new, alpha, l, e) in enumerate(stats))

        carry = lax.fori_loop(lo, hi, body, carry)
    for p in range(parts):
        _, l, acc = carry[p]
        o_ref[p * tp:(p + 1) * tp, :] = (acc / l).astype(o_ref.dtype)


def dilated_attention(qkv, *, batch, seq_len):
    n = batch * seq_len
    nq = seq_len // DIL_TQ
    biases = [geo.bias_tiles() for geo in DIL_GEOMETRY]

    def q_spec(g):
        return pl.BlockSpec((None, DIL_TQ, HEAD_DIM),
                            lambda b, j, i: (N_HEADS_A + N_HEADS_B_GROUP * g + j, b * nq + i, 0))

    def kv_spec(part, g):
        return pl.BlockSpec((None, seq_len, HEAD_DIM),
                            lambda b, j, i: (part * N_HEADS_QKV + N_HEADS_A + N_HEADS_B_GROUP * g + j, b, 0))

    est = 12 * seq_len * HEAD_DIM * 2 + 8 * DIL_TQ * max(DIL_KC) * 4 + 2 * sum(b.size for b in biases) * 4
    groups = range(len(DILATIONS))
    return pl.pallas_call(
        functools.partial(_dilated_kernel, seq_len=seq_len),
        grid=(batch, N_HEADS_B_GROUP, nq),
        in_specs=[pl.BlockSpec(b.shape, lambda b_, j, i: (0, 0, 0)) for b in biases]
        + [q_spec(g) for g in groups] + [kv_spec(1, g) for g in groups] + [kv_spec(2, g) for g in groups],
        out_specs=pl.BlockSpec((DIL_TQ, HEAD_DIM), lambda b, j, i: (b * nq + i, j)),
        out_shape=jax.ShapeDtypeStruct((n, N_HEADS_B_GROUP * HEAD_DIM), BF16),
        compiler_params=_cparams(("parallel", "parallel", "arbitrary"), est),
        name="dilated_attention",
    )(*biases, *([qkv] * 9))


def _attn_out_kernel(x_ref, oa_ref, ob_ref, w_ref, o_ref):
    ka = oa_ref.shape[1]
    o_ref[...] = x_ref[...] + _dot(oa_ref[...], w_ref[:ka, :]) + _dot(ob_ref[...], w_ref[ka:, :])


def attn_out(x, o_a, o_b, w, *, tm=512):
    n, d = x.shape
    ka, kb = o_a.shape[1], o_b.shape[1]
    tm = min(tm, n)
    est = 4 * tm * d * 4 + 2 * (ka + kb) * d * 2 + 2 * tm * (ka + kb) * 2
    return pl.pallas_call(
        _attn_out_kernel,
        grid=(n // tm,),
        in_specs=[
            pl.BlockSpec((tm, d), lambda i: (i, 0)),
            pl.BlockSpec((tm, ka), lambda i: (i, 0)),
            pl.BlockSpec((tm, kb), lambda i: (i, 0)),
            pl.BlockSpec((ka + kb, d), lambda i: (0, 0)),
        ],
        out_specs=pl.BlockSpec((tm, d), lambda i: (i, 0)),
        out_shape=jax.ShapeDtypeStruct((n, d), F32),
        compiler_params=_cparams(("parallel",), est),
        name="attn_out",
    )(x, o_a, o_b, w)


def _ffn_kernel(x_ref, g_ref, wg_ref, wu_ref, wd_ref, o_ref, xn_ref):
    f = pl.program_id(1)

    @pl.when(f == 0)
    def _():
        xn_ref[...] = _rmsnorm(x_ref[...], g_ref[...]).astype(BF16)
        o_ref[...] = x_ref[...]

    xn = xn_ref[...]
    gate = _dot(xn, wg_ref[...])
    up = _dot(xn, wu_ref[...])
    hid = (gate * _sigmoid(gate) * up).astype(BF16)
    o_ref[...] += _dot(hid, wd_ref[...])


def ffn(x, g, wg, wu, wd, *, tm=512, tf=512):
    n, d = x.shape
    ff = wg.shape[1]
    tm = min(tm, n)
    est = 3 * tm * d * 4 + tm * d * 2 + 6 * d * tf * 2 + 3 * tm * tf * 4
    return pl.pallas_call(
        _ffn_kernel,
        grid=(n // tm, ff // tf),
        in_specs=[
            pl.BlockSpec((tm, d), lambda i, f: (i, 0), pipeline_mode=pl.Buffered(1)),
            pl.BlockSpec((1, d), lambda i, f: (0, 0)),
            pl.BlockSpec((d, tf), lambda i, f: (0, f)),
            pl.BlockSpec((d, tf), lambda i, f: (0, f)),
            pl.BlockSpec((tf, d), lambda i, f: (f, 0)),
        ],
        out_specs=pl.BlockSpec((tm, d), lambda i, f: (i, 0)),
        out_shape=jax.ShapeDtypeStruct((n, d), F32),
        scratch_shapes=[pltpu.VMEM((tm, d), BF16)],
        compiler_params=_cparams(("parallel", "arbitrary"), est),
        name="ffn",
    )(x, g.reshape(1, d), wg, wu, wd)


def _conv_kernel(xm_ref, xp_ref, xn_ref, w_ref, b_ref, o_ref, *rest, tr, n_row_blocks):
    ext_ref = rest[-1]
    i = pl.program_id(2)
    hb = BF16_SUBLANES
    pad = SSM_CONV // 2
    ext_ref[0:hb, :] = jnp.where(i > 0, xp_ref[...].astype(F32), 0.0)
    ext_ref[hb:hb + tr, :] = xm_ref[...].astype(F32)
    ext_ref[hb + tr:2 * hb + tr, :] = jnp.where(i < n_row_blocks - 1, xn_ref[...].astype(F32), 0.0)
    acc = jnp.broadcast_to(b_ref[...], o_ref.shape)
    for k in range(SSM_CONV):
        acc = acc + ext_ref[pl.ds(hb - pad + k, tr), :] * w_ref[k:k + 1, :]
    out = acc * _sigmoid(acc)
    o_ref[...] = out.astype(o_ref.dtype)
    if len(rest) == 2:
        ot_ref = rest[0]
        L = ot_ref.shape[-1]
        for c in range(tr // L):
            ot_ref[c] = out[c * L:(c + 1) * L, :].T.astype(ot_ref.dtype)


def conv_silu(zx, conv_w, conv_b, *, width, first_tile, n_tiles, col_offset, batch, seq_len, tr, transposed=False):
    n = batch * seq_len
    tile_w = zx.shape[2]
    per = tile_w // width
    tr = min(tr, seq_len)
    nr = seq_len // tr
    hb = BF16_SUBLANES
    seq_hb = seq_len // hb
    n_hb = n // hb

    def main_map(c, b, i):
        return (first_tile + c // per, b * nr + i, c % per)

    def prev_map(c, b, i):
        return (first_tile + c // per, jnp.maximum(b * seq_hb + i * (tr // hb) - 1, 0), c % per)

    def next_map(c, b, i):
        return (first_tile + c // per, jnp.minimum(b * seq_hb + (i + 1) * (tr // hb), n_hb - 1), c % per)

    L = SSM_CHUNK
    out_specs = [pl.BlockSpec((None, tr, width), lambda c, b, i: (c, b * nr + i, 0))]
    out_shape = [jax.ShapeDtypeStruct((n_tiles, n, width), BF16)]
    if transposed:
        out_specs.append(pl.BlockSpec((None, tr // L, width, L), lambda c, b, i: (c, b * nr + i, 0, 0)))
        out_shape.append(jax.ShapeDtypeStruct((n_tiles, n // L, width, L), BF16))
    est = 4 * tr * width * 2 + (tr + 2 * hb) * width * 4 + 4 * tr * width * 4 + 4 * tr * width * 2
    outs = pl.pallas_call(
        functools.partial(_conv_kernel, tr=tr, n_row_blocks=nr),
        grid=(n_tiles, batch, nr),
        in_specs=[
            pl.BlockSpec((None, tr, width), main_map),
            pl.BlockSpec((None, hb, width), prev_map),
            pl.BlockSpec((None, hb, width), next_map),
            pl.BlockSpec((SSM_CONV, width), lambda c, b, i: (0, col_offset // width + c)),
            pl.BlockSpec((1, width), lambda c, b, i: (0, col_offset // width + c)),
        ],
        out_specs=out_specs,
        out_shape=out_shape,
        scratch_shapes=[pltpu.VMEM((tr + 2 * hb, width), F32)],
        compiler_params=_cparams(("parallel", "parallel", "arbitrary"), est),
        name="conv_silu",
    )(zx, zx, zx, conv_w, conv_b.reshape(1, -1))
    return outs if transposed else outs[0]


def _lane_cumsum(a):
    lane = lax.broadcasted_iota(jnp.int32, a.shape, 1)
    shift = 1
    while shift < a.shape[1]:
        a = a + jnp.where(lane >= shift, pltpu.roll(a, shift, 1), 0.0)
        shift *= 2
    return a


def _ssd_decay_rows(dt_ref, dtbias_ref, alog_ref, seg_scr, dt_scr, *, hpg):
    L = SSM_CHUNK
    dt_all = _softplus(dt_ref[...] + dtbias_ref[...])
    a_all = dt_all * (-jnp.exp(alog_ref[...]))
    cum = _lane_cumsum(a_all)
    suf = cum[:, L - 1:L] - cum + a_all
    row = lax.broadcasted_iota(jnp.int32, cum.shape, 0)
    seg_scr[...] = jnp.where((row & (2 * hpg - 1)) < hpg, cum, suf)
    dt_scr[...] = dt_all


def _ssd_kernel(xa_ref, ba_ref, bta_ref, ca_ref, dta_ref, xb_ref, btb_ref, cb_ref, dtb_ref,
                dtbias_ref, alog_ref, dskip_ref, yf_ref, yb_ref,
                sf_ref, sb_ref, sega_scr, dta_scr, segb_scr, dtb_scr, *, hpg):
    L = SSM_CHUNK
    P = SSM_HEAD_DIM
    pairs = hpg // 2

    @pl.when(pl.program_id(1) == 0)
    def _():
        sf_ref[...] = jnp.zeros_like(sf_ref)
        sb_ref[...] = jnp.zeros_like(sb_ref)

    _ssd_decay_rows(dta_ref, dtbias_ref, alog_ref, sega_scr, dta_scr, hpg=hpg)
    _ssd_decay_rows(dtb_ref, dtbias_ref, alog_ref, segb_scr, dtb_scr, hpg=hpg)

    li = lax.broadcasted_iota(jnp.int32, (L, L), 0)
    si = lax.broadcasted_iota(jnp.int32, (L, L), 1)
    causal = li >= si
    anti = li <= si
    low_lanes = lax.broadcasted_iota(jnp.int32, (L, 2 * P), 1) < P
    high_lanes = jnp.logical_not(low_lanes)
    low_lanes_row = lax.broadcasted_iota(jnp.int32, (1, 2 * P), 1) < P

    def lanes_of(col):
        return jnp.broadcast_to(col, (L, 2 * P))

    def columns(seg_r, dt_r):
        stacked = jnp.concatenate([seg_r, dt_r, jnp.zeros((L - 4 * hpg, L), F32)], axis=0)
        return stacked.T

    spread_bwd = (lax.broadcasted_iota(jnp.int32, (L, hpg * P), 0)
                  == hpg + lax.broadcasted_iota(jnp.int32, (L, hpg * P), 1) // P).astype(BF16)

    def advance_state(s_ref, g, lanes, state, btf, xm0, xm1, seg_r, dt_r, h0, h1, tot_lane):
        acc = None
        decays = []
        for h, xm in ((h0, xm0), (h1, xm1)):
            tot = seg_r[h:h + 1, tot_lane:tot_lane + 1]
            coef = jnp.exp(tot - seg_r[h:h + 1, :]) * dt_r[h:h + 1, :]
            part = _dot((btf * coef).astype(BF16), xm)
            acc = part if acc is None else acc + part
            decays.append(jnp.exp(tot))
        decay = jnp.where(low_lanes_row, decays[0], decays[1])
        s_ref[g, :, lanes] = state[:, lanes] * decay + acc

    def group_body(g, c):
        r0 = pl.multiple_of(g * 2 * hpg, 2 * hpg)

        seg_r = sega_scr[pl.ds(r0, 2 * hpg), :]
        dt_r = dta_scr[pl.ds(r0, 2 * hpg), :]
        cols = columns(seg_r, dt_r)
        xg = xa_ref[g]
        cg = ca_ref[g]
        state = sf_ref[g]
        carried = _dot(cg, state.astype(BF16))
        cbm = _dot_nt(cg, ba_ref[g])
        btf = bta_ref[g].astype(F32)
        for p in range(pairs):
            lanes = slice(p * 2 * P, (p + 1) * 2 * P)
            xpf = xg[:, lanes].astype(F32)
            xm0 = jnp.where(low_lanes, xpf, 0.0).astype(BF16)
            xm1 = jnp.where(high_lanes, xpf, 0.0).astype(BF16)
            h0, h1 = 2 * p, 2 * p + 1
            f0, f1 = lanes_of(cols[:, h0:h0 + 1]), lanes_of(cols[:, h1:h1 + 1])
            y = carried[:, lanes] * jnp.where(low_lanes, jnp.exp(f0), jnp.exp(f1)) + dskip_ref[g][:, lanes] * xpf
            for hh, f_cols, xm in ((h0, f0, xm0), (h1, f1, xm1)):
                wf = jnp.exp(jnp.where(causal, f_cols - seg_r[hh:hh + 1, :], NEG_INF)) * dt_r[hh:hh + 1, :]
                b_cols = lanes_of(cols[:, hpg + hh:hpg + hh + 1])
                wb = jnp.exp(jnp.where(anti, b_cols - seg_r[hpg + hh:hpg + hh + 1, :], NEG_INF)) \
                    * dt_r[hpg + hh:hpg + hh + 1, :]
                y = y + _dot((cbm * (wf + wb)).astype(BF16), xm)
            yf_ref[g, :, lanes] = y.astype(yf_ref.dtype)
            advance_state(sf_ref, g, lanes, state, btf, xm0, xm1, seg_r, dt_r, h0, h1, L - 1)

        seg_r = segb_scr[pl.ds(r0, 2 * hpg), :]
        dt_r = dtb_scr[pl.ds(r0, 2 * hpg), :]
        xg = xb_ref[g]
        state = sb_ref[g]
        grow = _dot(jnp.exp(columns(seg_r, dt_r)).astype(BF16), spread_bwd)
        y = _dot(cb_ref[g], state.astype(BF16)) * grow
        yb_ref[g] = y.astype(yb_ref.dtype)
        btf = btb_ref[g].astype(F32)
        for p in range(pairs):
            lanes = slice(p * 2 * P, (p + 1) * 2 * P)
            xpf = xg[:, lanes].astype(F32)
            xm0 = jnp.where(low_lanes, xpf, 0.0).astype(BF16)
            xm1 = jnp.where(high_lanes, xpf, 0.0).astype(BF16)
            advance_state(sb_ref, g, lanes, state, btf, xm0, xm1, seg_r, dt_r, hpg + 2 * p, hpg + 2 * p + 1, 0)
        return c

    lax.fori_loop(0, SSM_GROUPS, group_body, 0)


def ssd(xs, b_nat, b_t, c_nat, dt_rows, dt_bias_rows, a_log_rows, d_skip_rows, *, batch, seq_len):
    groups, n, gw = xs.shape
    hpg = gw // SSM_HEAD_DIM
    assert hpg & (hpg - 1) == 0
    L = SSM_CHUNK
    nc = seq_len // L
    rows = groups * 2 * hpg

    def fwd(b, k):
        return b * nc + k

    def bwd(b, k):
        return b * nc + nc - 1 - k

    def specs(chunk, with_b_nat):
        out = [pl.BlockSpec((groups, L, gw), lambda b, k: (0, chunk(b, k), 0))]
        if with_b_nat:
            out.append(pl.BlockSpec((groups, L, SSM_STATE), lambda b, k: (0, chunk(b, k), 0)))
        out += [
            pl.BlockSpec((groups, None, SSM_STATE, L), lambda b, k: (0, chunk(b, k), 0, 0)),
            pl.BlockSpec((groups, L, SSM_STATE), lambda b, k: (0, chunk(b, k), 0)),
            pl.BlockSpec((None, rows, L), lambda b, k: (chunk(b, k), 0, 0)),
        ]
        return out

    const = [
        pl.BlockSpec((rows, 1), lambda b, k: (0, 0)),
        pl.BlockSpec((rows, 1), lambda b, k: (0, 0)),
        pl.BlockSpec((groups, 1, gw), lambda b, k: (0, 0, 0)),
    ]
    y_shape = jax.ShapeDtypeStruct((groups, n, gw), BF16)
    est = 4 * (groups * L * gw * 2 + 3 * groups * L * SSM_STATE * 2 + rows * L * 4) + 4 * groups * L * gw * 2 \
        + 2 * groups * SSM_STATE * gw * 4 + 4 * rows * L * 4 + 64 * L * L * 4
    return pl.pallas_call(
        functools.partial(_ssd_kernel, hpg=hpg),
        grid=(batch, nc),
        in_specs=specs(fwd, True) + specs(bwd, False) + const,
        out_specs=[pl.BlockSpec((groups, L, gw), lambda b, k: (0, fwd(b, k), 0)),
                   pl.BlockSpec((groups, L, gw), lambda b, k: (0, bwd(b, k), 0))],
        out_shape=[y_shape, y_shape],
        scratch_shapes=[pltpu.VMEM((groups, SSM_STATE, gw), F32), pltpu.VMEM((groups, SSM_STATE, gw), F32)]
        + [pltpu.VMEM((rows, L), F32)] * 4,
        compiler_params=_cparams(("parallel", "arbitrary"), est),
        name="ssd",
    )(xs, b_nat, b_t, c_nat, dt_rows, xs, b_t, c_nat, dt_rows, dt_bias_rows, a_log_rows, d_skip_rows)


def _mamba_out_kernel(yf_ref, yb_ref, z_ref, gg_ref, w_ref, x_ref, o_ref, ssq_ref, *, d_inner):
    step = pl.program_id(1)
    groups_per_step, _, gw = z_ref.shape

    @pl.when(step == 0)
    def _():
        o_ref[...] = jnp.zeros_like(o_ref)
        ssq_ref[...] = jnp.zeros_like(ssq_ref)

    gated = []
    for k in range(groups_per_step):
        z = z_ref[k].astype(F32)
        yz = (yf_ref[k].astype(F32) + yb_ref[k].astype(F32)) * (z * _sigmoid(z))
        ssq_ref[...] += jnp.sum(yz * yz, axis=-1, keepdims=True)
        gated.append((yz * gg_ref[:, k * gw:(k + 1) * gw]).astype(BF16))
    o_ref[...] += _dot(jnp.concatenate(gated, axis=1), w_ref[...])

    @pl.when(step == pl.num_programs(1) - 1)
    def _():
        o_ref[...] = x_ref[...] + o_ref[...] * lax.rsqrt(ssq_ref[...] / d_inner + EPS)


def mamba_out(y_f, y_b, zx, g_gate, w_out, x, *, tm=512, groups_per_step=2):
    groups, n, gw = y_f.shape
    d = x.shape[1]
    tm = min(tm, n)
    gps = groups_per_step
    est = gps * (6 * tm * gw * 2 + 2 * gw * d * 2) + 3 * tm * d * 4 + 2 * tm * gps * gw * 4
    return pl.pallas_call(
        functools.partial(_mamba_out_kernel, d_inner=groups * gw),
        grid=(n // tm, groups // gps),
        in_specs=[
            pl.BlockSpec((gps, tm, gw), lambda i, s: (s, i, 0)),
            pl.BlockSpec((gps, tm, gw), lambda i, s: (s, i, 0)),
            pl.BlockSpec((gps, tm, gw), lambda i, s: (s, i, 0)),
            pl.BlockSpec((1, gps * gw), lambda i, s: (0, s)),
            pl.BlockSpec((gps * gw, d), lambda i, s: (s, 0)),
            pl.BlockSpec((tm, d), lambda i, s: (i, 0), pipeline_mode=pl.Buffered(1)),
        ],
        out_specs=pl.BlockSpec((tm, d), lambda i, s: (i, 0)),
        out_shape=jax.ShapeDtypeStruct((n, d), F32),
        scratch_shapes=[pltpu.VMEM((tm, 1), F32)],
        compiler_params=_cparams(("parallel", "arbitrary"), est),
        name="mamba_out",
    )(y_f, y_b, zx, g_gate.reshape(1, -1), w_out, x)


def _router_kernel(xa_ref, xb_ref, g_ref, w_ref, o_ref, xn_ref, *, tiles_a):
    i = pl.program_id(0)

    @pl.when(i < tiles_a)
    def _():
        xn_ref[...] = _rmsnorm(xa_ref[...], g_ref[...])

    @pl.when(i >= tiles_a)
    def _():
        xn_ref[...] = _rmsnorm(xb_ref[...], g_ref[...])

    xn = xn_ref[...]
    logits = jnp.dot(xn, w_ref[...], preferred_element_type=F32, precision=lax.Precision.HIGHEST)
    lane = lax.broadcasted_iota(jnp.int32, logits.shape, 1)
    logits = jnp.where(lane < N_EXPERTS, logits, NEG_INF)
    v1 = jnp.max(logits, axis=-1, keepdims=True)
    i1 = jnp.min(jnp.where(logits == v1, lane, LANES), axis=-1, keepdims=True)
    rest = jnp.where(lane == i1, NEG_INF, logits)
    v2 = jnp.max(rest, axis=-1, keepdims=True)
    i2 = jnp.min(jnp.where(rest == v2, lane, LANES), axis=-1, keepdims=True)
    e2 = jnp.exp(v2 - v1)
    g1 = 1.0 / (1.0 + e2)
    g2 = e2 / (1.0 + e2)
    out = jnp.where(lane == 0, i1.astype(F32),
                    jnp.where(lane == 1, i2.astype(F32),
                              jnp.where(lane == 2, g1, jnp.where(lane == 3, g2, 0.0))))
    o_ref[...] = out


def router(x_a, x_b, g, w_router, *, tm=512):
    (na, d), nb = x_a.shape, x_b.shape[0]
    tm = min(tm, na, nb)
    tiles_a, tiles_b = na // tm, nb // tm
    w = jnp.zeros((d, LANES), F32).at[:, :w_router.shape[1]].set(w_router)
    est = 4 * tm * d * 4 + 2 * d * LANES * 4 + 2 * tm * LANES * 4 + 3 * tm * d * 4
    return pl.pallas_call(
        functools.partial(_router_kernel, tiles_a=tiles_a),
        grid=(tiles_a + tiles_b,),
        in_specs=[
            pl.BlockSpec((tm, d), lambda i: (jnp.minimum(i, tiles_a - 1), 0)),
            pl.BlockSpec((tm, d), lambda i: (jnp.maximum(i - tiles_a, 0), 0)),
            pl.BlockSpec((1, d), lambda i: (0, 0)),
            pl.BlockSpec((d, LANES), lambda i: (0, 0)),
        ],
        out_specs=[pl.BlockSpec((tm, LANES), lambda i: (i, 0)), pl.BlockSpec((tm, d), lambda i: (i, 0))],
        out_shape=[jax.ShapeDtypeStruct((na + nb, LANES), F32), jax.ShapeDtypeStruct((na + nb, d), F32)],
        compiler_params=_cparams(("parallel",), est),
        name="router",
    )(x_a, x_b, g.reshape(1, d), w)


def _row_copy(src_hbm, dst_vmem, sem, src_row, dst_row):
    return pltpu.make_async_copy(src_hbm.at[pl.ds(src_row, 1), :], dst_vmem.at[pl.ds(dst_row, 1), :], sem)


def _gather_kernel(tok_ref, used_ref, x_hbm, o_ref, buf_ref, sem):
    rows = buf_ref.shape[0]
    base = pl.program_id(0) * rows
    live = base < used_ref[0]

    def start(r, c):
        _row_copy(x_hbm, buf_ref, sem, tok_ref[base + r], r).start()
        return c

    def wait(r, c):
        _row_copy(x_hbm, buf_ref, sem, 0, r).wait()
        return c

    @pl.when(live)
    def _():
        lax.fori_loop(0, rows, start, 0, unroll=8)
        lax.fori_loop(0, rows, wait, 0, unroll=8)
        o_ref[...] = buf_ref[...].astype(o_ref.dtype)

    @pl.when(jnp.logical_not(live))
    def _():
        o_ref[...] = jnp.zeros_like(o_ref)


def gather_rows(tok_of_slot, slots_used, x, *, rows=GATHER_ROWS):
    slots = tok_of_slot.shape[0]
    d = x.shape[1]
    est = rows * d * 4 * 2 + 2 * rows * d * 2
    return pl.pallas_call(
        _gather_kernel,
        grid_spec=pltpu.PrefetchScalarGridSpec(
            num_scalar_prefetch=2,
            grid=(slots // rows,),
            in_specs=[pl.BlockSpec(memory_space=pl.ANY)],
            out_specs=pl.BlockSpec((rows, d), lambda i, tok, used: (i, 0)),
            scratch_shapes=[pltpu.VMEM((rows, d), F32), pltpu.SemaphoreType.DMA(())],
        ),
        out_shape=jax.ShapeDtypeStruct((slots, d), BF16),
        compiler_params=_cparams(("arbitrary",), est),
        name="gather_rows",
    )(tok_of_slot, slots_used, x)


def _expert_kernel(blk_e_ref, n_used_ref, x_ref, wg_ref, wu_ref, wd_ref, o_ref):
    i = pl.program_id(0)
    f = pl.program_id(1)

    @pl.when(f == 0)
    def _():
        o_ref[...] = jnp.zeros_like(o_ref)

    @pl.when(i < n_used_ref[0])
    def _():
        x = x_ref[...]
        gate = _dot(x, wg_ref[...].astype(BF16))
        up = _dot(x, wu_ref[...].astype(BF16))
        hid = (gate * _sigmoid(gate) * up).astype(BF16)
        o_ref[...] += _dot(hid, wd_ref[...].astype(BF16))


def expert_ffn(blk_expert, n_used, xs, wg, wu, wd, *, tm=MOE_TM, tf=512):
    slots, d = xs.shape
    ff = wg.shape[2]
    nf = ff // tf
    n_blocks = slots // tm

    def live(i, n_used):
        return jnp.minimum(i, n_used[0] - 1)

    def f_eff(i, f, n_used):
        return jnp.where(i < n_used[0], f, nf - 1)

    wbytes = wg.dtype.itemsize + wu.dtype.itemsize + wd.dtype.itemsize
    est = tm * d * 2 + 2 * d * tf * wbytes + 3 * d * tf * 2 + 2 * tm * d * 4 + 3 * tm * tf * 4
    return pl.pallas_call(
        _expert_kernel,
        grid_spec=pltpu.PrefetchScalarGridSpec(
            num_scalar_prefetch=2,
            grid=(n_blocks, nf),
            in_specs=[
                pl.BlockSpec((tm, d), lambda i, f, be, nu: (live(i, nu), 0), pipeline_mode=pl.Buffered(1)),
                pl.BlockSpec((None, d, tf), lambda i, f, be, nu: (be[live(i, nu)], 0, f_eff(i, f, nu))),
                pl.BlockSpec((None, d, tf), lambda i, f, be, nu: (be[live(i, nu)], 0, f_eff(i, f, nu))),
                pl.BlockSpec((None, tf, d), lambda i, f, be, nu: (be[live(i, nu)], f_eff(i, f, nu), 0)),
            ],
            out_specs=pl.BlockSpec((tm, d), lambda i, f, be, nu: (i, 0)),
        ),
        out_shape=jax.ShapeDtypeStruct((slots, d), F32),
        compiler_params=_cparams(("arbitrary", "arbitrary"), est),
        name="expert_ffn",
    )(blk_expert, n_used, xs, wg, wu, wd)


def _combine_kernel(slot_ref, y_hbm, x_ref, r_ref, g_ref, o_ref, buf_ref, sem):
    rows = x_ref.shape[0]
    base = pl.program_id(0) * rows

    def start(r, c):
        for k in range(TOP_K):
            _row_copy(y_hbm, buf_ref.at[k], sem, slot_ref[TOP_K * (base + r) + k], r).start()
        return c

    def wait(r, c):
        for k in range(TOP_K):
            _row_copy(y_hbm, buf_ref.at[k], sem, 0, r).wait()
        return c

    lax.fori_loop(0, rows, start, 0, unroll=8)
    lax.fori_loop(0, rows, wait, 0, unroll=8)
    gates = r_ref[...]
    out = x_ref[...] + gates[:, 2:3] * buf_ref[0] + gates[:, 3:4] * buf_ref[1]
    o_ref[...] = _rmsnorm(out, g_ref[...])


def combine_norm(slot_of_assignment, ys, x, routed, g_final, *, rows=GATHER_ROWS):
    n, d = x.shape
    rows = min(rows, n)
    est = 2 * rows * d * 4 + 4 * rows * d * 4 + 4 * rows * d * 4
    return pl.pallas_call(
        _combine_kernel,
        grid_spec=pltpu.PrefetchScalarGridSpec(
            num_scalar_prefetch=1,
            grid=(n // rows,),
            in_specs=[
                pl.BlockSpec(memory_space=pl.ANY),
                pl.BlockSpec((rows, d), lambda i, s: (i, 0)),
                pl.BlockSpec((rows, LANES), lambda i, s: (i, 0)),
                pl.BlockSpec((1, d), lambda i, s: (0, 0)),
            ],
            out_specs=pl.BlockSpec((rows, d), lambda i, s: (i, 0)),
            scratch_shapes=[pltpu.VMEM((TOP_K, rows, d), F32), pltpu.SemaphoreType.DMA(())],
        ),
        out_shape=jax.ShapeDtypeStruct((n, d), F32),
        compiler_params=_cparams(("arbitrary",), est),
        name="combine_norm",
    )(slot_of_assignment, ys, x, routed, g_final.reshape(1, d))


def moe_plan(routed, *, tm):
    n = routed.shape[0]
    experts = routed[:, :TOP_K].astype(jnp.int32).reshape(-1)
    onehot = (experts[:, None] == jnp.arange(N_EXPERTS)[None, :]).astype(jnp.int32)
    rank = jnp.sum((jnp.cumsum(onehot, axis=0) - onehot) * onehot, axis=1)
    counts = jnp.sum(onehot, axis=0)
    padded = ((counts + tm - 1) // tm) * tm
    ends = jnp.cumsum(padded)
    starts = ends - padded
    slot = (starts[experts] + rank).astype(jnp.int32)
    n_blocks = (n * TOP_K) // tm + N_EXPERTS
    tok = jnp.repeat(jnp.arange(n, dtype=jnp.int32), TOP_K)
    tok_of_slot = jnp.zeros((n_blocks * tm,), jnp.int32).at[slot].set(tok)
    blk_expert = jnp.minimum(
        jnp.searchsorted(ends, jnp.arange(n_blocks, dtype=jnp.int32) * tm, side="right"), N_EXPERTS - 1
    ).astype(jnp.int32)
    n_used = (ends[-1] // tm).astype(jnp.int32).reshape(1)
    return slot, tok_of_slot, blk_expert, n_used


def _prepare_weights(w_qkv, w_o, w_ff_gate, w_ff_up, w_ff_down, w_in_c, w_out_c):
    d_inner = w_out_c.shape[1]
    main_cols = 2 * d_inner + 2 * SSM_GROUPS * SSM_STATE
    return dict(
        w_qkv=w_qkv[0].astype(BF16), w_o=w_o[0].astype(BF16),
        w_ff_gate=w_ff_gate[0].astype(BF16), w_ff_up=w_ff_up[0].astype(BF16), w_ff_down=w_ff_down[0].astype(BF16),
        w_in_main=w_in_c[0][:, :main_cols].astype(BF16), w_in_dt=w_in_c[0][:, main_cols:].astype(BF16),
        w_out=w_out_c[0].astype(BF16),
    )


def _mixer_layers(x3, wb, g_mix, g_ffn, rpb, conv_w, conv_b, dt_bias, a_log, d_skip, g_gate):
    batch, seq_len, d = x3.shape
    n = batch * seq_len
    x = x3.reshape(n, d)

    cos, sin = rope_tables(seq_len)
    qkv = qkv_proj(x, g_mix[0], wb["w_qkv"], cos, sin, seq_len=seq_len)
    bias_tiles = natten_bias_tiles(rpb[0], seq_len // GRID_W)
    o_a = natten(qkv, bias_tiles, batch=batch, seq_len=seq_len)
    o_b = dilated_attention(qkv, batch=batch, seq_len=seq_len)
    x = attn_out(x, o_a, o_b, wb["w_o"])
    x = ffn(x, g_ffn[0], wb["w_ff_gate"], wb["w_ff_up"], wb["w_ff_down"])

    d_inner = wb["w_out"].shape[0]
    gw = d_inner // SSM_GROUPS
    hpg = gw // SSM_HEAD_DIM
    heads = SSM_GROUPS * hpg
    zx = norm_matmul(x, g_mix[1], wb["w_in_main"], tn=gw, out_dtype=BF16)
    dt_raw = norm_matmul(x, g_mix[1], wb["w_in_dt"], tn=2 * heads, out_dtype=F32)[0]
    z_tiles = d_inner // gw
    xs = conv_silu(zx, conv_w[0], conv_b[0], width=gw, first_tile=z_tiles, n_tiles=SSM_GROUPS,
                   col_offset=0, batch=batch, seq_len=seq_len, tr=2048)
    bc_tiles = SSM_GROUPS * SSM_STATE // gw
    b_nat, b_t = conv_silu(zx, conv_w[0], conv_b[0], width=SSM_STATE, first_tile=2 * z_tiles, n_tiles=SSM_GROUPS,
                           col_offset=d_inner, batch=batch, seq_len=seq_len, tr=4096, transposed=True)
    c_nat = conv_silu(zx, conv_w[0], conv_b[0], width=SSM_STATE, first_tile=2 * z_tiles + bc_tiles,
                      n_tiles=SSM_GROUPS, col_offset=d_inner + SSM_GROUPS * SSM_STATE,
                      batch=batch, seq_len=seq_len, tr=4096)
    L = SSM_CHUNK
    rows = SSM_GROUPS * 2 * hpg
    dt_rows = dt_raw.reshape(n // L, L, 2, SSM_GROUPS, hpg).transpose(0, 3, 2, 4, 1).reshape(n // L, rows, L)

    def per_row(p):
        return p.reshape(2, SSM_GROUPS, hpg).transpose(1, 0, 2).reshape(rows, 1)

    d_skip_rows = jnp.repeat(d_skip[0].reshape(SSM_GROUPS, 1, hpg), SSM_HEAD_DIM, axis=2)
    y_f, y_b = ssd(xs, b_nat, b_t, c_nat, dt_rows, per_row(dt_bias[0]), per_row(a_log[0]), d_skip_rows,
                   batch=batch, seq_len=seq_len)
    return mamba_out(y_f, y_b, zx, g_gate[0], wb["w_out"], x)


def kernel(x_prompt, x_sample, g_mix, g_ffn, w_qkv, rpb, w_o, w_ff_gate, w_ff_up, w_ff_down, w_in_c, conv_w, conv_b,
           dt_bias, a_log, d_skip, g_gate, w_out_c, w_router, w_e_gate, w_e_up, w_e_down, g_final):
    wb = _prepare_weights(w_qkv, w_o, w_ff_gate, w_ff_up, w_ff_down, w_in_c, w_out_c)
    args = (wb, g_mix, g_ffn, rpb, conv_w, conv_b, dt_bias, a_log, d_skip, g_gate)
    x_p = _mixer_layers(x_prompt, *args)
    x_s = _mixer_layers(x_sample, *args)

    n_p = x_p.shape[0]
    routed, xn = router(x_p, x_s, g_ffn[1], w_router[0])
    slot, tok_of_slot, blk_expert, n_used = moe_plan(routed, tm=MOE_TM)
    xs_sorted = gather_rows(tok_of_slot, n_used * MOE_TM, xn)
    ys = expert_ffn(blk_expert, n_used, xs_sorted, w_e_gate[0].astype(BF16), w_e_up[0].astype(BF16), w_e_down[0])
    out_p = combine_norm(slot[:TOP_K * n_p], ys, x_p, routed[:n_p], g_final)
    out_s = combine_norm(slot[TOP_K * n_p:], ys, x_s, routed[n_p:], g_final)
    return out_p.reshape(x_prompt.shape), out_s.reshape(x_sample.shape)
```

```python
import functools
import math

import jax
import jax.numpy as jnp
import numpy as np
from jax import lax
from jax.experimental import pallas as pl
from jax.experimental.pallas import tpu as pltpu

F32 = jnp.float32
BF16 = jnp.bfloat16
EPS = 1e-6
NEG_INF = float("-inf")

GRID_W = 64
HEAD_DIM = 128
N_HEADS_A = 4
N_HEADS_B_GROUP = 4
DILATIONS = (1, 4, 16)
BAND_RADIUS = 64
N_HEADS_QKV = N_HEADS_A + N_HEADS_B_GROUP * len(DILATIONS)
WIN_H = 8
WIN_W = 16
ROPE_THETA = 10000.0
SSM_HEAD_DIM = 64
SSM_GROUPS = 8
SSM_STATE = 128
SSM_CONV = 5
SSM_CHUNK = 128
N_EXPERTS = 8
TOP_K = 2

V7X_VMEM_BYTES = 64 * 1024 * 1024
LANES = 128
BF16_SUBLANES = 16

NAT_ROWS = 8
NAT_KROWS = NAT_ROWS + WIN_H - 1
DIL_TQ = 512
DIL_KC = (256, 256, 512)
DIL_PARTS = 4
MOE_TM = 1024
GATHER_ROWS = 256
FF_TILE = 512


def _cparams(semantics, vmem_estimate):
    limit = int(min(max(2 * vmem_estimate, 32 * 1024 * 1024), V7X_VMEM_BYTES - 8 * 1024 * 1024))
    return pltpu.CompilerParams(dimension_semantics=semantics, vmem_limit_bytes=limit)


def _rmsnorm(x, g):
    return x * lax.rsqrt(jnp.mean(x * x, axis=-1, keepdims=True) + EPS) * g


def _sigmoid(x):
    return 1.0 / (1.0 + jnp.exp(-x))


def _softplus(x):
    return jnp.maximum(x, 0.0) + jnp.log(1.0 + jnp.exp(-jnp.abs(x)))


def _dot(a, b):
    return jnp.dot(a, b, preferred_element_type=F32)


def _dot_nt(a, b):
    return lax.dot_general(a, b, (((1,), (1,)), ((), ())), preferred_element_type=F32)


def _norm_matmul_kernel(x_ref, g_ref, w_ref, o_ref, xn_ref):
    @pl.when(pl.program_id(1) == 0)
    def _():
        xn_ref[...] = _rmsnorm(x_ref[...], g_ref[...]).astype(BF16)

    o_ref[...] = _dot(xn_ref[...], w_ref[...]).astype(o_ref.dtype)


def column_tiles(w, tn):
    *lead, k, m = w.shape
    w = w.astype(BF16).reshape(*lead, k, m // tn, tn)
    return jnp.swapaxes(w, -3, -2)


def norm_matmul(x, g, w_tiles, *, out_dtype, tm=1024):
    n, k = x.shape
    n_tiles, _, tn = w_tiles.shape
    m = n_tiles * tn
    tm = min(tm, n)
    est = 2 * tm * k * 4 + tm * k * 2 + 2 * k * tn * 2 + 2 * tm * tn * 4
    return pl.pallas_call(
        _norm_matmul_kernel,
        grid=(n // tm, m // tn),
        in_specs=[
            pl.BlockSpec((tm, k), lambda i, j: (i, 0)),
            pl.BlockSpec((1, k), lambda i, j: (0, 0)),
            pl.BlockSpec((None, k, tn), lambda i, j: (j, 0, 0)),
        ],
        out_specs=pl.BlockSpec((None, tm, tn), lambda i, j: (j, i, 0)),
        out_shape=jax.ShapeDtypeStruct((m // tn, n, tn), out_dtype),
        scratch_shapes=[pltpu.VMEM((tm, k), BF16)],
        compiler_params=_cparams(("parallel", "arbitrary"), est),
        name="norm_matmul",
    )(x, g.reshape(1, k), w_tiles)


def _qkv_kernel(x_ref, g_ref, w_ref, cos_ref, sin_ref, o_ref, xn_ref, *, heads_per_tile, scale):
    j = pl.program_id(1)
    tiles_per_part = N_HEADS_QKV // heads_per_tile

    @pl.when(j == 0)
    def _():
        xn_ref[...] = _rmsnorm(x_ref[...], g_ref[...]).astype(BF16)

    use_rope = jnp.logical_and(j < 2 * tiles_per_part, j % tiles_per_part != 0)
    mult = jnp.where(j < tiles_per_part, scale, 1.0)
    c = jnp.where(use_rope, cos_ref[...], 1.0) * mult
    s = jnp.where(use_rope, sin_ref[...], 0.0) * mult
    xn = xn_ref[...]
    for pair in range(heads_per_tile // 2):
        r = _dot(xn, w_ref[:, pair * 2 * HEAD_DIM:(pair + 1) * 2 * HEAD_DIM])
        for k in range(2):
            p = r[:, k * HEAD_DIM:(k + 1) * HEAD_DIM]
            o_ref[2 * pair + k] = (p * c + pltpu.roll(p, HEAD_DIM // 2, 1) * s).astype(o_ref.dtype)


def qkv_proj(x, g, w_tiles, cos, sin, *, seq_len, tm=1024):
    n, k = x.shape
    hpt = N_HEADS_A
    n_tiles, _, tn = w_tiles.shape
    assert tn == hpt * HEAD_DIM
    m = n_tiles * tn
    tm = min(tm, seq_len)
    tiles_per_seq = seq_len // tm
    est = 2 * tm * k * 4 + tm * k * 2 + 2 * k * tn * 2 + 2 * tm * tn * 2 + 4 * tm * HEAD_DIM * 4 + tm * tn * 4
    return pl.pallas_call(
        functools.partial(_qkv_kernel, heads_per_tile=hpt, scale=HEAD_DIM ** -0.5),
        grid=(n // tm, m // tn),
        in_specs=[
            pl.BlockSpec((tm, k), lambda i, j: (i, 0)),
            pl.BlockSpec((1, k), lambda i, j: (0, 0)),
            pl.BlockSpec((None, k, tn), lambda i, j: (j, 0, 0)),
            pl.BlockSpec((tm, HEAD_DIM), lambda i, j: (i % tiles_per_seq, 0)),
            pl.BlockSpec((tm, HEAD_DIM), lambda i, j: (i % tiles_per_seq, 0)),
        ],
        out_specs=pl.BlockSpec((hpt, tm, HEAD_DIM), lambda i, j: (j, i, 0)),
        out_shape=jax.ShapeDtypeStruct((m // HEAD_DIM, n, HEAD_DIM), BF16),
        scratch_shapes=[pltpu.VMEM((tm, k), BF16)],
        compiler_params=_cparams(("parallel", "arbitrary"), est),
        name="qkv_proj",
    )(x, g.reshape(1, k), w_tiles, cos, sin)


def rope_tables(seq_len):
    half = HEAD_DIM // 2
    inv = ROPE_THETA ** (-jnp.arange(half, dtype=F32) / half)
    ang = jnp.arange(seq_len, dtype=F32)[:, None] * inv[None, :]
    cos = jnp.cos(ang)
    sin = jnp.sin(ang)
    return jnp.concatenate([cos, cos], axis=1), jnp.concatenate([-sin, sin], axis=1)


def _natten_kernel(q_ref, k_ref, v_ref, bias_ref, o_ref, *, n_blocks, grid_rows):
    blk = pl.program_id(2)
    first_row = jnp.where(blk == 0, 0,
                          jnp.where(blk == n_blocks - 1, grid_rows - NAT_KROWS, blk * NAT_ROWS - WIN_H // 2))
    start = pl.multiple_of(first_row * GRID_W, GRID_W)
    kw = k_ref[pl.ds(start, NAT_KROWS * GRID_W), :]
    vw = v_ref[pl.ds(start, NAT_KROWS * GRID_W), :]
    s = _dot_nt(q_ref[...], kw) + bias_ref[...]
    m = jnp.max(s, axis=-1, keepdims=True)
    p = jnp.exp(s - m)
    l = jnp.sum(p, axis=-1, keepdims=True)
    o_ref[...] = (_dot(p.astype(BF16), vw) / l).astype(o_ref.dtype)


def natten_bias_tiles(rpb, grid_rows):
    n_blocks = grid_rows // NAT_ROWS
    r0s = np.array([0, NAT_ROWS, (n_blocks - 1) * NAT_ROWS])
    k0s = np.array([0, NAT_ROWS - WIN_H // 2, grid_rows - NAT_KROWS])
    r = r0s[:, None] + np.arange(NAT_ROWS)[None, :]
    kr = k0s[:, None] + np.arange(NAT_KROWS)[None, :]
    rs = np.clip(r - WIN_H // 2, 0, grid_rows - WIN_H)
    row_ok = (kr[:, None, :] >= rs[:, :, None]) & (kr[:, None, :] < rs[:, :, None] + WIN_H)
    d_row = np.clip(kr[:, None, :] - r[:, :, None] + (WIN_H - 1), 0, 2 * WIN_H - 2)
    c = np.arange(GRID_W)
    cs = np.clip(c - WIN_W // 2, 0, GRID_W - WIN_W)
    col_ok = (c[None, :] >= cs[:, None]) & (c[None, :] < cs[:, None] + WIN_W)
    d_col = np.clip(c[None, :] - c[:, None] + (WIN_W - 1), 0, 2 * WIN_W - 2)
    sel_row = (d_row[..., None] == np.arange(2 * WIN_H - 1)).astype(np.float32)
    sel_col = (d_col[..., None] == np.arange(2 * WIN_W - 1)).astype(np.float32)
    rows = jnp.einsum("tikr,hrc->thikc", sel_row, rpb.astype(F32), precision=lax.Precision.HIGHEST)
    bias = jnp.einsum("thikc,qwc->thiqkw", rows, sel_col, precision=lax.Precision.HIGHEST)
    ok = row_ok[:, :, None, :, None] & col_ok[None, None, :, None, :]
    bias = jnp.where(ok[:, None], bias, NEG_INF)
    return bias.reshape(3, rpb.shape[0], NAT_ROWS * GRID_W, NAT_KROWS * GRID_W)


def natten(qkv, bias_tiles, *, batch, seq_len):
    n = batch * seq_len
    grid_rows = seq_len // GRID_W
    n_blocks = grid_rows // NAT_ROWS
    tq = NAT_ROWS * GRID_W
    tk = NAT_KROWS * GRID_W

    def tile_kind(blk):
        return jnp.where(blk == 0, 0, jnp.where(blk == n_blocks - 1, 2, 1))

    est = 4 * seq_len * HEAD_DIM * 2 + 2 * tq * tk * 4 + 3 * tq * tk * 4
    return pl.pallas_call(
        functools.partial(_natten_kernel, n_blocks=n_blocks, grid_rows=grid_rows),
        grid=(batch, N_HEADS_A, n_blocks),
        in_specs=[
            pl.BlockSpec((None, tq, HEAD_DIM), lambda b, h, i: (h, b * n_blocks + i, 0)),
            pl.BlockSpec((None, seq_len, HEAD_DIM), lambda b, h, i: (N_HEADS_QKV + h, b, 0)),
            pl.BlockSpec((None, seq_len, HEAD_DIM), lambda b, h, i: (2 * N_HEADS_QKV + h, b, 0)),
            pl.BlockSpec((None, None, tq, tk), lambda b, h, i: (tile_kind(i), h, 0, 0)),
        ],
        out_specs=pl.BlockSpec((tq, HEAD_DIM), lambda b, h, i: (b * n_blocks + i, h)),
        out_shape=jax.ShapeDtypeStruct((n, N_HEADS_A * HEAD_DIM), BF16),
        compiler_params=_cparams(("parallel", "parallel", "arbitrary"), est),
        name="natten",
    )(qkv, qkv, qkv, bias_tiles)


class _DilatedGeometry:
    def __init__(self, dil, kc):
        self.dil, self.kc = dil, kc
        self.reach = BAND_RADIUS * dil
        self.tp = DIL_TQ // DIL_PARTS
        assert kc % self.tp == 0
        self.halo = -(-self.reach // kc) * kc
        self.n_chunks = (DIL_TQ + 2 * self.halo) // kc
        self.n_tiles = (kc // self.tp) * (self.n_chunks - 1) + DIL_PARTS

    def tile_index(self, chunk, part):
        return (self.kc // self.tp) * chunk + (DIL_PARTS - 1 - part)

    def bias_tiles(self):
        row = np.arange(self.tp)[:, None]
        col = np.arange(self.kc)[None, :]
        tiles = []
        for u in range(self.n_tiles):
            rel = col - row + (u - (DIL_PARTS - 1)) * self.tp - self.halo
            ok = (np.abs(rel) <= self.reach) & (rel % self.dil == 0)
            tiles.append(np.where(ok, 0.0, -np.inf))
        return np.stack(tiles).astype(np.float32)


DIL_GEOMETRY = tuple(_DilatedGeometry(d, kc) for d, kc in zip(DILATIONS, DIL_KC))


def _dilated_kernel(b0, b1, b2, q0, q1, q2, k0, k1, k2, v0, v1, v2, o_ref, *, seq_len):
    tq = DIL_TQ
    parts = DIL_PARTS
    tp = tq // parts
    t0 = pl.program_id(2) * tq
    carry = tuple((jnp.full((tp, 1), -1e30, F32), jnp.zeros((tp, 1), F32), jnp.zeros((tp, HEAD_DIM), F32))
                  for _ in range(parts))
    for geo, bias_ref, q_ref, k_ref, v_ref in zip(DIL_GEOMETRY, (b0, b1, b2), (q0, q1, q2), (k0, k1, k2), (v0, v1, v2)):
        kc = geo.kc
        window_start = t0 - geo.halo
        lo = jnp.maximum(0, (geo.halo - t0) // kc)
        hi = jnp.minimum(geo.n_chunks, (seq_len - window_start) // kc)
        qs = tuple(q_ref[p * tp:(p + 1) * tp, :] for p in range(parts))

        def body(ci, carry, geo=geo, bias_ref=bias_ref, k_ref=k_ref, v_ref=v_ref, window_start=window_start, qs=qs):
            ks = pl.multiple_of(window_start + ci * geo.kc, geo.kc)
            kk = k_ref[pl.ds(ks, geo.kc), :]
            vv = v_ref[pl.ds(ks, geo.kc), :]
            scores = [_dot_nt(qs[p], kk) + bias_ref[geo.tile_index(ci, p)] for p in range(parts)]
            stats = []
            for p in range(parts):
                m, l, _ = carry[p]
                m_new = jnp.maximum(m, jnp.max(scores[p], axis=-1, keepdims=True))
                alpha = jnp.exp(m - m_new)
                e = jnp.exp(scores[p] - m_new)
                stats.append((m_new, alpha, alpha * l + jnp.sum(e, axis=-1, keepdims=True), e.astype(BF16)))
            return tuple((m_new, l, alpha * carry[p][2] + _dot(e, vv))
                         for p, (m_new, alpha, l, e) in enumerate(stats))

        carry = lax.fori_loop(lo, hi, body, carry)
    for p in range(parts):
        _, l, acc = carry[p]
        o_ref[p * tp:(p + 1) * tp, :] = (acc / l).astype(o_ref.dtype)


def dilated_attention(qkv, *, batch, seq_len):
    n = batch * seq_len
    nq = seq_len // DIL_TQ
    biases = [geo.bias_tiles() for geo in DIL_GEOMETRY]

    def q_spec(g):
        return pl.BlockSpec((None, DIL_TQ, HEAD_DIM),
                            lambda b, j, i: (N_HEADS_A + N_HEADS_B_GROUP * g + j, b * nq + i, 0))

    def kv_spec(part, g):
        return pl.BlockSpec((None, seq_len, HEAD_DIM),
                            lambda b, j, i: (part * N_HEADS_QKV + N_HEADS_A + N_HEADS_B_GROUP * g + j, b, 0))

    est = 12 * seq_len * HEAD_DIM * 2 + 8 * DIL_TQ * max(DIL_KC) * 4 + 2 * sum(b.size for b in biases) * 4
    groups = range(len(DILATIONS))
    return pl.pallas_call(
        functools.partial(_dilated_kernel, seq_len=seq_len),
        grid=(batch, N_HEADS_B_GROUP, nq),
        in_specs=[pl.BlockSpec(b.shape, lambda b_, j, i: (0, 0, 0)) for b in biases]
        + [q_spec(g) for g in groups] + [kv_spec(1, g) for g in groups] + [kv_spec(2, g) for g in groups],
        out_specs=pl.BlockSpec((DIL_TQ, HEAD_DIM), lambda b, j, i: (b * nq + i, j)),
        out_shape=jax.ShapeDtypeStruct((n, N_HEADS_B_GROUP * HEAD_DIM), BF16),
        compiler_params=_cparams(("parallel", "parallel", "arbitrary"), est),
        name="dilated_attention",
    )(*biases, *([qkv] * 9))


def _attn_out_kernel(x_ref, oa_ref, ob_ref, w_ref, o_ref):
    ka = oa_ref.shape[1]
    o_ref[...] = x_ref[...] + _dot(oa_ref[...], w_ref[:ka, :]) + _dot(ob_ref[...], w_ref[ka:, :])


def attn_out(x, o_a, o_b, w, *, tm=512):
    n, d = x.shape
    ka, kb = o_a.shape[1], o_b.shape[1]
    tm = min(tm, n)
    est = 4 * tm * d * 4 + 2 * (ka + kb) * d * 2 + 2 * tm * (ka + kb) * 2
    return pl.pallas_call(
        _attn_out_kernel,
        grid=(n // tm,),
        in_specs=[
            pl.BlockSpec((tm, d), lambda i: (i, 0)),
            pl.BlockSpec((tm, ka), lambda i: (i, 0)),
            pl.BlockSpec((tm, kb), lambda i: (i, 0)),
            pl.BlockSpec((ka + kb, d), lambda i: (0, 0)),
        ],
        out_specs=pl.BlockSpec((tm, d), lambda i: (i, 0)),
        out_shape=jax.ShapeDtypeStruct((n, d), F32),
        compiler_params=_cparams(("parallel",), est),
        name="attn_out",
    )(x, o_a, o_b, w)


def _ffn_kernel(x_ref, g_ref, wg_ref, wu_ref, wd_ref, o_ref, xn_ref):
    f = pl.program_id(1)

    @pl.when(f == 0)
    def _():
        xn_ref[...] = _rmsnorm(x_ref[...], g_ref[...]).astype(BF16)
        o_ref[...] = x_ref[...]

    xn = xn_ref[...]
    gate = _dot(xn, wg_ref[...])
    up = _dot(xn, wu_ref[...])
    hid = (gate * _sigmoid(gate) * up).astype(BF16)
    o_ref[...] += _dot(hid, wd_ref[...])


def ffn(x, g, wg_tiles, wu_tiles, wd, *, tm=512):
    n, d = x.shape
    nf, _, tf = wg_tiles.shape
    tm = min(tm, n)
    est = 4 * tm * d * 4 + tm * d * 2 + 6 * d * tf * 2 + 3 * tm * tf * 4
    return pl.pallas_call(
        _ffn_kernel,
        grid=(n // tm, nf),
        in_specs=[
            pl.BlockSpec((tm, d), lambda i, f: (i, 0)),
            pl.BlockSpec((1, d), lambda i, f: (0, 0)),
            pl.BlockSpec((None, d, tf), lambda i, f: (f, 0, 0)),
            pl.BlockSpec((None, d, tf), lambda i, f: (f, 0, 0)),
            pl.BlockSpec((tf, d), lambda i, f: (f, 0)),
        ],
        out_specs=pl.BlockSpec((tm, d), lambda i, f: (i, 0)),
        out_shape=jax.ShapeDtypeStruct((n, d), F32),
        scratch_shapes=[pltpu.VMEM((tm, d), BF16)],
        compiler_params=_cparams(("parallel", "arbitrary"), est),
        name="ffn",
    )(x, g.reshape(1, d), wg_tiles, wu_tiles, wd)


def _conv_kernel(xm_ref, xp_ref, xn_ref, w_ref, b_ref, o_ref, *rest, tr, n_row_blocks):
    ext_ref = rest[-1]
    i = pl.program_id(2)
    hb = BF16_SUBLANES
    pad = SSM_CONV // 2
    ext_ref[0:hb, :] = jnp.where(i > 0, xp_ref[...].astype(F32), 0.0)
    ext_ref[hb:hb + tr, :] = xm_ref[...].astype(F32)
    ext_ref[hb + tr:2 * hb + tr, :] = jnp.where(i < n_row_blocks - 1, xn_ref[...].astype(F32), 0.0)
    acc = jnp.broadcast_to(b_ref[...], o_ref.shape)
    for k in range(SSM_CONV):
        acc = acc + ext_ref[pl.ds(hb - pad + k, tr), :] * w_ref[k:k + 1, :]
    out = acc * _sigmoid(acc)
    o_ref[...] = out.astype(o_ref.dtype)
    if len(rest) == 2:
        ot_ref = rest[0]
        L = ot_ref.shape[-1]
        for c in range(tr // L):
            ot_ref[c] = out[c * L:(c + 1) * L, :].T.astype(ot_ref.dtype)


def conv_silu(zx, conv_w, conv_b, *, width, first_tile, n_tiles, col_offset, batch, seq_len, tr, transposed=False):
    n = batch * seq_len
    tile_w = zx.shape[2]
    per = tile_w // width
    tr = min(tr, seq_len)
    nr = seq_len // tr
    hb = BF16_SUBLANES
    seq_hb = seq_len // hb
    n_hb = n // hb

    def main_map(c, b, i):
        return (first_tile + c // per, b * nr + i, c % per)

    def prev_map(c, b, i):
        return (first_tile + c // per, jnp.maximum(b * seq_hb + i * (tr // hb) - 1, 0), c % per)

    def next_map(c, b, i):
        return (first_tile + c // per, jnp.minimum(b * seq_hb + (i + 1) * (tr // hb), n_hb - 1), c % per)

    L = SSM_CHUNK
    out_specs = [pl.BlockSpec((None, tr, width), lambda c, b, i: (c, b * nr + i, 0))]
    out_shape = [jax.ShapeDtypeStruct((n_tiles, n, width), BF16)]
    if transposed:
        out_specs.append(pl.BlockSpec((None, tr // L, width, L), lambda c, b, i: (c, b * nr + i, 0, 0)))
        out_shape.append(jax.ShapeDtypeStruct((n_tiles, n // L, width, L), BF16))
    est = 4 * tr * width * 2 + (tr + 2 * hb) * width * 4 + 4 * tr * width * 4 + 4 * tr * width * 2
    outs = pl.pallas_call(
        functools.partial(_conv_kernel, tr=tr, n_row_blocks=nr),
        grid=(n_tiles, batch, nr),
        in_specs=[
            pl.BlockSpec((None, tr, width), main_map),
            pl.BlockSpec((None, hb, width), prev_map),
            pl.BlockSpec((None, hb, width), next_map),
            pl.BlockSpec((SSM_CONV, width), lambda c, b, i: (0, col_offset // width + c)),
            pl.BlockSpec((1, width), lambda c, b, i: (0, col_offset // width + c)),
        ],
        out_specs=out_specs,
        out_shape=out_shape,
        scratch_shapes=[pltpu.VMEM((tr + 2 * hb, width), F32)],
        compiler_params=_cparams(("parallel", "parallel", "arbitrary"), est),
        name="conv_silu",
    )(zx, zx, zx, conv_w, conv_b.reshape(1, -1))
    return outs if transposed else outs[0]


def _lane_cumsum(a):
    lane = lax.broadcasted_iota(jnp.int32, a.shape, 1)
    shift = 1
    while shift < a.shape[1]:
        a = a + jnp.where(lane >= shift, pltpu.roll(a, shift, 1), 0.0)
        shift *= 2
    return a


def _ssd_decay_rows(dt_ref, dtbias_ref, alog_ref, seg_scr, dt_scr, *, hpg):
    L = SSM_CHUNK
    dt_all = _softplus(dt_ref[...] + dtbias_ref[...])
    a_all = dt_all * (-jnp.exp(alog_ref[...]))
    cum = _lane_cumsum(a_all)
    suf = cum[:, L - 1:L] - cum + a_all
    row = lax.broadcasted_iota(jnp.int32, cum.shape, 0)
    seg_scr[...] = jnp.where((row & (2 * hpg - 1)) < hpg, cum, suf)
    dt_scr[...] = dt_all


def _ssd_kernel(xa_ref, ba_ref, bta_ref, ca_ref, dta_ref, xb_ref, btb_ref, cb_ref, dtb_ref,
                dtbias_ref, alog_ref, dskip_ref, yf_ref, yb_ref,
                sf_ref, sb_ref, sega_scr, dta_scr, segb_scr, dtb_scr, *, hpg):
    L = SSM_CHUNK
    P = SSM_HEAD_DIM
    pairs = hpg // 2

    @pl.when(pl.program_id(1) == 0)
    def _():
        sf_ref[...] = jnp.zeros_like(sf_ref)
        sb_ref[...] = jnp.zeros_like(sb_ref)

    _ssd_decay_rows(dta_ref, dtbias_ref, alog_ref, sega_scr, dta_scr, hpg=hpg)
    _ssd_decay_rows(dtb_ref, dtbias_ref, alog_ref, segb_scr, dtb_scr, hpg=hpg)

    li = lax.broadcasted_iota(jnp.int32, (L, L), 0)
    si = lax.broadcasted_iota(jnp.int32, (L, L), 1)
    causal = li >= si
    anti = li <= si
    low_lanes = lax.broadcasted_iota(jnp.int32, (L, 2 * P), 1) < P
    high_lanes = jnp.logical_not(low_lanes)
    low_lanes_row = lax.broadcasted_iota(jnp.int32, (1, 2 * P), 1) < P

    def lanes_of(col):
        return jnp.broadcast_to(col, (L, 2 * P))

    def columns(seg_r, dt_r):
        stacked = jnp.concatenate([seg_r, dt_r, jnp.zeros((L - 4 * hpg, L), F32)], axis=0)
        return stacked.T

    spread_bwd = (lax.broadcasted_iota(jnp.int32, (L, hpg * P), 0)
                  == hpg + lax.broadcasted_iota(jnp.int32, (L, hpg * P), 1) // P).astype(BF16)

    def advance_state(s_ref, g, lanes, state, btf, xm0, xm1, seg_r, dt_r, h0, h1, tot_lane):
        acc = None
        decays = []
        for h, xm in ((h0, xm0), (h1, xm1)):
            tot = seg_r[h:h + 1, tot_lane:tot_lane + 1]
            coef = jnp.exp(tot - seg_r[h:h + 1, :]) * dt_r[h:h + 1, :]
            part = _dot((btf * coef).astype(BF16), xm)
            acc = part if acc is None else acc + part
            decays.append(jnp.exp(tot))
        decay = jnp.where(low_lanes_row, decays[0], decays[1])
        s_ref[g, :, lanes] = state[:, lanes] * decay + acc

    def group_body(g, c):
        r0 = pl.multiple_of(g * 2 * hpg, 2 * hpg)

        seg_r = sega_scr[pl.ds(r0, 2 * hpg), :]
        dt_r = dta_scr[pl.ds(r0, 2 * hpg), :]
        cols = columns(seg_r, dt_r)
        xg = xa_ref[g]
        cg = ca_ref[g]
        state = sf_ref[g]
        carried = _dot(cg, state.astype(BF16))
        cbm = _dot_nt(cg, ba_ref[g])
        btf = bta_ref[g].astype(F32)
        for p in range(pairs):
            lanes = slice(p * 2 * P, (p + 1) * 2 * P)
            xpf = xg[:, lanes].astype(F32)
            xm0 = jnp.where(low_lanes, xpf, 0.0).astype(BF16)
            xm1 = jnp.where(high_lanes, xpf, 0.0).astype(BF16)
            h0, h1 = 2 * p, 2 * p + 1
            f0, f1 = lanes_of(cols[:, h0:h0 + 1]), lanes_of(cols[:, h1:h1 + 1])
            y = carried[:, lanes] * jnp.where(low_lanes, jnp.exp(f0), jnp.exp(f1)) + dskip_ref[g][:, lanes] * xpf
            for hh, f_cols, xm in ((h0, f0, xm0), (h1, f1, xm1)):
                wf = jnp.exp(jnp.where(causal, f_cols - seg_r[hh:hh + 1, :], NEG_INF)) * dt_r[hh:hh + 1, :]
                b_cols = lanes_of(cols[:, hpg + hh:hpg + hh + 1])
                wb = jnp.exp(jnp.where(anti, b_cols - seg_r[hpg + hh:hpg + hh + 1, :], NEG_INF)) \
                    * dt_r[hpg + hh:hpg + hh + 1, :]
                y = y + _dot((cbm * (wf + wb)).astype(BF16), xm)
            yf_ref[g, :, lanes] = y.astype(yf_ref.dtype)
            advance_state(sf_ref, g, lanes, state, btf, xm0, xm1, seg_r, dt_r, h0, h1, L - 1)

        seg_r = segb_scr[pl.ds(r0, 2 * hpg), :]
        dt_r = dtb_scr[pl.ds(r0, 2 * hpg), :]
        xg = xb_ref[g]
        state = sb_ref[g]
        grow = _dot(jnp.exp(columns(seg_r, dt_r)).astype(BF16), spread_bwd)
        y = _dot(cb_ref[g], state.astype(BF16)) * grow
        yb_ref[g] = y.astype(yb_ref.dtype)
        btf = btb_ref[g].astype(F32)
        for p in range(pairs):
            lanes = slice(p * 2 * P, (p + 1) * 2 * P)
            xpf = xg[:, lanes].astype(F32)
            xm0 = jnp.where(low_lanes, xpf, 0.0).astype(BF16)
            xm1 = jnp.where(high_lanes, xpf, 0.0).astype(BF16)
            advance_state(sb_ref, g, lanes, state, btf, xm0, xm1, seg_r, dt_r, hpg + 2 * p, hpg + 2 * p + 1, 0)
        return c

    lax.fori_loop(0, SSM_GROUPS, group_body, 0)


def ssd(xs, b_nat, b_t, c_nat, dt_rows, dt_bias_rows, a_log_rows, d_skip_rows, *, batch, seq_len):
    groups, n, gw = xs.shape
    hpg = gw // SSM_HEAD_DIM
    assert hpg & (hpg - 1) == 0
    L = SSM_CHUNK
    nc = seq_len // L
    rows = groups * 2 * hpg

    def fwd(b, k):
        return b * nc + k

    def bwd(b, k):
        return b * nc + nc - 1 - k

    def specs(chunk, with_b_nat):
        out = [pl.BlockSpec((groups, L, gw), lambda b, k: (0, chunk(b, k), 0))]
        if with_b_nat:
            out.append(pl.BlockSpec((groups, L, SSM_STATE), lambda b, k: (0, chunk(b, k), 0)))
        out += [
            pl.BlockSpec((groups, None, SSM_STATE, L), lambda b, k: (0, chunk(b, k), 0, 0)),
            pl.BlockSpec((groups, L, SSM_STATE), lambda b, k: (0, chunk(b, k), 0)),
            pl.BlockSpec((None, rows, L), lambda b, k: (chunk(b, k), 0, 0)),
        ]
        return out

    const = [
        pl.BlockSpec((rows, 1), lambda b, k: (0, 0)),
        pl.BlockSpec((rows, 1), lambda b, k: (0, 0)),
        pl.BlockSpec((groups, 1, gw), lambda b, k: (0, 0, 0)),
    ]
    y_shape = jax.ShapeDtypeStruct((groups, n, gw), BF16)
    est = 4 * (groups * L * gw * 2 + 3 * groups * L * SSM_STATE * 2 + rows * L * 4) + 4 * groups * L * gw * 2 \
        + 2 * groups * SSM_STATE * gw * 4 + 4 * rows * L * 4 + 64 * L * L * 4
    return pl.pallas_call(
        functools.partial(_ssd_kernel, hpg=hpg),
        grid=(batch, nc),
        in_specs=specs(fwd, True) + specs(bwd, False) + const,
        out_specs=[pl.BlockSpec((groups, L, gw), lambda b, k: (0, fwd(b, k), 0)),
                   pl.BlockSpec((groups, L, gw), lambda b, k: (0, bwd(b, k), 0))],
        out_shape=[y_shape, y_shape],
        scratch_shapes=[pltpu.VMEM((groups, SSM_STATE, gw), F32), pltpu.VMEM((groups, SSM_STATE, gw), F32)]
        + [pltpu.VMEM((rows, L), F32)] * 4,
        compiler_params=_cparams(("parallel", "arbitrary"), est),
        name="ssd",
    )(xs, b_nat, b_t, c_nat, dt_rows, xs, b_t, c_nat, dt_rows, dt_bias_rows, a_log_rows, d_skip_rows)


def _mamba_out_kernel(yf_ref, yb_ref, z_ref, gg_ref, w_ref, x_ref, o_ref, ssq_ref, *, d_inner):
    step = pl.program_id(1)
    groups_per_step, _, gw = z_ref.shape

    @pl.when(step == 0)
    def _():
        o_ref[...] = jnp.zeros_like(o_ref)
        ssq_ref[...] = jnp.zeros_like(ssq_ref)

    gated = []
    for k in range(groups_per_step):
        z = z_ref[k].astype(F32)
        yz = (yf_ref[k].astype(F32) + yb_ref[k].astype(F32)) * (z * _sigmoid(z))
        ssq_ref[...] += jnp.sum(yz * yz, axis=-1, keepdims=True)
        gated.append((yz * gg_ref[:, k * gw:(k + 1) * gw]).astype(BF16))
    o_ref[...] += _dot(jnp.concatenate(gated, axis=1), w_ref[...])

    @pl.when(step == pl.num_programs(1) - 1)
    def _():
        o_ref[...] = x_ref[...] + o_ref[...] * lax.rsqrt(ssq_ref[...] / d_inner + EPS)


def mamba_out(y_f, y_b, zx, g_gate, w_out, x, *, tm=512, groups_per_step=2):
    groups, n, gw = y_f.shape
    d = x.shape[1]
    tm = min(tm, n)
    gps = groups_per_step
    est = gps * (6 * tm * gw * 2 + 2 * gw * d * 2) + 3 * tm * d * 4 + 2 * tm * gps * gw * 4
    return pl.pallas_call(
        functools.partial(_mamba_out_kernel, d_inner=groups * gw),
        grid=(n // tm, groups // gps),
        in_specs=[
            pl.BlockSpec((gps, tm, gw), lambda i, s: (s, i, 0)),
            pl.BlockSpec((gps, tm, gw), lambda i, s: (s, i, 0)),
            pl.BlockSpec((gps, tm, gw), lambda i, s: (s, i, 0)),
            pl.BlockSpec((1, gps * gw), lambda i, s: (0, s)),
            pl.BlockSpec((gps * gw, d), lambda i, s: (s, 0)),
            pl.BlockSpec((tm, d), lambda i, s: (i, 0)),
        ],
        out_specs=pl.BlockSpec((tm, d), lambda i, s: (i, 0)),
        out_shape=jax.ShapeDtypeStruct((n, d), F32),
        scratch_shapes=[pltpu.VMEM((tm, 1), F32)],
        compiler_params=_cparams(("parallel", "arbitrary"), est),
        name="mamba_out",
    )(y_f, y_b, zx, g_gate.reshape(1, -1), w_out, x)


def _router_kernel(xa_ref, xb_ref, g_ref, w_ref, o_ref, xn_ref, *, tiles_a):
    i = pl.program_id(0)

    @pl.when(i < tiles_a)
    def _():
        xn_ref[...] = _rmsnorm(xa_ref[...], g_ref[...])

    @pl.when(i >= tiles_a)
    def _():
        xn_ref[...] = _rmsnorm(xb_ref[...], g_ref[...])

    xn = xn_ref[...]
    logits = jnp.dot(xn, w_ref[...], preferred_element_type=F32, precision=lax.Precision.HIGHEST)
    lane = lax.broadcasted_iota(jnp.int32, logits.shape, 1)
    logits = jnp.where(lane < N_EXPERTS, logits, NEG_INF)
    v1 = jnp.max(logits, axis=-1, keepdims=True)
    i1 = jnp.min(jnp.where(logits == v1, lane, LANES), axis=-1, keepdims=True)
    rest = jnp.where(lane == i1, NEG_INF, logits)
    v2 = jnp.max(rest, axis=-1, keepdims=True)
    i2 = jnp.min(jnp.where(rest == v2, lane, LANES), axis=-1, keepdims=True)
    e2 = jnp.exp(v2 - v1)
    g1 = 1.0 / (1.0 + e2)
    g2 = e2 / (1.0 + e2)
    out = jnp.where(lane == 0, i1.astype(F32),
                    jnp.where(lane == 1, i2.astype(F32),
                              jnp.where(lane == 2, g1, jnp.where(lane == 3, g2, 0.0))))
    o_ref[...] = out


def router(x_a, x_b, g, w_router, *, tm=512):
    (na, d), nb = x_a.shape, x_b.shape[0]
    tm = min(tm, na, nb)
    tiles_a, tiles_b = na // tm, nb // tm
    w = jnp.zeros((d, LANES), F32).at[:, :w_router.shape[1]].set(w_router)
    est = 4 * tm * d * 4 + 2 * d * LANES * 4 + 2 * tm * LANES * 4 + 3 * tm * d * 4
    return pl.pallas_call(
        functools.partial(_router_kernel, tiles_a=tiles_a),
        grid=(tiles_a + tiles_b,),
        in_specs=[
            pl.BlockSpec((tm, d), lambda i: (jnp.minimum(i, tiles_a - 1), 0)),
            pl.BlockSpec((tm, d), lambda i: (jnp.maximum(i - tiles_a, 0), 0)),
            pl.BlockSpec((1, d), lambda i: (0, 0)),
            pl.BlockSpec((d, LANES), lambda i: (0, 0)),
        ],
        out_specs=[pl.BlockSpec((tm, LANES), lambda i: (i, 0)), pl.BlockSpec((tm, d), lambda i: (i, 0))],
        out_shape=[jax.ShapeDtypeStruct((na + nb, LANES), F32), jax.ShapeDtypeStruct((na + nb, d), F32)],
        compiler_params=_cparams(("parallel",), est),
        name="router",
    )(x_a, x_b, g.reshape(1, d), w)


def _row_copy(src_hbm, dst_vmem, sem, src_row, dst_row):
    return pltpu.make_async_copy(src_hbm.at[pl.ds(src_row, 1), :], dst_vmem.at[pl.ds(dst_row, 1), :], sem)


def _gather_kernel(tok_ref, used_ref, x_hbm, o_ref, buf_ref, sem):
    rows = buf_ref.shape[0]
    base = pl.program_id(0) * rows
    live = base < used_ref[0]

    def start(r, c):
        _row_copy(x_hbm, buf_ref, sem, tok_ref[base + r], r).start()
        return c

    def wait(r, c):
        _row_copy(x_hbm, buf_ref, sem, 0, r).wait()
        return c

    @pl.when(live)
    def _():
        lax.fori_loop(0, rows, start, 0, unroll=8)
        lax.fori_loop(0, rows, wait, 0, unroll=8)
        o_ref[...] = buf_ref[...].astype(o_ref.dtype)

    @pl.when(jnp.logical_not(live))
    def _():
        o_ref[...] = jnp.zeros_like(o_ref)


def gather_rows(tok_of_slot, slots_used, x, *, rows=GATHER_ROWS):
    slots = tok_of_slot.shape[0]
    d = x.shape[1]
    est = rows * d * 4 * 2 + 2 * rows * d * 2
    return pl.pallas_call(
        _gather_kernel,
        grid_spec=pltpu.PrefetchScalarGridSpec(
            num_scalar_prefetch=2,
            grid=(slots // rows,),
            in_specs=[pl.BlockSpec(memory_space=pl.ANY)],
            out_specs=pl.BlockSpec((rows, d), lambda i, tok, used: (i, 0)),
            scratch_shapes=[pltpu.VMEM((rows, d), F32), pltpu.SemaphoreType.DMA(())],
        ),
        out_shape=jax.ShapeDtypeStruct((slots, d), BF16),
        compiler_params=_cparams(("arbitrary",), est),
        name="gather_rows",
    )(tok_of_slot, slots_used, x)


def _expert_kernel(blk_e_ref, n_used_ref, x_ref, wg_ref, wu_ref, wd_ref, o_ref):
    i = pl.program_id(0)
    f = pl.program_id(1)

    @pl.when(f == 0)
    def _():
        o_ref[...] = jnp.zeros_like(o_ref)

    @pl.when(i < n_used_ref[0])
    def _():
        x = x_ref[...]
        gate = _dot(x, wg_ref[...].astype(BF16))
        up = _dot(x, wu_ref[...].astype(BF16))
        hid = (gate * _sigmoid(gate) * up).astype(BF16)
        o_ref[...] += _dot(hid, wd_ref[...].astype(BF16))


def expert_ffn(blk_expert, n_used, xs, wg, wu, wd, *, tm=MOE_TM):
    slots, d = xs.shape
    _, nf, _, tf = wg.shape
    n_blocks = slots // tm

    def live(i, n_used):
        return jnp.minimum(i, n_used[0] - 1)

    def f_eff(i, f, n_used):
        return jnp.where(i < n_used[0], f, nf - 1)

    wbytes = wg.dtype.itemsize + wu.dtype.itemsize + wd.dtype.itemsize
    est = 2 * tm * d * 2 + 2 * d * tf * wbytes + 3 * d * tf * 2 + 2 * tm * d * 4 + 3 * tm * tf * 4
    return pl.pallas_call(
        _expert_kernel,
        grid_spec=pltpu.PrefetchScalarGridSpec(
            num_scalar_prefetch=2,
            grid=(n_blocks, nf),
            in_specs=[
                pl.BlockSpec((tm, d), lambda i, f, be, nu: (live(i, nu), 0)),
                pl.BlockSpec((None, None, d, tf), lambda i, f, be, nu: (be[live(i, nu)], f_eff(i, f, nu), 0, 0)),
                pl.BlockSpec((None, None, d, tf), lambda i, f, be, nu: (be[live(i, nu)], f_eff(i, f, nu), 0, 0)),
                pl.BlockSpec((None, tf, d), lambda i, f, be, nu: (be[live(i, nu)], f_eff(i, f, nu), 0)),
            ],
            out_specs=pl.BlockSpec((tm, d), lambda i, f, be, nu: (i, 0)),
        ),
        out_shape=jax.ShapeDtypeStruct((slots, d), F32),
        compiler_params=_cparams(("arbitrary", "arbitrary"), est),
        name="expert_ffn",
    )(blk_expert, n_used, xs, wg, wu, wd)


def _combine_kernel(slot_ref, y_hbm, x_ref, r_ref, g_ref, o_ref, buf_ref, sem):
    rows = x_ref.shape[0]
    base = pl.program_id(0) * rows

    def start(r, c):
        for k in range(TOP_K):
            _row_copy(y_hbm, buf_ref.at[k], sem, slot_ref[TOP_K * (base + r) + k], r).start()
        return c

    def wait(r, c):
        for k in range(TOP_K):
            _row_copy(y_hbm, buf_ref.at[k], sem, 0, r).wait()
        return c

    lax.fori_loop(0, rows, start, 0, unroll=8)
    lax.fori_loop(0, rows, wait, 0, unroll=8)
    gates = r_ref[...]
    out = x_ref[...] + gates[:, 2:3] * buf_ref[0] + gates[:, 3:4] * buf_ref[1]
    o_ref[...] = _rmsnorm(out, g_ref[...])


def combine_norm(slot_of_assignment, ys, x, routed, g_final, *, rows=GATHER_ROWS):
    n, d = x.shape
    rows = min(rows, n)
    est = 2 * rows * d * 4 + 4 * rows * d * 4 + 4 * rows * d * 4
    return pl.pallas_call(
        _combine_kernel,
        grid_spec=pltpu.PrefetchScalarGridSpec(
            num_scalar_prefetch=1,
            grid=(n // rows,),
            in_specs=[
                pl.BlockSpec(memory_space=pl.ANY),
                pl.BlockSpec((rows, d), lambda i, s: (i, 0)),
                pl.BlockSpec((rows, LANES), lambda i, s: (i, 0)),
                pl.BlockSpec((1, d), lambda i, s: (0, 0)),
            ],
            out_specs=pl.BlockSpec((rows, d), lambda i, s: (i, 0)),
            scratch_shapes=[pltpu.VMEM((TOP_K, rows, d), F32), pltpu.SemaphoreType.DMA(())],
        ),
        out_shape=jax.ShapeDtypeStruct((n, d), F32),
        compiler_params=_cparams(("arbitrary",), est),
        name="combine_norm",
    )(slot_of_assignment, ys, x, routed, g_final.reshape(1, d))


def moe_plan(routed, *, tm):
    n = routed.shape[0]
    experts = routed[:, :TOP_K].astype(jnp.int32).reshape(-1)
    onehot = (experts[:, None] == jnp.arange(N_EXPERTS)[None, :]).astype(jnp.int32)
    rank = jnp.sum((jnp.cumsum(onehot, axis=0) - onehot) * onehot, axis=1)
    counts = jnp.sum(onehot, axis=0)
    padded = ((counts + tm - 1) // tm) * tm
    ends = jnp.cumsum(padded)
    starts = ends - padded
    slot = (starts[experts] + rank).astype(jnp.int32)
    n_blocks = (n * TOP_K) // tm + N_EXPERTS
    tok = jnp.repeat(jnp.arange(n, dtype=jnp.int32), TOP_K)
    tok_of_slot = jnp.zeros((n_blocks * tm,), jnp.int32).at[slot].set(tok)
    blk_expert = jnp.minimum(
        jnp.searchsorted(ends, jnp.arange(n_blocks, dtype=jnp.int32) * tm, side="right"), N_EXPERTS - 1
    ).astype(jnp.int32)
    n_used = (ends[-1] // tm).astype(jnp.int32).reshape(1)
    return slot, tok_of_slot, blk_expert, n_used


def _prepare_weights(w_qkv, w_o, w_ff_gate, w_ff_up, w_ff_down, w_in_c, w_out_c):
    d_inner = w_out_c.shape[1]
    gw = d_inner // SSM_GROUPS
    main_cols = 2 * d_inner + 2 * SSM_GROUPS * SSM_STATE
    return dict(
        w_qkv=column_tiles(w_qkv[0], N_HEADS_A * HEAD_DIM), w_o=w_o[0].astype(BF16),
        w_ff_gate=column_tiles(w_ff_gate[0], FF_TILE), w_ff_up=column_tiles(w_ff_up[0], FF_TILE),
        w_ff_down=w_ff_down[0].astype(BF16),
        w_in_main=column_tiles(w_in_c[0][:, :main_cols], gw),
        w_in_dt=column_tiles(w_in_c[0][:, main_cols:], w_in_c.shape[2] - main_cols),
        w_out=w_out_c[0].astype(BF16),
    )


def _mixer_layers(x3, wb, g_mix, g_ffn, rpb, conv_w, conv_b, dt_bias, a_log, d_skip, g_gate):
    batch, seq_len, d = x3.shape
    n = batch * seq_len
    x = x3.reshape(n, d)

    cos, sin = rope_tables(seq_len)
    qkv = qkv_proj(x, g_mix[0], wb["w_qkv"], cos, sin, seq_len=seq_len)
    bias_tiles = natten_bias_tiles(rpb[0], seq_len // GRID_W)
    o_a = natten(qkv, bias_tiles, batch=batch, seq_len=seq_len)
    o_b = dilated_attention(qkv, batch=batch, seq_len=seq_len)
    x = attn_out(x, o_a, o_b, wb["w_o"])
    x = ffn(x, g_ffn[0], wb["w_ff_gate"], wb["w_ff_up"], wb["w_ff_down"])

    d_inner = wb["w_out"].shape[0]
    gw = d_inner // SSM_GROUPS
    hpg = gw // SSM_HEAD_DIM
    heads = SSM_GROUPS * hpg
    zx = norm_matmul(x, g_mix[1], wb["w_in_main"], out_dtype=BF16)
    dt_raw = norm_matmul(x, g_mix[1], wb["w_in_dt"], out_dtype=F32)[0]
    z_tiles = d_inner // gw
    xs = conv_silu(zx, conv_w[0], conv_b[0], width=gw, first_tile=z_tiles, n_tiles=SSM_GROUPS,
                   col_offset=0, batch=batch, seq_len=seq_len, tr=2048)
    bc_tiles = SSM_GROUPS * SSM_STATE // gw
    b_nat, b_t = conv_silu(zx, conv_w[0], conv_b[0], width=SSM_STATE, first_tile=2 * z_tiles, n_tiles=SSM_GROUPS,
                           col_offset=d_inner, batch=batch, seq_len=seq_len, tr=4096, transposed=True)
    c_nat = conv_silu(zx, conv_w[0], conv_b[0], width=SSM_STATE, first_tile=2 * z_tiles + bc_tiles,
                      n_tiles=SSM_GROUPS, col_offset=d_inner + SSM_GROUPS * SSM_STATE,
                      batch=batch, seq_len=seq_len, tr=4096)
    L = SSM_CHUNK
    rows = SSM_GROUPS * 2 * hpg
    dt_rows = dt_raw.reshape(n // L, L, 2, SSM_GROUPS, hpg).transpose(0, 3, 2, 4, 1).reshape(n // L, rows, L)

    def per_row(p):
        return p.reshape(2, SSM_GROUPS, hpg).transpose(1, 0, 2).reshape(rows, 1)

    d_skip_rows = jnp.repeat(d_skip[0].reshape(SSM_GROUPS, 1, hpg), SSM_HEAD_DIM, axis=2)
    y_f, y_b = ssd(xs, b_nat, b_t, c_nat, dt_rows, per_row(dt_bias[0]), per_row(a_log[0]), d_skip_rows,
                   batch=batch, seq_len=seq_len)
    return mamba_out(y_f, y_b, zx, g_gate[0], wb["w_out"], x)


def kernel(x_prompt, x_sample, g_mix, g_ffn, w_qkv, rpb, w_o, w_ff_gate, w_ff_up, w_ff_down, w_in_c, conv_w, conv_b,
           dt_bias, a_log, d_skip, g_gate, w_out_c, w_router, w_e_gate, w_e_up, w_e_down, g_final):
    wb = _prepare_weights(w_qkv, w_o, w_ff_gate, w_ff_up, w_ff_down, w_in_c, w_out_c)
    args = (wb, g_mix, g_ffn, rpb, conv_w, conv_b, dt_bias, a_log, d_skip, g_gate)
    x_p = _mixer_layers(x_prompt, *args)
    x_s = _mixer_layers(x_sample, *args)

    n_p = x_p.shape[0]
    routed, xn = router(x_p, x_s, g_ffn[1], w_router[0])
    slot, tok_of_slot, blk_expert, n_used = moe_plan(routed, tm=MOE_TM)
    xs_sorted = gather_rows(tok_of_slot, n_used * MOE_TM, xn)
    ys = expert_ffn(blk_expert, n_used, xs_sorted, column_tiles(w_e_gate[0], FF_TILE),
                    column_tiles(w_e_up[0], FF_TILE), w_e_down[0])
    out_p = combine_norm(slot[:TOP_K * n_p], ys, x_p, routed[:n_p], g_final)
    out_s = combine_norm(slot[TOP_K * n_p:], ys, x_s, routed[n_p:], g_final)
    return out_p.reshape(x_prompt.shape), out_s.reshape(x_sample.shape)
```

```python
import functools
import math

import jax
import jax.numpy as jnp
import numpy as np
from jax import lax
from jax.experimental import pallas as pl
from jax.experimental.pallas import tpu as pltpu

F32 = jnp.float32
BF16 = jnp.bfloat16
EPS = 1e-6
NEG_INF = float("-inf")

GRID_W = 64
HEAD_DIM = 128
N_HEADS_A = 4
N_HEADS_B_GROUP = 4
DILATIONS = (1, 4, 16)
BAND_RADIUS = 64
N_HEADS_QKV = N_HEADS_A + N_HEADS_B_GROUP * len(DILATIONS)
WIN_H = 8
WIN_W = 16
ROPE_THETA = 10000.0
SSM_HEAD_DIM = 64
SSM_GROUPS = 8
SSM_STATE = 128
SSM_CONV = 5
SSM_CHUNK = 128
N_EXPERTS = 8
TOP_K = 2

V7X_VMEM_BYTES = 64 * 1024 * 1024
LANES = 128
BF16_SUBLANES = 16

NAT_ROWS = 8
NAT_KROWS = NAT_ROWS + WIN_H - 1
DIL_TQ = 512
DIL_KC = (256, 256, 512)
DIL_PARTS = 4
MOE_TM = 512
GATHER_ROWS = 256
FF_TILE = 512


def _cparams(semantics, vmem_estimate):
    limit = int(min(max(2 * vmem_estimate, 32 * 1024 * 1024), V7X_VMEM_BYTES - 8 * 1024 * 1024))
    return pltpu.CompilerParams(dimension_semantics=semantics, vmem_limit_bytes=limit)


def _rmsnorm(x, g):
    return x * lax.rsqrt(jnp.mean(x * x, axis=-1, keepdims=True) + EPS) * g


def _sigmoid(x):
    return 1.0 / (1.0 + jnp.exp(-x))


def _softplus(x):
    return jnp.maximum(x, 0.0) + jnp.log(1.0 + jnp.exp(-jnp.abs(x)))


def _dot(a, b):
    return jnp.dot(a, b, preferred_element_type=F32)


def _dot_nt(a, b):
    return lax.dot_general(a, b, (((1,), (1,)), ((), ())), preferred_element_type=F32)


def _norm_matmul_kernel(x_ref, g_ref, w_ref, o_ref, xn_ref):
    @pl.when(pl.program_id(1) == 0)
    def _():
        xn_ref[...] = _rmsnorm(x_ref[...], g_ref[...]).astype(BF16)

    r = _dot(xn_ref[...], w_ref[...])
    width = o_ref.shape[-1]
    for t in range(o_ref.shape[0]):
        o_ref[t] = r[:, t * width:(t + 1) * width].astype(o_ref.dtype)


def column_tiles(w, tn):
    *lead, k, m = w.shape
    w = w.astype(BF16).reshape(*lead, k, m // tn, tn)
    return jnp.swapaxes(w, -3, -2)


def norm_matmul(x, g, w_tiles, *, out_width, out_dtype, tm=1024):
    n, k = x.shape
    n_tiles, _, tn = w_tiles.shape
    m = n_tiles * tn
    per = tn // out_width
    tm = min(tm, n)
    est = 2 * tm * k * 4 + tm * k * 2 + 2 * k * tn * 2 + 3 * tm * tn * 4
    return pl.pallas_call(
        _norm_matmul_kernel,
        grid=(n // tm, m // tn),
        in_specs=[
            pl.BlockSpec((tm, k), lambda i, j: (i, 0)),
            pl.BlockSpec((1, k), lambda i, j: (0, 0)),
            pl.BlockSpec((None, k, tn), lambda i, j: (j, 0, 0)),
        ],
        out_specs=pl.BlockSpec((per, tm, out_width), lambda i, j: (j, i, 0)),
        out_shape=jax.ShapeDtypeStruct((m // out_width, n, out_width), out_dtype),
        scratch_shapes=[pltpu.VMEM((tm, k), BF16)],
        compiler_params=_cparams(("parallel", "arbitrary"), est),
        name="norm_matmul",
    )(x, g.reshape(1, k), w_tiles)


def _qkv_kernel(x_ref, g_ref, w_ref, cos_ref, sin_ref, o_ref, xn_ref, *, heads_per_tile, scale):
    j = pl.program_id(1)
    tiles_per_part = N_HEADS_QKV // heads_per_tile

    @pl.when(j == 0)
    def _():
        xn_ref[...] = _rmsnorm(x_ref[...], g_ref[...]).astype(BF16)

    use_rope = jnp.logical_and(j < 2 * tiles_per_part, j % tiles_per_part != 0)
    mult = jnp.where(j < tiles_per_part, scale, 1.0)
    c = jnp.where(use_rope, cos_ref[...], 1.0) * mult
    s = jnp.where(use_rope, sin_ref[...], 0.0) * mult
    xn = xn_ref[...]
    for pair in range(heads_per_tile // 2):
        r = _dot(xn, w_ref[:, pair * 2 * HEAD_DIM:(pair + 1) * 2 * HEAD_DIM])
        for k in range(2):
            p = r[:, k * HEAD_DIM:(k + 1) * HEAD_DIM]
            o_ref[2 * pair + k] = (p * c + pltpu.roll(p, HEAD_DIM // 2, 1) * s).astype(o_ref.dtype)


def qkv_proj(x, g, w_tiles, cos, sin, *, seq_len, tm=1024):
    n, k = x.shape
    hpt = N_HEADS_A
    n_tiles, _, tn = w_tiles.shape
    assert tn == hpt * HEAD_DIM
    m = n_tiles * tn
    tm = min(tm, seq_len)
    tiles_per_seq = seq_len // tm
    est = 2 * tm * k * 4 + tm * k * 2 + 2 * k * tn * 2 + 2 * tm * tn * 2 + 4 * tm * HEAD_DIM * 4 + tm * tn * 4
    return pl.pallas_call(
        functools.partial(_qkv_kernel, heads_per_tile=hpt, scale=HEAD_DIM ** -0.5),
        grid=(n // tm, m // tn),
        in_specs=[
            pl.BlockSpec((tm, k), lambda i, j: (i, 0)),
            pl.BlockSpec((1, k), lambda i, j: (0, 0)),
            pl.BlockSpec((None, k, tn), lambda i, j: (j, 0, 0)),
            pl.BlockSpec((tm, HEAD_DIM), lambda i, j: (i % tiles_per_seq, 0)),
            pl.BlockSpec((tm, HEAD_DIM), lambda i, j: (i % tiles_per_seq, 0)),
        ],
        out_specs=pl.BlockSpec((hpt, tm, HEAD_DIM), lambda i, j: (j, i, 0)),
        out_shape=jax.ShapeDtypeStruct((m // HEAD_DIM, n, HEAD_DIM), BF16),
        scratch_shapes=[pltpu.VMEM((tm, k), BF16)],
        compiler_params=_cparams(("parallel", "arbitrary"), est),
        name="qkv_proj",
    )(x, g.reshape(1, k), w_tiles, cos, sin)


def rope_tables(seq_len):
    half = HEAD_DIM // 2
    inv = ROPE_THETA ** (-jnp.arange(half, dtype=F32) / half)
    ang = jnp.arange(seq_len, dtype=F32)[:, None] * inv[None, :]
    cos = jnp.cos(ang)
    sin = jnp.sin(ang)
    return jnp.concatenate([cos, cos], axis=1), jnp.concatenate([-sin, sin], axis=1)


def _natten_kernel(q_ref, k_ref, v_ref, bias_ref, o_ref, *, n_blocks, grid_rows):
    blk = pl.program_id(2)
    first_row = jnp.where(blk == 0, 0,
                          jnp.where(blk == n_blocks - 1, grid_rows - NAT_KROWS, blk * NAT_ROWS - WIN_H // 2))
    start = pl.multiple_of(first_row * GRID_W, GRID_W)
    kw = k_ref[pl.ds(start, NAT_KROWS * GRID_W), :]
    vw = v_ref[pl.ds(start, NAT_KROWS * GRID_W), :]
    s = _dot_nt(q_ref[...], kw) + bias_ref[...]
    m = jnp.max(s, axis=-1, keepdims=True)
    p = jnp.exp(s - m)
    l = jnp.sum(p, axis=-1, keepdims=True)
    o_ref[...] = (_dot(p.astype(BF16), vw) / l).astype(o_ref.dtype)


def natten_bias_tiles(rpb, grid_rows):
    n_blocks = grid_rows // NAT_ROWS
    r0s = np.array([0, NAT_ROWS, (n_blocks - 1) * NAT_ROWS])
    k0s = np.array([0, NAT_ROWS - WIN_H // 2, grid_rows - NAT_KROWS])
    r = r0s[:, None] + np.arange(NAT_ROWS)[None, :]
    kr = k0s[:, None] + np.arange(NAT_KROWS)[None, :]
    rs = np.clip(r - WIN_H // 2, 0, grid_rows - WIN_H)
    row_ok = (kr[:, None, :] >= rs[:, :, None]) & (kr[:, None, :] < rs[:, :, None] + WIN_H)
    d_row = np.clip(kr[:, None, :] - r[:, :, None] + (WIN_H - 1), 0, 2 * WIN_H - 2)
    c = np.arange(GRID_W)
    cs = np.clip(c - WIN_W // 2, 0, GRID_W - WIN_W)
    col_ok = (c[None, :] >= cs[:, None]) & (c[None, :] < cs[:, None] + WIN_W)
    d_col = np.clip(c[None, :] - c[:, None] + (WIN_W - 1), 0, 2 * WIN_W - 2)
    sel_row = (d_row[..., None] == np.arange(2 * WIN_H - 1)).astype(np.float32)
    sel_col = (d_col[..., None] == np.arange(2 * WIN_W - 1)).astype(np.float32)
    rows = jnp.einsum("tikr,hrc->thikc", sel_row, rpb.astype(F32), precision=lax.Precision.HIGHEST)
    bias = jnp.einsum("thikc,qwc->thiqkw", rows, sel_col, precision=lax.Precision.HIGHEST)
    ok = row_ok[:, :, None, :, None] & col_ok[None, None, :, None, :]
    bias = jnp.where(ok[:, None], bias, NEG_INF)
    return bias.reshape(3, rpb.shape[0], NAT_ROWS * GRID_W, NAT_KROWS * GRID_W)


def natten(qkv, bias_tiles, *, batch, seq_len):
    n = batch * seq_len
    grid_rows = seq_len // GRID_W
    n_blocks = grid_rows // NAT_ROWS
    tq = NAT_ROWS * GRID_W
    tk = NAT_KROWS * GRID_W

    def tile_kind(blk):
        return jnp.where(blk == 0, 0, jnp.where(blk == n_blocks - 1, 2, 1))

    est = 4 * seq_len * HEAD_DIM * 2 + 2 * tq * tk * 4 + 3 * tq * tk * 4
    return pl.pallas_call(
        functools.partial(_natten_kernel, n_blocks=n_blocks, grid_rows=grid_rows),
        grid=(batch, N_HEADS_A, n_blocks),
        in_specs=[
            pl.BlockSpec((None, tq, HEAD_DIM), lambda b, h, i: (h, b * n_blocks + i, 0)),
            pl.BlockSpec((None, seq_len, HEAD_DIM), lambda b, h, i: (N_HEADS_QKV + h, b, 0)),
            pl.BlockSpec((None, seq_len, HEAD_DIM), lambda b, h, i: (2 * N_HEADS_QKV + h, b, 0)),
            pl.BlockSpec((None, None, tq, tk), lambda b, h, i: (tile_kind(i), h, 0, 0)),
        ],
        out_specs=pl.BlockSpec((tq, HEAD_DIM), lambda b, h, i: (b * n_blocks + i, h)),
        out_shape=jax.ShapeDtypeStruct((n, N_HEADS_A * HEAD_DIM), BF16),
        compiler_params=_cparams(("parallel", "parallel", "arbitrary"), est),
        name="natten",
    )(qkv, qkv, qkv, bias_tiles)


class _DilatedGeometry:
    def __init__(self, dil, kc):
        self.dil, self.kc = dil, kc
        self.reach = BAND_RADIUS * dil
        self.tp = DIL_TQ // DIL_PARTS
        assert kc % self.tp == 0
        self.halo = -(-self.reach // kc) * kc
        self.n_chunks = (DIL_TQ + 2 * self.halo) // kc
        self.n_tiles = (kc // self.tp) * (self.n_chunks - 1) + DIL_PARTS

    def tile_index(self, chunk, part):
        return (self.kc // self.tp) * chunk + (DIL_PARTS - 1 - part)

    def bias_tiles(self):
        row = np.arange(self.tp)[:, None]
        col = np.arange(self.kc)[None, :]
        tiles = []
        for u in range(self.n_tiles):
            rel = col - row + (u - (DIL_PARTS - 1)) * self.tp - self.halo
            ok = (np.abs(rel) <= self.reach) & (rel % self.dil == 0)
            tiles.append(np.where(ok, 0.0, -np.inf))
        return np.stack(tiles).astype(np.float32)


DIL_GEOMETRY = tuple(_DilatedGeometry(d, kc) for d, kc in zip(DILATIONS, DIL_KC))


def _dilated_kernel(b0, b1, b2, q0, q1, q2, k0, k1, k2, v0, v1, v2, o_ref, *, seq_len):
    tq = DIL_TQ
    parts = DIL_PARTS
    tp = tq // parts
    t0 = pl.program_id(2) * tq
    carry = tuple((jnp.full((tp, 1), -1e30, F32), jnp.zeros((tp, 1), F32), jnp.zeros((tp, HEAD_DIM), F32))
                  for _ in range(parts))
    for geo, bias_ref, q_ref, k_ref, v_ref in zip(DIL_GEOMETRY, (b0, b1, b2), (q0, q1, q2), (k0, k1, k2), (v0, v1, v2)):
        kc = geo.kc
        window_start = t0 - geo.halo
        lo = jnp.maximum(0, (geo.halo - t0) // kc)
        hi = jnp.minimum(geo.n_chunks, (seq_len - window_start) // kc)
        qs = tuple(q_ref[p * tp:(p + 1) * tp, :] for p in range(parts))

        def body(ci, carry, geo=geo, bias_ref=bias_ref, k_ref=k_ref, v_ref=v_ref, window_start=window_start, qs=qs):
            ks = pl.multiple_of(window_start + ci * geo.kc, geo.kc)
            kk = k_ref[pl.ds(ks, geo.kc), :]
            vv = v_ref[pl.ds(ks, geo.kc), :]
            scores = [_dot_nt(qs[p], kk) + bias_ref[geo.tile_index(ci, p)] for p in range(parts)]
            stats = []
            for p in range(parts):
                m, l, _ = carry[p]
                m_new = jnp.maximum(m, jnp.max(scores[p], axis=-1, keepdims=True))
                alpha = jnp.exp(m - m_new)
                e = jnp.exp(scores[p] - m_new)
                stats.append((m_new, alpha, alpha * l + jnp.sum(e, axis=-1, keepdims=True), e.astype(BF16)))
            return tuple((m_new, l, alpha * carry[p][2] + _dot(e, vv))
                         for p, (m_new, alpha, l, e) in enumerate(stats))

        carry = lax.fori_loop(lo, hi, body, carry)
    for p in range(parts):
        _, l, acc = carry[p]
        o_ref[p * tp:(p + 1) * tp, :] = (acc / l).astype(o_ref.dtype)


def dilated_attention(qkv, *, batch, seq_len):
    n = batch * seq_len
    nq = seq_len // DIL_TQ
    biases = [geo.bias_tiles() for geo in DIL_GEOMETRY]

    def q_spec(g):
        return pl.BlockSpec((None, DIL_TQ, HEAD_DIM),
                            lambda b, j, i: (N_HEADS_A + N_HEADS_B_GROUP * g + j, b * nq + i, 0))

    def kv_spec(part, g):
        return pl.BlockSpec((None, seq_len, HEAD_DIM),
                            lambda b, j, i: (part * N_HEADS_QKV + N_HEADS_A + N_HEADS_B_GROUP * g + j, b, 0))

    est = 12 * seq_len * HEAD_DIM * 2 + 8 * DIL_TQ * max(DIL_KC) * 4 + 2 * sum(b.size for b in biases) * 4
    groups = range(len(DILATIONS))
    return pl.pallas_call(
        functools.partial(_dilated_kernel, seq_len=seq_len),
        grid=(batch, N_HEADS_B_GROUP, nq),
        in_specs=[pl.BlockSpec(b.shape, lambda b_, j, i: (0, 0, 0)) for b in biases]
        + [q_spec(g) for g in groups] + [kv_spec(1, g) for g in groups] + [kv_spec(2, g) for g in groups],
        out_specs=pl.BlockSpec((DIL_TQ, HEAD_DIM), lambda b, j, i: (b * nq + i, j)),
        out_shape=jax.ShapeDtypeStruct((n, N_HEADS_B_GROUP * HEAD_DIM), BF16),
        compiler_params=_cparams(("parallel", "parallel", "arbitrary"), est),
        name="dilated_attention",
    )(*biases, *([qkv] * 9))


def _attn_out_kernel(x_ref, oa_ref, ob_ref, w_ref, o_ref):
    ka = oa_ref.shape[1]
    o_ref[...] = x_ref[...] + _dot(oa_ref[...], w_ref[:ka, :]) + _dot(ob_ref[...], w_ref[ka:, :])


def attn_out(x, o_a, o_b, w, *, tm=512):
    n, d = x.shape
    ka, kb = o_a.shape[1], o_b.shape[1]
    tm = min(tm, n)
    est = 4 * tm * d * 4 + 2 * (ka + kb) * d * 2 + 2 * tm * (ka + kb) * 2
    return pl.pallas_call(
        _attn_out_kernel,
        grid=(n // tm,),
        in_specs=[
            pl.BlockSpec((tm, d), lambda i: (i, 0)),
            pl.BlockSpec((tm, ka), lambda i: (i, 0)),
            pl.BlockSpec((tm, kb), lambda i: (i, 0)),
            pl.BlockSpec((ka + kb, d), lambda i: (0, 0)),
        ],
        out_specs=pl.BlockSpec((tm, d), lambda i: (i, 0)),
        out_shape=jax.ShapeDtypeStruct((n, d), F32),
        compiler_params=_cparams(("parallel",), est),
        name="attn_out",
    )(x, o_a, o_b, w)


def _ffn_kernel(x_ref, g_ref, wg_ref, wu_ref, wd_ref, o_ref, xn_ref):
    f = pl.program_id(1)

    @pl.when(f == 0)
    def _():
        xn_ref[...] = _rmsnorm(x_ref[...], g_ref[...]).astype(BF16)
        o_ref[...] = x_ref[...]

    xn = xn_ref[...]
    gate = _dot(xn, wg_ref[...])
    up = _dot(xn, wu_ref[...])
    hid = (gate * _sigmoid(gate) * up).astype(BF16)
    o_ref[...] += _dot(hid, wd_ref[...])


def ffn(x, g, wg_tiles, wu_tiles, wd, *, tm=512):
    n, d = x.shape
    nf, _, tf = wg_tiles.shape
    tm = min(tm, n)
    est = 4 * tm * d * 4 + tm * d * 2 + 6 * d * tf * 2 + 3 * tm * tf * 4
    return pl.pallas_call(
        _ffn_kernel,
        grid=(n // tm, nf),
        in_specs=[
            pl.BlockSpec((tm, d), lambda i, f: (i, 0)),
            pl.BlockSpec((1, d), lambda i, f: (0, 0)),
            pl.BlockSpec((None, d, tf), lambda i, f: (f, 0, 0)),
            pl.BlockSpec((None, d, tf), lambda i, f: (f, 0, 0)),
            pl.BlockSpec((tf, d), lambda i, f: (f, 0)),
        ],
        out_specs=pl.BlockSpec((tm, d), lambda i, f: (i, 0)),
        out_shape=jax.ShapeDtypeStruct((n, d), F32),
        scratch_shapes=[pltpu.VMEM((tm, d), BF16)],
        compiler_params=_cparams(("parallel", "arbitrary"), est),
        name="ffn",
    )(x, g.reshape(1, d), wg_tiles, wu_tiles, wd)


def _conv_kernel(xm_ref, xp_ref, xn_ref, w_ref, b_ref, o_ref, *rest, tr, n_row_blocks):
    ext_ref = rest[-1]
    i = pl.program_id(2)
    hb = BF16_SUBLANES
    pad = SSM_CONV // 2
    ext_ref[0:hb, :] = jnp.where(i > 0, xp_ref[...].astype(F32), 0.0)
    ext_ref[hb:hb + tr, :] = xm_ref[...].astype(F32)
    ext_ref[hb + tr:2 * hb + tr, :] = jnp.where(i < n_row_blocks - 1, xn_ref[...].astype(F32), 0.0)
    acc = jnp.broadcast_to(b_ref[...], o_ref.shape)
    for k in range(SSM_CONV):
        acc = acc + ext_ref[pl.ds(hb - pad + k, tr), :] * w_ref[k:k + 1, :]
    out = acc * _sigmoid(acc)
    o_ref[...] = out.astype(o_ref.dtype)
    if len(rest) == 2:
        ot_ref = rest[0]
        L = ot_ref.shape[-1]
        for c in range(tr // L):
            ot_ref[c] = out[c * L:(c + 1) * L, :].T.astype(ot_ref.dtype)


def conv_silu(zx, conv_w, conv_b, *, width, first_tile, n_tiles, col_offset, batch, seq_len, tr, transposed=False):
    n = batch * seq_len
    tile_w = zx.shape[2]
    per = tile_w // width
    tr = min(tr, seq_len)
    nr = seq_len // tr
    hb = BF16_SUBLANES
    seq_hb = seq_len // hb
    n_hb = n // hb

    def main_map(c, b, i):
        return (first_tile + c // per, b * nr + i, c % per)

    def prev_map(c, b, i):
        return (first_tile + c // per, jnp.maximum(b * seq_hb + i * (tr // hb) - 1, 0), c % per)

    def next_map(c, b, i):
        return (first_tile + c // per, jnp.minimum(b * seq_hb + (i + 1) * (tr // hb), n_hb - 1), c % per)

    L = SSM_CHUNK
    out_specs = [pl.BlockSpec((None, tr, width), lambda c, b, i: (c, b * nr + i, 0))]
    out_shape = [jax.ShapeDtypeStruct((n_tiles, n, width), BF16)]
    if transposed:
        out_specs.append(pl.BlockSpec((None, tr // L, width, L), lambda c, b, i: (c, b * nr + i, 0, 0)))
        out_shape.append(jax.ShapeDtypeStruct((n_tiles, n // L, width, L), BF16))
    est = 4 * tr * width * 2 + (tr + 2 * hb) * width * 4 + 4 * tr * width * 4 + 4 * tr * width * 2
    outs = pl.pallas_call(
        functools.partial(_conv_kernel, tr=tr, n_row_blocks=nr),
        grid=(n_tiles, batch, nr),
        in_specs=[
            pl.BlockSpec((None, tr, width), main_map),
            pl.BlockSpec((None, hb, width), prev_map),
            pl.BlockSpec((None, hb, width), next_map),
            pl.BlockSpec((SSM_CONV, width), lambda c, b, i: (0, col_offset // width + c)),
            pl.BlockSpec((1, width), lambda c, b, i: (0, col_offset // width + c)),
        ],
        out_specs=out_specs,
        out_shape=out_shape,
        scratch_shapes=[pltpu.VMEM((tr + 2 * hb, width), F32)],
        compiler_params=_cparams(("parallel", "parallel", "arbitrary"), est),
        name="conv_silu",
    )(zx, zx, zx, conv_w, conv_b.reshape(1, -1))
    return outs if transposed else outs[0]


def _lane_cumsum(a):
    lane = lax.broadcasted_iota(jnp.int32, a.shape, 1)
    shift = 1
    while shift < a.shape[1]:
        a = a + jnp.where(lane >= shift, pltpu.roll(a, shift, 1), 0.0)
        shift *= 2
    return a


def _ssd_decay_rows(dt_ref, dtbias_ref, alog_ref, seg_scr, dt_scr, *, hpg):
    L = SSM_CHUNK
    dt_all = _softplus(dt_ref[...] + dtbias_ref[...])
    a_all = dt_all * (-jnp.exp(alog_ref[...]))
    cum = _lane_cumsum(a_all)
    suf = cum[:, L - 1:L] - cum + a_all
    row = lax.broadcasted_iota(jnp.int32, cum.shape, 0)
    seg_scr[...] = jnp.where((row & (2 * hpg - 1)) < hpg, cum, suf)
    dt_scr[...] = dt_all


def _ssd_kernel(xa_ref, ba_ref, bta_ref, ca_ref, dta_ref, xb_ref, btb_ref, cb_ref, dtb_ref,
                dtbias_ref, alog_ref, dskip_ref, yf_ref, yb_ref,
                sf_ref, sb_ref, sega_scr, dta_scr, segb_scr, dtb_scr, *, hpg):
    L = SSM_CHUNK
    P = SSM_HEAD_DIM
    pairs = hpg // 2

    @pl.when(pl.program_id(1) == 0)
    def _():
        sf_ref[...] = jnp.zeros_like(sf_ref)
        sb_ref[...] = jnp.zeros_like(sb_ref)

    _ssd_decay_rows(dta_ref, dtbias_ref, alog_ref, sega_scr, dta_scr, hpg=hpg)
    _ssd_decay_rows(dtb_ref, dtbias_ref, alog_ref, segb_scr, dtb_scr, hpg=hpg)

    li = lax.broadcasted_iota(jnp.int32, (L, L), 0)
    si = lax.broadcasted_iota(jnp.int32, (L, L), 1)
    causal = li >= si
    anti = li <= si
    low_lanes = lax.broadcasted_iota(jnp.int32, (L, 2 * P), 1) < P
    high_lanes = jnp.logical_not(low_lanes)
    low_lanes_row = lax.broadcasted_iota(jnp.int32, (1, 2 * P), 1) < P

    def lanes_of(col):
        return jnp.broadcast_to(col, (L, 2 * P))

    def columns(seg_r, dt_r):
        stacked = jnp.concatenate([seg_r, dt_r, jnp.zeros((L - 4 * hpg, L), F32)], axis=0)
        return stacked.T

    spread_bwd = (lax.broadcasted_iota(jnp.int32, (L, hpg * P), 0)
                  == hpg + lax.broadcasted_iota(jnp.int32, (L, hpg * P), 1) // P).astype(BF16)

    def advance_state(s_ref, g, lanes, state, btf, xm0, xm1, seg_r, dt_r, h0, h1, tot_lane):
        acc = None
        decays = []
        for h, xm in ((h0, xm0), (h1, xm1)):
            tot = seg_r[h:h + 1, tot_lane:tot_lane + 1]
            coef = jnp.exp(tot - seg_r[h:h + 1, :]) * dt_r[h:h + 1, :]
            part = _dot((btf * coef).astype(BF16), xm)
            acc = part if acc is None else acc + part
            decays.append(jnp.exp(tot))
        decay = jnp.where(low_lanes_row, decays[0], decays[1])
        s_ref[g, :, lanes] = state[:, lanes] * decay + acc

    def group_body(g, c):
        r0 = pl.multiple_of(g * 2 * hpg, 2 * hpg)

        seg_r = sega_scr[pl.ds(r0, 2 * hpg), :]
        dt_r = dta_scr[pl.ds(r0, 2 * hpg), :]
        cols = columns(seg_r, dt_r)
        xg = xa_ref[g]
        cg = ca_ref[g]
        state = sf_ref[g]
        carried = _dot(cg, state.astype(BF16))
        cbm = _dot_nt(cg, ba_ref[g])
        btf = bta_ref[g].astype(F32)
        for p in range(pairs):
            lanes = slice(p * 2 * P, (p + 1) * 2 * P)
            xpf = xg[:, lanes].astype(F32)
            xm0 = jnp.where(low_lanes, xpf, 0.0).astype(BF16)
            xm1 = jnp.where(high_lanes, xpf, 0.0).astype(BF16)
            h0, h1 = 2 * p, 2 * p + 1
            f0, f1 = lanes_of(cols[:, h0:h0 + 1]), lanes_of(cols[:, h1:h1 + 1])
            y = carried[:, lanes] * jnp.where(low_lanes, jnp.exp(f0), jnp.exp(f1)) + dskip_ref[g][:, lanes] * xpf
            for hh, f_cols, xm in ((h0, f0, xm0), (h1, f1, xm1)):
                wf = jnp.exp(jnp.where(causal, f_cols - seg_r[hh:hh + 1, :], NEG_INF)) * dt_r[hh:hh + 1, :]
                b_cols = lanes_of(cols[:, hpg + hh:hpg + hh + 1])
                wb = jnp.exp(jnp.where(anti, b_cols - seg_r[hpg + hh:hpg + hh + 1, :], NEG_INF)) \
                    * dt_r[hpg + hh:hpg + hh + 1, :]
                y = y + _dot((cbm * (wf + wb)).astype(BF16), xm)
            yf_ref[g, :, lanes] = y.astype(yf_ref.dtype)
            advance_state(sf_ref, g, lanes, state, btf, xm0, xm1, seg_r, dt_r, h0, h1, L - 1)

        seg_r = segb_scr[pl.ds(r0, 2 * hpg), :]
        dt_r = dtb_scr[pl.ds(r0, 2 * hpg), :]
        xg = xb_ref[g]
        state = sb_ref[g]
        grow = _dot(jnp.exp(columns(seg_r, dt_r)).astype(BF16), spread_bwd)
        y = _dot(cb_ref[g], state.astype(BF16)) * grow
        yb_ref[g] = y.astype(yb_ref.dtype)
        btf = btb_ref[g].astype(F32)
        for p in range(pairs):
            lanes = slice(p * 2 * P, (p + 1) * 2 * P)
            xpf = xg[:, lanes].astype(F32)
            xm0 = jnp.where(low_lanes, xpf, 0.0).astype(BF16)
            xm1 = jnp.where(high_lanes, xpf, 0.0).astype(BF16)
            advance_state(sb_ref, g, lanes, state, btf, xm0, xm1, seg_r, dt_r, hpg + 2 * p, hpg + 2 * p + 1, 0)
        return c

    lax.fori_loop(0, SSM_GROUPS, group_body, 0)


def ssd(xs, b_nat, b_t, c_nat, dt_rows, dt_bias_rows, a_log_rows, d_skip_rows, *, batch, seq_len):
    groups, n, gw = xs.shape
    hpg = gw // SSM_HEAD_DIM
    assert hpg & (hpg - 1) == 0
    L = SSM_CHUNK
    nc = seq_len // L
    rows = groups * 2 * hpg

    def fwd(b, k):
        return b * nc + k

    def bwd(b, k):
        return b * nc + nc - 1 - k

    def specs(chunk, with_b_nat):
        out = [pl.BlockSpec((groups, L, gw), lambda b, k: (0, chunk(b, k), 0))]
        if with_b_nat:
            out.append(pl.BlockSpec((groups, L, SSM_STATE), lambda b, k: (0, chunk(b, k), 0)))
        out += [
            pl.BlockSpec((groups, None, SSM_STATE, L), lambda b, k: (0, chunk(b, k), 0, 0)),
            pl.BlockSpec((groups, L, SSM_STATE), lambda b, k: (0, chunk(b, k), 0)),
            pl.BlockSpec((None, rows, L), lambda b, k: (chunk(b, k), 0, 0)),
        ]
        return out

    const = [
        pl.BlockSpec((rows, 1), lambda b, k: (0, 0)),
        pl.BlockSpec((rows, 1), lambda b, k: (0, 0)),
        pl.BlockSpec((groups, 1, gw), lambda b, k: (0, 0, 0)),
    ]
    y_shape = jax.ShapeDtypeStruct((groups, n, gw), BF16)
    est = 4 * (groups * L * gw * 2 + 3 * groups * L * SSM_STATE * 2 + rows * L * 4) + 4 * groups * L * gw * 2 \
        + 2 * groups * SSM_STATE * gw * 4 + 4 * rows * L * 4 + 64 * L * L * 4
    return pl.pallas_call(
        functools.partial(_ssd_kernel, hpg=hpg),
        grid=(batch, nc),
        in_specs=specs(fwd, True) + specs(bwd, False) + const,
        out_specs=[pl.BlockSpec((groups, L, gw), lambda b, k: (0, fwd(b, k), 0)),
                   pl.BlockSpec((groups, L, gw), lambda b, k: (0, bwd(b, k), 0))],
        out_shape=[y_shape, y_shape],
        scratch_shapes=[pltpu.VMEM((groups, SSM_STATE, gw), F32), pltpu.VMEM((groups, SSM_STATE, gw), F32)]
        + [pltpu.VMEM((rows, L), F32)] * 4,
        compiler_params=_cparams(("parallel", "arbitrary"), est),
        name="ssd",
    )(xs, b_nat, b_t, c_nat, dt_rows, xs, b_t, c_nat, dt_rows, dt_bias_rows, a_log_rows, d_skip_rows)


def _mamba_out_kernel(yf_ref, yb_ref, z_ref, gg_ref, w_ref, x_ref, o_ref, ssq_ref, *, d_inner):
    step = pl.program_id(1)
    groups_per_step, _, gw = z_ref.shape

    @pl.when(step == 0)
    def _():
        o_ref[...] = jnp.zeros_like(o_ref)
        ssq_ref[...] = jnp.zeros_like(ssq_ref)

    gated = []
    for k in range(groups_per_step):
        z = z_ref[k].astype(F32)
        yz = (yf_ref[k].astype(F32) + yb_ref[k].astype(F32)) * (z * _sigmoid(z))
        ssq_ref[...] += jnp.sum(yz * yz, axis=-1, keepdims=True)
        gated.append((yz * gg_ref[:, k * gw:(k + 1) * gw]).astype(BF16))
    o_ref[...] += _dot(jnp.concatenate(gated, axis=1), w_ref[...])

    @pl.when(step == pl.num_programs(1) - 1)
    def _():
        o_ref[...] = x_ref[...] + o_ref[...] * lax.rsqrt(ssq_ref[...] / d_inner + EPS)


def mamba_out(y_f, y_b, zx, g_gate, w_out, x, *, tm=512, groups_per_step=2):
    groups, n, gw = y_f.shape
    d = x.shape[1]
    tm = min(tm, n)
    gps = groups_per_step
    est = gps * (6 * tm * gw * 2 + 2 * gw * d * 2) + 3 * tm * d * 4 + 2 * tm * gps * gw * 4
    return pl.pallas_call(
        functools.partial(_mamba_out_kernel, d_inner=groups * gw),
        grid=(n // tm, groups // gps),
        in_specs=[
            pl.BlockSpec((gps, tm, gw), lambda i, s: (s, i, 0)),
            pl.BlockSpec((gps, tm, gw), lambda i, s: (s, i, 0)),
            pl.BlockSpec((gps, tm, gw), lambda i, s: (s, i, 0)),
            pl.BlockSpec((1, gps * gw), lambda i, s: (0, s)),
            pl.BlockSpec((gps * gw, d), lambda i, s: (s, 0)),
            pl.BlockSpec((tm, d), lambda i, s: (i, 0)),
        ],
        out_specs=pl.BlockSpec((tm, d), lambda i, s: (i, 0)),
        out_shape=jax.ShapeDtypeStruct((n, d), F32),
        scratch_shapes=[pltpu.VMEM((tm, 1), F32)],
        compiler_params=_cparams(("parallel", "arbitrary"), est),
        name="mamba_out",
    )(y_f, y_b, zx, g_gate.reshape(1, -1), w_out, x)


def _router_kernel(xa_ref, xb_ref, g_ref, w_ref, o_ref, xn_ref, *, tiles_a):
    i = pl.program_id(0)

    @pl.when(i < tiles_a)
    def _():
        xn_ref[...] = _rmsnorm(xa_ref[...], g_ref[...])

    @pl.when(i >= tiles_a)
    def _():
        xn_ref[...] = _rmsnorm(xb_ref[...], g_ref[...])

    xn = xn_ref[...]
    logits = jnp.dot(xn, w_ref[...], preferred_element_type=F32, precision=lax.Precision.HIGHEST)
    lane = lax.broadcasted_iota(jnp.int32, logits.shape, 1)
    logits = jnp.where(lane < N_EXPERTS, logits, NEG_INF)
    v1 = jnp.max(logits, axis=-1, keepdims=True)
    i1 = jnp.min(jnp.where(logits == v1, lane, LANES), axis=-1, keepdims=True)
    rest = jnp.where(lane == i1, NEG_INF, logits)
    v2 = jnp.max(rest, axis=-1, keepdims=True)
    i2 = jnp.min(jnp.where(rest == v2, lane, LANES), axis=-1, keepdims=True)
    e2 = jnp.exp(v2 - v1)
    g1 = 1.0 / (1.0 + e2)
    g2 = e2 / (1.0 + e2)
    out = jnp.where(lane == 0, i1.astype(F32),
                    jnp.where(lane == 1, i2.astype(F32),
                              jnp.where(lane == 2, g1, jnp.where(lane == 3, g2, 0.0))))
    o_ref[...] = out


def router(x_a, x_b, g, w_router, *, tm=512):
    (na, d), nb = x_a.shape, x_b.shape[0]
    tm = min(tm, na, nb)
    tiles_a, tiles_b = na // tm, nb // tm
    w = jnp.zeros((d, LANES), F32).at[:, :w_router.shape[1]].set(w_router)
    est = 4 * tm * d * 4 + 2 * d * LANES * 4 + 2 * tm * LANES * 4 + 3 * tm * d * 4
    return pl.pallas_call(
        functools.partial(_router_kernel, tiles_a=tiles_a),
        grid=(tiles_a + tiles_b,),
        in_specs=[
            pl.BlockSpec((tm, d), lambda i: (jnp.minimum(i, tiles_a - 1), 0)),
            pl.BlockSpec((tm, d), lambda i: (jnp.maximum(i - tiles_a, 0), 0)),
            pl.BlockSpec((1, d), lambda i: (0, 0)),
            pl.BlockSpec((d, LANES), lambda i: (0, 0)),
        ],
        out_specs=[pl.BlockSpec((tm, LANES), lambda i: (i, 0)), pl.BlockSpec((tm, d), lambda i: (i, 0))],
        out_shape=[jax.ShapeDtypeStruct((na + nb, LANES), F32), jax.ShapeDtypeStruct((na + nb, d), F32)],
        compiler_params=_cparams(("parallel",), est),
        name="router",
    )(x_a, x_b, g.reshape(1, d), w)


def _row_copy(src_hbm, dst_vmem, sem, src_row, dst_row):
    return pltpu.make_async_copy(src_hbm.at[pl.ds(src_row, 1), :], dst_vmem.at[pl.ds(dst_row, 1), :], sem)


EXPERT_ISSUE_STEPS = 4


def _expert_kernel(blk_e_ref, n_used_ref, tok_ref, x_hbm, wg_ref, wu_ref, wd_ref, o_ref, rows_ref, x_ref, sems):
    i = pl.program_id(0)
    f = pl.program_id(1)
    tm = x_ref.shape[0]
    per_step = tm // EXPERT_ISSUE_STEPS
    n_used = n_used_ref[0]
    slot = i % 2

    def issue(block, dst_slot, first_row, count):
        def body(r, c):
            _row_copy(x_hbm, rows_ref.at[dst_slot], sems.at[dst_slot], tok_ref[block * tm + first_row + r],
                      first_row + r).start()
            return c
        lax.fori_loop(0, count, body, 0, unroll=8)

    @pl.when(jnp.logical_and(i == 0, f == 0))
    def _():
        issue(0, 0, 0, tm)

    @pl.when(jnp.logical_and(f == 0, i < n_used))
    def _():
        def wait(r, c):
            _row_copy(x_hbm, rows_ref.at[slot], sems.at[slot], 0, r).wait()
            return c
        lax.fori_loop(0, tm, wait, 0, unroll=8)
        x_ref[...] = rows_ref[slot].astype(x_ref.dtype)

    @pl.when(f == 0)
    def _():
        o_ref[...] = jnp.zeros_like(o_ref)

    @pl.when(jnp.logical_and(f < EXPERT_ISSUE_STEPS, i + 1 < n_used))
    def _():
        issue(i + 1, 1 - slot, f * per_step, per_step)

    @pl.when(i < n_used)
    def _():
        x = x_ref[...]
        gate = _dot(x, wg_ref[...])
        up = _dot(x, wu_ref[...])
        hid = (gate * _sigmoid(gate) * up).astype(BF16)
        o_ref[...] += _dot(hid, wd_ref[...])


def expert_ffn(blk_expert, n_used, tok_of_slot, x_rows, wg, wu, wd, *, tm=MOE_TM, tf=1024):
    slots = tok_of_slot.shape[0]
    d = x_rows.shape[1]
    nf = wg.shape[2] // tf
    assert nf >= EXPERT_ISSUE_STEPS and tm % (8 * EXPERT_ISSUE_STEPS) == 0
    n_blocks = slots // tm

    def live(i, n_used):
        return jnp.minimum(i, n_used[0] - 1)

    def f_eff(i, f, n_used):
        return jnp.where(i < n_used[0], f, nf - 1)

    est = 2 * tm * d * 4 + tm * d * 2 + 12 * d * tf + 2 * tm * d * 4 + 3 * tm * tf * 4
    return pl.pallas_call(
        _expert_kernel,
        grid_spec=pltpu.PrefetchScalarGridSpec(
            num_scalar_prefetch=3,
            grid=(n_blocks, nf),
            in_specs=[
                pl.BlockSpec(memory_space=pl.ANY),
                pl.BlockSpec((None, d, tf), lambda i, f, be, nu, tok: (be[live(i, nu)], 0, f_eff(i, f, nu))),
                pl.BlockSpec((None, d, tf), lambda i, f, be, nu, tok: (be[live(i, nu)], 0, f_eff(i, f, nu))),
                pl.BlockSpec((None, tf, d), lambda i, f, be, nu, tok: (be[live(i, nu)], f_eff(i, f, nu), 0)),
            ],
            out_specs=pl.BlockSpec((tm, d), lambda i, f, be, nu, tok: (i, 0)),
            scratch_shapes=[pltpu.VMEM((2, tm, d), F32), pltpu.VMEM((tm, d), BF16), pltpu.SemaphoreType.DMA((2,))],
        ),
        out_shape=jax.ShapeDtypeStruct((slots, d), F32),
        compiler_params=_cparams(("arbitrary", "arbitrary"), est),
        name="expert_ffn",
    )(blk_expert, n_used, tok_of_slot, x_rows, wg, wu, wd)


def _combine_kernel(slot_ref, y_hbm, x_ref, r_ref, g_ref, o_ref, buf_ref, sem):
    rows = x_ref.shape[0]
    base = pl.program_id(0) * rows

    def start(r, c):
        for k in range(TOP_K):
            _row_copy(y_hbm, buf_ref.at[k], sem, slot_ref[TOP_K * (base + r) + k], r).start()
        return c

    def wait(r, c):
        for k in range(TOP_K):
            _row_copy(y_hbm, buf_ref.at[k], sem, 0, r).wait()
        return c

    lax.fori_loop(0, rows, start, 0, unroll=8)
    lax.fori_loop(0, rows, wait, 0, unroll=8)
    gates = r_ref[...]
    out = x_ref[...] + gates[:, 2:3] * buf_ref[0] + gates[:, 3:4] * buf_ref[1]
    o_ref[...] = _rmsnorm(out, g_ref[...])


def combine_norm(slot_of_assignment, ys, x, routed, g_final, *, rows=GATHER_ROWS):
    n, d = x.shape
    rows = min(rows, n)
    est = 2 * rows * d * 4 + 4 * rows * d * 4 + 4 * rows * d * 4
    return pl.pallas_call(
        _combine_kernel,
        grid_spec=pltpu.PrefetchScalarGridSpec(
            num_scalar_prefetch=1,
            grid=(n // rows,),
            in_specs=[
                pl.BlockSpec(memory_space=pl.ANY),
                pl.BlockSpec((rows, d), lambda i, s: (i, 0)),
                pl.BlockSpec((rows, LANES), lambda i, s: (i, 0)),
                pl.BlockSpec((1, d), lambda i, s: (0, 0)),
            ],
            out_specs=pl.BlockSpec((rows, d), lambda i, s: (i, 0)),
            scratch_shapes=[pltpu.VMEM((TOP_K, rows, d), F32), pltpu.SemaphoreType.DMA(())],
        ),
        out_shape=jax.ShapeDtypeStruct((n, d), F32),
        compiler_params=_cparams(("arbitrary",), est),
        name="combine_norm",
    )(slot_of_assignment, ys, x, routed, g_final.reshape(1, d))


def moe_plan(routed, *, tm):
    n = routed.shape[0]
    experts = routed[:, :TOP_K].astype(jnp.int32).reshape(-1)
    onehot = (experts[:, None] == jnp.arange(N_EXPERTS)[None, :]).astype(jnp.int32)
    rank = jnp.sum((jnp.cumsum(onehot, axis=0) - onehot) * onehot, axis=1)
    counts = jnp.sum(onehot, axis=0)
    padded = ((counts + tm - 1) // tm) * tm
    ends = jnp.cumsum(padded)
    starts = ends - padded
    slot = (starts[experts] + rank).astype(jnp.int32)
    n_blocks = (n * TOP_K) // tm + N_EXPERTS
    tok = jnp.repeat(jnp.arange(n, dtype=jnp.int32), TOP_K)
    tok_of_slot = jnp.zeros((n_blocks * tm,), jnp.int32).at[slot].set(tok)
    blk_expert = jnp.minimum(
        jnp.searchsorted(ends, jnp.arange(n_blocks, dtype=jnp.int32) * tm, side="right"), N_EXPERTS - 1
    ).astype(jnp.int32)
    n_used = (ends[-1] // tm).astype(jnp.int32).reshape(1)
    return slot, tok_of_slot, blk_expert, n_used


def _prepare_weights(w_qkv, w_o, w_ff_gate, w_ff_up, w_ff_down, w_in_c, w_out_c):
    d_inner = w_out_c.shape[1]
    gw = d_inner // SSM_GROUPS
    main_cols = 2 * d_inner + 2 * SSM_GROUPS * SSM_STATE
    return dict(
        w_qkv=column_tiles(w_qkv[0], N_HEADS_A * HEAD_DIM), w_o=w_o[0].astype(BF16),
        w_ff_gate=column_tiles(w_ff_gate[0], FF_TILE), w_ff_up=column_tiles(w_ff_up[0], FF_TILE),
        w_ff_down=w_ff_down[0].astype(BF16),
        w_in_main=column_tiles(w_in_c[0][:, :main_cols], 2 * gw),
        w_in_dt=column_tiles(w_in_c[0][:, main_cols:], w_in_c.shape[2] - main_cols),
        w_out=w_out_c[0].astype(BF16),
    )


def _mixer_layers(x3, wb, g_mix, g_ffn, rpb, conv_w, conv_b, dt_bias, a_log, d_skip, g_gate):
    batch, seq_len, d = x3.shape
    n = batch * seq_len
    x = x3.reshape(n, d)

    cos, sin = rope_tables(seq_len)
    qkv = qkv_proj(x, g_mix[0], wb["w_qkv"], cos, sin, seq_len=seq_len)
    bias_tiles = natten_bias_tiles(rpb[0], seq_len // GRID_W)
    o_a = natten(qkv, bias_tiles, batch=batch, seq_len=seq_len)
    o_b = dilated_attention(qkv, batch=batch, seq_len=seq_len)
    x = attn_out(x, o_a, o_b, wb["w_o"])
    x = ffn(x, g_ffn[0], wb["w_ff_gate"], wb["w_ff_up"], wb["w_ff_down"])

    d_inner = wb["w_out"].shape[0]
    gw = d_inner // SSM_GROUPS
    hpg = gw // SSM_HEAD_DIM
    heads = SSM_GROUPS * hpg
    zx = norm_matmul(x, g_mix[1], wb["w_in_main"], out_width=gw, out_dtype=BF16)
    dt_raw = norm_matmul(x, g_mix[1], wb["w_in_dt"], out_width=2 * heads, out_dtype=F32)[0]
    z_tiles = d_inner // gw
    xs = conv_silu(zx, conv_w[0], conv_b[0], width=gw, first_tile=z_tiles, n_tiles=SSM_GROUPS,
                   col_offset=0, batch=batch, seq_len=seq_len, tr=2048)
    bc_tiles = SSM_GROUPS * SSM_STATE // gw
    b_nat, b_t = conv_silu(zx, conv_w[0], conv_b[0], width=SSM_STATE, first_tile=2 * z_tiles, n_tiles=SSM_GROUPS,
                           col_offset=d_inner, batch=batch, seq_len=seq_len, tr=4096, transposed=True)
    c_nat = conv_silu(zx, conv_w[0], conv_b[0], width=SSM_STATE, first_tile=2 * z_tiles + bc_tiles,
                      n_tiles=SSM_GROUPS, col_offset=d_inner + SSM_GROUPS * SSM_STATE,
                      batch=batch, seq_len=seq_len, tr=4096)
    L = SSM_CHUNK
    rows = SSM_GROUPS * 2 * hpg
    dt_rows = dt_raw.reshape(n // L, L, 2, SSM_GROUPS, hpg).transpose(0, 3, 2, 4, 1).reshape(n // L, rows, L)

    def per_row(p):
        return p.reshape(2, SSM_GROUPS, hpg).transpose(1, 0, 2).reshape(rows, 1)

    d_skip_rows = jnp.repeat(d_skip[0].reshape(SSM_GROUPS, 1, hpg), SSM_HEAD_DIM, axis=2)
    y_f, y_b = ssd(xs, b_nat, b_t, c_nat, dt_rows, per_row(dt_bias[0]), per_row(a_log[0]), d_skip_rows,
                   batch=batch, seq_len=seq_len)
    return mamba_out(y_f, y_b, zx, g_gate[0], wb["w_out"], x)


def kernel(x_prompt, x_sample, g_mix, g_ffn, w_qkv, rpb, w_o, w_ff_gate, w_ff_up, w_ff_down, w_in_c, conv_w, conv_b,
           dt_bias, a_log, d_skip, g_gate, w_out_c, w_router, w_e_gate, w_e_up, w_e_down, g_final):
    wb = _prepare_weights(w_qkv, w_o, w_ff_gate, w_ff_up, w_ff_down, w_in_c, w_out_c)
    args = (wb, g_mix, g_ffn, rpb, conv_w, conv_b, dt_bias, a_log, d_skip, g_gate)
    x_p = _mixer_layers(x_prompt, *args)
    x_s = _mixer_layers(x_sample, *args)

    n_p = x_p.shape[0]
    routed, xn = router(x_p, x_s, g_ffn[1], w_router[0])
    slot, tok_of_slot, blk_expert, n_used = moe_plan(routed, tm=MOE_TM)
    ys = expert_ffn(blk_expert, n_used, tok_of_slot, xn, w_e_gate[0].astype(BF16), w_e_up[0].astype(BF16),
                    w_e_down[0].astype(BF16))
    out_p = combine_norm(slot[:TOP_K * n_p], ys, x_p, routed[:n_p], g_final)
    out_s = combine_norm(slot[TOP_K * n_p:], ys, x_s, routed[n_p:], g_final)
    return out_p.reshape(x_prompt.shape), out_s.reshape(x_sample.shape)
```

```python
import functools
import math

import jax
import jax.numpy as jnp
import numpy as np
from jax import lax
from jax.experimental import pallas as pl
from jax.experimental.pallas import tpu as pltpu

F32 = jnp.float32
BF16 = jnp.bfloat16
EPS = 1e-6
NEG_INF = float("-inf")

GRID_W = 64
HEAD_DIM = 128
N_HEADS_A = 4
N_HEADS_B_GROUP = 4
DILATIONS = (1, 4, 16)
BAND_RADIUS = 64
N_HEADS_QKV = N_HEADS_A + N_HEADS_B_GROUP * len(DILATIONS)
WIN_H = 8
WIN_W = 16
ROPE_THETA = 10000.0
SSM_HEAD_DIM = 64
SSM_GROUPS = 8
SSM_STATE = 128
SSM_CONV = 5
SSM_CHUNK = 128
N_EXPERTS = 8
TOP_K = 2

V7X_VMEM_BYTES = 64 * 1024 * 1024
LANES = 128
BF16_SUBLANES = 16

NAT_ROWS = 8
NAT_KROWS = NAT_ROWS + WIN_H - 1
DIL_TQ = 512
DIL_KC = (256, 256, 512)
DIL_PARTS = 4
MOE_TM = 512
GATHER_ROWS = 256
FF_TILE = 512


def _cparams(semantics, vmem_estimate):
    limit = int(min(max(2 * vmem_estimate, 32 * 1024 * 1024), V7X_VMEM_BYTES - 8 * 1024 * 1024))
    return pltpu.CompilerParams(dimension_semantics=semantics, vmem_limit_bytes=limit)


def _rmsnorm(x, g):
    return x * lax.rsqrt(jnp.mean(x * x, axis=-1, keepdims=True) + EPS) * g


def _sigmoid(x):
    return 1.0 / (1.0 + jnp.exp(-x))


def _softplus(x):
    return jnp.maximum(x, 0.0) + jnp.log(1.0 + jnp.exp(-jnp.abs(x)))


def _dot(a, b):
    return jnp.dot(a, b, preferred_element_type=F32)


def _dot_nt(a, b):
    return lax.dot_general(a, b, (((1,), (1,)), ((), ())), preferred_element_type=F32)


def _norm_matmul_kernel(x_ref, g_ref, w_ref, o_ref, xn_ref):
    @pl.when(pl.program_id(1) == 0)
    def _():
        xn_ref[...] = _rmsnorm(x_ref[...], g_ref[...]).astype(BF16)

    r = _dot(xn_ref[...], w_ref[...])
    width = o_ref.shape[-1]
    for t in range(o_ref.shape[0]):
        o_ref[t] = r[:, t * width:(t + 1) * width].astype(o_ref.dtype)


def column_tiles(w, tn):
    *lead, k, m = w.shape
    w = w.astype(BF16).reshape(*lead, k, m // tn, tn)
    return jnp.swapaxes(w, -3, -2)


def norm_matmul(x, g, w_tiles, *, out_width, out_dtype, tm=1024):
    n, k = x.shape
    n_tiles, _, tn = w_tiles.shape
    m = n_tiles * tn
    per = tn // out_width
    tm = min(tm, n)
    est = 2 * tm * k * 4 + tm * k * 2 + 2 * k * tn * 2 + 3 * tm * tn * 4
    return pl.pallas_call(
        _norm_matmul_kernel,
        grid=(n // tm, m // tn),
        in_specs=[
            pl.BlockSpec((tm, k), lambda i, j: (i, 0)),
            pl.BlockSpec((1, k), lambda i, j: (0, 0)),
            pl.BlockSpec((None, k, tn), lambda i, j: (j, 0, 0)),
        ],
        out_specs=pl.BlockSpec((per, tm, out_width), lambda i, j: (j, i, 0)),
        out_shape=jax.ShapeDtypeStruct((m // out_width, n, out_width), out_dtype),
        scratch_shapes=[pltpu.VMEM((tm, k), BF16)],
        compiler_params=_cparams(("parallel", "arbitrary"), est),
        name="norm_matmul",
    )(x, g.reshape(1, k), w_tiles)


def _qkv_kernel(x_ref, g_ref, w_ref, cos_ref, sin_ref, o_ref, xn_ref, *, heads_per_tile, scale):
    j = pl.program_id(1)
    tiles_per_part = N_HEADS_QKV // heads_per_tile

    @pl.when(j == 0)
    def _():
        xn_ref[...] = _rmsnorm(x_ref[...], g_ref[...]).astype(BF16)

    use_rope = jnp.logical_and(j < 2 * tiles_per_part, j % tiles_per_part != 0)
    mult = jnp.where(j < tiles_per_part, scale, 1.0)
    c = jnp.where(use_rope, cos_ref[...], 1.0) * mult
    s = jnp.where(use_rope, sin_ref[...], 0.0) * mult
    xn = xn_ref[...]
    for pair in range(heads_per_tile // 2):
        r = _dot(xn, w_ref[:, pair * 2 * HEAD_DIM:(pair + 1) * 2 * HEAD_DIM])
        for k in range(2):
            p = r[:, k * HEAD_DIM:(k + 1) * HEAD_DIM]
            o_ref[2 * pair + k] = (p * c + pltpu.roll(p, HEAD_DIM // 2, 1) * s).astype(o_ref.dtype)


def qkv_proj(x, g, w_tiles, cos, sin, *, seq_len, tm=1024):
    n, k = x.shape
    hpt = N_HEADS_A
    n_tiles, _, tn = w_tiles.shape
    assert tn == hpt * HEAD_DIM
    m = n_tiles * tn
    tm = min(tm, seq_len)
    tiles_per_seq = seq_len // tm
    est = 2 * tm * k * 4 + tm * k * 2 + 2 * k * tn * 2 + 2 * tm * tn * 2 + 4 * tm * HEAD_DIM * 4 + tm * tn * 4
    return pl.pallas_call(
        functools.partial(_qkv_kernel, heads_per_tile=hpt, scale=HEAD_DIM ** -0.5),
        grid=(n // tm, m // tn),
        in_specs=[
            pl.BlockSpec((tm, k), lambda i, j: (i, 0)),
            pl.BlockSpec((1, k), lambda i, j: (0, 0)),
            pl.BlockSpec((None, k, tn), lambda i, j: (j, 0, 0)),
            pl.BlockSpec((tm, HEAD_DIM), lambda i, j: (i % tiles_per_seq, 0)),
            pl.BlockSpec((tm, HEAD_DIM), lambda i, j: (i % tiles_per_seq, 0)),
        ],
        out_specs=pl.BlockSpec((hpt, tm, HEAD_DIM), lambda i, j: (j, i, 0)),
        out_shape=jax.ShapeDtypeStruct((m // HEAD_DIM, n, HEAD_DIM), BF16),
        scratch_shapes=[pltpu.VMEM((tm, k), BF16)],
        compiler_params=_cparams(("parallel", "arbitrary"), est),
        name="qkv_proj",
    )(x, g.reshape(1, k), w_tiles, cos, sin)


def rope_tables(seq_len):
    half = HEAD_DIM // 2
    inv = ROPE_THETA ** (-jnp.arange(half, dtype=F32) / half)
    ang = jnp.arange(seq_len, dtype=F32)[:, None] * inv[None, :]
    cos = jnp.cos(ang)
    sin = jnp.sin(ang)
    return jnp.concatenate([cos, cos], axis=1), jnp.concatenate([-sin, sin], axis=1)


def _natten_kernel(q_ref, k_ref, v_ref, bias_ref, o_ref, *, n_blocks, grid_rows):
    blk = pl.program_id(2)
    first_row = jnp.where(blk == 0, 0,
                          jnp.where(blk == n_blocks - 1, grid_rows - NAT_KROWS, blk * NAT_ROWS - WIN_H // 2))
    start = pl.multiple_of(first_row * GRID_W, GRID_W)
    kw = k_ref[pl.ds(start, NAT_KROWS * GRID_W), :]
    vw = v_ref[pl.ds(start, NAT_KROWS * GRID_W), :]
    s = _dot_nt(q_ref[...], kw) + bias_ref[...]
    m = jnp.max(s, axis=-1, keepdims=True)
    p = jnp.exp(s - m)
    l = jnp.sum(p, axis=-1, keepdims=True)
    o_ref[...] = (_dot(p.astype(BF16), vw) / l).astype(o_ref.dtype)


def natten_bias_tiles(rpb, grid_rows):
    n_blocks = grid_rows // NAT_ROWS
    r0s = np.array([0, NAT_ROWS, (n_blocks - 1) * NAT_ROWS])
    k0s = np.array([0, NAT_ROWS - WIN_H // 2, grid_rows - NAT_KROWS])
    r = r0s[:, None] + np.arange(NAT_ROWS)[None, :]
    kr = k0s[:, None] + np.arange(NAT_KROWS)[None, :]
    rs = np.clip(r - WIN_H // 2, 0, grid_rows - WIN_H)
    row_ok = (kr[:, None, :] >= rs[:, :, None]) & (kr[:, None, :] < rs[:, :, None] + WIN_H)
    d_row = np.clip(kr[:, None, :] - r[:, :, None] + (WIN_H - 1), 0, 2 * WIN_H - 2)
    c = np.arange(GRID_W)
    cs = np.clip(c - WIN_W // 2, 0, GRID_W - WIN_W)
    col_ok = (c[None, :] >= cs[:, None]) & (c[None, :] < cs[:, None] + WIN_W)
    d_col = np.clip(c[None, :] - c[:, None] + (WIN_W - 1), 0, 2 * WIN_W - 2)
    sel_row = (d_row[..., None] == np.arange(2 * WIN_H - 1)).astype(np.float32)
    sel_col = (d_col[..., None] == np.arange(2 * WIN_W - 1)).astype(np.float32)
    rows = jnp.einsum("tikr,hrc->thikc", sel_row, rpb.astype(F32), precision=lax.Precision.HIGHEST)
    bias = jnp.einsum("thikc,qwc->thiqkw", rows, sel_col, precision=lax.Precision.HIGHEST)
    ok = row_ok[:, :, None, :, None] & col_ok[None, None, :, None, :]
    bias = jnp.where(ok[:, None], bias, NEG_INF)
    return bias.reshape(3, rpb.shape[0], NAT_ROWS * GRID_W, NAT_KROWS * GRID_W)


def natten(qkv, bias_tiles, *, batch, seq_len):
    n = batch * seq_len
    grid_rows = seq_len // GRID_W
    n_blocks = grid_rows // NAT_ROWS
    tq = NAT_ROWS * GRID_W
    tk = NAT_KROWS * GRID_W

    def tile_kind(blk):
        return jnp.where(blk == 0, 0, jnp.where(blk == n_blocks - 1, 2, 1))

    est = 4 * seq_len * HEAD_DIM * 2 + 2 * tq * tk * 4 + 3 * tq * tk * 4
    return pl.pallas_call(
        functools.partial(_natten_kernel, n_blocks=n_blocks, grid_rows=grid_rows),
        grid=(batch, N_HEADS_A, n_blocks),
        in_specs=[
            pl.BlockSpec((None, tq, HEAD_DIM), lambda b, h, i: (h, b * n_blocks + i, 0)),
            pl.BlockSpec((None, seq_len, HEAD_DIM), lambda b, h, i: (N_HEADS_QKV + h, b, 0)),
            pl.BlockSpec((None, seq_len, HEAD_DIM), lambda b, h, i: (2 * N_HEADS_QKV + h, b, 0)),
            pl.BlockSpec((None, None, tq, tk), lambda b, h, i: (tile_kind(i), h, 0, 0)),
        ],
        out_specs=pl.BlockSpec((tq, HEAD_DIM), lambda b, h, i: (b * n_blocks + i, h)),
        out_shape=jax.ShapeDtypeStruct((n, N_HEADS_A * HEAD_DIM), BF16),
        compiler_params=_cparams(("parallel", "parallel", "arbitrary"), est),
        name="natten",
    )(qkv, qkv, qkv, bias_tiles)


class _DilatedGeometry:
    def __init__(self, dil, kc):
        self.dil, self.kc = dil, kc
        self.reach = BAND_RADIUS * dil
        self.tp = DIL_TQ // DIL_PARTS
        assert kc % self.tp == 0
        self.halo = -(-self.reach // kc) * kc
        self.n_chunks = (DIL_TQ + 2 * self.halo) // kc
        self.n_tiles = (kc // self.tp) * (self.n_chunks - 1) + DIL_PARTS

    def tile_index(self, chunk, part):
        return (self.kc // self.tp) * chunk + (DIL_PARTS - 1 - part)

    def bias_tiles(self):
        row = np.arange(self.tp)[:, None]
        col = np.arange(self.kc)[None, :]
        tiles = []
        for u in range(self.n_tiles):
            rel = col - row + (u - (DIL_PARTS - 1)) * self.tp - self.halo
            ok = (np.abs(rel) <= self.reach) & (rel % self.dil == 0)
            tiles.append(np.where(ok, 0.0, -np.inf))
        return np.stack(tiles).astype(np.float32)


DIL_GEOMETRY = tuple(_DilatedGeometry(d, kc) for d, kc in zip(DILATIONS, DIL_KC))


def _dilated_kernel(b0, b1, b2, q0, q1, q2, k0, k1, k2, v0, v1, v2, o_ref, *, seq_len):
    tq = DIL_TQ
    parts = DIL_PARTS
    tp = tq // parts
    t0 = pl.program_id(2) * tq
    carry = tuple((jnp.full((tp, 1), -1e30, F32), jnp.zeros((tp, 1), F32), jnp.zeros((tp, HEAD_DIM), F32))
                  for _ in range(parts))
    for geo, bias_ref, q_ref, k_ref, v_ref in zip(DIL_GEOMETRY, (b0, b1, b2), (q0, q1, q2), (k0, k1, k2), (v0, v1, v2)):
        kc = geo.kc
        window_start = t0 - geo.halo
        lo = jnp.maximum(0, (geo.halo - t0) // kc)
        hi = jnp.minimum(geo.n_chunks, (seq_len - window_start) // kc)
        qs = tuple(q_ref[p * tp:(p + 1) * tp, :] for p in range(parts))

        def body(ci, carry, geo=geo, bias_ref=bias_ref, k_ref=k_ref, v_ref=v_ref, window_start=window_start, qs=qs):
            ks = pl.multiple_of(window_start + ci * geo.kc, geo.kc)
            kk = k_ref[pl.ds(ks, geo.kc), :]
            vv = v_ref[pl.ds(ks, geo.kc), :]
            scores = [_dot_nt(qs[p], kk) + bias_ref[geo.tile_index(ci, p)] for p in range(parts)]
            stats = []
            for p in range(parts):
                m, l, _ = carry[p]
                m_new = jnp.maximum(m, jnp.max(scores[p], axis=-1, keepdims=True))
                alpha = jnp.exp(m - m_new)
                e = jnp.exp(scores[p] - m_new)
                stats.append((m_new, alpha, alpha * l + jnp.sum(e, axis=-1, keepdims=True), e.astype(BF16)))
            return tuple((m_new, l, alpha * carry[p][2] + _dot(e, vv))
                         for p, (m_new, alpha, l, e) in enumerate(stats))

        carry = lax.fori_loop(lo, hi, body, carry)
    for p in range(parts):
        _, l, acc = carry[p]
        o_ref[p * tp:(p + 1) * tp, :] = (acc / l).astype(o_ref.dtype)


def dilated_attention(qkv, *, batch, seq_len):
    n = batch * seq_len
    nq = seq_len // DIL_TQ
    biases = [geo.bias_tiles() for geo in DIL_GEOMETRY]

    def q_spec(g):
        return pl.BlockSpec((None, DIL_TQ, HEAD_DIM),
                            lambda b, j, i: (N_HEADS_A + N_HEADS_B_GROUP * g + j, b * nq + i, 0))

    def kv_spec(part, g):
        return pl.BlockSpec((None, seq_len, HEAD_DIM),
                            lambda b, j, i: (part * N_HEADS_QKV + N_HEADS_A + N_HEADS_B_GROUP * g + j, b, 0))

    est = 12 * seq_len * HEAD_DIM * 2 + 8 * DIL_TQ * max(DIL_KC) * 4 + 2 * sum(b.size for b in biases) * 4
    groups = range(len(DILATIONS))
    return pl.pallas_call(
        functools.partial(_dilated_kernel, seq_len=seq_len),
        grid=(batch, N_HEADS_B_GROUP, nq),
        in_specs=[pl.BlockSpec(b.shape, lambda b_, j, i: (0, 0, 0)) for b in biases]
        + [q_spec(g) for g in groups] + [kv_spec(1, g) for g in groups] + [kv_spec(2, g) for g in groups],
        out_specs=pl.BlockSpec((DIL_TQ, HEAD_DIM), lambda b, j, i: (b * nq + i, j)),
        out_shape=jax.ShapeDtypeStruct((n, N_HEADS_B_GROUP * HEAD_DIM), BF16),
        compiler_params=_cparams(("parallel", "parallel", "arbitrary"), est),
        name="dilated_attention",
    )(*biases, *([qkv] * 9))


def _attn_out_kernel(x_ref, oa_ref, ob_ref, w_ref, o_ref):
    ka = oa_ref.shape[1]
    o_ref[...] = x_ref[...] + _dot(oa_ref[...], w_ref[:ka, :]) + _dot(ob_ref[...], w_ref[ka:, :])


def attn_out(x, o_a, o_b, w, *, tm=512):
    n, d = x.shape
    ka, kb = o_a.shape[1], o_b.shape[1]
    tm = min(tm, n)
    est = 4 * tm * d * 4 + 2 * (ka + kb) * d * 2 + 2 * tm * (ka + kb) * 2
    return pl.pallas_call(
        _attn_out_kernel,
        grid=(n // tm,),
        in_specs=[
            pl.BlockSpec((tm, d), lambda i: (i, 0)),
            pl.BlockSpec((tm, ka), lambda i: (i, 0)),
            pl.BlockSpec((tm, kb), lambda i: (i, 0)),
            pl.BlockSpec((ka + kb, d), lambda i: (0, 0)),
        ],
        out_specs=pl.BlockSpec((tm, d), lambda i: (i, 0)),
        out_shape=jax.ShapeDtypeStruct((n, d), F32),
        compiler_params=_cparams(("parallel",), est),
        name="attn_out",
    )(x, o_a, o_b, w)


def _ffn_kernel(x_ref, g_ref, wg_ref, wu_ref, wd_ref, o_ref, xn_ref):
    f = pl.program_id(1)

    @pl.when(f == 0)
    def _():
        xn_ref[...] = _rmsnorm(x_ref[...], g_ref[...]).astype(BF16)
        o_ref[...] = x_ref[...]

    xn = xn_ref[...]
    gate = _dot(xn, wg_ref[...])
    up = _dot(xn, wu_ref[...])
    hid = (gate * _sigmoid(gate) * up).astype(BF16)
    o_ref[...] += _dot(hid, wd_ref[...])


def ffn(x, g, wg_tiles, wu_tiles, wd, *, tm=512):
    n, d = x.shape
    nf, _, tf = wg_tiles.shape
    tm = min(tm, n)
    est = 4 * tm * d * 4 + tm * d * 2 + 6 * d * tf * 2 + 3 * tm * tf * 4
    return pl.pallas_call(
        _ffn_kernel,
        grid=(n // tm, nf),
        in_specs=[
            pl.BlockSpec((tm, d), lambda i, f: (i, 0)),
            pl.BlockSpec((1, d), lambda i, f: (0, 0)),
            pl.BlockSpec((None, d, tf), lambda i, f: (f, 0, 0)),
            pl.BlockSpec((None, d, tf), lambda i, f: (f, 0, 0)),
            pl.BlockSpec((tf, d), lambda i, f: (f, 0)),
        ],
        out_specs=pl.BlockSpec((tm, d), lambda i, f: (i, 0)),
        out_shape=jax.ShapeDtypeStruct((n, d), F32),
        scratch_shapes=[pltpu.VMEM((tm, d), BF16)],
        compiler_params=_cparams(("parallel", "arbitrary"), est),
        name="ffn",
    )(x, g.reshape(1, d), wg_tiles, wu_tiles, wd)


def _conv_kernel(xm_ref, xp_ref, xn_ref, w_ref, b_ref, o_ref, *rest, tr, n_row_blocks):
    ext_ref = rest[-1]
    i = pl.program_id(2)
    hb = BF16_SUBLANES
    pad = SSM_CONV // 2
    ext_ref[0:hb, :] = jnp.where(i > 0, xp_ref[...].astype(F32), 0.0)
    ext_ref[hb:hb + tr, :] = xm_ref[...].astype(F32)
    ext_ref[hb + tr:2 * hb + tr, :] = jnp.where(i < n_row_blocks - 1, xn_ref[...].astype(F32), 0.0)
    acc = jnp.broadcast_to(b_ref[...], o_ref.shape)
    for k in range(SSM_CONV):
        acc = acc + ext_ref[pl.ds(hb - pad + k, tr), :] * w_ref[k:k + 1, :]
    out = acc * _sigmoid(acc)
    o_ref[...] = out.astype(o_ref.dtype)
    if len(rest) == 2:
        ot_ref = rest[0]
        L = ot_ref.shape[-1]
        for c in range(tr // L):
            ot_ref[c] = out[c * L:(c + 1) * L, :].T.astype(ot_ref.dtype)


def conv_silu(zx, conv_w, conv_b, *, width, first_tile, n_tiles, col_offset, batch, seq_len, tr, transposed=False):
    n = batch * seq_len
    tile_w = zx.shape[2]
    per = tile_w // width
    tr = min(tr, seq_len)
    nr = seq_len // tr
    hb = BF16_SUBLANES
    seq_hb = seq_len // hb
    n_hb = n // hb

    def main_map(c, b, i):
        return (first_tile + c // per, b * nr + i, c % per)

    def prev_map(c, b, i):
        return (first_tile + c // per, jnp.maximum(b * seq_hb + i * (tr // hb) - 1, 0), c % per)

    def next_map(c, b, i):
        return (first_tile + c // per, jnp.minimum(b * seq_hb + (i + 1) * (tr // hb), n_hb - 1), c % per)

    L = SSM_CHUNK
    out_specs = [pl.BlockSpec((None, tr, width), lambda c, b, i: (c, b * nr + i, 0))]
    out_shape = [jax.ShapeDtypeStruct((n_tiles, n, width), BF16)]
    if transposed:
        out_specs.append(pl.BlockSpec((None, tr // L, width, L), lambda c, b, i: (c, b * nr + i, 0, 0)))
        out_shape.append(jax.ShapeDtypeStruct((n_tiles, n // L, width, L), BF16))
    est = 4 * tr * width * 2 + (tr + 2 * hb) * width * 4 + 4 * tr * width * 4 + 4 * tr * width * 2
    outs = pl.pallas_call(
        functools.partial(_conv_kernel, tr=tr, n_row_blocks=nr),
        grid=(n_tiles, batch, nr),
        in_specs=[
            pl.BlockSpec((None, tr, width), main_map),
            pl.BlockSpec((None, hb, width), prev_map),
            pl.BlockSpec((None, hb, width), next_map),
            pl.BlockSpec((SSM_CONV, width), lambda c, b, i: (0, col_offset // width + c)),
            pl.BlockSpec((1, width), lambda c, b, i: (0, col_offset // width + c)),
        ],
        out_specs=out_specs,
        out_shape=out_shape,
        scratch_shapes=[pltpu.VMEM((tr + 2 * hb, width), F32)],
        compiler_params=_cparams(("parallel", "parallel", "arbitrary"), est),
        name="conv_silu",
    )(zx, zx, zx, conv_w, conv_b.reshape(1, -1))
    return outs if transposed else outs[0]


def _lane_cumsum(a):
    lane = lax.broadcasted_iota(jnp.int32, a.shape, 1)
    shift = 1
    while shift < a.shape[1]:
        a = a + jnp.where(lane >= shift, pltpu.roll(a, shift, 1), 0.0)
        shift *= 2
    return a


def _ssd_decay_kernel(dt_ref, dtbias_ref, alog_ref, seg_ref, dtact_ref, *, hpg):
    L = SSM_CHUNK
    chunks, rows, _ = dt_ref.shape
    dt_act = _softplus(dt_ref[...] + dtbias_ref[...])
    dt_all = dt_act.reshape(chunks * rows, L)
    a_all = (dt_act * (-jnp.exp(alog_ref[...]))).reshape(chunks * rows, L)
    cum = _lane_cumsum(a_all)
    suf = cum[:, L - 1:L] - cum + a_all
    row = lax.broadcasted_iota(jnp.int32, cum.shape, 0)
    seg_ref[...] = jnp.where((row & (2 * hpg - 1)) < hpg, cum, suf).reshape(chunks, rows, L)
    dtact_ref[...] = dt_all.reshape(chunks, rows, L)


def ssd_decay_rows(dt_rows, dt_bias_rows, a_log_rows, *, hpg, chunks_per_step=8):
    n_chunks, rows, L = dt_rows.shape
    assert hpg & (hpg - 1) == 0
    cps = math.gcd(chunks_per_step, n_chunks)
    spec = pl.BlockSpec((cps, rows, L), lambda i: (i, 0, 0))
    const = pl.BlockSpec((rows, 1), lambda i: (0, 0))
    shape = jax.ShapeDtypeStruct(dt_rows.shape, F32)
    return pl.pallas_call(
        functools.partial(_ssd_decay_kernel, hpg=hpg),
        grid=(n_chunks // cps,),
        in_specs=[spec, const, const],
        out_specs=[spec, spec],
        out_shape=[shape, shape],
        compiler_params=_cparams(("parallel",), 16 * cps * rows * L * 4),
        name="ssd_decay_rows",
    )(dt_rows, dt_bias_rows, a_log_rows)


def _ssd_kernel(xa_ref, ba_ref, bta_ref, ca_ref, sega_ref, dta_ref, xb_ref, btb_ref, cb_ref, segb_ref, dtb_ref,
                dskip_ref, yf_ref, yb_ref, sf_ref, sb_ref, *, hpg):
    L = SSM_CHUNK
    P = SSM_HEAD_DIM
    pairs = hpg // 2

    @pl.when(pl.program_id(1) == 0)
    def _():
        sf_ref[...] = jnp.zeros_like(sf_ref)
        sb_ref[...] = jnp.zeros_like(sb_ref)

    li = lax.broadcasted_iota(jnp.int32, (L, L), 0)
    si = lax.broadcasted_iota(jnp.int32, (L, L), 1)
    causal = li >= si
    anti = li <= si
    low_lanes = lax.broadcasted_iota(jnp.int32, (L, 2 * P), 1) < P
    high_lanes = jnp.logical_not(low_lanes)
    low_lanes_row = lax.broadcasted_iota(jnp.int32, (1, 2 * P), 1) < P

    def lanes_of(col):
        return jnp.broadcast_to(col, (L, 2 * P))

    def columns(seg_r, dt_r):
        stacked = jnp.concatenate([seg_r, dt_r, jnp.zeros((L - 4 * hpg, L), F32)], axis=0)
        return stacked.T

    spread_bwd = (lax.broadcasted_iota(jnp.int32, (L, hpg * P), 0)
                  == hpg + lax.broadcasted_iota(jnp.int32, (L, hpg * P), 1) // P).astype(BF16)

    def advance_state(s_ref, g, lanes, state, btf, xm0, xm1, seg_r, dt_r, h0, h1, tot_lane):
        acc = None
        decays = []
        for h, xm in ((h0, xm0), (h1, xm1)):
            tot = seg_r[h:h + 1, tot_lane:tot_lane + 1]
            coef = jnp.exp(tot - seg_r[h:h + 1, :]) * dt_r[h:h + 1, :]
            part = _dot((btf * coef).astype(BF16), xm)
            acc = part if acc is None else acc + part
            decays.append(jnp.exp(tot))
        decay = jnp.where(low_lanes_row, decays[0], decays[1])
        s_ref[g, :, lanes] = state[:, lanes] * decay + acc

    def group_body(g, c):
        r0 = pl.multiple_of(g * 2 * hpg, 2 * hpg)

        seg_r = sega_ref[pl.ds(r0, 2 * hpg), :]
        dt_r = dta_ref[pl.ds(r0, 2 * hpg), :]
        cols = columns(seg_r, dt_r)
        xg = xa_ref[g]
        cg = ca_ref[g]
        state = sf_ref[g]
        carried = _dot(cg, state.astype(BF16))
        cbm = _dot_nt(cg, ba_ref[g])
        btf = bta_ref[g].astype(F32)
        for p in range(pairs):
            lanes = slice(p * 2 * P, (p + 1) * 2 * P)
            xpf = xg[:, lanes].astype(F32)
            xm0 = jnp.where(low_lanes, xpf, 0.0).astype(BF16)
            xm1 = jnp.where(high_lanes, xpf, 0.0).astype(BF16)
            h0, h1 = 2 * p, 2 * p + 1
            f0, f1 = lanes_of(cols[:, h0:h0 + 1]), lanes_of(cols[:, h1:h1 + 1])
            y = carried[:, lanes] * jnp.where(low_lanes, jnp.exp(f0), jnp.exp(f1)) + dskip_ref[g][:, lanes] * xpf
            for hh, f_cols, xm in ((h0, f0, xm0), (h1, f1, xm1)):
                wf = jnp.exp(jnp.where(causal, f_cols - seg_r[hh:hh + 1, :], NEG_INF)) * dt_r[hh:hh + 1, :]
                b_cols = lanes_of(cols[:, hpg + hh:hpg + hh + 1])
                wb = jnp.exp(jnp.where(anti, b_cols - seg_r[hpg + hh:hpg + hh + 1, :], NEG_INF)) \
                    * dt_r[hpg + hh:hpg + hh + 1, :]
                y = y + _dot((cbm * (wf + wb)).astype(BF16), xm)
            yf_ref[g, :, lanes] = y.astype(yf_ref.dtype)
            advance_state(sf_ref, g, lanes, state, btf, xm0, xm1, seg_r, dt_r, h0, h1, L - 1)

        seg_r = segb_ref[pl.ds(r0, 2 * hpg), :]
        dt_r = dtb_ref[pl.ds(r0, 2 * hpg), :]
        xg = xb_ref[g]
        state = sb_ref[g]
        grow = _dot(jnp.exp(columns(seg_r, dt_r)).astype(BF16), spread_bwd)
        y = _dot(cb_ref[g], state.astype(BF16)) * grow
        yb_ref[g] = y.astype(yb_ref.dtype)
        btf = btb_ref[g].astype(F32)
        for p in range(pairs):
            lanes = slice(p * 2 * P, (p + 1) * 2 * P)
            xpf = xg[:, lanes].astype(F32)
            xm0 = jnp.where(low_lanes, xpf, 0.0).astype(BF16)
            xm1 = jnp.where(high_lanes, xpf, 0.0).astype(BF16)
            advance_state(sb_ref, g, lanes, state, btf, xm0, xm1, seg_r, dt_r, hpg + 2 * p, hpg + 2 * p + 1, 0)
        return c

    lax.fori_loop(0, SSM_GROUPS, group_body, 0, unroll=2)


def ssd(xs, b_nat, b_t, c_nat, seg_rows, dt_act_rows, d_skip_rows, *, batch, seq_len):
    groups, n, gw = xs.shape
    hpg = gw // SSM_HEAD_DIM
    assert hpg & (hpg - 1) == 0
    L = SSM_CHUNK
    nc = seq_len // L
    rows = groups * 2 * hpg

    def fwd(b, k):
        return b * nc + k

    def bwd(b, k):
        return b * nc + nc - 1 - k

    def specs(chunk, with_b_nat):
        out = [pl.BlockSpec((groups, L, gw), lambda b, k: (0, chunk(b, k), 0))]
        if with_b_nat:
            out.append(pl.BlockSpec((groups, L, SSM_STATE), lambda b, k: (0, chunk(b, k), 0)))
        out += [
            pl.BlockSpec((groups, None, SSM_STATE, L), lambda b, k: (0, chunk(b, k), 0, 0)),
            pl.BlockSpec((groups, L, SSM_STATE), lambda b, k: (0, chunk(b, k), 0)),
            pl.BlockSpec((None, rows, L), lambda b, k: (chunk(b, k), 0, 0)),
            pl.BlockSpec((None, rows, L), lambda b, k: (chunk(b, k), 0, 0)),
        ]
        return out

    const = [pl.BlockSpec((groups, 1, gw), lambda b, k: (0, 0, 0))]
    y_shape = jax.ShapeDtypeStruct((groups, n, gw), BF16)
    est = 4 * (groups * L * gw * 2 + 3 * groups * L * SSM_STATE * 2 + 2 * rows * L * 4) + 4 * groups * L * gw * 2 \
        + 2 * groups * SSM_STATE * gw * 4 + 64 * L * L * 4
    return pl.pallas_call(
        functools.partial(_ssd_kernel, hpg=hpg),
        grid=(batch, nc),
        in_specs=specs(fwd, True) + specs(bwd, False) + const,
        out_specs=[pl.BlockSpec((groups, L, gw), lambda b, k: (0, fwd(b, k), 0)),
                   pl.BlockSpec((groups, L, gw), lambda b, k: (0, bwd(b, k), 0))],
        out_shape=[y_shape, y_shape],
        scratch_shapes=[pltpu.VMEM((groups, SSM_STATE, gw), F32), pltpu.VMEM((groups, SSM_STATE, gw), F32)],
        compiler_params=_cparams(("parallel", "arbitrary"), est),
        name="ssd",
    )(xs, b_nat, b_t, c_nat, seg_rows, dt_act_rows, xs, b_t, c_nat, seg_rows, dt_act_rows, d_skip_rows)


def _mamba_out_kernel(yf_ref, yb_ref, z_ref, gg_ref, w_ref, x_ref, o_ref, ssq_ref, *, d_inner):
    step = pl.program_id(1)
    groups_per_step, _, gw = z_ref.shape

    @pl.when(step == 0)
    def _():
        o_ref[...] = jnp.zeros_like(o_ref)
        ssq_ref[...] = jnp.zeros_like(ssq_ref)

    gated = []
    for k in range(groups_per_step):
        z = z_ref[k].astype(F32)
        yz = (yf_ref[k].astype(F32) + yb_ref[k].astype(F32)) * (z * _sigmoid(z))
        ssq_ref[...] += jnp.sum(yz * yz, axis=-1, keepdims=True)
        gated.append((yz * gg_ref[:, k * gw:(k + 1) * gw]).astype(BF16))
    o_ref[...] += _dot(jnp.concatenate(gated, axis=1), w_ref[...])

    @pl.when(step == pl.num_programs(1) - 1)
    def _():
        o_ref[...] = x_ref[...] + o_ref[...] * lax.rsqrt(ssq_ref[...] / d_inner + EPS)


def mamba_out(y_f, y_b, zx, g_gate, w_out, x, *, tm=512, groups_per_step=4):
    groups, n, gw = y_f.shape
    d = x.shape[1]
    tm = min(tm, n)
    gps = groups_per_step
    est = gps * (6 * tm * gw * 2 + 2 * gw * d * 2) + 3 * tm * d * 4 + 2 * tm * gps * gw * 4
    return pl.pallas_call(
        functools.partial(_mamba_out_kernel, d_inner=groups * gw),
        grid=(n // tm, groups // gps),
        in_specs=[
            pl.BlockSpec((gps, tm, gw), lambda i, s: (s, i, 0)),
            pl.BlockSpec((gps, tm, gw), lambda i, s: (s, i, 0)),
            pl.BlockSpec((gps, tm, gw), lambda i, s: (s, i, 0)),
            pl.BlockSpec((1, gps * gw), lambda i, s: (0, s)),
            pl.BlockSpec((gps * gw, d), lambda i, s: (s, 0)),
            pl.BlockSpec((tm, d), lambda i, s: (i, 0)),
        ],
        out_specs=pl.BlockSpec((tm, d), lambda i, s: (i, 0)),
        out_shape=jax.ShapeDtypeStruct((n, d), F32),
        scratch_shapes=[pltpu.VMEM((tm, 1), F32)],
        compiler_params=_cparams(("parallel", "arbitrary"), est),
        name="mamba_out",
    )(y_f, y_b, zx, g_gate.reshape(1, -1), w_out, x)


def _router_kernel(xa_ref, xb_ref, g_ref, w_ref, o_ref, xn_ref, *, tiles_a):
    i = pl.program_id(0)

    @pl.when(i < tiles_a)
    def _():
        xn_ref[...] = _rmsnorm(xa_ref[...], g_ref[...])

    @pl.when(i >= tiles_a)
    def _():
        xn_ref[...] = _rmsnorm(xb_ref[...], g_ref[...])

    xn = xn_ref[...]
    logits = jnp.dot(xn, w_ref[...], preferred_element_type=F32, precision=lax.Precision.HIGHEST)
    lane = lax.broadcasted_iota(jnp.int32, logits.shape, 1)
    logits = jnp.where(lane < N_EXPERTS, logits, NEG_INF)
    v1 = jnp.max(logits, axis=-1, keepdims=True)
    i1 = jnp.min(jnp.where(logits == v1, lane, LANES), axis=-1, keepdims=True)
    rest = jnp.where(lane == i1, NEG_INF, logits)
    v2 = jnp.max(rest, axis=-1, keepdims=True)
    i2 = jnp.min(jnp.where(rest == v2, lane, LANES), axis=-1, keepdims=True)
    e2 = jnp.exp(v2 - v1)
    g1 = 1.0 / (1.0 + e2)
    g2 = e2 / (1.0 + e2)
    out = jnp.where(lane == 0, i1.astype(F32),
                    jnp.where(lane == 1, i2.astype(F32),
                              jnp.where(lane == 2, g1, jnp.where(lane == 3, g2, 0.0))))
    o_ref[...] = out


def router(x_a, x_b, g, w_router, *, tm=512):
    (na, d), nb = x_a.shape, x_b.shape[0]
    tm = min(tm, na, nb)
    tiles_a, tiles_b = na // tm, nb // tm
    w = jnp.zeros((d, LANES), F32).at[:, :w_router.shape[1]].set(w_router)
    est = 4 * tm * d * 4 + 2 * d * LANES * 4 + 2 * tm * LANES * 4 + 3 * tm * d * 4
    return pl.pallas_call(
        functools.partial(_router_kernel, tiles_a=tiles_a),
        grid=(tiles_a + tiles_b,),
        in_specs=[
            pl.BlockSpec((tm, d), lambda i: (jnp.minimum(i, tiles_a - 1), 0)),
            pl.BlockSpec((tm, d), lambda i: (jnp.maximum(i - tiles_a, 0), 0)),
            pl.BlockSpec((1, d), lambda i: (0, 0)),
            pl.BlockSpec((d, LANES), lambda i: (0, 0)),
        ],
        out_specs=[pl.BlockSpec((tm, LANES), lambda i: (i, 0)), pl.BlockSpec((tm, d), lambda i: (i, 0))],
        out_shape=[jax.ShapeDtypeStruct((na + nb, LANES), F32), jax.ShapeDtypeStruct((na + nb, d), F32)],
        compiler_params=_cparams(("parallel",), est),
        name="router",
    )(x_a, x_b, g.reshape(1, d), w)


def _row_copy(src_hbm, dst_vmem, sem, src_row, dst_row):
    return pltpu.make_async_copy(src_hbm.at[pl.ds(src_row, 1), :], dst_vmem.at[pl.ds(dst_row, 1), :], sem)


EXPERT_ISSUE_STEPS = 4


def _expert_kernel(blk_e_ref, n_used_ref, tok_ref, x_hbm, wg_ref, wu_ref, wd_ref, o_ref, rows_ref, x_ref, sems):
    i = pl.program_id(0)
    f = pl.program_id(1)
    tm = x_ref.shape[0]
    per_step = tm // EXPERT_ISSUE_STEPS
    n_used = n_used_ref[0]
    slot = i % 2

    def issue(block, dst_slot, first_row, count):
        def body(r, c):
            _row_copy(x_hbm, rows_ref.at[dst_slot], sems.at[dst_slot], tok_ref[block * tm + first_row + r],
                      first_row + r).start()
            return c
        lax.fori_loop(0, count, body, 0, unroll=8)

    @pl.when(jnp.logical_and(i == 0, f == 0))
    def _():
        issue(0, 0, 0, tm)

    @pl.when(jnp.logical_and(f == 0, i < n_used))
    def _():
        def wait(r, c):
            _row_copy(x_hbm, rows_ref.at[slot], sems.at[slot], 0, r).wait()
            return c
        lax.fori_loop(0, tm, wait, 0, unroll=8)
        x_ref[...] = rows_ref[slot].astype(x_ref.dtype)

    @pl.when(f == 0)
    def _():
        o_ref[...] = jnp.zeros_like(o_ref)

    @pl.when(jnp.logical_and(f < EXPERT_ISSUE_STEPS, i + 1 < n_used))
    def _():
        issue(i + 1, 1 - slot, f * per_step, per_step)

    @pl.when(i < n_used)
    def _():
        x = x_ref[...]
        gate = _dot(x, wg_ref[...])
        up = _dot(x, wu_ref[...])
        hid = (gate * _sigmoid(gate) * up).astype(BF16)
        o_ref[...] += _dot(hid, wd_ref[...])


def expert_ffn(blk_expert, n_used, tok_of_slot, x_rows, wg, wu, wd, *, tm=MOE_TM, tf=1024):
    slots = tok_of_slot.shape[0]
    d = x_rows.shape[1]
    nf = wg.shape[2] // tf
    assert nf >= EXPERT_ISSUE_STEPS and tm % (8 * EXPERT_ISSUE_STEPS) == 0
    n_blocks = slots // tm

    def live(i, n_used):
        return jnp.minimum(i, n_used[0] - 1)

    def f_eff(i, f, n_used):
        return jnp.where(i < n_used[0], f, nf - 1)

    est = 2 * tm * d * 4 + tm * d * 2 + 12 * d * tf + 2 * tm * d * 4 + 3 * tm * tf * 4
    return pl.pallas_call(
        _expert_kernel,
        grid_spec=pltpu.PrefetchScalarGridSpec(
            num_scalar_prefetch=3,
            grid=(n_blocks, nf),
            in_specs=[
                pl.BlockSpec(memory_space=pl.ANY),
                pl.BlockSpec((None, d, tf), lambda i, f, be, nu, tok: (be[live(i, nu)], 0, f_eff(i, f, nu))),
                pl.BlockSpec((None, d, tf), lambda i, f, be, nu, tok: (be[live(i, nu)], 0, f_eff(i, f, nu))),
                pl.BlockSpec((None, tf, d), lambda i, f, be, nu, tok: (be[live(i, nu)], f_eff(i, f, nu), 0)),
            ],
            out_specs=pl.BlockSpec((tm, d), lambda i, f, be, nu, tok: (i, 0)),
            scratch_shapes=[pltpu.VMEM((2, tm, d), F32), pltpu.VMEM((tm, d), BF16), pltpu.SemaphoreType.DMA((2,))],
        ),
        out_shape=jax.ShapeDtypeStruct((slots, d), F32),
        compiler_params=_cparams(("arbitrary", "arbitrary"), est),
        name="expert_ffn",
    )(blk_expert, n_used, tok_of_slot, x_rows, wg, wu, wd)


def _combine_kernel(slot_ref, y_hbm, x_ref, r_ref, g_ref, o_ref, buf_ref, sem):
    rows = x_ref.shape[0]
    base = pl.program_id(0) * rows

    def start(r, c):
        for k in range(TOP_K):
            _row_copy(y_hbm, buf_ref.at[k], sem, slot_ref[TOP_K * (base + r) + k], r).start()
        return c

    def wait(r, c):
        for k in range(TOP_K):
            _row_copy(y_hbm, buf_ref.at[k], sem, 0, r).wait()
        return c

    lax.fori_loop(0, rows, start, 0, unroll=8)
    lax.fori_loop(0, rows, wait, 0, unroll=8)
    gates = r_ref[...]
    out = x_ref[...] + gates[:, 2:3] * buf_ref[0] + gates[:, 3:4] * buf_ref[1]
    o_ref[...] = _rmsnorm(out, g_ref[...])


def combine_norm(slot_of_assignment, ys, x, routed, g_final, *, rows=GATHER_ROWS):
    n, d = x.shape
    rows = min(rows, n)
    est = 2 * rows * d * 4 + 4 * rows * d * 4 + 4 * rows * d * 4
    return pl.pallas_call(
        _combine_kernel,
        grid_spec=pltpu.PrefetchScalarGridSpec(
            num_scalar_prefetch=1,
            grid=(n // rows,),
            in_specs=[
                pl.BlockSpec(memory_space=pl.ANY),
                pl.BlockSpec((rows, d), lambda i, s: (i, 0)),
                pl.BlockSpec((rows, LANES), lambda i, s: (i, 0)),
                pl.BlockSpec((1, d), lambda i, s: (0, 0)),
            ],
            out_specs=pl.BlockSpec((rows, d), lambda i, s: (i, 0)),
            scratch_shapes=[pltpu.VMEM((TOP_K, rows, d), F32), pltpu.SemaphoreType.DMA(())],
        ),
        out_shape=jax.ShapeDtypeStruct((n, d), F32),
        compiler_params=_cparams(("arbitrary",), est),
        name="combine_norm",
    )(slot_of_assignment, ys, x, routed, g_final.reshape(1, d))


def moe_plan(routed, *, tm):
    n = routed.shape[0]
    experts = routed[:, :TOP_K].astype(jnp.int32).reshape(-1)
    onehot = (experts[:, None] == jnp.arange(N_EXPERTS)[None, :]).astype(jnp.int32)
    rank = jnp.sum((jnp.cumsum(onehot, axis=0) - onehot) * onehot, axis=1)
    counts = jnp.sum(onehot, axis=0)
    padded = ((counts + tm - 1) // tm) * tm
    ends = jnp.cumsum(padded)
    starts = ends - padded
    slot = (starts[experts] + rank).astype(jnp.int32)
    n_blocks = (n * TOP_K) // tm + N_EXPERTS
    tok = jnp.repeat(jnp.arange(n, dtype=jnp.int32), TOP_K)
    tok_of_slot = jnp.zeros((n_blocks * tm,), jnp.int32).at[slot].set(tok)
    blk_expert = jnp.minimum(
        jnp.searchsorted(ends, jnp.arange(n_blocks, dtype=jnp.int32) * tm, side="right"), N_EXPERTS - 1
    ).astype(jnp.int32)
    n_used = (ends[-1] // tm).astype(jnp.int32).reshape(1)
    return slot, tok_of_slot, blk_expert, n_used


def _prepare_weights(w_qkv, w_o, w_ff_gate, w_ff_up, w_ff_down, w_in_c, w_out_c):
    d_inner = w_out_c.shape[1]
    gw = d_inner // SSM_GROUPS
    main_cols = 2 * d_inner + 2 * SSM_GROUPS * SSM_STATE
    return dict(
        w_qkv=column_tiles(w_qkv[0], N_HEADS_A * HEAD_DIM), w_o=w_o[0].astype(BF16),
        w_ff_gate=column_tiles(w_ff_gate[0], FF_TILE), w_ff_up=column_tiles(w_ff_up[0], FF_TILE),
        w_ff_down=w_ff_down[0].astype(BF16),
        w_in_main=column_tiles(w_in_c[0][:, :main_cols], 2 * gw),
        w_in_dt=column_tiles(w_in_c[0][:, main_cols:], w_in_c.shape[2] - main_cols),
        w_out=w_out_c[0].astype(BF16),
    )


def _mixer_layers(x3, wb, g_mix, g_ffn, rpb, conv_w, conv_b, dt_bias, a_log, d_skip, g_gate):
    batch, seq_len, d = x3.shape
    n = batch * seq_len
    x = x3.reshape(n, d)

    cos, sin = rope_tables(seq_len)
    qkv = qkv_proj(x, g_mix[0], wb["w_qkv"], cos, sin, seq_len=seq_len)
    bias_tiles = natten_bias_tiles(rpb[0], seq_len // GRID_W)
    o_a = natten(qkv, bias_tiles, batch=batch, seq_len=seq_len)
    o_b = dilated_attention(qkv, batch=batch, seq_len=seq_len)
    x = attn_out(x, o_a, o_b, wb["w_o"])
    x = ffn(x, g_ffn[0], wb["w_ff_gate"], wb["w_ff_up"], wb["w_ff_down"])

    d_inner = wb["w_out"].shape[0]
    gw = d_inner // SSM_GROUPS
    hpg = gw // SSM_HEAD_DIM
    heads = SSM_GROUPS * hpg
    zx = norm_matmul(x, g_mix[1], wb["w_in_main"], out_width=gw, out_dtype=BF16)
    dt_raw = norm_matmul(x, g_mix[1], wb["w_in_dt"], out_width=2 * heads, out_dtype=F32)[0]
    z_tiles = d_inner // gw
    xs = conv_silu(zx, conv_w[0], conv_b[0], width=gw, first_tile=z_tiles, n_tiles=SSM_GROUPS,
                   col_offset=0, batch=batch, seq_len=seq_len, tr=2048)
    bc_tiles = SSM_GROUPS * SSM_STATE // gw
    b_nat, b_t = conv_silu(zx, conv_w[0], conv_b[0], width=SSM_STATE, first_tile=2 * z_tiles, n_tiles=SSM_GROUPS,
                           col_offset=d_inner, batch=batch, seq_len=seq_len, tr=4096, transposed=True)
    c_nat = conv_silu(zx, conv_w[0], conv_b[0], width=SSM_STATE, first_tile=2 * z_tiles + bc_tiles,
                      n_tiles=SSM_GROUPS, col_offset=d_inner + SSM_GROUPS * SSM_STATE,
                      batch=batch, seq_len=seq_len, tr=4096)
    L = SSM_CHUNK
    rows = SSM_GROUPS * 2 * hpg
    dt_rows = dt_raw.reshape(n // L, L, 2, SSM_GROUPS, hpg).transpose(0, 3, 2, 4, 1).reshape(n // L, rows, L)

    def per_row(p):
        return p.reshape(2, SSM_GROUPS, hpg).transpose(1, 0, 2).reshape(rows, 1)

    d_skip_rows = jnp.repeat(d_skip[0].reshape(SSM_GROUPS, 1, hpg), SSM_HEAD_DIM, axis=2)
    seg_rows, dt_act_rows = ssd_decay_rows(dt_rows, per_row(dt_bias[0]), per_row(a_log[0]), hpg=hpg)
    y_f, y_b = ssd(xs, b_nat, b_t, c_nat, seg_rows, dt_act_rows, d_skip_rows, batch=batch, seq_len=seq_len)
    return mamba_out(y_f, y_b, zx, g_gate[0], wb["w_out"], x)


def kernel(x_prompt, x_sample, g_mix, g_ffn, w_qkv, rpb, w_o, w_ff_gate, w_ff_up, w_ff_down, w_in_c, conv_w, conv_b,
           dt_bias, a_log, d_skip, g_gate, w_out_c, w_router, w_e_gate, w_e_up, w_e_down, g_final):
    wb = _prepare_weights(w_qkv, w_o, w_ff_gate, w_ff_up, w_ff_down, w_in_c, w_out_c)
    args = (wb, g_mix, g_ffn, rpb, conv_w, conv_b, dt_bias, a_log, d_skip, g_gate)
    x_p = _mixer_layers(x_prompt, *args)
    x_s = _mixer_layers(x_sample, *args)

    n_p = x_p.shape[0]
    routed, xn = router(x_p, x_s, g_ffn[1], w_router[0])
    slot, tok_of_slot, blk_expert, n_used = moe_plan(routed, tm=MOE_TM)
    ys = expert_ffn(blk_expert, n_used, tok_of_slot, xn, w_e_gate[0].astype(BF16), w_e_up[0].astype(BF16),
                    w_e_down[0].astype(BF16))
    out_p = combine_norm(slot[:TOP_K * n_p], ys, x_p, routed[:n_p], g_final)
    out_s = combine_norm(slot[TOP_K * n_p:], ys, x_s, routed[n_p:], g_final)
    return out_p.reshape(x_prompt.shape), out_s.reshape(x_sample.shape)
```

```python
import functools
import math

import jax
import jax.numpy as jnp
import numpy as np
from jax import lax
from jax.experimental import pallas as pl
from jax.experimental.pallas import tpu as pltpu

F32 = jnp.float32
BF16 = jnp.bfloat16
EPS = 1e-6
NEG_INF = float("-inf")

GRID_W = 64
HEAD_DIM = 128
N_HEADS_A = 4
N_HEADS_B_GROUP = 4
DILATIONS = (1, 4, 16)
BAND_RADIUS = 64
N_HEADS_QKV = N_HEADS_A + N_HEADS_B_GROUP * len(DILATIONS)
WIN_H = 8
WIN_W = 16
ROPE_THETA = 10000.0
SSM_HEAD_DIM = 64
SSM_GROUPS = 8
SSM_STATE = 128
SSM_CONV = 5
SSM_CHUNK = 128
N_EXPERTS = 8
TOP_K = 2

V7X_VMEM_BYTES = 64 * 1024 * 1024
LANES = 128
BF16_SUBLANES = 16

NAT_ROWS = 8
NAT_KROWS = NAT_ROWS + WIN_H - 1
DIL_TQ = 512
DIL_KC = (256, 256, 512)
DIL_PARTS = 4
MOE_TM = 512
GATHER_ROWS = 256
FF_TILE = 512


def _cparams(semantics, vmem_estimate):
    limit = int(min(max(2 * vmem_estimate, 32 * 1024 * 1024), V7X_VMEM_BYTES - 8 * 1024 * 1024))
    return pltpu.CompilerParams(dimension_semantics=semantics, vmem_limit_bytes=limit)


def _rmsnorm(x, g):
    return x * lax.rsqrt(jnp.mean(x * x, axis=-1, keepdims=True) + EPS) * g


def _sigmoid(x):
    return 1.0 / (1.0 + jnp.exp(-x))


def _softplus(x):
    return jnp.maximum(x, 0.0) + jnp.log(1.0 + jnp.exp(-jnp.abs(x)))


def _dot(a, b):
    return jnp.dot(a, b, preferred_element_type=F32)


def _dot_nt(a, b):
    return lax.dot_general(a, b, (((1,), (1,)), ((), ())), preferred_element_type=F32)


def _norm_matmul_kernel(x_ref, g_ref, w_ref, o_ref, xn_ref):
    @pl.when(pl.program_id(1) == 0)
    def _():
        xn_ref[...] = _rmsnorm(x_ref[...], g_ref[...]).astype(BF16)

    r = _dot(xn_ref[...], w_ref[...])
    width = o_ref.shape[-1]
    for t in range(o_ref.shape[0]):
        o_ref[t] = r[:, t * width:(t + 1) * width].astype(o_ref.dtype)


def column_tiles(w, tn):
    *lead, k, m = w.shape
    w = w.astype(BF16).reshape(*lead, k, m // tn, tn)
    return jnp.swapaxes(w, -3, -2)


def norm_matmul(x, g, w_tiles, *, out_width, out_dtype, tm=1024):
    n, k = x.shape
    n_tiles, _, tn = w_tiles.shape
    m = n_tiles * tn
    per = tn // out_width
    tm = min(tm, n)
    est = 2 * tm * k * 4 + tm * k * 2 + 2 * k * tn * 2 + 3 * tm * tn * 4
    return pl.pallas_call(
        _norm_matmul_kernel,
        grid=(n // tm, m // tn),
        in_specs=[
            pl.BlockSpec((tm, k), lambda i, j: (i, 0)),
            pl.BlockSpec((1, k), lambda i, j: (0, 0)),
            pl.BlockSpec((None, k, tn), lambda i, j: (j, 0, 0)),
        ],
        out_specs=pl.BlockSpec((per, tm, out_width), lambda i, j: (j, i, 0)),
        out_shape=jax.ShapeDtypeStruct((m // out_width, n, out_width), out_dtype),
        scratch_shapes=[pltpu.VMEM((tm, k), BF16)],
        compiler_params=_cparams(("parallel", "arbitrary"), est),
        name="norm_matmul",
    )(x, g.reshape(1, k), w_tiles)


def _qkv_kernel(x_ref, g_ref, w_ref, cos_ref, sin_ref, o_ref, xn_ref, *, heads_per_tile, scale):
    j = pl.program_id(1)
    tiles_per_part = N_HEADS_QKV // heads_per_tile

    @pl.when(j == 0)
    def _():
        xn_ref[...] = _rmsnorm(x_ref[...], g_ref[...]).astype(BF16)

    use_rope = jnp.logical_and(j < 2 * tiles_per_part, j % tiles_per_part != 0)
    mult = jnp.where(j < tiles_per_part, scale, 1.0)
    c = jnp.where(use_rope, cos_ref[...], 1.0) * mult
    s = jnp.where(use_rope, sin_ref[...], 0.0) * mult
    xn = xn_ref[...]
    for pair in range(heads_per_tile // 2):
        r = _dot(xn, w_ref[:, pair * 2 * HEAD_DIM:(pair + 1) * 2 * HEAD_DIM])
        for k in range(2):
            p = r[:, k * HEAD_DIM:(k + 1) * HEAD_DIM]
            o_ref[2 * pair + k] = (p * c + pltpu.roll(p, HEAD_DIM // 2, 1) * s).astype(o_ref.dtype)


def qkv_proj(x, g, w_tiles, cos, sin, *, seq_len, tm=1024):
    n, k = x.shape
    hpt = N_HEADS_A
    n_tiles, _, tn = w_tiles.shape
    assert tn == hpt * HEAD_DIM
    m = n_tiles * tn
    tm = min(tm, seq_len)
    tiles_per_seq = seq_len // tm
    est = 2 * tm * k * 4 + tm * k * 2 + 2 * k * tn * 2 + 2 * tm * tn * 2 + 4 * tm * HEAD_DIM * 4 + tm * tn * 4
    return pl.pallas_call(
        functools.partial(_qkv_kernel, heads_per_tile=hpt, scale=HEAD_DIM ** -0.5),
        grid=(n // tm, m // tn),
        in_specs=[
            pl.BlockSpec((tm, k), lambda i, j: (i, 0)),
            pl.BlockSpec((1, k), lambda i, j: (0, 0)),
            pl.BlockSpec((None, k, tn), lambda i, j: (j, 0, 0)),
            pl.BlockSpec((tm, HEAD_DIM), lambda i, j: (i % tiles_per_seq, 0)),
            pl.BlockSpec((tm, HEAD_DIM), lambda i, j: (i % tiles_per_seq, 0)),
        ],
        out_specs=pl.BlockSpec((hpt, tm, HEAD_DIM), lambda i, j: (j, i, 0)),
        out_shape=jax.ShapeDtypeStruct((m // HEAD_DIM, n, HEAD_DIM), BF16),
        scratch_shapes=[pltpu.VMEM((tm, k), BF16)],
        compiler_params=_cparams(("parallel", "arbitrary"), est),
        name="qkv_proj",
    )(x, g.reshape(1, k), w_tiles, cos, sin)


def rope_tables(seq_len):
    half = HEAD_DIM // 2
    inv = ROPE_THETA ** (-jnp.arange(half, dtype=F32) / half)
    ang = jnp.arange(seq_len, dtype=F32)[:, None] * inv[None, :]
    cos = jnp.cos(ang)
    sin = jnp.sin(ang)
    return jnp.concatenate([cos, cos], axis=1), jnp.concatenate([-sin, sin], axis=1)


def _natten_kernel(q_ref, k_ref, v_ref, bias_ref, o_ref, *, n_blocks, grid_rows):
    blk = pl.program_id(2)
    first_row = jnp.where(blk == 0, 0,
                          jnp.where(blk == n_blocks - 1, grid_rows - NAT_KROWS, blk * NAT_ROWS - WIN_H // 2))
    start = pl.multiple_of(first_row * GRID_W, GRID_W)
    kw = k_ref[pl.ds(start, NAT_KROWS * GRID_W), :]
    vw = v_ref[pl.ds(start, NAT_KROWS * GRID_W), :]
    s = _dot_nt(q_ref[...], kw) + bias_ref[...]
    m = jnp.max(s, axis=-1, keepdims=True)
    p = jnp.exp(s - m)
    l = jnp.sum(p, axis=-1, keepdims=True)
    o_ref[...] = (_dot(p.astype(BF16), vw) / l).astype(o_ref.dtype)


def natten_bias_tiles(rpb, grid_rows):
    n_blocks = grid_rows // NAT_ROWS
    r0s = np.array([0, NAT_ROWS, (n_blocks - 1) * NAT_ROWS])
    k0s = np.array([0, NAT_ROWS - WIN_H // 2, grid_rows - NAT_KROWS])
    r = r0s[:, None] + np.arange(NAT_ROWS)[None, :]
    kr = k0s[:, None] + np.arange(NAT_KROWS)[None, :]
    rs = np.clip(r - WIN_H // 2, 0, grid_rows - WIN_H)
    row_ok = (kr[:, None, :] >= rs[:, :, None]) & (kr[:, None, :] < rs[:, :, None] + WIN_H)
    d_row = np.clip(kr[:, None, :] - r[:, :, None] + (WIN_H - 1), 0, 2 * WIN_H - 2)
    c = np.arange(GRID_W)
    cs = np.clip(c - WIN_W // 2, 0, GRID_W - WIN_W)
    col_ok = (c[None, :] >= cs[:, None]) & (c[None, :] < cs[:, None] + WIN_W)
    d_col = np.clip(c[None, :] - c[:, None] + (WIN_W - 1), 0, 2 * WIN_W - 2)
    sel_row = (d_row[..., None] == np.arange(2 * WIN_H - 1)).astype(np.float32)
    sel_col = (d_col[..., None] == np.arange(2 * WIN_W - 1)).astype(np.float32)
    rows = jnp.einsum("tikr,hrc->thikc", sel_row, rpb.astype(F32), precision=lax.Precision.HIGHEST)
    bias = jnp.einsum("thikc,qwc->thiqkw", rows, sel_col, precision=lax.Precision.HIGHEST)
    ok = row_ok[:, :, None, :, None] & col_ok[None, None, :, None, :]
    bias = jnp.where(ok[:, None], bias, NEG_INF)
    return bias.reshape(3, rpb.shape[0], NAT_ROWS * GRID_W, NAT_KROWS * GRID_W)


def natten(qkv, bias_tiles, *, batch, seq_len):
    n = batch * seq_len
    grid_rows = seq_len // GRID_W
    n_blocks = grid_rows // NAT_ROWS
    tq = NAT_ROWS * GRID_W
    tk = NAT_KROWS * GRID_W

    def tile_kind(blk):
        return jnp.where(blk == 0, 0, jnp.where(blk == n_blocks - 1, 2, 1))

    est = 4 * seq_len * HEAD_DIM * 2 + 2 * tq * tk * 4 + 3 * tq * tk * 4
    return pl.pallas_call(
        functools.partial(_natten_kernel, n_blocks=n_blocks, grid_rows=grid_rows),
        grid=(batch, N_HEADS_A, n_blocks),
        in_specs=[
            pl.BlockSpec((None, tq, HEAD_DIM), lambda b, h, i: (h, b * n_blocks + i, 0)),
            pl.BlockSpec((None, seq_len, HEAD_DIM), lambda b, h, i: (N_HEADS_QKV + h, b, 0)),
            pl.BlockSpec((None, seq_len, HEAD_DIM), lambda b, h, i: (2 * N_HEADS_QKV + h, b, 0)),
            pl.BlockSpec((None, None, tq, tk), lambda b, h, i: (tile_kind(i), h, 0, 0)),
        ],
        out_specs=pl.BlockSpec((tq, HEAD_DIM), lambda b, h, i: (b * n_blocks + i, h)),
        out_shape=jax.ShapeDtypeStruct((n, N_HEADS_A * HEAD_DIM), BF16),
        compiler_params=_cparams(("parallel", "parallel", "arbitrary"), est),
        name="natten",
    )(qkv, qkv, qkv, bias_tiles)


class _DilatedGeometry:
    def __init__(self, dil, kc):
        self.dil, self.kc = dil, kc
        self.reach = BAND_RADIUS * dil
        self.tp = DIL_TQ // DIL_PARTS
        assert kc % self.tp == 0
        self.halo = -(-self.reach // kc) * kc
        self.n_chunks = (DIL_TQ + 2 * self.halo) // kc
        self.n_tiles = (kc // self.tp) * (self.n_chunks - 1) + DIL_PARTS

    def tile_index(self, chunk, part):
        return (self.kc // self.tp) * chunk + (DIL_PARTS - 1 - part)

    def _allowed(self, u):
        rel = np.arange(self.kc)[None, :] - np.arange(self.tp)[:, None] + (u - (DIL_PARTS - 1)) * self.tp - self.halo
        return (np.abs(rel) <= self.reach) & (rel % self.dil == 0)

    def is_active(self, chunk, part):
        return bool(self._allowed(self.tile_index(chunk, part)).any())

    def bias_tiles(self):
        tiles = [np.where(self._allowed(u), 0.0, -np.inf) for u in range(self.n_tiles)]
        tiles.append(np.full((self.tp, self.kc), -np.inf))
        return np.stack(tiles).astype(np.float32)


DIL_GEOMETRY = tuple(_DilatedGeometry(d, kc) for d, kc in zip(DILATIONS, DIL_KC))


def _dilated_kernel(b0, b1, b2, q0, q1, q2, k0, k1, k2, v0, v1, v2, o_ref, *, seq_len):
    tq = DIL_TQ
    parts = DIL_PARTS
    tp = tq // parts
    t0 = pl.program_id(2) * tq
    carry = [(jnp.full((tp, 1), -1e30, F32), jnp.zeros((tp, 1), F32), jnp.zeros((tp, HEAD_DIM), F32))
             for _ in range(parts)]
    for geo, bias_ref, q_ref, k_ref, v_ref in zip(DIL_GEOMETRY, (b0, b1, b2), (q0, q1, q2), (k0, k1, k2), (v0, v1, v2)):
        kc = geo.kc
        qs = tuple(q_ref[p * tp:(p + 1) * tp, :] for p in range(parts))
        for ci in range(geo.n_chunks):
            active = [p for p in range(parts) if geo.is_active(ci, p)]
            start = t0 - geo.halo + ci * kc
            in_range = jnp.logical_and(start >= 0, start + kc <= seq_len)
            ks = pl.multiple_of(jnp.clip(start, 0, seq_len - kc), kc)
            kk = k_ref[pl.ds(ks, kc), :]
            vv = v_ref[pl.ds(ks, kc), :]
            scores = {p: _dot_nt(qs[p], kk) + bias_ref[jnp.where(in_range, geo.tile_index(ci, p), geo.n_tiles)]
                      for p in active}
            stats = {}
            for p in active:
                m, l, _ = carry[p]
                m_new = jnp.maximum(m, jnp.max(scores[p], axis=-1, keepdims=True))
                alpha = jnp.exp(m - m_new)
                e = jnp.exp(scores[p] - m_new)
                stats[p] = (m_new, alpha, alpha * l + jnp.sum(e, axis=-1, keepdims=True), e.astype(BF16))
            for p in active:
                m_new, alpha, l, e = stats[p]
                carry[p] = (m_new, l, alpha * carry[p][2] + _dot(e, vv))
    for p in range(parts):
        _, l, acc = carry[p]
        o_ref[p * tp:(p + 1) * tp, :] = (acc / l).astype(o_ref.dtype)


def dilated_attention(qkv, *, batch, seq_len):
    n = batch * seq_len
    nq = seq_len // DIL_TQ
    biases = [geo.bias_tiles() for geo in DIL_GEOMETRY]

    def q_spec(g):
        return pl.BlockSpec((None, DIL_TQ, HEAD_DIM),
                            lambda b, j, i: (N_HEADS_A + N_HEADS_B_GROUP * g + j, b * nq + i, 0))

    def kv_spec(part, g):
        return pl.BlockSpec((None, seq_len, HEAD_DIM),
                            lambda b, j, i: (part * N_HEADS_QKV + N_HEADS_A + N_HEADS_B_GROUP * g + j, b, 0))

    est = 12 * seq_len * HEAD_DIM * 2 + 8 * DIL_TQ * max(DIL_KC) * 4 + 2 * sum(b.size for b in biases) * 4
    groups = range(len(DILATIONS))
    return pl.pallas_call(
        functools.partial(_dilated_kernel, seq_len=seq_len),
        grid=(batch, N_HEADS_B_GROUP, nq),
        in_specs=[pl.BlockSpec(b.shape, lambda b_, j, i: (0, 0, 0)) for b in biases]
        + [q_spec(g) for g in groups] + [kv_spec(1, g) for g in groups] + [kv_spec(2, g) for g in groups],
        out_specs=pl.BlockSpec((DIL_TQ, HEAD_DIM), lambda b, j, i: (b * nq + i, j)),
        out_shape=jax.ShapeDtypeStruct((n, N_HEADS_B_GROUP * HEAD_DIM), BF16),
        compiler_params=_cparams(("parallel", "parallel", "arbitrary"), est),
        name="dilated_attention",
    )(*biases, *([qkv] * 9))


def _attn_out_kernel(x_ref, oa_ref, ob_ref, w_ref, o_ref):
    ka = oa_ref.shape[1]
    o_ref[...] = x_ref[...] + _dot(oa_ref[...], w_ref[:ka, :]) + _dot(ob_ref[...], w_ref[ka:, :])


def attn_out(x, o_a, o_b, w, *, tm=512):
    n, d = x.shape
    ka, kb = o_a.shape[1], o_b.shape[1]
    tm = min(tm, n)
    est = 4 * tm * d * 4 + 2 * (ka + kb) * d * 2 + 2 * tm * (ka + kb) * 2
    return pl.pallas_call(
        _attn_out_kernel,
        grid=(n // tm,),
        in_specs=[
            pl.BlockSpec((tm, d), lambda i: (i, 0)),
            pl.BlockSpec((tm, ka), lambda i: (i, 0)),
            pl.BlockSpec((tm, kb), lambda i: (i, 0)),
            pl.BlockSpec((ka + kb, d), lambda i: (0, 0)),
        ],
        out_specs=pl.BlockSpec((tm, d), lambda i: (i, 0)),
        out_shape=jax.ShapeDtypeStruct((n, d), F32),
        compiler_params=_cparams(("parallel",), est),
        name="attn_out",
    )(x, o_a, o_b, w)


def _ffn_kernel(x_ref, g_ref, wg_ref, wu_ref, wd_ref, o_ref, xn_ref):
    f = pl.program_id(1)

    @pl.when(f == 0)
    def _():
        xn_ref[...] = _rmsnorm(x_ref[...], g_ref[...]).astype(BF16)
        o_ref[...] = x_ref[...]

    xn = xn_ref[...]
    gate = _dot(xn, wg_ref[...])
    up = _dot(xn, wu_ref[...])
    hid = (gate * _sigmoid(gate) * up).astype(BF16)
    o_ref[...] += _dot(hid, wd_ref[...])


def ffn(x, g, wg_tiles, wu_tiles, wd, *, tm=512):
    n, d = x.shape
    nf, _, tf = wg_tiles.shape
    tm = min(tm, n)
    est = 4 * tm * d * 4 + tm * d * 2 + 6 * d * tf * 2 + 3 * tm * tf * 4
    return pl.pallas_call(
        _ffn_kernel,
        grid=(n // tm, nf),
        in_specs=[
            pl.BlockSpec((tm, d), lambda i, f: (i, 0)),
            pl.BlockSpec((1, d), lambda i, f: (0, 0)),
            pl.BlockSpec((None, d, tf), lambda i, f: (f, 0, 0)),
            pl.BlockSpec((None, d, tf), lambda i, f: (f, 0, 0)),
            pl.BlockSpec((tf, d), lambda i, f: (f, 0)),
        ],
        out_specs=pl.BlockSpec((tm, d), lambda i, f: (i, 0)),
        out_shape=jax.ShapeDtypeStruct((n, d), F32),
        scratch_shapes=[pltpu.VMEM((tm, d), BF16)],
        compiler_params=_cparams(("parallel", "arbitrary"), est),
        name="ffn",
    )(x, g.reshape(1, d), wg_tiles, wu_tiles, wd)


def _conv_kernel(xm_ref, xp_ref, xn_ref, w_ref, b_ref, o_ref, *rest, tr, n_row_blocks):
    ext_ref = rest[-1]
    i = pl.program_id(2)
    hb = BF16_SUBLANES
    pad = SSM_CONV // 2
    ext_ref[0:hb, :] = jnp.where(i > 0, xp_ref[...].astype(F32), 0.0)
    ext_ref[hb:hb + tr, :] = xm_ref[...].astype(F32)
    ext_ref[hb + tr:2 * hb + tr, :] = jnp.where(i < n_row_blocks - 1, xn_ref[...].astype(F32), 0.0)
    acc = jnp.broadcast_to(b_ref[...], o_ref.shape)
    for k in range(SSM_CONV):
        acc = acc + ext_ref[pl.ds(hb - pad + k, tr), :] * w_ref[k:k + 1, :]
    out = acc * _sigmoid(acc)
    o_ref[...] = out.astype(o_ref.dtype)
    if len(rest) == 2:
        ot_ref = rest[0]
        L = ot_ref.shape[-1]
        for c in range(tr // L):
            ot_ref[c] = out[c * L:(c + 1) * L, :].T.astype(ot_ref.dtype)


def conv_silu(zx, conv_w, conv_b, *, width, first_tile, n_tiles, col_offset, batch, seq_len, tr, transposed=False):
    n = batch * seq_len
    tile_w = zx.shape[2]
    per = tile_w // width
    tr = min(tr, seq_len)
    nr = seq_len // tr
    hb = BF16_SUBLANES
    seq_hb = seq_len // hb
    n_hb = n // hb

    def main_map(c, b, i):
        return (first_tile + c // per, b * nr + i, c % per)

    def prev_map(c, b, i):
        return (first_tile + c // per, jnp.maximum(b * seq_hb + i * (tr // hb) - 1, 0), c % per)

    def next_map(c, b, i):
        return (first_tile + c // per, jnp.minimum(b * seq_hb + (i + 1) * (tr // hb), n_hb - 1), c % per)

    L = SSM_CHUNK
    out_specs = [pl.BlockSpec((None, tr, width), lambda c, b, i: (c, b * nr + i, 0))]
    out_shape = [jax.ShapeDtypeStruct((n_tiles, n, width), BF16)]
    if transposed:
        out_specs.append(pl.BlockSpec((None, tr // L, width, L), lambda c, b, i: (c, b * nr + i, 0, 0)))
        out_shape.append(jax.ShapeDtypeStruct((n_tiles, n // L, width, L), BF16))
    est = 4 * tr * width * 2 + (tr + 2 * hb) * width * 4 + 4 * tr * width * 4 + 4 * tr * width * 2
    outs = pl.pallas_call(
        functools.partial(_conv_kernel, tr=tr, n_row_blocks=nr),
        grid=(n_tiles, batch, nr),
        in_specs=[
            pl.BlockSpec((None, tr, width), main_map),
            pl.BlockSpec((None, hb, width), prev_map),
            pl.BlockSpec((None, hb, width), next_map),
            pl.BlockSpec((SSM_CONV, width), lambda c, b, i: (0, col_offset // width + c)),
            pl.BlockSpec((1, width), lambda c, b, i: (0, col_offset // width + c)),
        ],
        out_specs=out_specs,
        out_shape=out_shape,
        scratch_shapes=[pltpu.VMEM((tr + 2 * hb, width), F32)],
        compiler_params=_cparams(("parallel", "parallel", "arbitrary"), est),
        name="conv_silu",
    )(zx, zx, zx, conv_w, conv_b.reshape(1, -1))
    return outs if transposed else outs[0]


def _lane_cumsum(a):
    lane = lax.broadcasted_iota(jnp.int32, a.shape, 1)
    shift = 1
    while shift < a.shape[1]:
        a = a + jnp.where(lane >= shift, pltpu.roll(a, shift, 1), 0.0)
        shift *= 2
    return a


def _ssd_decay_kernel(dt_ref, dtbias_ref, alog_ref, seg_ref, dtact_ref, *, hpg):
    L = SSM_CHUNK
    chunks, rows, _ = dt_ref.shape
    dt_act = _softplus(dt_ref[...] + dtbias_ref[...])
    dt_all = dt_act.reshape(chunks * rows, L)
    a_all = (dt_act * (-jnp.exp(alog_ref[...]))).reshape(chunks * rows, L)
    cum = _lane_cumsum(a_all)
    suf = cum[:, L - 1:L] - cum + a_all
    row = lax.broadcasted_iota(jnp.int32, cum.shape, 0)
    seg_ref[...] = jnp.where((row & (2 * hpg - 1)) < hpg, cum, suf).reshape(chunks, rows, L)
    dtact_ref[...] = dt_all.reshape(chunks, rows, L)


def ssd_decay_rows(dt_rows, dt_bias_rows, a_log_rows, *, hpg, chunks_per_step=8):
    n_chunks, rows, L = dt_rows.shape
    assert hpg & (hpg - 1) == 0
    cps = math.gcd(chunks_per_step, n_chunks)
    spec = pl.BlockSpec((cps, rows, L), lambda i: (i, 0, 0))
    const = pl.BlockSpec((rows, 1), lambda i: (0, 0))
    shape = jax.ShapeDtypeStruct(dt_rows.shape, F32)
    return pl.pallas_call(
        functools.partial(_ssd_decay_kernel, hpg=hpg),
        grid=(n_chunks // cps,),
        in_specs=[spec, const, const],
        out_specs=[spec, spec],
        out_shape=[shape, shape],
        compiler_params=_cparams(("parallel",), 16 * cps * rows * L * 4),
        name="ssd_decay_rows",
    )(dt_rows, dt_bias_rows, a_log_rows)


def _ssd_kernel(xa_ref, ba_ref, bta_ref, ca_ref, sega_ref, dta_ref, xb_ref, btb_ref, cb_ref, segb_ref, dtb_ref,
                dskip_ref, yf_ref, yb_ref, sf_ref, sb_ref, *, hpg):
    L = SSM_CHUNK
    P = SSM_HEAD_DIM
    pairs = hpg // 2

    @pl.when(pl.program_id(1) == 0)
    def _():
        sf_ref[...] = jnp.zeros_like(sf_ref)
        sb_ref[...] = jnp.zeros_like(sb_ref)

    li = lax.broadcasted_iota(jnp.int32, (L, L), 0)
    si = lax.broadcasted_iota(jnp.int32, (L, L), 1)
    causal = li >= si
    anti = li <= si
    low_lanes = lax.broadcasted_iota(jnp.int32, (L, 2 * P), 1) < P
    high_lanes = jnp.logical_not(low_lanes)
    low_lanes_row = lax.broadcasted_iota(jnp.int32, (1, 2 * P), 1) < P

    def lanes_of(col):
        return jnp.broadcast_to(col, (L, 2 * P))

    def columns(seg_r, dt_r):
        stacked = jnp.concatenate([seg_r, dt_r, jnp.zeros((L - 4 * hpg, L), F32)], axis=0)
        return stacked.T

    spread_bwd = (lax.broadcasted_iota(jnp.int32, (L, hpg * P), 0)
                  == hpg + lax.broadcasted_iota(jnp.int32, (L, hpg * P), 1) // P).astype(BF16)

    def advance_state(s_ref, g, lanes, state, btf, xm0, xm1, seg_r, dt_r, h0, h1, tot_lane):
        acc = None
        decays = []
        for h, xm in ((h0, xm0), (h1, xm1)):
            tot = seg_r[h:h + 1, tot_lane:tot_lane + 1]
            coef = jnp.exp(tot - seg_r[h:h + 1, :]) * dt_r[h:h + 1, :]
            part = _dot((btf * coef).astype(BF16), xm)
            acc = part if acc is None else acc + part
            decays.append(jnp.exp(tot))
        decay = jnp.where(low_lanes_row, decays[0], decays[1])
        s_ref[g, :, lanes] = state[:, lanes] * decay + acc

    def group_body(g, c):
        r0 = pl.multiple_of(g * 2 * hpg, 2 * hpg)

        seg_r = sega_ref[pl.ds(r0, 2 * hpg), :]
        dt_r = dta_ref[pl.ds(r0, 2 * hpg), :]
        cols = columns(seg_r, dt_r)
        xg = xa_ref[g]
        cg = ca_ref[g]
        state = sf_ref[g]
        carried = _dot(cg, state.astype(BF16))
        cbm = _dot_nt(cg, ba_ref[g])
        btf = bta_ref[g].astype(F32)
        for p in range(pairs):
            lanes = slice(p * 2 * P, (p + 1) * 2 * P)
            xpf = xg[:, lanes].astype(F32)
            xm0 = jnp.where(low_lanes, xpf, 0.0).astype(BF16)
            xm1 = jnp.where(high_lanes, xpf, 0.0).astype(BF16)
            h0, h1 = 2 * p, 2 * p + 1
            f0, f1 = lanes_of(cols[:, h0:h0 + 1]), lanes_of(cols[:, h1:h1 + 1])
            y = carried[:, lanes] * jnp.where(low_lanes, jnp.exp(f0), jnp.exp(f1)) + dskip_ref[g][:, lanes] * xpf
            for hh, f_cols, xm in ((h0, f0, xm0), (h1, f1, xm1)):
                wf = jnp.exp(jnp.where(causal, f_cols - seg_r[hh:hh + 1, :], NEG_INF)) * dt_r[hh:hh + 1, :]
                b_cols = lanes_of(cols[:, hpg + hh:hpg + hh + 1])
                wb = jnp.exp(jnp.where(anti, b_cols - seg_r[hpg + hh:hpg + hh + 1, :], NEG_INF)) \
                    * dt_r[hpg + hh:hpg + hh + 1, :]
                y = y + _dot((cbm * (wf + wb)).astype(BF16), xm)
            yf_ref[g, :, lanes] = y.astype(yf_ref.dtype)
            advance_state(sf_ref, g, lanes, state, btf, xm0, xm1, seg_r, dt_r, h0, h1, L - 1)

        seg_r = segb_ref[pl.ds(r0, 2 * hpg), :]
        dt_r = dtb_ref[pl.ds(r0, 2 * hpg), :]
        xg = xb_ref[g]
        state = sb_ref[g]
        grow = _dot(jnp.exp(columns(seg_r, dt_r)).astype(BF16), spread_bwd)
        y = _dot(cb_ref[g], state.astype(BF16)) * grow
        yb_ref[g] = y.astype(yb_ref.dtype)
        btf = btb_ref[g].astype(F32)
        for p in range(pairs):
            lanes = slice(p * 2 * P, (p + 1) * 2 * P)
            xpf = xg[:, lanes].astype(F32)
            xm0 = jnp.where(low_lanes, xpf, 0.0).astype(BF16)
            xm1 = jnp.where(high_lanes, xpf, 0.0).astype(BF16)
            advance_state(sb_ref, g, lanes, state, btf, xm0, xm1, seg_r, dt_r, hpg + 2 * p, hpg + 2 * p + 1, 0)
        return c

    lax.fori_loop(0, SSM_GROUPS, group_body, 0, unroll=2)


def ssd(xs, b_nat, b_t, c_nat, seg_rows, dt_act_rows, d_skip_rows, *, batch, seq_len):
    groups, n, gw = xs.shape
    hpg = gw // SSM_HEAD_DIM
    assert hpg & (hpg - 1) == 0
    L = SSM_CHUNK
    nc = seq_len // L
    rows = groups * 2 * hpg

    def fwd(b, k):
        return b * nc + k

    def bwd(b, k):
        return b * nc + nc - 1 - k

    def specs(chunk, with_b_nat):
        out = [pl.BlockSpec((groups, L, gw), lambda b, k: (0, chunk(b, k), 0))]
        if with_b_nat:
            out.append(pl.BlockSpec((groups, L, SSM_STATE), lambda b, k: (0, chunk(b, k), 0)))
        out += [
            pl.BlockSpec((groups, None, SSM_STATE, L), lambda b, k: (0, chunk(b, k), 0, 0)),
            pl.BlockSpec((groups, L, SSM_STATE), lambda b, k: (0, chunk(b, k), 0)),
            pl.BlockSpec((None, rows, L), lambda b, k: (chunk(b, k), 0, 0)),
            pl.BlockSpec((None, rows, L), lambda b, k: (chunk(b, k), 0, 0)),
        ]
        return out

    const = [pl.BlockSpec((groups, 1, gw), lambda b, k: (0, 0, 0))]
    y_shape = jax.ShapeDtypeStruct((groups, n, gw), BF16)
    est = 4 * (groups * L * gw * 2 + 3 * groups * L * SSM_STATE * 2 + 2 * rows * L * 4) + 4 * groups * L * gw * 2 \
        + 2 * groups * SSM_STATE * gw * 4 + 64 * L * L * 4
    return pl.pallas_call(
        functools.partial(_ssd_kernel, hpg=hpg),
        grid=(batch, nc),
        in_specs=specs(fwd, True) + specs(bwd, False) + const,
        out_specs=[pl.BlockSpec((groups, L, gw), lambda b, k: (0, fwd(b, k), 0)),
                   pl.BlockSpec((groups, L, gw), lambda b, k: (0, bwd(b, k), 0))],
        out_shape=[y_shape, y_shape],
        scratch_shapes=[pltpu.VMEM((groups, SSM_STATE, gw), F32), pltpu.VMEM((groups, SSM_STATE, gw), F32)],
        compiler_params=_cparams(("parallel", "arbitrary"), est),
        name="ssd",
    )(xs, b_nat, b_t, c_nat, seg_rows, dt_act_rows, xs, b_t, c_nat, seg_rows, dt_act_rows, d_skip_rows)


def _mamba_out_kernel(yf_ref, yb_ref, z_ref, gg_ref, w_ref, x_ref, o_ref, ssq_ref, *, d_inner):
    step = pl.program_id(1)
    groups_per_step, _, gw = z_ref.shape

    @pl.when(step == 0)
    def _():
        o_ref[...] = jnp.zeros_like(o_ref)
        ssq_ref[...] = jnp.zeros_like(ssq_ref)

    gated = []
    for k in range(groups_per_step):
        z = z_ref[k].astype(F32)
        yz = (yf_ref[k].astype(F32) + yb_ref[k].astype(F32)) * (z * _sigmoid(z))
        ssq_ref[...] += jnp.sum(yz * yz, axis=-1, keepdims=True)
        gated.append((yz * gg_ref[:, k * gw:(k + 1) * gw]).astype(BF16))
    o_ref[...] += _dot(jnp.concatenate(gated, axis=1), w_ref[...])

    @pl.when(step == pl.num_programs(1) - 1)
    def _():
        o_ref[...] = x_ref[...] + o_ref[...] * lax.rsqrt(ssq_ref[...] / d_inner + EPS)


def mamba_out(y_f, y_b, zx, g_gate, w_out, x, *, tm=512, groups_per_step=4):
    groups, n, gw = y_f.shape
    d = x.shape[1]
    tm = min(tm, n)
    gps = groups_per_step
    est = gps * (6 * tm * gw * 2 + 2 * gw * d * 2) + 3 * tm * d * 4 + 2 * tm * gps * gw * 4
    return pl.pallas_call(
        functools.partial(_mamba_out_kernel, d_inner=groups * gw),
        grid=(n // tm, groups // gps),
        in_specs=[
            pl.BlockSpec((gps, tm, gw), lambda i, s: (s, i, 0)),
            pl.BlockSpec((gps, tm, gw), lambda i, s: (s, i, 0)),
            pl.BlockSpec((gps, tm, gw), lambda i, s: (s, i, 0)),
            pl.BlockSpec((1, gps * gw), lambda i, s: (0, s)),
            pl.BlockSpec((gps * gw, d), lambda i, s: (s, 0)),
            pl.BlockSpec((tm, d), lambda i, s: (i, 0)),
        ],
        out_specs=pl.BlockSpec((tm, d), lambda i, s: (i, 0)),
        out_shape=jax.ShapeDtypeStruct((n, d), F32),
        scratch_shapes=[pltpu.VMEM((tm, 1), F32)],
        compiler_params=_cparams(("parallel", "arbitrary"), est),
        name="mamba_out",
    )(y_f, y_b, zx, g_gate.reshape(1, -1), w_out, x)


def _router_kernel(xa_ref, xb_ref, g_ref, w_ref, o_ref, xn_ref, *, tiles_a):
    i = pl.program_id(0)

    @pl.when(i < tiles_a)
    def _():
        xn_ref[...] = _rmsnorm(xa_ref[...], g_ref[...])

    @pl.when(i >= tiles_a)
    def _():
        xn_ref[...] = _rmsnorm(xb_ref[...], g_ref[...])

    xn = xn_ref[...]
    logits = jnp.dot(xn, w_ref[...], preferred_element_type=F32, precision=lax.Precision.HIGHEST)
    lane = lax.broadcasted_iota(jnp.int32, logits.shape, 1)
    logits = jnp.where(lane < N_EXPERTS, logits, NEG_INF)
    v1 = jnp.max(logits, axis=-1, keepdims=True)
    i1 = jnp.min(jnp.where(logits == v1, lane, LANES), axis=-1, keepdims=True)
    rest = jnp.where(lane == i1, NEG_INF, logits)
    v2 = jnp.max(rest, axis=-1, keepdims=True)
    i2 = jnp.min(jnp.where(rest == v2, lane, LANES), axis=-1, keepdims=True)
    e2 = jnp.exp(v2 - v1)
    g1 = 1.0 / (1.0 + e2)
    g2 = e2 / (1.0 + e2)
    out = jnp.where(lane == 0, i1.astype(F32),
                    jnp.where(lane == 1, i2.astype(F32),
                              jnp.where(lane == 2, g1, jnp.where(lane == 3, g2, 0.0))))
    o_ref[...] = out


def router(x_a, x_b, g, w_router, *, tm=512):
    (na, d), nb = x_a.shape, x_b.shape[0]
    tm = min(tm, na, nb)
    tiles_a, tiles_b = na // tm, nb // tm
    w = jnp.zeros((d, LANES), F32).at[:, :w_router.shape[1]].set(w_router)
    est = 4 * tm * d * 4 + 2 * d * LANES * 4 + 2 * tm * LANES * 4 + 3 * tm * d * 4
    return pl.pallas_call(
        functools.partial(_router_kernel, tiles_a=tiles_a),
        grid=(tiles_a + tiles_b,),
        in_specs=[
            pl.BlockSpec((tm, d), lambda i: (jnp.minimum(i, tiles_a - 1), 0)),
            pl.BlockSpec((tm, d), lambda i: (jnp.maximum(i - tiles_a, 0), 0)),
            pl.BlockSpec((1, d), lambda i: (0, 0)),
            pl.BlockSpec((d, LANES), lambda i: (0, 0)),
        ],
        out_specs=[pl.BlockSpec((tm, LANES), lambda i: (i, 0)), pl.BlockSpec((tm, d), lambda i: (i, 0))],
        out_shape=[jax.ShapeDtypeStruct((na + nb, LANES), F32), jax.ShapeDtypeStruct((na + nb, d), F32)],
        compiler_params=_cparams(("parallel",), est),
        name="router",
    )(x_a, x_b, g.reshape(1, d), w)


def _row_copy(src_hbm, dst_vmem, sem, src_row, dst_row):
    return pltpu.make_async_copy(src_hbm.at[pl.ds(src_row, 1), :], dst_vmem.at[pl.ds(dst_row, 1), :], sem)


EXPERT_ISSUE_STEPS = 4


def _expert_kernel(blk_e_ref, n_used_ref, tok_ref, x_hbm, wg_ref, wu_ref, wd_ref, o_ref, rows_ref, x_ref, sems):
    i = pl.program_id(0)
    f = pl.program_id(1)
    tm = x_ref.shape[0]
    per_step = tm // EXPERT_ISSUE_STEPS
    n_used = n_used_ref[0]
    slot = i % 2

    def issue(block, dst_slot, first_row, count):
        def body(r, c):
            _row_copy(x_hbm, rows_ref.at[dst_slot], sems.at[dst_slot], tok_ref[block * tm + first_row + r],
                      first_row + r).start()
            return c
        lax.fori_loop(0, count, body, 0, unroll=8)

    @pl.when(jnp.logical_and(i == 0, f == 0))
    def _():
        issue(0, 0, 0, tm)

    @pl.when(jnp.logical_and(f == 0, i < n_used))
    def _():
        def wait(r, c):
            _row_copy(x_hbm, rows_ref.at[slot], sems.at[slot], 0, r).wait()
            return c
        lax.fori_loop(0, tm, wait, 0, unroll=8)
        x_ref[...] = rows_ref[slot].astype(x_ref.dtype)

    @pl.when(f == 0)
    def _():
        o_ref[...] = jnp.zeros_like(o_ref)

    @pl.when(jnp.logical_and(f < EXPERT_ISSUE_STEPS, i + 1 < n_used))
    def _():
        issue(i + 1, 1 - slot, f * per_step, per_step)

    @pl.when(i < n_used)
    def _():
        x = x_ref[...]
        gate = _dot(x, wg_ref[...])
        up = _dot(x, wu_ref[...])
        hid = (gate * _sigmoid(gate) * up).astype(BF16)
        o_ref[...] += _dot(hid, wd_ref[...])


def expert_ffn(blk_expert, n_used, tok_of_slot, x_rows, wg, wu, wd, *, tm=MOE_TM, tf=1024):
    slots = tok_of_slot.shape[0]
    d = x_rows.shape[1]
    nf = wg.shape[2] // tf
    assert nf >= EXPERT_ISSUE_STEPS and tm % (8 * EXPERT_ISSUE_STEPS) == 0
    n_blocks = slots // tm

    def live(i, n_used):
        return jnp.minimum(i, n_used[0] - 1)

    def f_eff(i, f, n_used):
        return jnp.where(i < n_used[0], f, nf - 1)

    est = 2 * tm * d * 4 + tm * d * 2 + 12 * d * tf + 2 * tm * d * 4 + 3 * tm * tf * 4
    return pl.pallas_call(
        _expert_kernel,
        grid_spec=pltpu.PrefetchScalarGridSpec(
            num_scalar_prefetch=3,
            grid=(n_blocks, nf),
            in_specs=[
                pl.BlockSpec(memory_space=pl.ANY),
                pl.BlockSpec((None, d, tf), lambda i, f, be, nu, tok: (be[live(i, nu)], 0, f_eff(i, f, nu))),
                pl.BlockSpec((None, d, tf), lambda i, f, be, nu, tok: (be[live(i, nu)], 0, f_eff(i, f, nu))),
                pl.BlockSpec((None, tf, d), lambda i, f, be, nu, tok: (be[live(i, nu)], f_eff(i, f, nu), 0)),
            ],
            out_specs=pl.BlockSpec((tm, d), lambda i, f, be, nu, tok: (i, 0)),
            scratch_shapes=[pltpu.VMEM((2, tm, d), F32), pltpu.VMEM((tm, d), BF16), pltpu.SemaphoreType.DMA((2,))],
        ),
        out_shape=jax.ShapeDtypeStruct((slots, d), F32),
        compiler_params=_cparams(("arbitrary", "arbitrary"), est),
        name="expert_ffn",
    )(blk_expert, n_used, tok_of_slot, x_rows, wg, wu, wd)


def _combine_kernel(slot_ref, y_hbm, x_ref, r_ref, g_ref, o_ref, buf_ref, sem):
    rows = x_ref.shape[0]
    base = pl.program_id(0) * rows

    def start(r, c):
        for k in range(TOP_K):
            _row_copy(y_hbm, buf_ref.at[k], sem, slot_ref[TOP_K * (base + r) + k], r).start()
        return c

    def wait(r, c):
        for k in range(TOP_K):
            _row_copy(y_hbm, buf_ref.at[k], sem, 0, r).wait()
        return c

    lax.fori_loop(0, rows, start, 0, unroll=8)
    lax.fori_loop(0, rows, wait, 0, unroll=8)
    gates = r_ref[...]
    out = x_ref[...] + gates[:, 2:3] * buf_ref[0] + gates[:, 3:4] * buf_ref[1]
    o_ref[...] = _rmsnorm(out, g_ref[...])


def combine_norm(slot_of_assignment, ys, x, routed, g_final, *, rows=GATHER_ROWS):
    n, d = x.shape
    rows = min(rows, n)
    est = 2 * rows * d * 4 + 4 * rows * d * 4 + 4 * rows * d * 4
    return pl.pallas_call(
        _combine_kernel,
        grid_spec=pltpu.PrefetchScalarGridSpec(
            num_scalar_prefetch=1,
            grid=(n // rows,),
            in_specs=[
                pl.BlockSpec(memory_space=pl.ANY),
                pl.BlockSpec((rows, d), lambda i, s: (i, 0)),
                pl.BlockSpec((rows, LANES), lambda i, s: (i, 0)),
                pl.BlockSpec((1, d), lambda i, s: (0, 0)),
            ],
            out_specs=pl.BlockSpec((rows, d), lambda i, s: (i, 0)),
            scratch_shapes=[pltpu.VMEM((TOP_K, rows, d), F32), pltpu.SemaphoreType.DMA(())],
        ),
        out_shape=jax.ShapeDtypeStruct((n, d), F32),
        compiler_params=_cparams(("arbitrary",), est),
        name="combine_norm",
    )(slot_of_assignment, ys, x, routed, g_final.reshape(1, d))


def moe_plan(routed, *, tm):
    n = routed.shape[0]
    experts = routed[:, :TOP_K].astype(jnp.int32).reshape(-1)
    onehot = (experts[:, None] == jnp.arange(N_EXPERTS)[None, :]).astype(jnp.int32)
    rank = jnp.sum((jnp.cumsum(onehot, axis=0) - onehot) * onehot, axis=1)
    counts = jnp.sum(onehot, axis=0)
    padded = ((counts + tm - 1) // tm) * tm
    ends = jnp.cumsum(padded)
    starts = ends - padded
    slot = (starts[experts] + rank).astype(jnp.int32)
    n_blocks = (n * TOP_K) // tm + N_EXPERTS
    tok = jnp.repeat(jnp.arange(n, dtype=jnp.int32), TOP_K)
    tok_of_slot = jnp.zeros((n_blocks * tm,), jnp.int32).at[slot].set(tok, unique_indices=True,
                                                                      mode="promise_in_bounds")
    blk_expert = jnp.minimum(
        jnp.searchsorted(ends, jnp.arange(n_blocks, dtype=jnp.int32) * tm, side="right"), N_EXPERTS - 1
    ).astype(jnp.int32)
    n_used = (ends[-1] // tm).astype(jnp.int32).reshape(1)
    return slot, tok_of_slot, blk_expert, n_used


def _prepare_weights(w_qkv, w_o, w_ff_gate, w_ff_up, w_ff_down, w_in_c, w_out_c):
    d_inner = w_out_c.shape[1]
    gw = d_inner // SSM_GROUPS
    main_cols = 2 * d_inner + 2 * SSM_GROUPS * SSM_STATE
    return dict(
        w_qkv=column_tiles(w_qkv[0], N_HEADS_A * HEAD_DIM), w_o=w_o[0].astype(BF16),
        w_ff_gate=column_tiles(w_ff_gate[0], FF_TILE), w_ff_up=column_tiles(w_ff_up[0], FF_TILE),
        w_ff_down=w_ff_down[0].astype(BF16),
        w_in_main=column_tiles(w_in_c[0][:, :main_cols], 2 * gw),
        w_in_dt=column_tiles(w_in_c[0][:, main_cols:], w_in_c.shape[2] - main_cols),
        w_out=w_out_c[0].astype(BF16),
    )


def _mixer_layers(x3, wb, g_mix, g_ffn, rpb, conv_w, conv_b, dt_bias, a_log, d_skip, g_gate):
    batch, seq_len, d = x3.shape
    n = batch * seq_len
    x = x3.reshape(n, d)

    cos, sin = rope_tables(seq_len)
    qkv = qkv_proj(x, g_mix[0], wb["w_qkv"], cos, sin, seq_len=seq_len)
    bias_tiles = natten_bias_tiles(rpb[0], seq_len // GRID_W)
    o_a = natten(qkv, bias_tiles, batch=batch, seq_len=seq_len)
    o_b = dilated_attention(qkv, batch=batch, seq_len=seq_len)
    x = attn_out(x, o_a, o_b, wb["w_o"])
    x = ffn(x, g_ffn[0], wb["w_ff_gate"], wb["w_ff_up"], wb["w_ff_down"])

    d_inner = wb["w_out"].shape[0]
    gw = d_inner // SSM_GROUPS
    hpg = gw // SSM_HEAD_DIM
    heads = SSM_GROUPS * hpg
    zx = norm_matmul(x, g_mix[1], wb["w_in_main"], out_width=gw, out_dtype=BF16)
    dt_raw = norm_matmul(x, g_mix[1], wb["w_in_dt"], out_width=2 * heads, out_dtype=F32)[0]
    z_tiles = d_inner // gw
    xs = conv_silu(zx, conv_w[0], conv_b[0], width=gw, first_tile=z_tiles, n_tiles=SSM_GROUPS,
                   col_offset=0, batch=batch, seq_len=seq_len, tr=2048)
    bc_tiles = SSM_GROUPS * SSM_STATE // gw
    b_nat, b_t = conv_silu(zx, conv_w[0], conv_b[0], width=SSM_STATE, first_tile=2 * z_tiles, n_tiles=SSM_GROUPS,
                           col_offset=d_inner, batch=batch, seq_len=seq_len, tr=4096, transposed=True)
    c_nat = conv_silu(zx, conv_w[0], conv_b[0], width=SSM_STATE, first_tile=2 * z_tiles + bc_tiles,
                      n_tiles=SSM_GROUPS, col_offset=d_inner + SSM_GROUPS * SSM_STATE,
                      batch=batch, seq_len=seq_len, tr=4096)
    L = SSM_CHUNK
    rows = SSM_GROUPS * 2 * hpg
    dt_rows = dt_raw.reshape(n // L, L, 2, SSM_GROUPS, hpg).transpose(0, 3, 2, 4, 1).reshape(n // L, rows, L)

    def per_row(p):
        return p.reshape(2, SSM_GROUPS, hpg).transpose(1, 0, 2).reshape(rows, 1)

    d_skip_rows = jnp.repeat(d_skip[0].reshape(SSM_GROUPS, 1, hpg), SSM_HEAD_DIM, axis=2)
    seg_rows, dt_act_rows = ssd_decay_rows(dt_rows, per_row(dt_bias[0]), per_row(a_log[0]), hpg=hpg)
    y_f, y_b = ssd(xs, b_nat, b_t, c_nat, seg_rows, dt_act_rows, d_skip_rows, batch=batch, seq_len=seq_len)
    return mamba_out(y_f, y_b, zx, g_gate[0], wb["w_out"], x)


def kernel(x_prompt, x_sample, g_mix, g_ffn, w_qkv, rpb, w_o, w_ff_gate, w_ff_up, w_ff_down, w_in_c, conv_w, conv_b,
           dt_bias, a_log, d_skip, g_gate, w_out_c, w_router, w_e_gate, w_e_up, w_e_down, g_final):
    wb = _prepare_weights(w_qkv, w_o, w_ff_gate, w_ff_up, w_ff_down, w_in_c, w_out_c)
    args = (wb, g_mix, g_ffn, rpb, conv_w, conv_b, dt_bias, a_log, d_skip, g_gate)
    x_p = _mixer_layers(x_prompt, *args)
    x_s = _mixer_layers(x_sample, *args)

    n_p = x_p.shape[0]
    routed, xn = router(x_p, x_s, g_ffn[1], w_router[0])
    slot, tok_of_slot, blk_expert, n_used = moe_plan(routed, tm=MOE_TM)
    ys = expert_ffn(blk_expert, n_used, tok_of_slot, xn, w_e_gate[0].astype(BF16), w_e_up[0].astype(BF16),
                    w_e_down[0].astype(BF16))
    out_p = combine_norm(slot[:TOP_K * n_p], ys, x_p, routed[:n_p], g_final)
    out_s = combine_norm(slot[TOP_K * n_p:], ys, x_s, routed[n_p:], g_final)
    return out_p.reshape(x_prompt.shape), out_s.reshape(x_sample.shape)
```

```python
import functools
import math

import jax
import jax.numpy as jnp
import numpy as np
from jax import lax
from jax.experimental import pallas as pl
from jax.experimental.pallas import tpu as pltpu

F32 = jnp.float32
BF16 = jnp.bfloat16
EPS = 1e-6
NEG_INF = float("-inf")

GRID_W = 64
HEAD_DIM = 128
N_HEADS_A = 4
N_HEADS_B_GROUP = 4
DILATIONS = (1, 4, 16)
BAND_RADIUS = 64
N_HEADS_QKV = N_HEADS_A + N_HEADS_B_GROUP * len(DILATIONS)
WIN_H = 8
WIN_W = 16
ROPE_THETA = 10000.0
SSM_HEAD_DIM = 64
SSM_GROUPS = 8
SSM_STATE = 128
SSM_CONV = 5
SSM_CHUNK = 128
N_EXPERTS = 8
TOP_K = 2

V7X_VMEM_BYTES = 64 * 1024 * 1024
LANES = 128
BF16_SUBLANES = 16

NAT_ROWS = 8
NAT_KROWS = NAT_ROWS + WIN_H - 1
DIL_TQ = 512
DIL_KC = (256, 256, 512)
DIL_PARTS = 4
MOE_TM = 512
GATHER_ROWS = 256
FF_TILE = 512
QKV_TILE = 8 * HEAD_DIM


def _cparams(semantics, vmem_estimate):
    limit = int(min(max(2 * vmem_estimate, 32 * 1024 * 1024), V7X_VMEM_BYTES - 8 * 1024 * 1024))
    return pltpu.CompilerParams(dimension_semantics=semantics, vmem_limit_bytes=limit)


def _rmsnorm(x, g):
    return x * lax.rsqrt(jnp.mean(x * x, axis=-1, keepdims=True) + EPS) * g


def _sigmoid(x):
    return 1.0 / (1.0 + jnp.exp(-x))


def _softplus(x):
    return jnp.maximum(x, 0.0) + jnp.log(1.0 + jnp.exp(-jnp.abs(x)))


def _dot(a, b):
    return jnp.dot(a, b, preferred_element_type=F32)


def _dot_nt(a, b):
    return lax.dot_general(a, b, (((1,), (1,)), ((), ())), preferred_element_type=F32)


def _norm_matmul_kernel(x_ref, g_ref, w_ref, o_ref, xn_ref):
    @pl.when(pl.program_id(1) == 0)
    def _():
        xn_ref[...] = _rmsnorm(x_ref[...], g_ref[...]).astype(BF16)

    r = _dot(xn_ref[...], w_ref[...])
    width = o_ref.shape[-1]
    for t in range(o_ref.shape[0]):
        o_ref[t] = r[:, t * width:(t + 1) * width].astype(o_ref.dtype)


def column_tiles(w, tn):
    *lead, k, m = w.shape
    w = w.astype(BF16).reshape(*lead, k, m // tn, tn)
    return jnp.swapaxes(w, -3, -2)


def norm_matmul(x, g, w_tiles, *, out_width, out_dtype, tm=1024):
    n, k = x.shape
    n_tiles, _, tn = w_tiles.shape
    m = n_tiles * tn
    per = tn // out_width
    tm = min(tm, n)
    est = 2 * tm * k * 4 + tm * k * 2 + 2 * k * tn * 2 + 3 * tm * tn * 4
    return pl.pallas_call(
        _norm_matmul_kernel,
        grid=(n // tm, m // tn),
        in_specs=[
            pl.BlockSpec((tm, k), lambda i, j: (i, 0)),
            pl.BlockSpec((1, k), lambda i, j: (0, 0)),
            pl.BlockSpec((None, k, tn), lambda i, j: (j, 0, 0)),
        ],
        out_specs=pl.BlockSpec((per, tm, out_width), lambda i, j: (j, i, 0)),
        out_shape=jax.ShapeDtypeStruct((m // out_width, n, out_width), out_dtype),
        scratch_shapes=[pltpu.VMEM((tm, k), BF16)],
        compiler_params=_cparams(("parallel", "arbitrary"), est),
        name="norm_matmul",
    )(x, g.reshape(1, k), w_tiles)


def _qkv_kernel(x_ref, g_ref, w_ref, cos_ref, sin_ref, o_ref, xn_ref, *, heads_per_tile, scale):
    j = pl.program_id(1)
    tiles_per_part = N_HEADS_QKV // heads_per_tile

    @pl.when(j == 0)
    def _():
        xn_ref[...] = _rmsnorm(x_ref[...], g_ref[...]).astype(BF16)

    mult = jnp.where(j < tiles_per_part, scale, 1.0)
    first_head = (j % tiles_per_part) * heads_per_tile
    xn = xn_ref[...]
    for pair in range(heads_per_tile // 2):
        use_rope = jnp.logical_and(j < 2 * tiles_per_part, first_head + 2 * pair >= N_HEADS_A)
        c = jnp.where(use_rope, cos_ref[...], 1.0) * mult
        s = jnp.where(use_rope, sin_ref[...], 0.0) * mult
        r = _dot(xn, w_ref[:, pair * 2 * HEAD_DIM:(pair + 1) * 2 * HEAD_DIM])
        for k in range(2):
            p = r[:, k * HEAD_DIM:(k + 1) * HEAD_DIM]
            o_ref[2 * pair + k] = (p * c + pltpu.roll(p, HEAD_DIM // 2, 1) * s).astype(o_ref.dtype)


def qkv_proj(x, g, w_tiles, cos, sin, *, seq_len, tm=1024):
    n, k = x.shape
    n_tiles, _, tn = w_tiles.shape
    hpt = tn // HEAD_DIM
    assert N_HEADS_QKV % hpt == 0 and N_HEADS_A % 2 == 0 and hpt % 2 == 0
    assert tn == hpt * HEAD_DIM
    m = n_tiles * tn
    tm = min(tm, seq_len)
    tiles_per_seq = seq_len // tm
    est = 2 * tm * k * 4 + tm * k * 2 + 2 * k * tn * 2 + 2 * tm * tn * 2 + 4 * tm * HEAD_DIM * 4 + tm * tn * 4
    return pl.pallas_call(
        functools.partial(_qkv_kernel, heads_per_tile=hpt, scale=HEAD_DIM ** -0.5),
        grid=(n // tm, m // tn),
        in_specs=[
            pl.BlockSpec((tm, k), lambda i, j: (i, 0)),
            pl.BlockSpec((1, k), lambda i, j: (0, 0)),
            pl.BlockSpec((None, k, tn), lambda i, j: (j, 0, 0)),
            pl.BlockSpec((tm, HEAD_DIM), lambda i, j: (i % tiles_per_seq, 0)),
            pl.BlockSpec((tm, HEAD_DIM), lambda i, j: (i % tiles_per_seq, 0)),
        ],
        out_specs=pl.BlockSpec((hpt, tm, HEAD_DIM), lambda i, j: (j, i, 0)),
        out_shape=jax.ShapeDtypeStruct((m // HEAD_DIM, n, HEAD_DIM), BF16),
        scratch_shapes=[pltpu.VMEM((tm, k), BF16)],
        compiler_params=_cparams(("parallel", "arbitrary"), est),
        name="qkv_proj",
    )(x, g.reshape(1, k), w_tiles, cos, sin)


def rope_tables(seq_len):
    half = HEAD_DIM // 2
    inv = ROPE_THETA ** (-jnp.arange(half, dtype=F32) / half)
    ang = jnp.arange(seq_len, dtype=F32)[:, None] * inv[None, :]
    cos = jnp.cos(ang)
    sin = jnp.sin(ang)
    return jnp.concatenate([cos, cos], axis=1), jnp.concatenate([-sin, sin], axis=1)


def _natten_kernel(q_ref, k_ref, v_ref, bias_ref, o_ref, *, n_blocks, grid_rows):
    blk = pl.program_id(2)
    first_row = jnp.where(blk == 0, 0,
                          jnp.where(blk == n_blocks - 1, grid_rows - NAT_KROWS, blk * NAT_ROWS - WIN_H // 2))
    start = pl.multiple_of(first_row * GRID_W, GRID_W)
    kw = k_ref[pl.ds(start, NAT_KROWS * GRID_W), :]
    vw = v_ref[pl.ds(start, NAT_KROWS * GRID_W), :]
    s = _dot_nt(q_ref[...], kw) + bias_ref[...]
    m = jnp.max(s, axis=-1, keepdims=True)
    p = jnp.exp(s - m)
    l = jnp.sum(p, axis=-1, keepdims=True)
    o_ref[...] = (_dot(p.astype(BF16), vw) / l).astype(o_ref.dtype)


def natten_bias_tiles(rpb, grid_rows):
    n_blocks = grid_rows // NAT_ROWS
    r0s = np.array([0, NAT_ROWS, (n_blocks - 1) * NAT_ROWS])
    k0s = np.array([0, NAT_ROWS - WIN_H // 2, grid_rows - NAT_KROWS])
    r = r0s[:, None] + np.arange(NAT_ROWS)[None, :]
    kr = k0s[:, None] + np.arange(NAT_KROWS)[None, :]
    rs = np.clip(r - WIN_H // 2, 0, grid_rows - WIN_H)
    row_ok = (kr[:, None, :] >= rs[:, :, None]) & (kr[:, None, :] < rs[:, :, None] + WIN_H)
    d_row = np.clip(kr[:, None, :] - r[:, :, None] + (WIN_H - 1), 0, 2 * WIN_H - 2)
    c = np.arange(GRID_W)
    cs = np.clip(c - WIN_W // 2, 0, GRID_W - WIN_W)
    col_ok = (c[None, :] >= cs[:, None]) & (c[None, :] < cs[:, None] + WIN_W)
    d_col = np.clip(c[None, :] - c[:, None] + (WIN_W - 1), 0, 2 * WIN_W - 2)
    sel_row = (d_row[..., None] == np.arange(2 * WIN_H - 1)).astype(np.float32)
    sel_col = (d_col[..., None] == np.arange(2 * WIN_W - 1)).astype(np.float32)
    rows = jnp.einsum("tikr,hrc->thikc", sel_row, rpb.astype(F32), precision=lax.Precision.HIGHEST)
    bias = jnp.einsum("thikc,qwc->thiqkw", rows, sel_col, precision=lax.Precision.HIGHEST)
    ok = row_ok[:, :, None, :, None] & col_ok[None, None, :, None, :]
    bias = jnp.where(ok[:, None], bias, NEG_INF)
    return bias.reshape(3, rpb.shape[0], NAT_ROWS * GRID_W, NAT_KROWS * GRID_W)


def natten(qkv, bias_tiles, *, batch, seq_len):
    n = batch * seq_len
    grid_rows = seq_len // GRID_W
    n_blocks = grid_rows // NAT_ROWS
    tq = NAT_ROWS * GRID_W
    tk = NAT_KROWS * GRID_W

    def tile_kind(blk):
        return jnp.where(blk == 0, 0, jnp.where(blk == n_blocks - 1, 2, 1))

    est = 4 * seq_len * HEAD_DIM * 2 + 2 * tq * tk * 4 + 3 * tq * tk * 4
    return pl.pallas_call(
        functools.partial(_natten_kernel, n_blocks=n_blocks, grid_rows=grid_rows),
        grid=(batch, N_HEADS_A, n_blocks),
        in_specs=[
            pl.BlockSpec((None, tq, HEAD_DIM), lambda b, h, i: (h, b * n_blocks + i, 0)),
            pl.BlockSpec((None, seq_len, HEAD_DIM), lambda b, h, i: (N_HEADS_QKV + h, b, 0)),
            pl.BlockSpec((None, seq_len, HEAD_DIM), lambda b, h, i: (2 * N_HEADS_QKV + h, b, 0)),
            pl.BlockSpec((None, None, tq, tk), lambda b, h, i: (tile_kind(i), h, 0, 0)),
        ],
        out_specs=pl.BlockSpec((tq, HEAD_DIM), lambda b, h, i: (b * n_blocks + i, h)),
        out_shape=jax.ShapeDtypeStruct((n, N_HEADS_A * HEAD_DIM), BF16),
        compiler_params=_cparams(("parallel", "parallel", "arbitrary"), est),
        name="natten",
    )(qkv, qkv, qkv, bias_tiles)


class _DilatedGeometry:
    def __init__(self, dil, kc):
        self.dil, self.kc = dil, kc
        self.reach = BAND_RADIUS * dil
        self.tp = DIL_TQ // DIL_PARTS
        assert kc % self.tp == 0
        self.halo = -(-self.reach // kc) * kc
        self.n_chunks = (DIL_TQ + 2 * self.halo) // kc
        self.n_tiles = (kc // self.tp) * (self.n_chunks - 1) + DIL_PARTS

    def tile_index(self, chunk, part):
        return (self.kc // self.tp) * chunk + (DIL_PARTS - 1 - part)

    def _allowed(self, u):
        rel = np.arange(self.kc)[None, :] - np.arange(self.tp)[:, None] + (u - (DIL_PARTS - 1)) * self.tp - self.halo
        return (np.abs(rel) <= self.reach) & (rel % self.dil == 0)

    def is_active(self, chunk, part):
        return bool(self._allowed(self.tile_index(chunk, part)).any())

    def bias_tiles(self):
        tiles = [np.where(self._allowed(u), 0.0, -np.inf) for u in range(self.n_tiles)]
        tiles.append(np.full((self.tp, self.kc), -np.inf))
        return np.stack(tiles).astype(np.float32)


DIL_GEOMETRY = tuple(_DilatedGeometry(d, kc) for d, kc in zip(DILATIONS, DIL_KC))


def _dilated_kernel(b0, b1, b2, q0, q1, q2, k0, k1, k2, v0, v1, v2, o_ref, *, seq_len):
    tq = DIL_TQ
    parts = DIL_PARTS
    tp = tq // parts
    t0 = pl.program_id(2) * tq
    carry = [(jnp.full((tp, 1), -1e30, F32), jnp.zeros((tp, 1), F32), jnp.zeros((tp, HEAD_DIM), F32))
             for _ in range(parts)]
    for geo, bias_ref, q_ref, k_ref, v_ref in zip(DIL_GEOMETRY, (b0, b1, b2), (q0, q1, q2), (k0, k1, k2), (v0, v1, v2)):
        kc = geo.kc
        qs = tuple(q_ref[p * tp:(p + 1) * tp, :] for p in range(parts))
        for ci in range(geo.n_chunks):
            active = [p for p in range(parts) if geo.is_active(ci, p)]
            start = t0 - geo.halo + ci * kc
            in_range = jnp.logical_and(start >= 0, start + kc <= seq_len)
            ks = pl.multiple_of(jnp.clip(start, 0, seq_len - kc), kc)
            kk = k_ref[pl.ds(ks, kc), :]
            vv = v_ref[pl.ds(ks, kc), :]
            scores = {p: _dot_nt(qs[p], kk) + bias_ref[jnp.where(in_range, geo.tile_index(ci, p), geo.n_tiles)]
                      for p in active}
            stats = {}
            for p in active:
                m, l, _ = carry[p]
                m_new = jnp.maximum(m, jnp.max(scores[p], axis=-1, keepdims=True))
                alpha = jnp.exp(m - m_new)
                e = jnp.exp(scores[p] - m_new)
                stats[p] = (m_new, alpha, alpha * l + jnp.sum(e, axis=-1, keepdims=True), e.astype(BF16))
            for p in active:
                m_new, alpha, l, e = stats[p]
                carry[p] = (m_new, l, alpha * carry[p][2] + _dot(e, vv))
    for p in range(parts):
        _, l, acc = carry[p]
        o_ref[p * tp:(p + 1) * tp, :] = (acc / l).astype(o_ref.dtype)


def dilated_attention(qkv, *, batch, seq_len):
    n = batch * seq_len
    nq = seq_len // DIL_TQ
    biases = [geo.bias_tiles() for geo in DIL_GEOMETRY]

    def q_spec(g):
        return pl.BlockSpec((None, DIL_TQ, HEAD_DIM),
                            lambda b, j, i: (N_HEADS_A + N_HEADS_B_GROUP * g + j, b * nq + i, 0))

    def kv_spec(part, g):
        return pl.BlockSpec((None, seq_len, HEAD_DIM),
                            lambda b, j, i: (part * N_HEADS_QKV + N_HEADS_A + N_HEADS_B_GROUP * g + j, b, 0))

    est = 12 * seq_len * HEAD_DIM * 2 + 8 * DIL_TQ * max(DIL_KC) * 4 + 2 * sum(b.size for b in biases) * 4
    groups = range(len(DILATIONS))
    return pl.pallas_call(
        functools.partial(_dilated_kernel, seq_len=seq_len),
        grid=(batch, N_HEADS_B_GROUP, nq),
        in_specs=[pl.BlockSpec(b.shape, lambda b_, j, i: (0, 0, 0)) for b in biases]
        + [q_spec(g) for g in groups] + [kv_spec(1, g) for g in groups] + [kv_spec(2, g) for g in groups],
        out_specs=pl.BlockSpec((DIL_TQ, HEAD_DIM), lambda b, j, i: (b * nq + i, j)),
        out_shape=jax.ShapeDtypeStruct((n, N_HEADS_B_GROUP * HEAD_DIM), BF16),
        compiler_params=_cparams(("parallel", "parallel", "arbitrary"), est),
        name="dilated_attention",
    )(*biases, *([qkv] * 9))


def _attn_out_kernel(x_ref, oa_ref, ob_ref, w_ref, o_ref):
    ka = oa_ref.shape[1]
    o_ref[...] = x_ref[...] + _dot(oa_ref[...], w_ref[:ka, :]) + _dot(ob_ref[...], w_ref[ka:, :])


def attn_out(x, o_a, o_b, w, *, tm=512):
    n, d = x.shape
    ka, kb = o_a.shape[1], o_b.shape[1]
    tm = min(tm, n)
    est = 4 * tm * d * 4 + 2 * (ka + kb) * d * 2 + 2 * tm * (ka + kb) * 2
    return pl.pallas_call(
        _attn_out_kernel,
        grid=(n // tm,),
        in_specs=[
            pl.BlockSpec((tm, d), lambda i: (i, 0)),
            pl.BlockSpec((tm, ka), lambda i: (i, 0)),
            pl.BlockSpec((tm, kb), lambda i: (i, 0)),
            pl.BlockSpec((ka + kb, d), lambda i: (0, 0)),
        ],
        out_specs=pl.BlockSpec((tm, d), lambda i: (i, 0)),
        out_shape=jax.ShapeDtypeStruct((n, d), F32),
        compiler_params=_cparams(("parallel",), est),
        name="attn_out",
    )(x, o_a, o_b, w)


def _ffn_kernel(x_ref, g_ref, wg_ref, wu_ref, wd_ref, o_ref, xn_ref):
    f = pl.program_id(1)

    @pl.when(f == 0)
    def _():
        xn_ref[...] = _rmsnorm(x_ref[...], g_ref[...]).astype(BF16)
        o_ref[...] = x_ref[...]

    xn = xn_ref[...]
    gate = _dot(xn, wg_ref[...])
    up = _dot(xn, wu_ref[...])
    hid = (gate * _sigmoid(gate) * up).astype(BF16)
    o_ref[...] += _dot(hid, wd_ref[...])


def ffn(x, g, wg_tiles, wu_tiles, wd, *, tm=512):
    n, d = x.shape
    nf, _, tf = wg_tiles.shape
    tm = min(tm, n)
    est = 4 * tm * d * 4 + tm * d * 2 + 6 * d * tf * 2 + 3 * tm * tf * 4
    return pl.pallas_call(
        _ffn_kernel,
        grid=(n // tm, nf),
        in_specs=[
            pl.BlockSpec((tm, d), lambda i, f: (i, 0)),
            pl.BlockSpec((1, d), lambda i, f: (0, 0)),
            pl.BlockSpec((None, d, tf), lambda i, f: (f, 0, 0)),
            pl.BlockSpec((None, d, tf), lambda i, f: (f, 0, 0)),
            pl.BlockSpec((tf, d), lambda i, f: (f, 0)),
        ],
        out_specs=pl.BlockSpec((tm, d), lambda i, f: (i, 0)),
        out_shape=jax.ShapeDtypeStruct((n, d), F32),
        scratch_shapes=[pltpu.VMEM((tm, d), BF16)],
        compiler_params=_cparams(("parallel", "arbitrary"), est),
        name="ffn",
    )(x, g.reshape(1, d), wg_tiles, wu_tiles, wd)


def _conv_kernel(xm_ref, xp_ref, xn_ref, w_ref, b_ref, o_ref, *rest, tr, n_row_blocks):
    ext_ref = rest[-1]
    i = pl.program_id(2)
    hb = BF16_SUBLANES
    pad = SSM_CONV // 2
    ext_ref[0:hb, :] = jnp.where(i > 0, xp_ref[...].astype(F32), 0.0)
    ext_ref[hb:hb + tr, :] = xm_ref[...].astype(F32)
    ext_ref[hb + tr:2 * hb + tr, :] = jnp.where(i < n_row_blocks - 1, xn_ref[...].astype(F32), 0.0)
    acc = jnp.broadcast_to(b_ref[...], o_ref.shape)
    for k in range(SSM_CONV):
        acc = acc + ext_ref[pl.ds(hb - pad + k, tr), :] * w_ref[k:k + 1, :]
    out = acc * _sigmoid(acc)
    o_ref[...] = out.astype(o_ref.dtype)
    if len(rest) == 2:
        ot_ref = rest[0]
        L = ot_ref.shape[-1]
        for c in range(tr // L):
            ot_ref[c] = out[c * L:(c + 1) * L, :].T.astype(ot_ref.dtype)


def conv_silu(zx, conv_w, conv_b, *, width, first_tile, n_tiles, col_offset, batch, seq_len, tr, transposed=False):
    n = batch * seq_len
    tile_w = zx.shape[2]
    per = tile_w // width
    tr = min(tr, seq_len)
    nr = seq_len // tr
    hb = BF16_SUBLANES
    seq_hb = seq_len // hb
    n_hb = n // hb

    def main_map(c, b, i):
        return (first_tile + c // per, b * nr + i, c % per)

    def prev_map(c, b, i):
        return (first_tile + c // per, jnp.maximum(b * seq_hb + i * (tr // hb) - 1, 0), c % per)

    def next_map(c, b, i):
        return (first_tile + c // per, jnp.minimum(b * seq_hb + (i + 1) * (tr // hb), n_hb - 1), c % per)

    L = SSM_CHUNK
    out_specs = [pl.BlockSpec((None, tr, width), lambda c, b, i: (c, b * nr + i, 0))]
    out_shape = [jax.ShapeDtypeStruct((n_tiles, n, width), BF16)]
    if transposed:
        out_specs.append(pl.BlockSpec((None, tr // L, width, L), lambda c, b, i: (c, b * nr + i, 0, 0)))
        out_shape.append(jax.ShapeDtypeStruct((n_tiles, n // L, width, L), BF16))
    est = 4 * tr * width * 2 + (tr + 2 * hb) * width * 4 + 4 * tr * width * 4 + 4 * tr * width * 2
    outs = pl.pallas_call(
        functools.partial(_conv_kernel, tr=tr, n_row_blocks=nr),
        grid=(n_tiles, batch, nr),
        in_specs=[
            pl.BlockSpec((None, tr, width), main_map),
            pl.BlockSpec((None, hb, width), prev_map),
            pl.BlockSpec((None, hb, width), next_map),
            pl.BlockSpec((SSM_CONV, width), lambda c, b, i: (0, col_offset // width + c)),
            pl.BlockSpec((1, width), lambda c, b, i: (0, col_offset // width + c)),
        ],
        out_specs=out_specs,
        out_shape=out_shape,
        scratch_shapes=[pltpu.VMEM((tr + 2 * hb, width), F32)],
        compiler_params=_cparams(("parallel", "parallel", "arbitrary"), est),
        name="conv_silu",
    )(zx, zx, zx, conv_w, conv_b.reshape(1, -1))
    return outs if transposed else outs[0]


def _lane_cumsum(a):
    lane = lax.broadcasted_iota(jnp.int32, a.shape, 1)
    shift = 1
    while shift < a.shape[1]:
        a = a + jnp.where(lane >= shift, pltpu.roll(a, shift, 1), 0.0)
        shift *= 2
    return a


def _ssd_decay_kernel(dt_ref, dtbias_ref, alog_ref, seg_ref, dtact_ref, *, hpg):
    L = SSM_CHUNK
    chunks, rows, _ = dt_ref.shape
    dt_act = _softplus(dt_ref[...] + dtbias_ref[...])
    dt_all = dt_act.reshape(chunks * rows, L)
    a_all = (dt_act * (-jnp.exp(alog_ref[...]))).reshape(chunks * rows, L)
    cum = _lane_cumsum(a_all)
    suf = cum[:, L - 1:L] - cum + a_all
    row = lax.broadcasted_iota(jnp.int32, cum.shape, 0)
    seg_ref[...] = jnp.where((row & (2 * hpg - 1)) < hpg, cum, suf).reshape(chunks, rows, L)
    dtact_ref[...] = dt_all.reshape(chunks, rows, L)


def ssd_decay_rows(dt_rows, dt_bias_rows, a_log_rows, *, hpg, chunks_per_step=8):
    n_chunks, rows, L = dt_rows.shape
    assert hpg & (hpg - 1) == 0
    cps = math.gcd(chunks_per_step, n_chunks)
    spec = pl.BlockSpec((cps, rows, L), lambda i: (i, 0, 0))
    const = pl.BlockSpec((rows, 1), lambda i: (0, 0))
    shape = jax.ShapeDtypeStruct(dt_rows.shape, F32)
    return pl.pallas_call(
        functools.partial(_ssd_decay_kernel, hpg=hpg),
        grid=(n_chunks // cps,),
        in_specs=[spec, const, const],
        out_specs=[spec, spec],
        out_shape=[shape, shape],
        compiler_params=_cparams(("parallel",), 16 * cps * rows * L * 4),
        name="ssd_decay_rows",
    )(dt_rows, dt_bias_rows, a_log_rows)


def _ssd_kernel(xa_ref, ba_ref, bta_ref, ca_ref, sega_ref, dta_ref, xb_ref, btb_ref, cb_ref, segb_ref, dtb_ref,
                dskip_ref, yf_ref, yb_ref, sf_ref, sb_ref, *, hpg):
    L = SSM_CHUNK
    P = SSM_HEAD_DIM
    pairs = hpg // 2

    @pl.when(pl.program_id(1) == 0)
    def _():
        sf_ref[...] = jnp.zeros_like(sf_ref)
        sb_ref[...] = jnp.zeros_like(sb_ref)

    li = lax.broadcasted_iota(jnp.int32, (L, L), 0)
    si = lax.broadcasted_iota(jnp.int32, (L, L), 1)
    causal = li >= si
    anti = li <= si
    low_lanes = lax.broadcasted_iota(jnp.int32, (L, 2 * P), 1) < P
    high_lanes = jnp.logical_not(low_lanes)
    low_lanes_row = lax.broadcasted_iota(jnp.int32, (1, 2 * P), 1) < P

    def lanes_of(col):
        return jnp.broadcast_to(col, (L, 2 * P))

    def columns(seg_r, dt_r):
        stacked = jnp.concatenate([seg_r, dt_r, jnp.zeros((L - 4 * hpg, L), F32)], axis=0)
        return stacked.T

    spread_bwd = (lax.broadcasted_iota(jnp.int32, (L, hpg * P), 0)
                  == hpg + lax.broadcasted_iota(jnp.int32, (L, hpg * P), 1) // P).astype(BF16)

    def advance_state(s_ref, g, lanes, state, btf, xm0, xm1, seg_r, dt_r, h0, h1, tot_lane):
        acc = None
        decays = []
        for h, xm in ((h0, xm0), (h1, xm1)):
            tot = seg_r[h:h + 1, tot_lane:tot_lane + 1]
            coef = jnp.exp(tot - seg_r[h:h + 1, :]) * dt_r[h:h + 1, :]
            part = _dot((btf * coef).astype(BF16), xm)
            acc = part if acc is None else acc + part
            decays.append(jnp.exp(tot))
        decay = jnp.where(low_lanes_row, decays[0], decays[1])
        s_ref[g, :, lanes] = state[:, lanes] * decay + acc

    def group_body(g, c):
        r0 = pl.multiple_of(g * 2 * hpg, 2 * hpg)

        seg_r = sega_ref[pl.ds(r0, 2 * hpg), :]
        dt_r = dta_ref[pl.ds(r0, 2 * hpg), :]
        cols = columns(seg_r, dt_r)
        xg = xa_ref[g]
        cg = ca_ref[g]
        state = sf_ref[g]
        carried = _dot(cg, state.astype(BF16))
        cbm = _dot_nt(cg, ba_ref[g])
        btf = bta_ref[g].astype(F32)
        for p in range(pairs):
            lanes = slice(p * 2 * P, (p + 1) * 2 * P)
            xpf = xg[:, lanes].astype(F32)
            xm0 = jnp.where(low_lanes, xpf, 0.0).astype(BF16)
            xm1 = jnp.where(high_lanes, xpf, 0.0).astype(BF16)
            h0, h1 = 2 * p, 2 * p + 1
            f0, f1 = lanes_of(cols[:, h0:h0 + 1]), lanes_of(cols[:, h1:h1 + 1])
            y = carried[:, lanes] * jnp.where(low_lanes, jnp.exp(f0), jnp.exp(f1)) + dskip_ref[g][:, lanes] * xpf
            for hh, f_cols, xm in ((h0, f0, xm0), (h1, f1, xm1)):
                wf = jnp.exp(jnp.where(causal, f_cols - seg_r[hh:hh + 1, :], NEG_INF)) * dt_r[hh:hh + 1, :]
                b_cols = lanes_of(cols[:, hpg + hh:hpg + hh + 1])
                wb = jnp.exp(jnp.where(anti, b_cols - seg_r[hpg + hh:hpg + hh + 1, :], NEG_INF)) \
                    * dt_r[hpg + hh:hpg + hh + 1, :]
                y = y + _dot((cbm * (wf + wb)).astype(BF16), xm)
            yf_ref[g, :, lanes] = y.astype(yf_ref.dtype)
            advance_state(sf_ref, g, lanes, state, btf, xm0, xm1, seg_r, dt_r, h0, h1, L - 1)

        seg_r = segb_ref[pl.ds(r0, 2 * hpg), :]
        dt_r = dtb_ref[pl.ds(r0, 2 * hpg), :]
        xg = xb_ref[g]
        state = sb_ref[g]
        grow = _dot(jnp.exp(columns(seg_r, dt_r)).astype(BF16), spread_bwd)
        y = _dot(cb_ref[g], state.astype(BF16)) * grow
        yb_ref[g] = y.astype(yb_ref.dtype)
        btf = btb_ref[g].astype(F32)
        for p in range(pairs):
            lanes = slice(p * 2 * P, (p + 1) * 2 * P)
            xpf = xg[:, lanes].astype(F32)
            xm0 = jnp.where(low_lanes, xpf, 0.0).astype(BF16)
            xm1 = jnp.where(high_lanes, xpf, 0.0).astype(BF16)
            advance_state(sb_ref, g, lanes, state, btf, xm0, xm1, seg_r, dt_r, hpg + 2 * p, hpg + 2 * p + 1, 0)
        return c

    lax.fori_loop(0, SSM_GROUPS, group_body, 0, unroll=4)


def ssd(xs, b_nat, b_t, c_nat, seg_rows, dt_act_rows, d_skip_rows, *, batch, seq_len):
    groups, n, gw = xs.shape
    hpg = gw // SSM_HEAD_DIM
    assert hpg & (hpg - 1) == 0
    L = SSM_CHUNK
    nc = seq_len // L
    rows = groups * 2 * hpg

    def fwd(b, k):
        return b * nc + k

    def bwd(b, k):
        return b * nc + nc - 1 - k

    def specs(chunk, with_b_nat):
        out = [pl.BlockSpec((groups, L, gw), lambda b, k: (0, chunk(b, k), 0))]
        if with_b_nat:
            out.append(pl.BlockSpec((groups, L, SSM_STATE), lambda b, k: (0, chunk(b, k), 0)))
        out += [
            pl.BlockSpec((groups, None, SSM_STATE, L), lambda b, k: (0, chunk(b, k), 0, 0)),
            pl.BlockSpec((groups, L, SSM_STATE), lambda b, k: (0, chunk(b, k), 0)),
            pl.BlockSpec((None, rows, L), lambda b, k: (chunk(b, k), 0, 0)),
            pl.BlockSpec((None, rows, L), lambda b, k: (chunk(b, k), 0, 0)),
        ]
        return out

    const = [pl.BlockSpec((groups, 1, gw), lambda b, k: (0, 0, 0))]
    y_shape = jax.ShapeDtypeStruct((groups, n, gw), BF16)
    est = 4 * (groups * L * gw * 2 + 3 * groups * L * SSM_STATE * 2 + 2 * rows * L * 4) + 4 * groups * L * gw * 2 \
        + 2 * groups * SSM_STATE * gw * 4 + 64 * L * L * 4
    return pl.pallas_call(
        functools.partial(_ssd_kernel, hpg=hpg),
        grid=(batch, nc),
        in_specs=specs(fwd, True) + specs(bwd, False) + const,
        out_specs=[pl.BlockSpec((groups, L, gw), lambda b, k: (0, fwd(b, k), 0)),
                   pl.BlockSpec((groups, L, gw), lambda b, k: (0, bwd(b, k), 0))],
        out_shape=[y_shape, y_shape],
        scratch_shapes=[pltpu.VMEM((groups, SSM_STATE, gw), F32), pltpu.VMEM((groups, SSM_STATE, gw), F32)],
        compiler_params=_cparams(("parallel", "arbitrary"), est),
        name="ssd",
    )(xs, b_nat, b_t, c_nat, seg_rows, dt_act_rows, xs, b_t, c_nat, seg_rows, dt_act_rows, d_skip_rows)


def _mamba_out_kernel(yf_ref, yb_ref, z_ref, gg_ref, w_ref, x_ref, o_ref, ssq_ref, *, d_inner):
    step = pl.program_id(1)
    groups_per_step, _, gw = z_ref.shape

    @pl.when(step == 0)
    def _():
        o_ref[...] = jnp.zeros_like(o_ref)
        ssq_ref[...] = jnp.zeros_like(ssq_ref)

    gated = []
    for k in range(groups_per_step):
        z = z_ref[k].astype(F32)
        yz = (yf_ref[k].astype(F32) + yb_ref[k].astype(F32)) * (z * _sigmoid(z))
        ssq_ref[...] += jnp.sum(yz * yz, axis=-1, keepdims=True)
        gated.append((yz * gg_ref[:, k * gw:(k + 1) * gw]).astype(BF16))
    o_ref[...] += _dot(jnp.concatenate(gated, axis=1), w_ref[...])

    @pl.when(step == pl.num_programs(1) - 1)
    def _():
        o_ref[...] = x_ref[...] + o_ref[...] * lax.rsqrt(ssq_ref[...] / d_inner + EPS)


def mamba_out(y_f, y_b, zx, g_gate, w_out, x, *, tm=512, groups_per_step=4):
    groups, n, gw = y_f.shape
    d = x.shape[1]
    tm = min(tm, n)
    gps = groups_per_step
    est = gps * (6 * tm * gw * 2 + 2 * gw * d * 2) + 3 * tm * d * 4 + 2 * tm * gps * gw * 4
    return pl.pallas_call(
        functools.partial(_mamba_out_kernel, d_inner=groups * gw),
        grid=(n // tm, groups // gps),
        in_specs=[
            pl.BlockSpec((gps, tm, gw), lambda i, s: (s, i, 0)),
            pl.BlockSpec((gps, tm, gw), lambda i, s: (s, i, 0)),
            pl.BlockSpec((gps, tm, gw), lambda i, s: (s, i, 0)),
            pl.BlockSpec((1, gps * gw), lambda i, s: (0, s)),
            pl.BlockSpec((gps * gw, d), lambda i, s: (s, 0)),
            pl.BlockSpec((tm, d), lambda i, s: (i, 0)),
        ],
        out_specs=pl.BlockSpec((tm, d), lambda i, s: (i, 0)),
        out_shape=jax.ShapeDtypeStruct((n, d), F32),
        scratch_shapes=[pltpu.VMEM((tm, 1), F32)],
        compiler_params=_cparams(("parallel", "arbitrary"), est),
        name="mamba_out",
    )(y_f, y_b, zx, g_gate.reshape(1, -1), w_out, x)


def _router_kernel(xa_ref, xb_ref, g_ref, w_ref, o_ref, xn_ref, *, tiles_a):
    i = pl.program_id(0)

    @pl.when(i < tiles_a)
    def _():
        xn_ref[...] = _rmsnorm(xa_ref[...], g_ref[...])

    @pl.when(i >= tiles_a)
    def _():
        xn_ref[...] = _rmsnorm(xb_ref[...], g_ref[...])

    xn = xn_ref[...]
    logits = jnp.dot(xn, w_ref[...], preferred_element_type=F32, precision=lax.Precision.HIGHEST)
    lane = lax.broadcasted_iota(jnp.int32, logits.shape, 1)
    logits = jnp.where(lane < N_EXPERTS, logits, NEG_INF)
    v1 = jnp.max(logits, axis=-1, keepdims=True)
    i1 = jnp.min(jnp.where(logits == v1, lane, LANES), axis=-1, keepdims=True)
    rest = jnp.where(lane == i1, NEG_INF, logits)
    v2 = jnp.max(rest, axis=-1, keepdims=True)
    i2 = jnp.min(jnp.where(rest == v2, lane, LANES), axis=-1, keepdims=True)
    e2 = jnp.exp(v2 - v1)
    g1 = 1.0 / (1.0 + e2)
    g2 = e2 / (1.0 + e2)
    out = jnp.where(lane == 0, i1.astype(F32),
                    jnp.where(lane == 1, i2.astype(F32),
                              jnp.where(lane == 2, g1, jnp.where(lane == 3, g2, 0.0))))
    o_ref[...] = out


def router(x_a, x_b, g, w_router, *, tm=512):
    (na, d), nb = x_a.shape, x_b.shape[0]
    tm = min(tm, na, nb)
    tiles_a, tiles_b = na // tm, nb // tm
    w = jnp.zeros((d, LANES), F32).at[:, :w_router.shape[1]].set(w_router)
    est = 4 * tm * d * 4 + 2 * d * LANES * 4 + 2 * tm * LANES * 4 + 3 * tm * d * 4
    return pl.pallas_call(
        functools.partial(_router_kernel, tiles_a=tiles_a),
        grid=(tiles_a + tiles_b,),
        in_specs=[
            pl.BlockSpec((tm, d), lambda i: (jnp.minimum(i, tiles_a - 1), 0)),
            pl.BlockSpec((tm, d), lambda i: (jnp.maximum(i - tiles_a, 0), 0)),
            pl.BlockSpec((1, d), lambda i: (0, 0)),
            pl.BlockSpec((d, LANES), lambda i: (0, 0)),
        ],
        out_specs=[pl.BlockSpec((tm, LANES), lambda i: (i, 0)), pl.BlockSpec((tm, d), lambda i: (i, 0))],
        out_shape=[jax.ShapeDtypeStruct((na + nb, LANES), F32), jax.ShapeDtypeStruct((na + nb, d), F32)],
        compiler_params=_cparams(("parallel",), est),
        name="router",
    )(x_a, x_b, g.reshape(1, d), w)


def _row_copy(src_hbm, dst_vmem, sem, src_row, dst_row):
    return pltpu.make_async_copy(src_hbm.at[pl.ds(src_row, 1), :], dst_vmem.at[pl.ds(dst_row, 1), :], sem)


EXPERT_ISSUE_STEPS = 4


def _expert_kernel(blk_e_ref, n_used_ref, tok_ref, x_hbm, wg_ref, wu_ref, wd_ref, o_ref, rows_ref, x_ref, sems):
    i = pl.program_id(0)
    f = pl.program_id(1)
    tm = x_ref.shape[0]
    per_step = tm // EXPERT_ISSUE_STEPS
    n_used = n_used_ref[0]
    slot = i % 2

    def issue(block, dst_slot, first_row, count):
        def body(r, c):
            _row_copy(x_hbm, rows_ref.at[dst_slot], sems.at[dst_slot], tok_ref[block * tm + first_row + r],
                      first_row + r).start()
            return c
        lax.fori_loop(0, count, body, 0, unroll=8)

    @pl.when(jnp.logical_and(i == 0, f == 0))
    def _():
        issue(0, 0, 0, tm)

    @pl.when(jnp.logical_and(f == 0, i < n_used))
    def _():
        def wait(r, c):
            _row_copy(x_hbm, rows_ref.at[slot], sems.at[slot], 0, r).wait()
            return c
        lax.fori_loop(0, tm, wait, 0, unroll=8)
        x_ref[...] = rows_ref[slot].astype(x_ref.dtype)

    @pl.when(f == 0)
    def _():
        o_ref[...] = jnp.zeros_like(o_ref)

    @pl.when(jnp.logical_and(f < EXPERT_ISSUE_STEPS, i + 1 < n_used))
    def _():
        issue(i + 1, 1 - slot, f * per_step, per_step)

    @pl.when(i < n_used)
    def _():
        x = x_ref[...]
        gate = _dot(x, wg_ref[...])
        up = _dot(x, wu_ref[...])
        hid = (gate * _sigmoid(gate) * up).astype(BF16)
        o_ref[...] += _dot(hid, wd_ref[...])


def expert_ffn(blk_expert, n_used, tok_of_slot, x_rows, wg, wu, wd, *, tm=MOE_TM, tf=1024):
    slots = tok_of_slot.shape[0]
    d = x_rows.shape[1]
    nf = wg.shape[2] // tf
    assert nf >= EXPERT_ISSUE_STEPS and tm % (8 * EXPERT_ISSUE_STEPS) == 0
    n_blocks = slots // tm

    def live(i, n_used):
        return jnp.minimum(i, n_used[0] - 1)

    def f_eff(i, f, n_used):
        return jnp.where(i < n_used[0], f, nf - 1)

    est = 2 * tm * d * 4 + tm * d * 2 + 12 * d * tf + 2 * tm * d * 4 + 3 * tm * tf * 4
    return pl.pallas_call(
        _expert_kernel,
        grid_spec=pltpu.PrefetchScalarGridSpec(
            num_scalar_prefetch=3,
            grid=(n_blocks, nf),
            in_specs=[
                pl.BlockSpec(memory_space=pl.ANY),
                pl.BlockSpec((None, d, tf), lambda i, f, be, nu, tok: (be[live(i, nu)], 0, f_eff(i, f, nu))),
                pl.BlockSpec((None, d, tf), lambda i, f, be, nu, tok: (be[live(i, nu)], 0, f_eff(i, f, nu))),
                pl.BlockSpec((None, tf, d), lambda i, f, be, nu, tok: (be[live(i, nu)], f_eff(i, f, nu), 0)),
            ],
            out_specs=pl.BlockSpec((tm, d), lambda i, f, be, nu, tok: (i, 0)),
            scratch_shapes=[pltpu.VMEM((2, tm, d), F32), pltpu.VMEM((tm, d), BF16), pltpu.SemaphoreType.DMA((2,))],
        ),
        out_shape=jax.ShapeDtypeStruct((slots, d), F32),
        compiler_params=_cparams(("arbitrary", "arbitrary"), est),
        name="expert_ffn",
    )(blk_expert, n_used, tok_of_slot, x_rows, wg, wu, wd)


def _combine_kernel(slot_ref, y_hbm, x_ref, r_ref, g_ref, o_ref, buf_ref, sem):
    rows = x_ref.shape[0]
    base = pl.program_id(0) * rows

    def start(r, c):
        for k in range(TOP_K):
            _row_copy(y_hbm, buf_ref.at[k], sem, slot_ref[TOP_K * (base + r) + k], r).start()
        return c

    def wait(r, c):
        for k in range(TOP_K):
            _row_copy(y_hbm, buf_ref.at[k], sem, 0, r).wait()
        return c

    lax.fori_loop(0, rows, start, 0, unroll=8)
    lax.fori_loop(0, rows, wait, 0, unroll=8)
    gates = r_ref[...]
    out = x_ref[...] + gates[:, 2:3] * buf_ref[0] + gates[:, 3:4] * buf_ref[1]
    o_ref[...] = _rmsnorm(out, g_ref[...])


def combine_norm(slot_of_assignment, ys, x, routed, g_final, *, rows=GATHER_ROWS):
    n, d = x.shape
    rows = min(rows, n)
    est = 2 * rows * d * 4 + 4 * rows * d * 4 + 4 * rows * d * 4
    return pl.pallas_call(
        _combine_kernel,
        grid_spec=pltpu.PrefetchScalarGridSpec(
            num_scalar_prefetch=1,
            grid=(n // rows,),
            in_specs=[
                pl.BlockSpec(memory_space=pl.ANY),
                pl.BlockSpec((rows, d), lambda i, s: (i, 0)),
                pl.BlockSpec((rows, LANES), lambda i, s: (i, 0)),
                pl.BlockSpec((1, d), lambda i, s: (0, 0)),
            ],
            out_specs=pl.BlockSpec((rows, d), lambda i, s: (i, 0)),
            scratch_shapes=[pltpu.VMEM((TOP_K, rows, d), F32), pltpu.SemaphoreType.DMA(())],
        ),
        out_shape=jax.ShapeDtypeStruct((n, d), F32),
        compiler_params=_cparams(("arbitrary",), est),
        name="combine_norm",
    )(slot_of_assignment, ys, x, routed, g_final.reshape(1, d))


def moe_plan(routed, *, tm):
    n = routed.shape[0]
    experts = routed[:, :TOP_K].astype(jnp.int32).reshape(-1)
    onehot = (experts[:, None] == jnp.arange(N_EXPERTS)[None, :]).astype(jnp.int32)
    rank = jnp.sum((jnp.cumsum(onehot, axis=0) - onehot) * onehot, axis=1)
    counts = jnp.sum(onehot, axis=0)
    padded = ((counts + tm - 1) // tm) * tm
    ends = jnp.cumsum(padded)
    starts = ends - padded
    slot = (starts[experts] + rank).astype(jnp.int32)
    n_blocks = (n * TOP_K) // tm + N_EXPERTS
    tok = jnp.repeat(jnp.arange(n, dtype=jnp.int32), TOP_K)
    tok_of_slot = jnp.zeros((n_blocks * tm,), jnp.int32).at[slot].set(tok, unique_indices=True,
                                                                      mode="promise_in_bounds")
    blk_expert = jnp.minimum(
        jnp.searchsorted(ends, jnp.arange(n_blocks, dtype=jnp.int32) * tm, side="right"), N_EXPERTS - 1
    ).astype(jnp.int32)
    n_used = (ends[-1] // tm).astype(jnp.int32).reshape(1)
    return slot, tok_of_slot, blk_expert, n_used


def _prepare_weights(w_qkv, w_o, w_ff_gate, w_ff_up, w_ff_down, w_in_c, w_out_c):
    d_inner = w_out_c.shape[1]
    gw = d_inner // SSM_GROUPS
    main_cols = 2 * d_inner + 2 * SSM_GROUPS * SSM_STATE
    return dict(
        w_qkv=column_tiles(w_qkv[0], QKV_TILE), w_o=w_o[0].astype(BF16),
        w_ff_gate=column_tiles(w_ff_gate[0], FF_TILE), w_ff_up=column_tiles(w_ff_up[0], FF_TILE),
        w_ff_down=w_ff_down[0].astype(BF16),
        w_in_main=column_tiles(w_in_c[0][:, :main_cols], 2 * gw),
        w_in_dt=column_tiles(w_in_c[0][:, main_cols:], w_in_c.shape[2] - main_cols),
        w_out=w_out_c[0].astype(BF16),
    )


def _mixer_layers(x3, wb, g_mix, g_ffn, rpb, conv_w, conv_b, dt_bias, a_log, d_skip, g_gate):
    batch, seq_len, d = x3.shape
    n = batch * seq_len
    x = x3.reshape(n, d)

    cos, sin = rope_tables(seq_len)
    qkv = qkv_proj(x, g_mix[0], wb["w_qkv"], cos, sin, seq_len=seq_len)
    bias_tiles = natten_bias_tiles(rpb[0], seq_len // GRID_W)
    o_a = natten(qkv, bias_tiles, batch=batch, seq_len=seq_len)
    o_b = dilated_attention(qkv, batch=batch, seq_len=seq_len)
    x = attn_out(x, o_a, o_b, wb["w_o"])
    x = ffn(x, g_ffn[0], wb["w_ff_gate"], wb["w_ff_up"], wb["w_ff_down"])

    d_inner = wb["w_out"].shape[0]
    gw = d_inner // SSM_GROUPS
    hpg = gw // SSM_HEAD_DIM
    heads = SSM_GROUPS * hpg
    zx = norm_matmul(x, g_mix[1], wb["w_in_main"], out_width=gw, out_dtype=BF16)
    dt_raw = norm_matmul(x, g_mix[1], wb["w_in_dt"], out_width=2 * heads, out_dtype=F32)[0]
    z_tiles = d_inner // gw
    xs = conv_silu(zx, conv_w[0], conv_b[0], width=gw, first_tile=z_tiles, n_tiles=SSM_GROUPS,
                   col_offset=0, batch=batch, seq_len=seq_len, tr=2048)
    bc_tiles = SSM_GROUPS * SSM_STATE // gw
    b_nat, b_t = conv_silu(zx, conv_w[0], conv_b[0], width=SSM_STATE, first_tile=2 * z_tiles, n_tiles=SSM_GROUPS,
                           col_offset=d_inner, batch=batch, seq_len=seq_len, tr=4096, transposed=True)
    c_nat = conv_silu(zx, conv_w[0], conv_b[0], width=SSM_STATE, first_tile=2 * z_tiles + bc_tiles,
                      n_tiles=SSM_GROUPS, col_offset=d_inner + SSM_GROUPS * SSM_STATE,
                      batch=batch, seq_len=seq_len, tr=4096)
    L = SSM_CHUNK
    rows = SSM_GROUPS * 2 * hpg
    dt_rows = dt_raw.reshape(n // L, L, 2, SSM_GROUPS, hpg).transpose(0, 3, 2, 4, 1).reshape(n // L, rows, L)

    def per_row(p):
        return p.reshape(2, SSM_GROUPS, hpg).transpose(1, 0, 2).reshape(rows, 1)

    d_skip_rows = jnp.repeat(d_skip[0].reshape(SSM_GROUPS, 1, hpg), SSM_HEAD_DIM, axis=2)
    seg_rows, dt_act_rows = ssd_decay_rows(dt_rows, per_row(dt_bias[0]), per_row(a_log[0]), hpg=hpg)
    y_f, y_b = ssd(xs, b_nat, b_t, c_nat, seg_rows, dt_act_rows, d_skip_rows, batch=batch, seq_len=seq_len)
    return mamba_out(y_f, y_b, zx, g_gate[0], wb["w_out"], x)


def kernel(x_prompt, x_sample, g_mix, g_ffn, w_qkv, rpb, w_o, w_ff_gate, w_ff_up, w_ff_down, w_in_c, conv_w, conv_b,
           dt_bias, a_log, d_skip, g_gate, w_out_c, w_router, w_e_gate, w_e_up, w_e_down, g_final):
    wb = _prepare_weights(w_qkv, w_o, w_ff_gate, w_ff_up, w_ff_down, w_in_c, w_out_c)
    args = (wb, g_mix, g_ffn, rpb, conv_w, conv_b, dt_bias, a_log, d_skip, g_gate)
    x_p = _mixer_layers(x_prompt, *args)
    x_s = _mixer_layers(x_sample, *args)

    n_p = x_p.shape[0]
    routed, xn = router(x_p, x_s, g_ffn[1], w_router[0])
    slot, tok_of_slot, blk_expert, n_used = moe_plan(routed, tm=MOE_TM)
    ys = expert_ffn(blk_expert, n_used, tok_of_slot, xn, w_e_gate[0].astype(BF16), w_e_up[0].astype(BF16),
                    w_e_down[0].astype(BF16))
    out_p = combine_norm(slot[:TOP_K * n_p], ys, x_p, routed[:n_p], g_final)
    out_s = combine_norm(slot[TOP_K * n_p:], ys, x_s, routed[n_p:], g_final)
    return out_p.reshape(x_prompt.shape), out_s.reshape(x_sample.shape)
```

```python
import functools
import math

import jax
import jax.numpy as jnp
import numpy as np
from jax import lax
from jax.experimental import pallas as pl
from jax.experimental.pallas import tpu as pltpu

F32 = jnp.float32
BF16 = jnp.bfloat16
EPS = 1e-6
NEG_INF = float("-inf")

GRID_W = 64
HEAD_DIM = 128
N_HEADS_A = 4
N_HEADS_B_GROUP = 4
DILATIONS = (1, 4, 16)
BAND_RADIUS = 64
N_HEADS_QKV = N_HEADS_A + N_HEADS_B_GROUP * len(DILATIONS)
WIN_H = 8
WIN_W = 16
ROPE_THETA = 10000.0
SSM_HEAD_DIM = 64
SSM_GROUPS = 8
SSM_STATE = 128
SSM_CONV = 5
SSM_CHUNK = 128
N_EXPERTS = 8
TOP_K = 2

V7X_VMEM_BYTES = 64 * 1024 * 1024
LANES = 128
BF16_SUBLANES = 16

NAT_ROWS = 8
DIL_TQ = 512
DIL_KC = (256, 256, 512)
DIL_PARTS = 4
MOE_TM = 512
GATHER_ROWS = 256
FF_TILE = 512
QKV_TILE = 8 * HEAD_DIM


def _cparams(semantics, vmem_estimate):
    limit = int(min(max(2 * vmem_estimate, 32 * 1024 * 1024), V7X_VMEM_BYTES - 8 * 1024 * 1024))
    return pltpu.CompilerParams(dimension_semantics=semantics, vmem_limit_bytes=limit)


def _rmsnorm(x, g):
    return x * lax.rsqrt(jnp.mean(x * x, axis=-1, keepdims=True) + EPS) * g


def _sigmoid(x):
    return 1.0 / (1.0 + jnp.exp(-x))


def _softplus(x):
    return jnp.maximum(x, 0.0) + jnp.log(1.0 + jnp.exp(-jnp.abs(x)))


def _dot(a, b):
    return jnp.dot(a, b, preferred_element_type=F32)


def _dot_nt(a, b):
    return lax.dot_general(a, b, (((1,), (1,)), ((), ())), preferred_element_type=F32)


def _norm_matmul_kernel(x_ref, g_ref, w_ref, o_ref, xn_ref):
    @pl.when(pl.program_id(1) == 0)
    def _():
        xn_ref[...] = _rmsnorm(x_ref[...], g_ref[...]).astype(BF16)

    r = _dot(xn_ref[...], w_ref[...])
    width = o_ref.shape[-1]
    for t in range(o_ref.shape[0]):
        o_ref[t] = r[:, t * width:(t + 1) * width].astype(o_ref.dtype)


def column_tiles(w, tn):
    *lead, k, m = w.shape
    w = w.astype(BF16).reshape(*lead, k, m // tn, tn)
    return jnp.swapaxes(w, -3, -2)


def norm_matmul(x, g, w_tiles, *, out_width, out_dtype, tm=1024):
    n, k = x.shape
    n_tiles, _, tn = w_tiles.shape
    m = n_tiles * tn
    per = tn // out_width
    tm = min(tm, n)
    est = 2 * tm * k * 4 + tm * k * 2 + 2 * k * tn * 2 + 3 * tm * tn * 4
    return pl.pallas_call(
        _norm_matmul_kernel,
        grid=(n // tm, m // tn),
        in_specs=[
            pl.BlockSpec((tm, k), lambda i, j: (i, 0)),
            pl.BlockSpec((1, k), lambda i, j: (0, 0)),
            pl.BlockSpec((None, k, tn), lambda i, j: (j, 0, 0)),
        ],
        out_specs=pl.BlockSpec((per, tm, out_width), lambda i, j: (j, i, 0)),
        out_shape=jax.ShapeDtypeStruct((m // out_width, n, out_width), out_dtype),
        scratch_shapes=[pltpu.VMEM((tm, k), BF16)],
        compiler_params=_cparams(("parallel", "arbitrary"), est),
        name="norm_matmul",
    )(x, g.reshape(1, k), w_tiles)


def _qkv_kernel(x_ref, g_ref, w_ref, cos_ref, sin_ref, o_ref, xn_ref, *, heads_per_tile, scale):
    j = pl.program_id(1)
    tiles_per_part = N_HEADS_QKV // heads_per_tile

    @pl.when(j == 0)
    def _():
        xn_ref[...] = _rmsnorm(x_ref[...], g_ref[...]).astype(BF16)

    mult = jnp.where(j < tiles_per_part, scale, 1.0)
    first_head = (j % tiles_per_part) * heads_per_tile
    xn = xn_ref[...]
    for pair in range(heads_per_tile // 2):
        use_rope = jnp.logical_and(j < 2 * tiles_per_part, first_head + 2 * pair >= N_HEADS_A)
        c = jnp.where(use_rope, cos_ref[...], 1.0) * mult
        s = jnp.where(use_rope, sin_ref[...], 0.0) * mult
        r = _dot(xn, w_ref[:, pair * 2 * HEAD_DIM:(pair + 1) * 2 * HEAD_DIM])
        for k in range(2):
            p = r[:, k * HEAD_DIM:(k + 1) * HEAD_DIM]
            o_ref[2 * pair + k] = (p * c + pltpu.roll(p, HEAD_DIM // 2, 1) * s).astype(o_ref.dtype)


def qkv_proj(x, g, w_tiles, cos, sin, *, seq_len, tm=1024):
    n, k = x.shape
    n_tiles, _, tn = w_tiles.shape
    hpt = tn // HEAD_DIM
    assert N_HEADS_QKV % hpt == 0 and N_HEADS_A % 2 == 0 and hpt % 2 == 0
    assert tn == hpt * HEAD_DIM
    m = n_tiles * tn
    tm = min(tm, seq_len)
    tiles_per_seq = seq_len // tm
    est = 2 * tm * k * 4 + tm * k * 2 + 2 * k * tn * 2 + 2 * tm * tn * 2 + 4 * tm * HEAD_DIM * 4 + tm * tn * 4
    return pl.pallas_call(
        functools.partial(_qkv_kernel, heads_per_tile=hpt, scale=HEAD_DIM ** -0.5),
        grid=(n // tm, m // tn),
        in_specs=[
            pl.BlockSpec((tm, k), lambda i, j: (i, 0)),
            pl.BlockSpec((1, k), lambda i, j: (0, 0)),
            pl.BlockSpec((None, k, tn), lambda i, j: (j, 0, 0)),
            pl.BlockSpec((tm, HEAD_DIM), lambda i, j: (i % tiles_per_seq, 0)),
            pl.BlockSpec((tm, HEAD_DIM), lambda i, j: (i % tiles_per_seq, 0)),
        ],
        out_specs=pl.BlockSpec((hpt, tm, HEAD_DIM), lambda i, j: (j, i, 0)),
        out_shape=jax.ShapeDtypeStruct((m // HEAD_DIM, n, HEAD_DIM), BF16),
        scratch_shapes=[pltpu.VMEM((tm, k), BF16)],
        compiler_params=_cparams(("parallel", "arbitrary"), est),
        name="qkv_proj",
    )(x, g.reshape(1, k), w_tiles, cos, sin)


def rope_tables(seq_len):
    half = HEAD_DIM // 2
    inv = ROPE_THETA ** (-jnp.arange(half, dtype=F32) / half)
    ang = jnp.arange(seq_len, dtype=F32)[:, None] * inv[None, :]
    cos = jnp.cos(ang)
    sin = jnp.sin(ang)
    return jnp.concatenate([cos, cos], axis=1), jnp.concatenate([-sin, sin], axis=1)


def _natten_kernel(q_ref, k_ref, v_ref, bias_ref, o_ref, *, grid_rows):
    blk = pl.program_id(2)
    tk = WIN_H * GRID_W
    starts, scores, probs = [], [], []
    for ri in range(NAT_ROWS):
        first_row = jnp.clip(blk * NAT_ROWS + ri - WIN_H // 2, 0, grid_rows - WIN_H)
        starts.append(pl.multiple_of(first_row * GRID_W, GRID_W))
        scores.append(_dot_nt(q_ref[ri * GRID_W:(ri + 1) * GRID_W, :], k_ref[pl.ds(starts[ri], tk), :]) + bias_ref[ri])
    for ri in range(NAT_ROWS):
        p = jnp.exp(scores[ri] - jnp.max(scores[ri], axis=-1, keepdims=True))
        probs.append((p.astype(BF16), jnp.sum(p, axis=-1, keepdims=True)))
    for ri in range(NAT_ROWS):
        p, l = probs[ri]
        o_ref[ri * GRID_W:(ri + 1) * GRID_W, :] = (_dot(p, v_ref[pl.ds(starts[ri], tk), :]) / l).astype(o_ref.dtype)


def natten_bias_tiles(rpb, grid_rows):
    n_blocks = grid_rows // NAT_ROWS
    r0s = np.array([0, NAT_ROWS, (n_blocks - 1) * NAT_ROWS])
    r = r0s[:, None] + np.arange(NAT_ROWS)[None, :]
    rs = np.clip(r - WIN_H // 2, 0, grid_rows - WIN_H)
    d_row = rs[:, :, None] - r[:, :, None] + (WIN_H - 1) + np.arange(WIN_H)[None, None, :]
    c = np.arange(GRID_W)
    cs = np.clip(c - WIN_W // 2, 0, GRID_W - WIN_W)
    col_ok = (c[None, :] >= cs[:, None]) & (c[None, :] < cs[:, None] + WIN_W)
    d_col = np.clip(c[None, :] - c[:, None] + (WIN_W - 1), 0, 2 * WIN_W - 2)
    sel_row = (d_row[..., None] == np.arange(2 * WIN_H - 1)).astype(np.float32)
    sel_col = (d_col[..., None] == np.arange(2 * WIN_W - 1)).astype(np.float32)
    rows = jnp.einsum("tikr,hrc->thikc", sel_row, rpb.astype(F32), precision=lax.Precision.HIGHEST)
    bias = jnp.einsum("thikc,qwc->thiqkw", rows, sel_col, precision=lax.Precision.HIGHEST)
    bias = jnp.where(col_ok[None, None, None, :, None, :], bias, NEG_INF)
    return bias.reshape(3, rpb.shape[0], NAT_ROWS, GRID_W, WIN_H * GRID_W)


def natten(qkv, bias_tiles, *, batch, seq_len):
    n = batch * seq_len
    grid_rows = seq_len // GRID_W
    n_blocks = grid_rows // NAT_ROWS
    tq = NAT_ROWS * GRID_W
    tk = WIN_H * GRID_W

    def tile_kind(blk):
        return jnp.where(blk == 0, 0, jnp.where(blk == n_blocks - 1, 2, 1))

    est = 4 * seq_len * HEAD_DIM * 2 + 2 * tq * tk * 4 + 6 * tq * tk * 4
    return pl.pallas_call(
        functools.partial(_natten_kernel, grid_rows=grid_rows),
        grid=(batch, N_HEADS_A, n_blocks),
        in_specs=[
            pl.BlockSpec((None, tq, HEAD_DIM), lambda b, h, i: (h, b * n_blocks + i, 0)),
            pl.BlockSpec((None, seq_len, HEAD_DIM), lambda b, h, i: (N_HEADS_QKV + h, b, 0)),
            pl.BlockSpec((None, seq_len, HEAD_DIM), lambda b, h, i: (2 * N_HEADS_QKV + h, b, 0)),
            pl.BlockSpec((None, None, NAT_ROWS, GRID_W, tk), lambda b, h, i: (tile_kind(i), h, 0, 0, 0)),
        ],
        out_specs=pl.BlockSpec((tq, HEAD_DIM), lambda b, h, i: (b * n_blocks + i, h)),
        out_shape=jax.ShapeDtypeStruct((n, N_HEADS_A * HEAD_DIM), BF16),
        compiler_params=_cparams(("parallel", "parallel", "arbitrary"), est),
        name="natten",
    )(qkv, qkv, qkv, bias_tiles)


class _DilatedGeometry:
    def __init__(self, dil, kc):
        self.dil, self.kc = dil, kc
        self.reach = BAND_RADIUS * dil
        self.tp = DIL_TQ // DIL_PARTS
        assert kc % self.tp == 0
        self.halo = -(-self.reach // kc) * kc
        self.n_chunks = (DIL_TQ + 2 * self.halo) // kc
        self.n_tiles = (kc // self.tp) * (self.n_chunks - 1) + DIL_PARTS

    def tile_index(self, chunk, part):
        return (self.kc // self.tp) * chunk + (DIL_PARTS - 1 - part)

    def _allowed(self, u):
        rel = np.arange(self.kc)[None, :] - np.arange(self.tp)[:, None] + (u - (DIL_PARTS - 1)) * self.tp - self.halo
        return (np.abs(rel) <= self.reach) & (rel % self.dil == 0)

    def is_active(self, chunk, part):
        return bool(self._allowed(self.tile_index(chunk, part)).any())

    def bias_tiles(self):
        tiles = [np.where(self._allowed(u), 0.0, -np.inf) for u in range(self.n_tiles)]
        tiles.append(np.full((self.tp, self.kc), -np.inf))
        return np.stack(tiles).astype(np.float32)


DIL_GEOMETRY = tuple(_DilatedGeometry(d, kc) for d, kc in zip(DILATIONS, DIL_KC))


def _dilated_kernel(b0, b1, b2, q0, q1, q2, k0, k1, k2, v0, v1, v2, o_ref, *, seq_len):
    tq = DIL_TQ
    parts = DIL_PARTS
    tp = tq // parts
    t0 = pl.program_id(2) * tq
    carry = [(jnp.full((tp, 1), -1e30, F32), jnp.zeros((tp, 1), F32), jnp.zeros((tp, HEAD_DIM), F32))
             for _ in range(parts)]
    for geo, bias_ref, q_ref, k_ref, v_ref in zip(DIL_GEOMETRY, (b0, b1, b2), (q0, q1, q2), (k0, k1, k2), (v0, v1, v2)):
        kc = geo.kc
        qs = tuple(q_ref[p * tp:(p + 1) * tp, :] for p in range(parts))
        for ci in range(geo.n_chunks):
            active = [p for p in range(parts) if geo.is_active(ci, p)]
            start = t0 - geo.halo + ci * kc
            in_range = jnp.logical_and(start >= 0, start + kc <= seq_len)
            ks = pl.multiple_of(jnp.clip(start, 0, seq_len - kc), kc)
            kk = k_ref[pl.ds(ks, kc), :]
            vv = v_ref[pl.ds(ks, kc), :]
            scores = {p: _dot_nt(qs[p], kk) + bias_ref[jnp.where(in_range, geo.tile_index(ci, p), geo.n_tiles)]
                      for p in active}
            stats = {}
            for p in active:
                m, l, _ = carry[p]
                m_new = jnp.maximum(m, jnp.max(scores[p], axis=-1, keepdims=True))
                alpha = jnp.exp(m - m_new)
                e = jnp.exp(scores[p] - m_new)
                stats[p] = (m_new, alpha, alpha * l + jnp.sum(e, axis=-1, keepdims=True), e.astype(BF16))
            for p in active:
                m_new, alpha, l, e = stats[p]
                carry[p] = (m_new, l, alpha * carry[p][2] + _dot(e, vv))
    for p in range(parts):
        _, l, acc = carry[p]
        o_ref[p * tp:(p + 1) * tp, :] = (acc / l).astype(o_ref.dtype)


def dilated_attention(qkv, *, batch, seq_len):
    n = batch * seq_len
    nq = seq_len // DIL_TQ
    biases = [geo.bias_tiles() for geo in DIL_GEOMETRY]

    def q_spec(g):
        return pl.BlockSpec((None, DIL_TQ, HEAD_DIM),
                            lambda b, j, i: (N_HEADS_A + N_HEADS_B_GROUP * g + j, b * nq + i, 0))

    def kv_spec(part, g):
        return pl.BlockSpec((None, seq_len, HEAD_DIM),
                            lambda b, j, i: (part * N_HEADS_QKV + N_HEADS_A + N_HEADS_B_GROUP * g + j, b, 0))

    est = 12 * seq_len * HEAD_DIM * 2 + 8 * DIL_TQ * max(DIL_KC) * 4 + 2 * sum(b.size for b in biases) * 4
    groups = range(len(DILATIONS))
    return pl.pallas_call(
        functools.partial(_dilated_kernel, seq_len=seq_len),
        grid=(batch, N_HEADS_B_GROUP, nq),
        in_specs=[pl.BlockSpec(b.shape, lambda b_, j, i: (0, 0, 0)) for b in biases]
        + [q_spec(g) for g in groups] + [kv_spec(1, g) for g in groups] + [kv_spec(2, g) for g in groups],
        out_specs=pl.BlockSpec((DIL_TQ, HEAD_DIM), lambda b, j, i: (b * nq + i, j)),
        out_shape=jax.ShapeDtypeStruct((n, N_HEADS_B_GROUP * HEAD_DIM), BF16),
        compiler_params=_cparams(("parallel", "parallel", "arbitrary"), est),
        name="dilated_attention",
    )(*biases, *([qkv] * 9))


def _attn_out_kernel(x_ref, oa_ref, ob_ref, w_ref, o_ref):
    ka = oa_ref.shape[1]
    o_ref[...] = x_ref[...] + _dot(oa_ref[...], w_ref[:ka, :]) + _dot(ob_ref[...], w_ref[ka:, :])


def attn_out(x, o_a, o_b, w, *, tm=512):
    n, d = x.shape
    ka, kb = o_a.shape[1], o_b.shape[1]
    tm = min(tm, n)
    est = 4 * tm * d * 4 + 2 * (ka + kb) * d * 2 + 2 * tm * (ka + kb) * 2
    return pl.pallas_call(
        _attn_out_kernel,
        grid=(n // tm,),
        in_specs=[
            pl.BlockSpec((tm, d), lambda i: (i, 0)),
            pl.BlockSpec((tm, ka), lambda i: (i, 0)),
            pl.BlockSpec((tm, kb), lambda i: (i, 0)),
            pl.BlockSpec((ka + kb, d), lambda i: (0, 0)),
        ],
        out_specs=pl.BlockSpec((tm, d), lambda i: (i, 0)),
        out_shape=jax.ShapeDtypeStruct((n, d), F32),
        compiler_params=_cparams(("parallel",), est),
        name="attn_out",
    )(x, o_a, o_b, w)


def _ffn_kernel(x_ref, g_ref, wg_ref, wu_ref, wd_ref, o_ref, xn_ref):
    f = pl.program_id(1)

    @pl.when(f == 0)
    def _():
        xn_ref[...] = _rmsnorm(x_ref[...], g_ref[...]).astype(BF16)
        o_ref[...] = x_ref[...]

    xn = xn_ref[...]
    gate = _dot(xn, wg_ref[...])
    up = _dot(xn, wu_ref[...])
    hid = (gate * _sigmoid(gate) * up).astype(BF16)
    o_ref[...] += _dot(hid, wd_ref[...])


def ffn(x, g, wg_tiles, wu_tiles, wd, *, tm=512):
    n, d = x.shape
    nf, _, tf = wg_tiles.shape
    tm = min(tm, n)
    est = 4 * tm * d * 4 + tm * d * 2 + 6 * d * tf * 2 + 3 * tm * tf * 4
    return pl.pallas_call(
        _ffn_kernel,
        grid=(n // tm, nf),
        in_specs=[
            pl.BlockSpec((tm, d), lambda i, f: (i, 0)),
            pl.BlockSpec((1, d), lambda i, f: (0, 0)),
            pl.BlockSpec((None, d, tf), lambda i, f: (f, 0, 0)),
            pl.BlockSpec((None, d, tf), lambda i, f: (f, 0, 0)),
            pl.BlockSpec((tf, d), lambda i, f: (f, 0)),
        ],
        out_specs=pl.BlockSpec((tm, d), lambda i, f: (i, 0)),
        out_shape=jax.ShapeDtypeStruct((n, d), F32),
        scratch_shapes=[pltpu.VMEM((tm, d), BF16)],
        compiler_params=_cparams(("parallel", "arbitrary"), est),
        name="ffn",
    )(x, g.reshape(1, d), wg_tiles, wu_tiles, wd)


def _conv_kernel(xm_ref, xp_ref, xn_ref, w_ref, b_ref, o_ref, *rest, tr, n_row_blocks):
    ext_ref = rest[-1]
    i = pl.program_id(2)
    hb = BF16_SUBLANES
    pad = SSM_CONV // 2
    ext_ref[0:hb, :] = jnp.where(i > 0, xp_ref[...].astype(F32), 0.0)
    ext_ref[hb:hb + tr, :] = xm_ref[...].astype(F32)
    ext_ref[hb + tr:2 * hb + tr, :] = jnp.where(i < n_row_blocks - 1, xn_ref[...].astype(F32), 0.0)
    ext = ext_ref[...]
    n_ext = ext.shape[0]
    acc = jnp.broadcast_to(b_ref[...], o_ref.shape)
    for k in range(SSM_CONV):
        shifted = ext if k == pad else pltpu.roll(ext, (pad - k) % n_ext, 0)
        acc = acc + shifted[hb:hb + tr, :] * w_ref[k:k + 1, :]
    out = acc * _sigmoid(acc)
    o_ref[...] = out.astype(o_ref.dtype)
    if len(rest) == 2:
        ot_ref = rest[0]
        L = ot_ref.shape[-1]
        for c in range(tr // L):
            ot_ref[c] = out[c * L:(c + 1) * L, :].T.astype(ot_ref.dtype)


def conv_silu(zx, conv_w, conv_b, *, width, first_tile, n_tiles, col_offset, batch, seq_len, tr, transposed=False):
    n = batch * seq_len
    tile_w = zx.shape[2]
    per = tile_w // width
    tr = min(tr, seq_len)
    nr = seq_len // tr
    hb = BF16_SUBLANES
    seq_hb = seq_len // hb
    n_hb = n // hb

    def main_map(c, b, i):
        return (first_tile + c // per, b * nr + i, c % per)

    def prev_map(c, b, i):
        return (first_tile + c // per, jnp.maximum(b * seq_hb + i * (tr // hb) - 1, 0), c % per)

    def next_map(c, b, i):
        return (first_tile + c // per, jnp.minimum(b * seq_hb + (i + 1) * (tr // hb), n_hb - 1), c % per)

    L = SSM_CHUNK
    out_specs = [pl.BlockSpec((None, tr, width), lambda c, b, i: (c, b * nr + i, 0))]
    out_shape = [jax.ShapeDtypeStruct((n_tiles, n, width), BF16)]
    if transposed:
        out_specs.append(pl.BlockSpec((None, tr // L, width, L), lambda c, b, i: (c, b * nr + i, 0, 0)))
        out_shape.append(jax.ShapeDtypeStruct((n_tiles, n // L, width, L), BF16))
    est = 4 * tr * width * 2 + (tr + 2 * hb) * width * 4 + 4 * tr * width * 4 + 4 * tr * width * 2
    outs = pl.pallas_call(
        functools.partial(_conv_kernel, tr=tr, n_row_blocks=nr),
        grid=(n_tiles, batch, nr),
        in_specs=[
            pl.BlockSpec((None, tr, width), main_map),
            pl.BlockSpec((None, hb, width), prev_map),
            pl.BlockSpec((None, hb, width), next_map),
            pl.BlockSpec((SSM_CONV, width), lambda c, b, i: (0, col_offset // width + c)),
            pl.BlockSpec((1, width), lambda c, b, i: (0, col_offset // width + c)),
        ],
        out_specs=out_specs,
        out_shape=out_shape,
        scratch_shapes=[pltpu.VMEM((tr + 2 * hb, width), F32)],
        compiler_params=_cparams(("parallel", "parallel", "arbitrary"), est),
        name="conv_silu",
    )(zx, zx, zx, conv_w, conv_b.reshape(1, -1))
    return outs if transposed else outs[0]


def _lane_cumsum(a):
    lane = lax.broadcasted_iota(jnp.int32, a.shape, 1)
    shift = 1
    while shift < a.shape[1]:
        a = a + jnp.where(lane >= shift, pltpu.roll(a, shift, 1), 0.0)
        shift *= 2
    return a


def _ssd_decay_kernel(dt_ref, dtbias_ref, alog_ref, seg_ref, dtact_ref, *, hpg):
    L = SSM_CHUNK
    chunks, rows, _ = dt_ref.shape
    dt_act = _softplus(dt_ref[...] + dtbias_ref[...])
    dt_all = dt_act.reshape(chunks * rows, L)
    a_all = (dt_act * (-jnp.exp(alog_ref[...]))).reshape(chunks * rows, L)
    cum = _lane_cumsum(a_all)
    suf = cum[:, L - 1:L] - cum + a_all
    row = lax.broadcasted_iota(jnp.int32, cum.shape, 0)
    seg_ref[...] = jnp.where((row & (2 * hpg - 1)) < hpg, cum, suf).reshape(chunks, rows, L)
    dtact_ref[...] = dt_all.reshape(chunks, rows, L)


def ssd_decay_rows(dt_rows, dt_bias_rows, a_log_rows, *, hpg, chunks_per_step=8):
    n_chunks, rows, L = dt_rows.shape
    assert hpg & (hpg - 1) == 0
    cps = math.gcd(chunks_per_step, n_chunks)
    spec = pl.BlockSpec((cps, rows, L), lambda i: (i, 0, 0))
    const = pl.BlockSpec((rows, 1), lambda i: (0, 0))
    shape = jax.ShapeDtypeStruct(dt_rows.shape, F32)
    return pl.pallas_call(
        functools.partial(_ssd_decay_kernel, hpg=hpg),
        grid=(n_chunks // cps,),
        in_specs=[spec, const, const],
        out_specs=[spec, spec],
        out_shape=[shape, shape],
        compiler_params=_cparams(("parallel",), 16 * cps * rows * L * 4),
        name="ssd_decay_rows",
    )(dt_rows, dt_bias_rows, a_log_rows)


def _ssd_kernel(xa_ref, ba_ref, bta_ref, ca_ref, sega_ref, dta_ref, xb_ref, btb_ref, cb_ref, segb_ref, dtb_ref,
                dskip_ref, yf_ref, yb_ref, sf_ref, sb_ref, *, hpg):
    L = SSM_CHUNK
    P = SSM_HEAD_DIM
    pairs = hpg // 2

    @pl.when(pl.program_id(1) == 0)
    def _():
        sf_ref[...] = jnp.zeros_like(sf_ref)
        sb_ref[...] = jnp.zeros_like(sb_ref)

    li = lax.broadcasted_iota(jnp.int32, (L, L), 0)
    si = lax.broadcasted_iota(jnp.int32, (L, L), 1)
    causal = li >= si
    anti = li <= si
    low_lanes = lax.broadcasted_iota(jnp.int32, (L, 2 * P), 1) < P
    high_lanes = jnp.logical_not(low_lanes)
    low_lanes_row = lax.broadcasted_iota(jnp.int32, (1, 2 * P), 1) < P

    def lanes_of(col):
        return jnp.broadcast_to(col, (L, 2 * P))

    def columns(seg_r, dt_r):
        stacked = jnp.concatenate([seg_r, dt_r, jnp.zeros((L - 4 * hpg, L), F32)], axis=0)
        return stacked.T

    spread_bwd = (lax.broadcasted_iota(jnp.int32, (L, hpg * P), 0)
                  == hpg + lax.broadcasted_iota(jnp.int32, (L, hpg * P), 1) // P).astype(BF16)

    def advance_state(s_ref, g, lanes, state, btf, xm0, xm1, seg_r, dt_r, h0, h1, tot_lane):
        acc = None
        decays = []
        for h, xm in ((h0, xm0), (h1, xm1)):
            tot = seg_r[h:h + 1, tot_lane:tot_lane + 1]
            coef = jnp.exp(tot - seg_r[h:h + 1, :]) * dt_r[h:h + 1, :]
            part = _dot((btf * coef).astype(BF16), xm)
            acc = part if acc is None else acc + part
            decays.append(jnp.exp(tot))
        decay = jnp.where(low_lanes_row, decays[0], decays[1])
        s_ref[g, :, lanes] = state[:, lanes] * decay + acc

    def group_body(g, c):
        r0 = pl.multiple_of(g * 2 * hpg, 2 * hpg)

        seg_r = sega_ref[pl.ds(r0, 2 * hpg), :]
        dt_r = dta_ref[pl.ds(r0, 2 * hpg), :]
        cols = columns(seg_r, dt_r)
        xg = xa_ref[g]
        cg = ca_ref[g]
        state = sf_ref[g]
        carried = _dot(cg, state.astype(BF16))
        cbm = _dot_nt(cg, ba_ref[g])
        btf = bta_ref[g].astype(F32)
        for p in range(pairs):
            lanes = slice(p * 2 * P, (p + 1) * 2 * P)
            xpf = xg[:, lanes].astype(F32)
            xm0 = jnp.where(low_lanes, xpf, 0.0).astype(BF16)
            xm1 = jnp.where(high_lanes, xpf, 0.0).astype(BF16)
            h0, h1 = 2 * p, 2 * p + 1
            f0, f1 = lanes_of(cols[:, h0:h0 + 1]), lanes_of(cols[:, h1:h1 + 1])
            y = carried[:, lanes] * jnp.where(low_lanes, jnp.exp(f0), jnp.exp(f1)) + dskip_ref[g][:, lanes] * xpf
            for hh, f_cols, xm in ((h0, f0, xm0), (h1, f1, xm1)):
                wf = jnp.exp(jnp.where(causal, f_cols - seg_r[hh:hh + 1, :], NEG_INF)) * dt_r[hh:hh + 1, :]
                b_cols = lanes_of(cols[:, hpg + hh:hpg + hh + 1])
                wb = jnp.exp(jnp.where(anti, b_cols - seg_r[hpg + hh:hpg + hh + 1, :], NEG_INF)) \
                    * dt_r[hpg + hh:hpg + hh + 1, :]
                y = y + _dot((cbm * (wf + wb)).astype(BF16), xm)
            yf_ref[g, :, lanes] = y.astype(yf_ref.dtype)
            advance_state(sf_ref, g, lanes, state, btf, xm0, xm1, seg_r, dt_r, h0, h1, L - 1)

        seg_r = segb_ref[pl.ds(r0, 2 * hpg), :]
        dt_r = dtb_ref[pl.ds(r0, 2 * hpg), :]
        xg = xb_ref[g]
        state = sb_ref[g]
        grow = _dot(jnp.exp(columns(seg_r, dt_r)).astype(BF16), spread_bwd)
        y = _dot(cb_ref[g], state.astype(BF16)) * grow
        yb_ref[g] = y.astype(yb_ref.dtype)
        btf = btb_ref[g].astype(F32)
        for p in range(pairs):
            lanes = slice(p * 2 * P, (p + 1) * 2 * P)
            xpf = xg[:, lanes].astype(F32)
            xm0 = jnp.where(low_lanes, xpf, 0.0).astype(BF16)
            xm1 = jnp.where(high_lanes, xpf, 0.0).astype(BF16)
            advance_state(sb_ref, g, lanes, state, btf, xm0, xm1, seg_r, dt_r, hpg + 2 * p, hpg + 2 * p + 1, 0)
        return c

    lax.fori_loop(0, SSM_GROUPS, group_body, 0, unroll=4)


def ssd(xs, b_nat, b_t, c_nat, seg_rows, dt_act_rows, d_skip_rows, *, batch, seq_len):
    groups, n, gw = xs.shape
    hpg = gw // SSM_HEAD_DIM
    assert hpg & (hpg - 1) == 0
    L = SSM_CHUNK
    nc = seq_len // L
    rows = groups * 2 * hpg

    def fwd(b, k):
        return b * nc + k

    def bwd(b, k):
        return b * nc + nc - 1 - k

    def specs(chunk, with_b_nat):
        out = [pl.BlockSpec((groups, L, gw), lambda b, k: (0, chunk(b, k), 0))]
        if with_b_nat:
            out.append(pl.BlockSpec((groups, L, SSM_STATE), lambda b, k: (0, chunk(b, k), 0)))
        out += [
            pl.BlockSpec((groups, None, SSM_STATE, L), lambda b, k: (0, chunk(b, k), 0, 0)),
            pl.BlockSpec((groups, L, SSM_STATE), lambda b, k: (0, chunk(b, k), 0)),
            pl.BlockSpec((None, rows, L), lambda b, k: (chunk(b, k), 0, 0)),
            pl.BlockSpec((None, rows, L), lambda b, k: (chunk(b, k), 0, 0)),
        ]
        return out

    const = [pl.BlockSpec((groups, 1, gw), lambda b, k: (0, 0, 0))]
    y_shape = jax.ShapeDtypeStruct((groups, n, gw), BF16)
    est = 4 * (groups * L * gw * 2 + 3 * groups * L * SSM_STATE * 2 + 2 * rows * L * 4) + 4 * groups * L * gw * 2 \
        + 2 * groups * SSM_STATE * gw * 4 + 64 * L * L * 4
    return pl.pallas_call(
        functools.partial(_ssd_kernel, hpg=hpg),
        grid=(batch, nc),
        in_specs=specs(fwd, True) + specs(bwd, False) + const,
        out_specs=[pl.BlockSpec((groups, L, gw), lambda b, k: (0, fwd(b, k), 0)),
                   pl.BlockSpec((groups, L, gw), lambda b, k: (0, bwd(b, k), 0))],
        out_shape=[y_shape, y_shape],
        scratch_shapes=[pltpu.VMEM((groups, SSM_STATE, gw), F32), pltpu.VMEM((groups, SSM_STATE, gw), F32)],
        compiler_params=_cparams(("parallel", "arbitrary"), est),
        name="ssd",
    )(xs, b_nat, b_t, c_nat, seg_rows, dt_act_rows, xs, b_t, c_nat, seg_rows, dt_act_rows, d_skip_rows)


def _mamba_out_kernel(yf_ref, yb_ref, z_ref, gg_ref, w_ref, x_ref, o_ref, ssq_ref, *, d_inner):
    step = pl.program_id(1)
    groups_per_step, _, gw = z_ref.shape

    @pl.when(step == 0)
    def _():
        o_ref[...] = jnp.zeros_like(o_ref)
        ssq_ref[...] = jnp.zeros_like(ssq_ref)

    gated = []
    for k in range(groups_per_step):
        z = z_ref[k].astype(F32)
        yz = (yf_ref[k].astype(F32) + yb_ref[k].astype(F32)) * (z * _sigmoid(z))
        ssq_ref[...] += jnp.sum(yz * yz, axis=-1, keepdims=True)
        gated.append((yz * gg_ref[:, k * gw:(k + 1) * gw]).astype(BF16))
    o_ref[...] += _dot(jnp.concatenate(gated, axis=1), w_ref[...])

    @pl.when(step == pl.num_programs(1) - 1)
    def _():
        o_ref[...] = x_ref[...] + o_ref[...] * lax.rsqrt(ssq_ref[...] / d_inner + EPS)


def mamba_out(y_f, y_b, zx, g_gate, w_out, x, *, tm=512, groups_per_step=4):
    groups, n, gw = y_f.shape
    d = x.shape[1]
    tm = min(tm, n)
    gps = groups_per_step
    est = gps * (6 * tm * gw * 2 + 2 * gw * d * 2) + 3 * tm * d * 4 + 2 * tm * gps * gw * 4
    return pl.pallas_call(
        functools.partial(_mamba_out_kernel, d_inner=groups * gw),
        grid=(n // tm, groups // gps),
        in_specs=[
            pl.BlockSpec((gps, tm, gw), lambda i, s: (s, i, 0)),
            pl.BlockSpec((gps, tm, gw), lambda i, s: (s, i, 0)),
            pl.BlockSpec((gps, tm, gw), lambda i, s: (s, i, 0)),
            pl.BlockSpec((1, gps * gw), lambda i, s: (0, s)),
            pl.BlockSpec((gps * gw, d), lambda i, s: (s, 0)),
            pl.BlockSpec((tm, d), lambda i, s: (i, 0)),
        ],
        out_specs=pl.BlockSpec((tm, d), lambda i, s: (i, 0)),
        out_shape=jax.ShapeDtypeStruct((n, d), F32),
        scratch_shapes=[pltpu.VMEM((tm, 1), F32)],
        compiler_params=_cparams(("parallel", "arbitrary"), est),
        name="mamba_out",
    )(y_f, y_b, zx, g_gate.reshape(1, -1), w_out, x)


def _router_kernel(xa_ref, xb_ref, g_ref, w_ref, o_ref, xn_ref, *, tiles_a):
    i = pl.program_id(0)

    @pl.when(i < tiles_a)
    def _():
        xn_ref[...] = _rmsnorm(xa_ref[...], g_ref[...])

    @pl.when(i >= tiles_a)
    def _():
        xn_ref[...] = _rmsnorm(xb_ref[...], g_ref[...])

    xn = xn_ref[...]
    logits = jnp.dot(xn, w_ref[...], preferred_element_type=F32, precision=lax.Precision.HIGHEST)
    lane = lax.broadcasted_iota(jnp.int32, logits.shape, 1)
    logits = jnp.where(lane < N_EXPERTS, logits, NEG_INF)
    v1 = jnp.max(logits, axis=-1, keepdims=True)
    i1 = jnp.min(jnp.where(logits == v1, lane, LANES), axis=-1, keepdims=True)
    rest = jnp.where(lane == i1, NEG_INF, logits)
    v2 = jnp.max(rest, axis=-1, keepdims=True)
    i2 = jnp.min(jnp.where(rest == v2, lane, LANES), axis=-1, keepdims=True)
    e2 = jnp.exp(v2 - v1)
    g1 = 1.0 / (1.0 + e2)
    g2 = e2 / (1.0 + e2)
    out = jnp.where(lane == 0, i1.astype(F32),
                    jnp.where(lane == 1, i2.astype(F32),
                              jnp.where(lane == 2, g1, jnp.where(lane == 3, g2, 0.0))))
    o_ref[...] = out


def router(x_a, x_b, g, w_router, *, tm=512):
    (na, d), nb = x_a.shape, x_b.shape[0]
    tm = min(tm, na, nb)
    tiles_a, tiles_b = na // tm, nb // tm
    w = jnp.zeros((d, LANES), F32).at[:, :w_router.shape[1]].set(w_router)
    est = 4 * tm * d * 4 + 2 * d * LANES * 4 + 2 * tm * LANES * 4 + 3 * tm * d * 4
    return pl.pallas_call(
        functools.partial(_router_kernel, tiles_a=tiles_a),
        grid=(tiles_a + tiles_b,),
        in_specs=[
            pl.BlockSpec((tm, d), lambda i: (jnp.minimum(i, tiles_a - 1), 0)),
            pl.BlockSpec((tm, d), lambda i: (jnp.maximum(i - tiles_a, 0), 0)),
            pl.BlockSpec((1, d), lambda i: (0, 0)),
            pl.BlockSpec((d, LANES), lambda i: (0, 0)),
        ],
        out_specs=[pl.BlockSpec((tm, LANES), lambda i: (i, 0)), pl.BlockSpec((tm, d), lambda i: (i, 0))],
        out_shape=[jax.ShapeDtypeStruct((na + nb, LANES), F32), jax.ShapeDtypeStruct((na + nb, d), F32)],
        compiler_params=_cparams(("parallel",), est),
        name="router",
    )(x_a, x_b, g.reshape(1, d), w)


def _row_copy(src_hbm, dst_vmem, sem, src_row, dst_row):
    return pltpu.make_async_copy(src_hbm.at[pl.ds(src_row, 1), :], dst_vmem.at[pl.ds(dst_row, 1), :], sem)


EXPERT_ISSUE_STEPS = 4


def _expert_kernel(blk_e_ref, n_used_ref, tok_ref, x_hbm, wg_ref, wu_ref, wd_ref, o_ref, rows_ref, x_ref, sems):
    i = pl.program_id(0)
    f = pl.program_id(1)
    tm = x_ref.shape[0]
    per_step = tm // EXPERT_ISSUE_STEPS
    n_used = n_used_ref[0]
    slot = i % 2

    def issue(block, dst_slot, first_row, count):
        def body(r, c):
            _row_copy(x_hbm, rows_ref.at[dst_slot], sems.at[dst_slot], tok_ref[block * tm + first_row + r],
                      first_row + r).start()
            return c
        lax.fori_loop(0, count, body, 0, unroll=8)

    @pl.when(jnp.logical_and(i == 0, f == 0))
    def _():
        issue(0, 0, 0, tm)

    @pl.when(jnp.logical_and(f == 0, i < n_used))
    def _():
        def wait(r, c):
            _row_copy(x_hbm, rows_ref.at[slot], sems.at[slot], 0, r).wait()
            return c
        lax.fori_loop(0, tm, wait, 0, unroll=8)
        x_ref[...] = rows_ref[slot].astype(x_ref.dtype)

    @pl.when(f == 0)
    def _():
        o_ref[...] = jnp.zeros_like(o_ref)

    @pl.when(jnp.logical_and(f < EXPERT_ISSUE_STEPS, i + 1 < n_used))
    def _():
        issue(i + 1, 1 - slot, f * per_step, per_step)

    @pl.when(i < n_used)
    def _():
        x = x_ref[...]
        gate = _dot(x, wg_ref[...])
        up = _dot(x, wu_ref[...])
        hid = (gate * _sigmoid(gate) * up).astype(BF16)
        o_ref[...] += _dot(hid, wd_ref[...])


def expert_ffn(blk_expert, n_used, tok_of_slot, x_rows, wg, wu, wd, *, tm=MOE_TM, tf=1024):
    slots = tok_of_slot.shape[0]
    d = x_rows.shape[1]
    nf = wg.shape[2] // tf
    assert nf >= EXPERT_ISSUE_STEPS and tm % (8 * EXPERT_ISSUE_STEPS) == 0
    n_blocks = slots // tm

    def live(i, n_used):
        return jnp.minimum(i, n_used[0] - 1)

    def f_eff(i, f, n_used):
        return jnp.where(i < n_used[0], f, nf - 1)

    est = 2 * tm * d * 4 + tm * d * 2 + 12 * d * tf + 2 * tm * d * 4 + 3 * tm * tf * 4
    return pl.pallas_call(
        _expert_kernel,
        grid_spec=pltpu.PrefetchScalarGridSpec(
            num_scalar_prefetch=3,
            grid=(n_blocks, nf),
            in_specs=[
                pl.BlockSpec(memory_space=pl.ANY),
                pl.BlockSpec((None, d, tf), lambda i, f, be, nu, tok: (be[live(i, nu)], 0, f_eff(i, f, nu))),
                pl.BlockSpec((None, d, tf), lambda i, f, be, nu, tok: (be[live(i, nu)], 0, f_eff(i, f, nu))),
                pl.BlockSpec((None, tf, d), lambda i, f, be, nu, tok: (be[live(i, nu)], f_eff(i, f, nu), 0)),
            ],
            out_specs=pl.BlockSpec((tm, d), lambda i, f, be, nu, tok: (i, 0)),
            scratch_shapes=[pltpu.VMEM((2, tm, d), F32), pltpu.VMEM((tm, d), BF16), pltpu.SemaphoreType.DMA((2,))],
        ),
        out_shape=jax.ShapeDtypeStruct((slots, d), F32),
        compiler_params=_cparams(("arbitrary", "arbitrary"), est),
        name="expert_ffn",
    )(blk_expert, n_used, tok_of_slot, x_rows, wg, wu, wd)


def _combine_kernel(slot_ref, y_hbm, x_ref, r_ref, g_ref, o_ref, buf_ref, sem):
    rows = x_ref.shape[0]
    base = pl.program_id(0) * rows

    def start(r, c):
        for k in range(TOP_K):
            _row_copy(y_hbm, buf_ref.at[k], sem, slot_ref[TOP_K * (base + r) + k], r).start()
        return c

    def wait(r, c):
        for k in range(TOP_K):
            _row_copy(y_hbm, buf_ref.at[k], sem, 0, r).wait()
        return c

    lax.fori_loop(0, rows, start, 0, unroll=8)
    lax.fori_loop(0, rows, wait, 0, unroll=8)
    gates = r_ref[...]
    out = x_ref[...] + gates[:, 2:3] * buf_ref[0] + gates[:, 3:4] * buf_ref[1]
    o_ref[...] = _rmsnorm(out, g_ref[...])


def combine_norm(slot_of_assignment, ys, x, routed, g_final, *, rows=GATHER_ROWS):
    n, d = x.shape
    rows = min(rows, n)
    est = 2 * rows * d * 4 + 4 * rows * d * 4 + 4 * rows * d * 4
    return pl.pallas_call(
        _combine_kernel,
        grid_spec=pltpu.PrefetchScalarGridSpec(
            num_scalar_prefetch=1,
            grid=(n // rows,),
            in_specs=[
                pl.BlockSpec(memory_space=pl.ANY),
                pl.BlockSpec((rows, d), lambda i, s: (i, 0)),
                pl.BlockSpec((rows, LANES), lambda i, s: (i, 0)),
                pl.BlockSpec((1, d), lambda i, s: (0, 0)),
            ],
            out_specs=pl.BlockSpec((rows, d), lambda i, s: (i, 0)),
            scratch_shapes=[pltpu.VMEM((TOP_K, rows, d), F32), pltpu.SemaphoreType.DMA(())],
        ),
        out_shape=jax.ShapeDtypeStruct((n, d), F32),
        compiler_params=_cparams(("arbitrary",), est),
        name="combine_norm",
    )(slot_of_assignment, ys, x, routed, g_final.reshape(1, d))


def moe_plan(routed, *, tm):
    n = routed.shape[0]
    experts = routed[:, :TOP_K].astype(jnp.int32).reshape(-1)
    onehot = (experts[:, None] == jnp.arange(N_EXPERTS)[None, :]).astype(jnp.int32)
    rank = jnp.sum((jnp.cumsum(onehot, axis=0) - onehot) * onehot, axis=1)
    counts = jnp.sum(onehot, axis=0)
    padded = ((counts + tm - 1) // tm) * tm
    ends = jnp.cumsum(padded)
    starts = ends - padded
    slot = (starts[experts] + rank).astype(jnp.int32)
    n_blocks = (n * TOP_K) // tm + N_EXPERTS
    tok = jnp.repeat(jnp.arange(n, dtype=jnp.int32), TOP_K)
    tok_of_slot = jnp.zeros((n_blocks * tm,), jnp.int32).at[slot].set(tok, unique_indices=True,
                                                                      mode="promise_in_bounds")
    blk_expert = jnp.minimum(
        jnp.searchsorted(ends, jnp.arange(n_blocks, dtype=jnp.int32) * tm, side="right"), N_EXPERTS - 1
    ).astype(jnp.int32)
    n_used = (ends[-1] // tm).astype(jnp.int32).reshape(1)
    return slot, tok_of_slot, blk_expert, n_used


def _prepare_weights(w_qkv, w_o, w_ff_gate, w_ff_up, w_ff_down, w_in_c, w_out_c):
    d_inner = w_out_c.shape[1]
    gw = d_inner // SSM_GROUPS
    main_cols = 2 * d_inner + 2 * SSM_GROUPS * SSM_STATE
    return dict(
        w_qkv=column_tiles(w_qkv[0], QKV_TILE), w_o=w_o[0].astype(BF16),
        w_ff_gate=column_tiles(w_ff_gate[0], FF_TILE), w_ff_up=column_tiles(w_ff_up[0], FF_TILE),
        w_ff_down=w_ff_down[0].astype(BF16),
        w_in_main=column_tiles(w_in_c[0][:, :main_cols], 2 * gw),
        w_in_dt=column_tiles(w_in_c[0][:, main_cols:], w_in_c.shape[2] - main_cols),
        w_out=w_out_c[0].astype(BF16),
    )


def _mixer_layers(x3, wb, g_mix, g_ffn, rpb, conv_w, conv_b, dt_bias, a_log, d_skip, g_gate):
    batch, seq_len, d = x3.shape
    n = batch * seq_len
    x = x3.reshape(n, d)

    cos, sin = rope_tables(seq_len)
    qkv = qkv_proj(x, g_mix[0], wb["w_qkv"], cos, sin, seq_len=seq_len)
    bias_tiles = natten_bias_tiles(rpb[0], seq_len // GRID_W)
    o_a = natten(qkv, bias_tiles, batch=batch, seq_len=seq_len)
    o_b = dilated_attention(qkv, batch=batch, seq_len=seq_len)
    x = attn_out(x, o_a, o_b, wb["w_o"])
    x = ffn(x, g_ffn[0], wb["w_ff_gate"], wb["w_ff_up"], wb["w_ff_down"])

    d_inner = wb["w_out"].shape[0]
    gw = d_inner // SSM_GROUPS
    hpg = gw // SSM_HEAD_DIM
    heads = SSM_GROUPS * hpg
    zx = norm_matmul(x, g_mix[1], wb["w_in_main"], out_width=gw, out_dtype=BF16)
    dt_raw = norm_matmul(x, g_mix[1], wb["w_in_dt"], out_width=2 * heads, out_dtype=F32)[0]
    z_tiles = d_inner // gw
    xs = conv_silu(zx, conv_w[0], conv_b[0], width=gw, first_tile=z_tiles, n_tiles=SSM_GROUPS,
                   col_offset=0, batch=batch, seq_len=seq_len, tr=2048)
    bc_tiles = SSM_GROUPS * SSM_STATE // gw
    b_nat, b_t = conv_silu(zx, conv_w[0], conv_b[0], width=SSM_STATE, first_tile=2 * z_tiles, n_tiles=SSM_GROUPS,
                           col_offset=d_inner, batch=batch, seq_len=seq_len, tr=4096, transposed=True)
    c_nat = conv_silu(zx, conv_w[0], conv_b[0], width=SSM_STATE, first_tile=2 * z_tiles + bc_tiles,
                      n_tiles=SSM_GROUPS, col_offset=d_inner + SSM_GROUPS * SSM_STATE,
                      batch=batch, seq_len=seq_len, tr=4096)
    L = SSM_CHUNK
    rows = SSM_GROUPS * 2 * hpg
    dt_rows = dt_raw.reshape(n // L, L, 2, SSM_GROUPS, hpg).transpose(0, 3, 2, 4, 1).reshape(n // L, rows, L)

    def per_row(p):
        return p.reshape(2, SSM_GROUPS, hpg).transpose(1, 0, 2).reshape(rows, 1)

    d_skip_rows = jnp.repeat(d_skip[0].reshape(SSM_GROUPS, 1, hpg), SSM_HEAD_DIM, axis=2)
    seg_rows, dt_act_rows = ssd_decay_rows(dt_rows, per_row(dt_bias[0]), per_row(a_log[0]), hpg=hpg)
    y_f, y_b = ssd(xs, b_nat, b_t, c_nat, seg_rows, dt_act_rows, d_skip_rows, batch=batch, seq_len=seq_len)
    return mamba_out(y_f, y_b, zx, g_gate[0], wb["w_out"], x)


def kernel(x_prompt, x_sample, g_mix, g_ffn, w_qkv, rpb, w_o, w_ff_gate, w_ff_up, w_ff_down, w_in_c, conv_w, conv_b,
           dt_bias, a_log, d_skip, g_gate, w_out_c, w_router, w_e_gate, w_e_up, w_e_down, g_final):
    wb = _prepare_weights(w_qkv, w_o, w_ff_gate, w_ff_up, w_ff_down, w_in_c, w_out_c)
    args = (wb, g_mix, g_ffn, rpb, conv_w, conv_b, dt_bias, a_log, d_skip, g_gate)
    x_p = _mixer_layers(x_prompt, *args)
    x_s = _mixer_layers(x_sample, *args)

    n_p = x_p.shape[0]
    routed, xn = router(x_p, x_s, g_ffn[1], w_router[0])
    slot, tok_of_slot, blk_expert, n_used = moe_plan(routed, tm=MOE_TM)
    ys = expert_ffn(blk_expert, n_used, tok_of_slot, xn, w_e_gate[0].astype(BF16), w_e_up[0].astype(BF16),
                    w_e_down[0].astype(BF16))
    out_p = combine_norm(slot[:TOP_K * n_p], ys, x_p, routed[:n_p], g_final)
    out_s = combine_norm(slot[TOP_K * n_p:], ys, x_s, routed[n_p:], g_final)
    return out_p.reshape(x_prompt.shape), out_s.reshape(x_sample.shape)
```

```python
import functools
import math

import jax
import jax.numpy as jnp
import numpy as np
from jax import lax
from jax.experimental import pallas as pl
from jax.experimental.pallas import tpu as pltpu

F32 = jnp.float32
BF16 = jnp.bfloat16
EPS = 1e-6
NEG_INF = float("-inf")

GRID_W = 64
HEAD_DIM = 128
N_HEADS_A = 4
N_HEADS_B_GROUP = 4
DILATIONS = (1, 4, 16)
BAND_RADIUS = 64
N_HEADS_QKV = N_HEADS_A + N_HEADS_B_GROUP * len(DILATIONS)
WIN_H = 8
WIN_W = 16
ROPE_THETA = 10000.0
SSM_HEAD_DIM = 64
SSM_GROUPS = 8
SSM_STATE = 128
SSM_CONV = 5
SSM_CHUNK = 128
N_EXPERTS = 8
TOP_K = 2

V7X_VMEM_BYTES = 64 * 1024 * 1024
LANES = 128
BF16_SUBLANES = 16

NAT_ROWS = 8
DIL_TQ = 512
DIL_KC = (256, 256, 512)
DIL_PARTS = 4
MOE_TM = 512
GATHER_ROWS = 256
FF_TILE = 512
QKV_TILE = 8 * HEAD_DIM


def _cparams(semantics, vmem_estimate):
    limit = int(min(max(2 * vmem_estimate, 32 * 1024 * 1024), V7X_VMEM_BYTES - 8 * 1024 * 1024))
    return pltpu.CompilerParams(dimension_semantics=semantics, vmem_limit_bytes=limit)


def _rmsnorm(x, g):
    return x * lax.rsqrt(jnp.mean(x * x, axis=-1, keepdims=True) + EPS) * g


def _sigmoid(x):
    return 1.0 / (1.0 + jnp.exp(-x))


def _softplus(x):
    return jnp.maximum(x, 0.0) + jnp.log(1.0 + jnp.exp(-jnp.abs(x)))


def _dot(a, b):
    return jnp.dot(a, b, preferred_element_type=F32)


def _dot_nt(a, b):
    return lax.dot_general(a, b, (((1,), (1,)), ((), ())), preferred_element_type=F32)


def _norm_matmul_kernel(x_ref, g_ref, w_ref, o_ref, xn_ref):
    @pl.when(pl.program_id(1) == 0)
    def _():
        xn_ref[...] = _rmsnorm(x_ref[...], g_ref[...]).astype(BF16)

    r = _dot(xn_ref[...], w_ref[...])
    width = o_ref.shape[-1]
    for t in range(o_ref.shape[0]):
        o_ref[t] = r[:, t * width:(t + 1) * width].astype(o_ref.dtype)


def column_tiles(w, tn):
    *lead, k, m = w.shape
    w = w.astype(BF16).reshape(*lead, k, m // tn, tn)
    return jnp.swapaxes(w, -3, -2)


def norm_matmul(x, g, w_tiles, *, out_width, out_dtype, tm=1024):
    n, k = x.shape
    n_tiles, _, tn = w_tiles.shape
    m = n_tiles * tn
    per = tn // out_width
    tm = min(tm, n)
    est = 2 * tm * k * 4 + tm * k * 2 + 2 * k * tn * 2 + 3 * tm * tn * 4
    return pl.pallas_call(
        _norm_matmul_kernel,
        grid=(n // tm, m // tn),
        in_specs=[
            pl.BlockSpec((tm, k), lambda i, j: (i, 0)),
            pl.BlockSpec((1, k), lambda i, j: (0, 0)),
            pl.BlockSpec((None, k, tn), lambda i, j: (j, 0, 0)),
        ],
        out_specs=pl.BlockSpec((per, tm, out_width), lambda i, j: (j, i, 0)),
        out_shape=jax.ShapeDtypeStruct((m // out_width, n, out_width), out_dtype),
        scratch_shapes=[pltpu.VMEM((tm, k), BF16)],
        compiler_params=_cparams(("parallel", "arbitrary"), est),
        name="norm_matmul",
    )(x, g.reshape(1, k), w_tiles)


def _qkv_kernel(x_ref, g_ref, w_ref, cos_ref, sin_ref, o_ref, xn_ref, *, heads_per_tile, scale):
    j = pl.program_id(1)
    tiles_per_part = N_HEADS_QKV // heads_per_tile

    @pl.when(j == 0)
    def _():
        xn_ref[...] = _rmsnorm(x_ref[...], g_ref[...]).astype(BF16)

    mult = jnp.where(j < tiles_per_part, scale, 1.0)
    first_head = (j % tiles_per_part) * heads_per_tile
    xn = xn_ref[...]
    for pair in range(heads_per_tile // 2):
        use_rope = jnp.logical_and(j < 2 * tiles_per_part, first_head + 2 * pair >= N_HEADS_A)
        c = jnp.where(use_rope, cos_ref[...], 1.0) * mult
        s = jnp.where(use_rope, sin_ref[...], 0.0) * mult
        r = _dot(xn, w_ref[:, pair * 2 * HEAD_DIM:(pair + 1) * 2 * HEAD_DIM])
        for k in range(2):
            p = r[:, k * HEAD_DIM:(k + 1) * HEAD_DIM]
            o_ref[2 * pair + k] = (p * c + pltpu.roll(p, HEAD_DIM // 2, 1) * s).astype(o_ref.dtype)


def qkv_proj(x, g, w_tiles, cos, sin, *, seq_len, tm=1024):
    n, k = x.shape
    n_tiles, _, tn = w_tiles.shape
    hpt = tn // HEAD_DIM
    assert N_HEADS_QKV % hpt == 0 and N_HEADS_A % 2 == 0 and hpt % 2 == 0
    assert tn == hpt * HEAD_DIM
    m = n_tiles * tn
    tm = min(tm, seq_len)
    tiles_per_seq = seq_len // tm
    est = 2 * tm * k * 4 + tm * k * 2 + 2 * k * tn * 2 + 2 * tm * tn * 2 + 4 * tm * HEAD_DIM * 4 + tm * tn * 4
    return pl.pallas_call(
        functools.partial(_qkv_kernel, heads_per_tile=hpt, scale=HEAD_DIM ** -0.5),
        grid=(n // tm, m // tn),
        in_specs=[
            pl.BlockSpec((tm, k), lambda i, j: (i, 0)),
            pl.BlockSpec((1, k), lambda i, j: (0, 0)),
            pl.BlockSpec((None, k, tn), lambda i, j: (j, 0, 0)),
            pl.BlockSpec((tm, HEAD_DIM), lambda i, j: (i % tiles_per_seq, 0)),
            pl.BlockSpec((tm, HEAD_DIM), lambda i, j: (i % tiles_per_seq, 0)),
        ],
        out_specs=pl.BlockSpec((hpt, tm, HEAD_DIM), lambda i, j: (j, i, 0)),
        out_shape=jax.ShapeDtypeStruct((m // HEAD_DIM, n, HEAD_DIM), BF16),
        scratch_shapes=[pltpu.VMEM((tm, k), BF16)],
        compiler_params=_cparams(("parallel", "arbitrary"), est),
        name="qkv_proj",
    )(x, g.reshape(1, k), w_tiles, cos, sin)


def rope_tables(seq_len):
    half = HEAD_DIM // 2
    inv = ROPE_THETA ** (-jnp.arange(half, dtype=F32) / half)
    ang = jnp.arange(seq_len, dtype=F32)[:, None] * inv[None, :]
    cos = jnp.cos(ang)
    sin = jnp.sin(ang)
    return jnp.concatenate([cos, cos], axis=1), jnp.concatenate([-sin, sin], axis=1)


def _natten_kernel(q_ref, k_ref, v_ref, bias_ref, o_ref, *, grid_rows):
    blk = pl.program_id(2)
    tk = WIN_H * GRID_W
    starts, scores, probs = [], [], []
    for ri in range(NAT_ROWS):
        first_row = jnp.clip(blk * NAT_ROWS + ri - WIN_H // 2, 0, grid_rows - WIN_H)
        starts.append(pl.multiple_of(first_row * GRID_W, GRID_W))
        scores.append(_dot_nt(q_ref[ri * GRID_W:(ri + 1) * GRID_W, :], k_ref[pl.ds(starts[ri], tk), :]) + bias_ref[ri])
    for ri in range(NAT_ROWS):
        p = jnp.exp(scores[ri] - jnp.max(scores[ri], axis=-1, keepdims=True))
        probs.append((p.astype(BF16), jnp.sum(p, axis=-1, keepdims=True)))
    for ri in range(NAT_ROWS):
        p, l = probs[ri]
        o_ref[ri * GRID_W:(ri + 1) * GRID_W, :] = (_dot(p, v_ref[pl.ds(starts[ri], tk), :]) / l).astype(o_ref.dtype)


def natten_bias_tiles(rpb, grid_rows):
    n_blocks = grid_rows // NAT_ROWS
    r0s = np.array([0, NAT_ROWS, (n_blocks - 1) * NAT_ROWS])
    r = r0s[:, None] + np.arange(NAT_ROWS)[None, :]
    rs = np.clip(r - WIN_H // 2, 0, grid_rows - WIN_H)
    d_row = rs[:, :, None] - r[:, :, None] + (WIN_H - 1) + np.arange(WIN_H)[None, None, :]
    c = np.arange(GRID_W)
    cs = np.clip(c - WIN_W // 2, 0, GRID_W - WIN_W)
    col_ok = (c[None, :] >= cs[:, None]) & (c[None, :] < cs[:, None] + WIN_W)
    d_col = np.clip(c[None, :] - c[:, None] + (WIN_W - 1), 0, 2 * WIN_W - 2)
    sel_row = (d_row[..., None] == np.arange(2 * WIN_H - 1)).astype(np.float32)
    sel_col = (d_col[..., None] == np.arange(2 * WIN_W - 1)).astype(np.float32)
    rows = jnp.einsum("tikr,hrc->thikc", sel_row, rpb.astype(F32), precision=lax.Precision.HIGHEST)
    bias = jnp.einsum("thikc,qwc->thiqkw", rows, sel_col, precision=lax.Precision.HIGHEST)
    bias = jnp.where(col_ok[None, None, None, :, None, :], bias, NEG_INF)
    return bias.reshape(3, rpb.shape[0], NAT_ROWS, GRID_W, WIN_H * GRID_W)


def natten(qkv, bias_tiles, *, batch, seq_len):
    n = batch * seq_len
    grid_rows = seq_len // GRID_W
    n_blocks = grid_rows // NAT_ROWS
    tq = NAT_ROWS * GRID_W
    tk = WIN_H * GRID_W

    def tile_kind(blk):
        return jnp.where(blk == 0, 0, jnp.where(blk == n_blocks - 1, 2, 1))

    est = 4 * seq_len * HEAD_DIM * 2 + 2 * tq * tk * 4 + 6 * tq * tk * 4
    return pl.pallas_call(
        functools.partial(_natten_kernel, grid_rows=grid_rows),
        grid=(batch, N_HEADS_A, n_blocks),
        in_specs=[
            pl.BlockSpec((None, tq, HEAD_DIM), lambda b, h, i: (h, b * n_blocks + i, 0)),
            pl.BlockSpec((None, seq_len, HEAD_DIM), lambda b, h, i: (N_HEADS_QKV + h, b, 0)),
            pl.BlockSpec((None, seq_len, HEAD_DIM), lambda b, h, i: (2 * N_HEADS_QKV + h, b, 0)),
            pl.BlockSpec((None, None, NAT_ROWS, GRID_W, tk), lambda b, h, i: (tile_kind(i), h, 0, 0, 0)),
        ],
        out_specs=pl.BlockSpec((tq, HEAD_DIM), lambda b, h, i: (b * n_blocks + i, h)),
        out_shape=jax.ShapeDtypeStruct((n, N_HEADS_A * HEAD_DIM), BF16),
        compiler_params=_cparams(("parallel", "parallel", "arbitrary"), est),
        name="natten",
    )(qkv, qkv, qkv, bias_tiles)


class _DilatedGeometry:
    def __init__(self, dil, kc):
        self.dil, self.kc = dil, kc
        self.reach = BAND_RADIUS * dil
        self.tp = DIL_TQ // DIL_PARTS
        assert kc % self.tp == 0
        self.halo = -(-self.reach // kc) * kc
        self.n_chunks = (DIL_TQ + 2 * self.halo) // kc
        self.n_tiles = (kc // self.tp) * (self.n_chunks - 1) + DIL_PARTS

    def tile_index(self, chunk, part):
        return (self.kc // self.tp) * chunk + (DIL_PARTS - 1 - part)

    def _allowed(self, u):
        rel = np.arange(self.kc)[None, :] - np.arange(self.tp)[:, None] + (u - (DIL_PARTS - 1)) * self.tp - self.halo
        return (np.abs(rel) <= self.reach) & (rel % self.dil == 0)

    def is_active(self, chunk, part):
        return bool(self._allowed(self.tile_index(chunk, part)).any())

    def bias_tiles(self):
        tiles = [np.where(self._allowed(u), 0.0, -np.inf) for u in range(self.n_tiles)]
        tiles.append(np.full((self.tp, self.kc), -np.inf))
        return np.stack(tiles).astype(np.float32)


DIL_GEOMETRY = tuple(_DilatedGeometry(d, kc) for d, kc in zip(DILATIONS, DIL_KC))


def _dilated_kernel(b0, b1, b2, q0, q1, q2, k0, k1, k2, v0, v1, v2, o_ref, *, seq_len):
    tq = DIL_TQ
    parts = DIL_PARTS
    tp = tq // parts
    t0 = pl.program_id(2) * tq
    carry = [(jnp.full((tp, 1), -1e30, F32), jnp.zeros((tp, 1), F32), jnp.zeros((tp, HEAD_DIM), F32))
             for _ in range(parts)]
    for geo, bias_ref, q_ref, k_ref, v_ref in zip(DIL_GEOMETRY, (b0, b1, b2), (q0, q1, q2), (k0, k1, k2), (v0, v1, v2)):
        kc = geo.kc
        qs = tuple(q_ref[p * tp:(p + 1) * tp, :] for p in range(parts))
        chunks, scores = [], {}
        group_max = [None] * parts
        for ci in range(geo.n_chunks):
            active = [p for p in range(parts) if geo.is_active(ci, p)]
            start = t0 - geo.halo + ci * kc
            in_range = jnp.logical_and(start >= 0, start + kc <= seq_len)
            ks = pl.multiple_of(jnp.clip(start, 0, seq_len - kc), kc)
            chunks.append((active, ks))
            kk = k_ref[pl.ds(ks, kc), :]
            for p in active:
                s = _dot_nt(qs[p], kk) + bias_ref[jnp.where(in_range, geo.tile_index(ci, p), geo.n_tiles)]
                scores[ci, p] = s
                row_max = jnp.max(s, axis=-1, keepdims=True)
                group_max[p] = row_max if group_max[p] is None else jnp.maximum(group_max[p], row_max)
        new_max, state = [], []
        for p in range(parts):
            m, l, acc = carry[p]
            m_new = jnp.maximum(m, group_max[p])
            alpha = jnp.exp(m - m_new)
            new_max.append(m_new)
            state.append([alpha * l, alpha * acc])
        for ci, (active, ks) in enumerate(chunks):
            vv = v_ref[pl.ds(ks, kc), :]
            for p in active:
                e = jnp.exp(scores[ci, p] - new_max[p])
                state[p][0] = state[p][0] + jnp.sum(e, axis=-1, keepdims=True)
                state[p][1] = state[p][1] + _dot(e.astype(BF16), vv)
        carry = [(new_max[p], state[p][0], state[p][1]) for p in range(parts)]
    for p in range(parts):
        _, l, acc = carry[p]
        o_ref[p * tp:(p + 1) * tp, :] = (acc / l).astype(o_ref.dtype)


def dilated_attention(qkv, *, batch, seq_len):
    n = batch * seq_len
    nq = seq_len // DIL_TQ
    biases = [geo.bias_tiles() for geo in DIL_GEOMETRY]

    def q_spec(g):
        return pl.BlockSpec((None, DIL_TQ, HEAD_DIM),
                            lambda b, j, i: (N_HEADS_A + N_HEADS_B_GROUP * g + j, b * nq + i, 0))

    def kv_spec(part, g):
        return pl.BlockSpec((None, seq_len, HEAD_DIM),
                            lambda b, j, i: (part * N_HEADS_QKV + N_HEADS_A + N_HEADS_B_GROUP * g + j, b, 0))

    est = 12 * seq_len * HEAD_DIM * 2 + 8 * DIL_TQ * max(DIL_KC) * 4 + 2 * sum(b.size for b in biases) * 4
    groups = range(len(DILATIONS))
    return pl.pallas_call(
        functools.partial(_dilated_kernel, seq_len=seq_len),
        grid=(batch, N_HEADS_B_GROUP, nq),
        in_specs=[pl.BlockSpec(b.shape, lambda b_, j, i: (0, 0, 0)) for b in biases]
        + [q_spec(g) for g in groups] + [kv_spec(1, g) for g in groups] + [kv_spec(2, g) for g in groups],
        out_specs=pl.BlockSpec((DIL_TQ, HEAD_DIM), lambda b, j, i: (b * nq + i, j)),
        out_shape=jax.ShapeDtypeStruct((n, N_HEADS_B_GROUP * HEAD_DIM), BF16),
        compiler_params=_cparams(("parallel", "parallel", "arbitrary"), est),
        name="dilated_attention",
    )(*biases, *([qkv] * 9))


def _attn_out_kernel(x_ref, oa_ref, ob_ref, w_ref, o_ref):
    ka = oa_ref.shape[1]
    o_ref[...] = x_ref[...] + _dot(oa_ref[...], w_ref[:ka, :]) + _dot(ob_ref[...], w_ref[ka:, :])


def attn_out(x, o_a, o_b, w, *, tm=512):
    n, d = x.shape
    ka, kb = o_a.shape[1], o_b.shape[1]
    tm = min(tm, n)
    est = 4 * tm * d * 4 + 2 * (ka + kb) * d * 2 + 2 * tm * (ka + kb) * 2
    return pl.pallas_call(
        _attn_out_kernel,
        grid=(n // tm,),
        in_specs=[
            pl.BlockSpec((tm, d), lambda i: (i, 0)),
            pl.BlockSpec((tm, ka), lambda i: (i, 0)),
            pl.BlockSpec((tm, kb), lambda i: (i, 0)),
            pl.BlockSpec((ka + kb, d), lambda i: (0, 0)),
        ],
        out_specs=pl.BlockSpec((tm, d), lambda i: (i, 0)),
        out_shape=jax.ShapeDtypeStruct((n, d), F32),
        compiler_params=_cparams(("parallel",), est),
        name="attn_out",
    )(x, o_a, o_b, w)


def _ffn_kernel(x_ref, g_ref, wg_ref, wu_ref, wd_ref, o_ref, xn_ref):
    f = pl.program_id(1)

    @pl.when(f == 0)
    def _():
        xn_ref[...] = _rmsnorm(x_ref[...], g_ref[...]).astype(BF16)
        o_ref[...] = x_ref[...]

    xn = xn_ref[...]
    gate = _dot(xn, wg_ref[...])
    up = _dot(xn, wu_ref[...])
    hid = (gate * _sigmoid(gate) * up).astype(BF16)
    o_ref[...] += _dot(hid, wd_ref[...])


def ffn(x, g, wg_tiles, wu_tiles, wd, *, tm=512):
    n, d = x.shape
    nf, _, tf = wg_tiles.shape
    tm = min(tm, n)
    est = 4 * tm * d * 4 + tm * d * 2 + 6 * d * tf * 2 + 3 * tm * tf * 4
    return pl.pallas_call(
        _ffn_kernel,
        grid=(n // tm, nf),
        in_specs=[
            pl.BlockSpec((tm, d), lambda i, f: (i, 0)),
            pl.BlockSpec((1, d), lambda i, f: (0, 0)),
            pl.BlockSpec((None, d, tf), lambda i, f: (f, 0, 0)),
            pl.BlockSpec((None, d, tf), lambda i, f: (f, 0, 0)),
            pl.BlockSpec((tf, d), lambda i, f: (f, 0)),
        ],
        out_specs=pl.BlockSpec((tm, d), lambda i, f: (i, 0)),
        out_shape=jax.ShapeDtypeStruct((n, d), F32),
        scratch_shapes=[pltpu.VMEM((tm, d), BF16)],
        compiler_params=_cparams(("parallel", "arbitrary"), est),
        name="ffn",
    )(x, g.reshape(1, d), wg_tiles, wu_tiles, wd)


def _conv_kernel(xm_ref, xp_ref, xn_ref, w_ref, b_ref, o_ref, *rest, tr, n_row_blocks):
    ext_ref = rest[-1]
    i = pl.program_id(2)
    hb = BF16_SUBLANES
    pad = SSM_CONV // 2
    ext_ref[0:hb, :] = jnp.where(i > 0, xp_ref[...].astype(F32), 0.0)
    ext_ref[hb:hb + tr, :] = xm_ref[...].astype(F32)
    ext_ref[hb + tr:2 * hb + tr, :] = jnp.where(i < n_row_blocks - 1, xn_ref[...].astype(F32), 0.0)
    ext = ext_ref[...]
    n_ext = ext.shape[0]
    acc = jnp.broadcast_to(b_ref[...], o_ref.shape)
    for k in range(SSM_CONV):
        shifted = ext if k == pad else pltpu.roll(ext, (pad - k) % n_ext, 0)
        acc = acc + shifted[hb:hb + tr, :] * w_ref[k:k + 1, :]
    out = acc * _sigmoid(acc)
    o_ref[...] = out.astype(o_ref.dtype)
    if len(rest) == 2:
        ot_ref = rest[0]
        L = ot_ref.shape[-1]
        for c in range(tr // L):
            ot_ref[c] = out[c * L:(c + 1) * L, :].T.astype(ot_ref.dtype)


def conv_silu(zx, conv_w, conv_b, *, width, first_tile, n_tiles, col_offset, batch, seq_len, tr, transposed=False):
    n = batch * seq_len
    tile_w = zx.shape[2]
    per = tile_w // width
    tr = min(tr, seq_len)
    nr = seq_len // tr
    hb = BF16_SUBLANES
    seq_hb = seq_len // hb
    n_hb = n // hb

    def main_map(c, b, i):
        return (first_tile + c // per, b * nr + i, c % per)

    def prev_map(c, b, i):
        return (first_tile + c // per, jnp.maximum(b * seq_hb + i * (tr // hb) - 1, 0), c % per)

    def next_map(c, b, i):
        return (first_tile + c // per, jnp.minimum(b * seq_hb + (i + 1) * (tr // hb), n_hb - 1), c % per)

    L = SSM_CHUNK
    out_specs = [pl.BlockSpec((None, tr, width), lambda c, b, i: (c, b * nr + i, 0))]
    out_shape = [jax.ShapeDtypeStruct((n_tiles, n, width), BF16)]
    if transposed:
        out_specs.append(pl.BlockSpec((None, tr // L, width, L), lambda c, b, i: (c, b * nr + i, 0, 0)))
        out_shape.append(jax.ShapeDtypeStruct((n_tiles, n // L, width, L), BF16))
    est = 4 * tr * width * 2 + (tr + 2 * hb) * width * 4 + 4 * tr * width * 4 + 4 * tr * width * 2
    outs = pl.pallas_call(
        functools.partial(_conv_kernel, tr=tr, n_row_blocks=nr),
        grid=(n_tiles, batch, nr),
        in_specs=[
            pl.BlockSpec((None, tr, width), main_map),
            pl.BlockSpec((None, hb, width), prev_map),
            pl.BlockSpec((None, hb, width), next_map),
            pl.BlockSpec((SSM_CONV, width), lambda c, b, i: (0, col_offset // width + c)),
            pl.BlockSpec((1, width), lambda c, b, i: (0, col_offset // width + c)),
        ],
        out_specs=out_specs,
        out_shape=out_shape,
        scratch_shapes=[pltpu.VMEM((tr + 2 * hb, width), F32)],
        compiler_params=_cparams(("parallel", "parallel", "arbitrary"), est),
        name="conv_silu",
    )(zx, zx, zx, conv_w, conv_b.reshape(1, -1))
    return outs if transposed else outs[0]


def _lane_cumsum(a):
    lane = lax.broadcasted_iota(jnp.int32, a.shape, 1)
    shift = 1
    while shift < a.shape[1]:
        a = a + jnp.where(lane >= shift, pltpu.roll(a, shift, 1), 0.0)
        shift *= 2
    return a


def _ssd_decay_kernel(dt_ref, dtbias_ref, alog_ref, seg_ref, dtact_ref, *, hpg):
    L = SSM_CHUNK
    chunks, rows, _ = dt_ref.shape
    dt_act = _softplus(dt_ref[...] + dtbias_ref[...])
    dt_all = dt_act.reshape(chunks * rows, L)
    a_all = (dt_act * (-jnp.exp(alog_ref[...]))).reshape(chunks * rows, L)
    cum = _lane_cumsum(a_all)
    suf = cum[:, L - 1:L] - cum + a_all
    row = lax.broadcasted_iota(jnp.int32, cum.shape, 0)
    seg_ref[...] = jnp.where((row & (2 * hpg - 1)) < hpg, cum, suf).reshape(chunks, rows, L)
    dtact_ref[...] = dt_all.reshape(chunks, rows, L)


def ssd_decay_rows(dt_rows, dt_bias_rows, a_log_rows, *, hpg, chunks_per_step=8):
    n_chunks, rows, L = dt_rows.shape
    assert hpg & (hpg - 1) == 0
    cps = math.gcd(chunks_per_step, n_chunks)
    spec = pl.BlockSpec((cps, rows, L), lambda i: (i, 0, 0))
    const = pl.BlockSpec((rows, 1), lambda i: (0, 0))
    shape = jax.ShapeDtypeStruct(dt_rows.shape, F32)
    return pl.pallas_call(
        functools.partial(_ssd_decay_kernel, hpg=hpg),
        grid=(n_chunks // cps,),
        in_specs=[spec, const, const],
        out_specs=[spec, spec],
        out_shape=[shape, shape],
        compiler_params=_cparams(("parallel",), 16 * cps * rows * L * 4),
        name="ssd_decay_rows",
    )(dt_rows, dt_bias_rows, a_log_rows)


def _ssd_kernel(xa_ref, ba_ref, bta_ref, ca_ref, sega_ref, dta_ref, xb_ref, btb_ref, cb_ref, segb_ref, dtb_ref,
                dskip_ref, yf_ref, yb_ref, sf_ref, sb_ref, *, hpg):
    L = SSM_CHUNK
    P = SSM_HEAD_DIM
    pairs = hpg // 2

    @pl.when(pl.program_id(1) == 0)
    def _():
        sf_ref[...] = jnp.zeros_like(sf_ref)
        sb_ref[...] = jnp.zeros_like(sb_ref)

    li = lax.broadcasted_iota(jnp.int32, (L, L), 0)
    si = lax.broadcasted_iota(jnp.int32, (L, L), 1)
    causal = li >= si
    anti = li <= si
    low_lanes = lax.broadcasted_iota(jnp.int32, (L, 2 * P), 1) < P
    high_lanes = jnp.logical_not(low_lanes)
    low_lanes_row = lax.broadcasted_iota(jnp.int32, (1, 2 * P), 1) < P

    def lanes_of(col):
        return jnp.broadcast_to(col, (L, 2 * P))

    def columns(seg_r, dt_r):
        stacked = jnp.concatenate([seg_r, dt_r, jnp.zeros((L - 4 * hpg, L), F32)], axis=0)
        return stacked.T

    spread_bwd = (lax.broadcasted_iota(jnp.int32, (L, hpg * P), 0)
                  == hpg + lax.broadcasted_iota(jnp.int32, (L, hpg * P), 1) // P).astype(BF16)

    def advance_state(s_ref, g, lanes, state, btf, xm0, xm1, seg_r, dt_r, h0, h1, tot_lane):
        acc = None
        decays = []
        for h, xm in ((h0, xm0), (h1, xm1)):
            tot = seg_r[h:h + 1, tot_lane:tot_lane + 1]
            coef = jnp.exp(tot - seg_r[h:h + 1, :]) * dt_r[h:h + 1, :]
            part = _dot((btf * coef).astype(BF16), xm)
            acc = part if acc is None else acc + part
            decays.append(jnp.exp(tot))
        decay = jnp.where(low_lanes_row, decays[0], decays[1])
        s_ref[g, :, lanes] = state[:, lanes] * decay + acc

    def group_body(g, c):
        r0 = pl.multiple_of(g * 2 * hpg, 2 * hpg)

        seg_r = sega_ref[pl.ds(r0, 2 * hpg), :]
        dt_r = dta_ref[pl.ds(r0, 2 * hpg), :]
        cols = columns(seg_r, dt_r)
        xg = xa_ref[g]
        cg = ca_ref[g]
        state = sf_ref[g]
        carried = _dot(cg, state.astype(BF16))
        cbm = _dot_nt(cg, ba_ref[g])
        btf = bta_ref[g].astype(F32)
        for p in range(pairs):
            lanes = slice(p * 2 * P, (p + 1) * 2 * P)
            xpf = xg[:, lanes].astype(F32)
            xm0 = jnp.where(low_lanes, xpf, 0.0).astype(BF16)
            xm1 = jnp.where(high_lanes, xpf, 0.0).astype(BF16)
            h0, h1 = 2 * p, 2 * p + 1
            f0, f1 = lanes_of(cols[:, h0:h0 + 1]), lanes_of(cols[:, h1:h1 + 1])
            y = carried[:, lanes] * jnp.where(low_lanes, jnp.exp(f0), jnp.exp(f1)) + dskip_ref[g][:, lanes] * xpf
            for hh, f_cols, xm in ((h0, f0, xm0), (h1, f1, xm1)):
                wf = jnp.exp(jnp.where(causal, f_cols - seg_r[hh:hh + 1, :], NEG_INF)) * dt_r[hh:hh + 1, :]
                b_cols = lanes_of(cols[:, hpg + hh:hpg + hh + 1])
                wb = jnp.exp(jnp.where(anti, b_cols - seg_r[hpg + hh:hpg + hh + 1, :], NEG_INF)) \
                    * dt_r[hpg + hh:hpg + hh + 1, :]
                y = y + _dot((cbm * (wf + wb)).astype(BF16), xm)
            yf_ref[g, :, lanes] = y.astype(yf_ref.dtype)
            advance_state(sf_ref, g, lanes, state, btf, xm0, xm1, seg_r, dt_r, h0, h1, L - 1)

        seg_r = segb_ref[pl.ds(r0, 2 * hpg), :]
        dt_r = dtb_ref[pl.ds(r0, 2 * hpg), :]
        xg = xb_ref[g]
        state = sb_ref[g]
        grow = _dot(jnp.exp(columns(seg_r, dt_r)).astype(BF16), spread_bwd)
        y = _dot(cb_ref[g], state.astype(BF16)) * grow
        yb_ref[g] = y.astype(yb_ref.dtype)
        btf = btb_ref[g].astype(F32)
        for p in range(pairs):
            lanes = slice(p * 2 * P, (p + 1) * 2 * P)
            xpf = xg[:, lanes].astype(F32)
            xm0 = jnp.where(low_lanes, xpf, 0.0).astype(BF16)
            xm1 = jnp.where(high_lanes, xpf, 0.0).astype(BF16)
            advance_state(sb_ref, g, lanes, state, btf, xm0, xm1, seg_r, dt_r, hpg + 2 * p, hpg + 2 * p + 1, 0)
        return c

    lax.fori_loop(0, SSM_GROUPS, group_body, 0, unroll=4)


def ssd(xs, b_nat, b_t, c_nat, seg_rows, dt_act_rows, d_skip_rows, *, batch, seq_len):
    groups, n, gw = xs.shape
    hpg = gw // SSM_HEAD_DIM
    assert hpg & (hpg - 1) == 0
    L = SSM_CHUNK
    nc = seq_len // L
    rows = groups * 2 * hpg

    def fwd(b, k):
        return b * nc + k

    def bwd(b, k):
        return b * nc + nc - 1 - k

    def specs(chunk, with_b_nat):
        out = [pl.BlockSpec((groups, L, gw), lambda b, k: (0, chunk(b, k), 0))]
        if with_b_nat:
            out.append(pl.BlockSpec((groups, L, SSM_STATE), lambda b, k: (0, chunk(b, k), 0)))
        out += [
            pl.BlockSpec((groups, None, SSM_STATE, L), lambda b, k: (0, chunk(b, k), 0, 0)),
            pl.BlockSpec((groups, L, SSM_STATE), lambda b, k: (0, chunk(b, k), 0)),
            pl.BlockSpec((None, rows, L), lambda b, k: (chunk(b, k), 0, 0)),
            pl.BlockSpec((None, rows, L), lambda b, k: (chunk(b, k), 0, 0)),
        ]
        return out

    const = [pl.BlockSpec((groups, 1, gw), lambda b, k: (0, 0, 0))]
    y_shape = jax.ShapeDtypeStruct((groups, n, gw), BF16)
    est = 4 * (groups * L * gw * 2 + 3 * groups * L * SSM_STATE * 2 + 2 * rows * L * 4) + 4 * groups * L * gw * 2 \
        + 2 * groups * SSM_STATE * gw * 4 + 64 * L * L * 4
    return pl.pallas_call(
        functools.partial(_ssd_kernel, hpg=hpg),
        grid=(batch, nc),
        in_specs=specs(fwd, True) + specs(bwd, False) + const,
        out_specs=[pl.BlockSpec((groups, L, gw), lambda b, k: (0, fwd(b, k), 0)),
                   pl.BlockSpec((groups, L, gw), lambda b, k: (0, bwd(b, k), 0))],
        out_shape=[y_shape, y_shape],
        scratch_shapes=[pltpu.VMEM((groups, SSM_STATE, gw), F32), pltpu.VMEM((groups, SSM_STATE, gw), F32)],
        compiler_params=_cparams(("parallel", "arbitrary"), est),
        name="ssd",
    )(xs, b_nat, b_t, c_nat, seg_rows, dt_act_rows, xs, b_t, c_nat, seg_rows, dt_act_rows, d_skip_rows)


def _mamba_out_kernel(yf_ref, yb_ref, z_ref, gg_ref, w_ref, x_ref, o_ref, ssq_ref, *, d_inner):
    step = pl.program_id(1)
    groups_per_step, _, gw = z_ref.shape

    @pl.when(step == 0)
    def _():
        o_ref[...] = jnp.zeros_like(o_ref)
        ssq_ref[...] = jnp.zeros_like(ssq_ref)

    gated = []
    for k in range(groups_per_step):
        z = z_ref[k].astype(F32)
        yz = (yf_ref[k].astype(F32) + yb_ref[k].astype(F32)) * (z * _sigmoid(z))
        ssq_ref[...] += jnp.sum(yz * yz, axis=-1, keepdims=True)
        gated.append((yz * gg_ref[:, k * gw:(k + 1) * gw]).astype(BF16))
    o_ref[...] += _dot(jnp.concatenate(gated, axis=1), w_ref[...])

    @pl.when(step == pl.num_programs(1) - 1)
    def _():
        o_ref[...] = x_ref[...] + o_ref[...] * lax.rsqrt(ssq_ref[...] / d_inner + EPS)


def mamba_out(y_f, y_b, zx, g_gate, w_out, x, *, tm=512, groups_per_step=4):
    groups, n, gw = y_f.shape
    d = x.shape[1]
    tm = min(tm, n)
    gps = groups_per_step
    est = gps * (6 * tm * gw * 2 + 2 * gw * d * 2) + 3 * tm * d * 4 + 2 * tm * gps * gw * 4
    return pl.pallas_call(
        functools.partial(_mamba_out_kernel, d_inner=groups * gw),
        grid=(n // tm, groups // gps),
        in_specs=[
            pl.BlockSpec((gps, tm, gw), lambda i, s: (s, i, 0)),
            pl.BlockSpec((gps, tm, gw), lambda i, s: (s, i, 0)),
            pl.BlockSpec((gps, tm, gw), lambda i, s: (s, i, 0)),
            pl.BlockSpec((1, gps * gw), lambda i, s: (0, s)),
            pl.BlockSpec((gps * gw, d), lambda i, s: (s, 0)),
            pl.BlockSpec((tm, d), lambda i, s: (i, 0)),
        ],
        out_specs=pl.BlockSpec((tm, d), lambda i, s: (i, 0)),
        out_shape=jax.ShapeDtypeStruct((n, d), F32),
        scratch_shapes=[pltpu.VMEM((tm, 1), F32)],
        compiler_params=_cparams(("parallel", "arbitrary"), est),
        name="mamba_out",
    )(y_f, y_b, zx, g_gate.reshape(1, -1), w_out, x)


def _router_kernel(xa_ref, xb_ref, g_ref, w_ref, o_ref, xn_ref, *, tiles_a):
    i = pl.program_id(0)

    @pl.when(i < tiles_a)
    def _():
        xn_ref[...] = _rmsnorm(xa_ref[...], g_ref[...])

    @pl.when(i >= tiles_a)
    def _():
        xn_ref[...] = _rmsnorm(xb_ref[...], g_ref[...])

    xn = xn_ref[...]
    logits = jnp.dot(xn, w_ref[...], preferred_element_type=F32, precision=lax.Precision.HIGHEST)
    lane = lax.broadcasted_iota(jnp.int32, logits.shape, 1)
    logits = jnp.where(lane < N_EXPERTS, logits, NEG_INF)
    v1 = jnp.max(logits, axis=-1, keepdims=True)
    i1 = jnp.min(jnp.where(logits == v1, lane, LANES), axis=-1, keepdims=True)
    rest = jnp.where(lane == i1, NEG_INF, logits)
    v2 = jnp.max(rest, axis=-1, keepdims=True)
    i2 = jnp.min(jnp.where(rest == v2, lane, LANES), axis=-1, keepdims=True)
    e2 = jnp.exp(v2 - v1)
    g1 = 1.0 / (1.0 + e2)
    g2 = e2 / (1.0 + e2)
    out = jnp.where(lane == 0, i1.astype(F32),
                    jnp.where(lane == 1, i2.astype(F32),
                              jnp.where(lane == 2, g1, jnp.where(lane == 3, g2, 0.0))))
    o_ref[...] = out


def router(x_a, x_b, g, w_router, *, tm=512):
    (na, d), nb = x_a.shape, x_b.shape[0]
    tm = min(tm, na, nb)
    tiles_a, tiles_b = na // tm, nb // tm
    w = jnp.zeros((d, LANES), F32).at[:, :w_router.shape[1]].set(w_router)
    est = 4 * tm * d * 4 + 2 * d * LANES * 4 + 2 * tm * LANES * 4 + 3 * tm * d * 4
    return pl.pallas_call(
        functools.partial(_router_kernel, tiles_a=tiles_a),
        grid=(tiles_a + tiles_b,),
        in_specs=[
            pl.BlockSpec((tm, d), lambda i: (jnp.minimum(i, tiles_a - 1), 0)),
            pl.BlockSpec((tm, d), lambda i: (jnp.maximum(i - tiles_a, 0), 0)),
            pl.BlockSpec((1, d), lambda i: (0, 0)),
            pl.BlockSpec((d, LANES), lambda i: (0, 0)),
        ],
        out_specs=[pl.BlockSpec((tm, LANES), lambda i: (i, 0)), pl.BlockSpec((tm, d), lambda i: (i, 0))],
        out_shape=[jax.ShapeDtypeStruct((na + nb, LANES), F32), jax.ShapeDtypeStruct((na + nb, d), F32)],
        compiler_params=_cparams(("parallel",), est),
        name="router",
    )(x_a, x_b, g.reshape(1, d), w)


def _row_copy(src_hbm, dst_vmem, sem, src_row, dst_row):
    return pltpu.make_async_copy(src_hbm.at[pl.ds(src_row, 1), :], dst_vmem.at[pl.ds(dst_row, 1), :], sem)


EXPERT_ISSUE_STEPS = 4


def _expert_kernel(blk_e_ref, n_used_ref, tok_ref, x_hbm, wg_ref, wu_ref, wd_ref, o_ref, rows_ref, x_ref, sems):
    i = pl.program_id(0)
    f = pl.program_id(1)
    tm = x_ref.shape[0]
    per_step = tm // EXPERT_ISSUE_STEPS
    n_used = n_used_ref[0]
    slot = i % 2

    def issue(block, dst_slot, first_row, count):
        def body(r, c):
            _row_copy(x_hbm, rows_ref.at[dst_slot], sems.at[dst_slot], tok_ref[block * tm + first_row + r],
                      first_row + r).start()
            return c
        lax.fori_loop(0, count, body, 0, unroll=8)

    @pl.when(jnp.logical_and(i == 0, f == 0))
    def _():
        issue(0, 0, 0, tm)

    @pl.when(jnp.logical_and(f == 0, i < n_used))
    def _():
        def wait(r, c):
            _row_copy(x_hbm, rows_ref.at[slot], sems.at[slot], 0, r).wait()
            return c
        lax.fori_loop(0, tm, wait, 0, unroll=8)
        x_ref[...] = rows_ref[slot].astype(x_ref.dtype)

    @pl.when(f == 0)
    def _():
        o_ref[...] = jnp.zeros_like(o_ref)

    @pl.when(jnp.logical_and(f < EXPERT_ISSUE_STEPS, i + 1 < n_used))
    def _():
        issue(i + 1, 1 - slot, f * per_step, per_step)

    @pl.when(i < n_used)
    def _():
        x = x_ref[...]
        gate = _dot(x, wg_ref[...])
        up = _dot(x, wu_ref[...])
        hid = (gate * _sigmoid(gate) * up).astype(BF16)
        o_ref[...] += _dot(hid, wd_ref[...])


def expert_ffn(blk_expert, n_used, tok_of_slot, x_rows, wg, wu, wd, *, tm=MOE_TM, tf=1024):
    slots = tok_of_slot.shape[0]
    d = x_rows.shape[1]
    nf = wg.shape[2] // tf
    assert nf >= EXPERT_ISSUE_STEPS and tm % (8 * EXPERT_ISSUE_STEPS) == 0
    n_blocks = slots // tm

    def live(i, n_used):
        return jnp.minimum(i, n_used[0] - 1)

    def f_eff(i, f, n_used):
        return jnp.where(i < n_used[0], f, nf - 1)

    est = 2 * tm * d * 4 + tm * d * 2 + 12 * d * tf + 2 * tm * d * 4 + 3 * tm * tf * 4
    return pl.pallas_call(
        _expert_kernel,
        grid_spec=pltpu.PrefetchScalarGridSpec(
            num_scalar_prefetch=3,
            grid=(n_blocks, nf),
            in_specs=[
                pl.BlockSpec(memory_space=pl.ANY),
                pl.BlockSpec((None, d, tf), lambda i, f, be, nu, tok: (be[live(i, nu)], 0, f_eff(i, f, nu))),
                pl.BlockSpec((None, d, tf), lambda i, f, be, nu, tok: (be[live(i, nu)], 0, f_eff(i, f, nu))),
                pl.BlockSpec((None, tf, d), lambda i, f, be, nu, tok: (be[live(i, nu)], f_eff(i, f, nu), 0)),
            ],
            out_specs=pl.BlockSpec((tm, d), lambda i, f, be, nu, tok: (i, 0)),
            scratch_shapes=[pltpu.VMEM((2, tm, d), F32), pltpu.VMEM((tm, d), BF16), pltpu.SemaphoreType.DMA((2,))],
        ),
        out_shape=jax.ShapeDtypeStruct((slots, d), F32),
        compiler_params=_cparams(("arbitrary", "arbitrary"), est),
        name="expert_ffn",
    )(blk_expert, n_used, tok_of_slot, x_rows, wg, wu, wd)


def _combine_kernel(slot_ref, y_hbm, x_ref, r_ref, g_ref, o_ref, buf_ref, sem):
    rows = x_ref.shape[0]
    base = pl.program_id(0) * rows

    def start(r, c):
        for k in range(TOP_K):
            _row_copy(y_hbm, buf_ref.at[k], sem, slot_ref[TOP_K * (base + r) + k], r).start()
        return c

    def wait(r, c):
        for k in range(TOP_K):
            _row_copy(y_hbm, buf_ref.at[k], sem, 0, r).wait()
        return c

    lax.fori_loop(0, rows, start, 0, unroll=8)
    lax.fori_loop(0, rows, wait, 0, unroll=8)
    gates = r_ref[...]
    out = x_ref[...] + gates[:, 2:3] * buf_ref[0] + gates[:, 3:4] * buf_ref[1]
    o_ref[...] = _rmsnorm(out, g_ref[...])


def combine_norm(slot_of_assignment, ys, x, routed, g_final, *, rows=GATHER_ROWS):
    n, d = x.shape
    rows = min(rows, n)
    est = 2 * rows * d * 4 + 4 * rows * d * 4 + 4 * rows * d * 4
    return pl.pallas_call(
        _combine_kernel,
        grid_spec=pltpu.PrefetchScalarGridSpec(
            num_scalar_prefetch=1,
            grid=(n // rows,),
            in_specs=[
                pl.BlockSpec(memory_space=pl.ANY),
                pl.BlockSpec((rows, d), lambda i, s: (i, 0)),
                pl.BlockSpec((rows, LANES), lambda i, s: (i, 0)),
                pl.BlockSpec((1, d), lambda i, s: (0, 0)),
            ],
            out_specs=pl.BlockSpec((rows, d), lambda i, s: (i, 0)),
            scratch_shapes=[pltpu.VMEM((TOP_K, rows, d), F32), pltpu.SemaphoreType.DMA(())],
        ),
        out_shape=jax.ShapeDtypeStruct((n, d), F32),
        compiler_params=_cparams(("arbitrary",), est),
        name="combine_norm",
    )(slot_of_assignment, ys, x, routed, g_final.reshape(1, d))


def moe_plan(routed, *, tm):
    n = routed.shape[0]
    experts = routed[:, :TOP_K].astype(jnp.int32).reshape(-1)
    onehot = (experts[:, None] == jnp.arange(N_EXPERTS)[None, :]).astype(jnp.int32)
    rank = jnp.sum((jnp.cumsum(onehot, axis=0) - onehot) * onehot, axis=1)
    counts = jnp.sum(onehot, axis=0)
    padded = ((counts + tm - 1) // tm) * tm
    ends = jnp.cumsum(padded)
    starts = ends - padded
    slot = (starts[experts] + rank).astype(jnp.int32)
    n_blocks = (n * TOP_K) // tm + N_EXPERTS
    tok = jnp.repeat(jnp.arange(n, dtype=jnp.int32), TOP_K)
    tok_of_slot = jnp.zeros((n_blocks * tm,), jnp.int32).at[slot].set(tok, unique_indices=True,
                                                                      mode="promise_in_bounds")
    blk_expert = jnp.minimum(
        jnp.searchsorted(ends, jnp.arange(n_blocks, dtype=jnp.int32) * tm, side="right"), N_EXPERTS - 1
    ).astype(jnp.int32)
    n_used = (ends[-1] // tm).astype(jnp.int32).reshape(1)
    return slot, tok_of_slot, blk_expert, n_used


def _prepare_weights(w_qkv, w_o, w_ff_gate, w_ff_up, w_ff_down, w_in_c, w_out_c):
    d_inner = w_out_c.shape[1]
    gw = d_inner // SSM_GROUPS
    main_cols = 2 * d_inner + 2 * SSM_GROUPS * SSM_STATE
    return dict(
        w_qkv=column_tiles(w_qkv[0], QKV_TILE), w_o=w_o[0].astype(BF16),
        w_ff_gate=column_tiles(w_ff_gate[0], FF_TILE), w_ff_up=column_tiles(w_ff_up[0], FF_TILE),
        w_ff_down=w_ff_down[0].astype(BF16),
        w_in_main=column_tiles(w_in_c[0][:, :main_cols], 2 * gw),
        w_in_dt=column_tiles(w_in_c[0][:, main_cols:], w_in_c.shape[2] - main_cols),
        w_out=w_out_c[0].astype(BF16),
    )


def _mixer_layers(x3, wb, g_mix, g_ffn, rpb, conv_w, conv_b, dt_bias, a_log, d_skip, g_gate):
    batch, seq_len, d = x3.shape
    n = batch * seq_len
    x = x3.reshape(n, d)

    cos, sin = rope_tables(seq_len)
    qkv = qkv_proj(x, g_mix[0], wb["w_qkv"], cos, sin, seq_len=seq_len)
    bias_tiles = natten_bias_tiles(rpb[0], seq_len // GRID_W)
    o_a = natten(qkv, bias_tiles, batch=batch, seq_len=seq_len)
    o_b = dilated_attention(qkv, batch=batch, seq_len=seq_len)
    x = attn_out(x, o_a, o_b, wb["w_o"])
    x = ffn(x, g_ffn[0], wb["w_ff_gate"], wb["w_ff_up"], wb["w_ff_down"])

    d_inner = wb["w_out"].shape[0]
    gw = d_inner // SSM_GROUPS
    hpg = gw // SSM_HEAD_DIM
    heads = SSM_GROUPS * hpg
    zx = norm_matmul(x, g_mix[1], wb["w_in_main"], out_width=gw, out_dtype=BF16)
    dt_raw = norm_matmul(x, g_mix[1], wb["w_in_dt"], out_width=2 * heads, out_dtype=F32)[0]
    z_tiles = d_inner // gw
    xs = conv_silu(zx, conv_w[0], conv_b[0], width=gw, first_tile=z_tiles, n_tiles=SSM_GROUPS,
                   col_offset=0, batch=batch, seq_len=seq_len, tr=2048)
    bc_tiles = SSM_GROUPS * SSM_STATE // gw
    b_nat, b_t = conv_silu(zx, conv_w[0], conv_b[0], width=SSM_STATE, first_tile=2 * z_tiles, n_tiles=SSM_GROUPS,
                           col_offset=d_inner, batch=batch, seq_len=seq_len, tr=4096, transposed=True)
    c_nat = conv_silu(zx, conv_w[0], conv_b[0], width=SSM_STATE, first_tile=2 * z_tiles + bc_tiles,
                      n_tiles=SSM_GROUPS, col_offset=d_inner + SSM_GROUPS * SSM_STATE,
                      batch=batch, seq_len=seq_len, tr=4096)
    L = SSM_CHUNK
    rows = SSM_GROUPS * 2 * hpg
    dt_rows = dt_raw.reshape(n // L, L, 2, SSM_GROUPS, hpg).transpose(0, 3, 2, 4, 1).reshape(n // L, rows, L)

    def per_row(p):
        return p.reshape(2, SSM_GROUPS, hpg).transpose(1, 0, 2).reshape(rows, 1)

    d_skip_rows = jnp.repeat(d_skip[0].reshape(SSM_GROUPS, 1, hpg), SSM_HEAD_DIM, axis=2)
    seg_rows, dt_act_rows = ssd_decay_rows(dt_rows, per_row(dt_bias[0]), per_row(a_log[0]), hpg=hpg)
    y_f, y_b = ssd(xs, b_nat, b_t, c_nat, seg_rows, dt_act_rows, d_skip_rows, batch=batch, seq_len=seq_len)
    return mamba_out(y_f, y_b, zx, g_gate[0], wb["w_out"], x)


def kernel(x_prompt, x_sample, g_mix, g_ffn, w_qkv, rpb, w_o, w_ff_gate, w_ff_up, w_ff_down, w_in_c, conv_w, conv_b,
           dt_bias, a_log, d_skip, g_gate, w_out_c, w_router, w_e_gate, w_e_up, w_e_down, g_final):
    wb = _prepare_weights(w_qkv, w_o, w_ff_gate, w_ff_up, w_ff_down, w_in_c, w_out_c)
    args = (wb, g_mix, g_ffn, rpb, conv_w, conv_b, dt_bias, a_log, d_skip, g_gate)
    x_p = _mixer_layers(x_prompt, *args)
    x_s = _mixer_layers(x_sample, *args)

    n_p = x_p.shape[0]
    routed, xn = router(x_p, x_s, g_ffn[1], w_router[0])
    slot, tok_of_slot, blk_expert, n_used = moe_plan(routed, tm=MOE_TM)
    ys = expert_ffn(blk_expert, n_used, tok_of_slot, xn, w_e_gate[0].astype(BF16), w_e_up[0].astype(BF16),
                    w_e_down[0].astype(BF16))
    out_p = combine_norm(slot[:TOP_K * n_p], ys, x_p, routed[:n_p], g_final)
    out_s = combine_norm(slot[TOP_K * n_p:], ys, x_s, routed[n_p:], g_final)
    return out_p.reshape(x_prompt.shape), out_s.reshape(x_sample.shape)
```
